```python
import jax
import jax.numpy as jnp
from jax import lax
import numpy as np

D_MODEL = 1024
BATCH = 32
SEQ = 2048
DEPTH = 4

N_MIXERS = 3
GRID_W = 64
DEEPNORM_ALPHA = (2 * DEPTH) ** 0.25
DEEPNORM_BETA = (8 * DEPTH) ** -0.25
LN_EPS = 1e-5
RMS_EPS = 1e-6

A_HEADS = 8
A_DQK = 64
A_DV = D_MODEL // A_HEADS
A_CHUNK = 64
A_FGATE_BIAS = 3.0
A_M_INIT = -1e30
A_QK = A_HEADS * A_DQK
A_PROJ = 2 * A_QK + 2 * D_MODEL + 4 * A_HEADS

B_HEAD = 64
B_HEADS = D_MODEL // B_HEAD
B_DECAY_LORA = 64
B_AAA_LORA = 64
B_GATE_LORA = 128
B_GN_EPS = 64e-5
B_PROJ = 3 * D_MODEL + 2 * B_DECAY_LORA + 2 * B_AAA_LORA + B_GATE_LORA

C_HEAD_DIM = 128
C_Q_HEADS = D_MODEL // C_HEAD_DIM
C_KV_HEADS = 2
C_GROUP = C_Q_HEADS // C_KV_HEADS
C_Q_BLOCK = 128
ROPE_THETA = 10000.0
C_PROJ = (C_Q_HEADS + 2 * C_KV_HEADS) * C_HEAD_DIM

N_EXPERTS = 32
TOP_K = 4
D_EXPERT = D_MODEL
SWIGLU_LIMIT = 7.0
SWIGLU_ALPHA = 1.702
MOE_BLOCK = 256

kernel_name = 'hybrid_mlstm_rwkv7_axialgqa_moe_encoder'


def layer_norm(x, g, b):
    xf = x.astype(jnp.float32)
    mu = xf.mean(-1, keepdims=True)
    var = jnp.square(xf - mu).mean(-1, keepdims=True)
    return ((xf - mu) * lax.rsqrt(var + LN_EPS) * g + b).astype(x.dtype)


def rms_norm(x, g):
    xf = x.astype(jnp.float32)
    return (xf * lax.rsqrt(jnp.square(xf).mean(-1, keepdims=True) + RMS_EPS) * g).astype(x.dtype)


def stack_directions(t_fwd, t_bwd, axis):
    return jnp.concatenate([t_fwd, jnp.flip(t_bwd, axis=axis)], axis=0)


def merge_directions(y, axis):
    half = y.shape[0] // 2
    return y[:half] + jnp.flip(y[half:], axis=axis)


def mlstm_chunkwise(q, k, v, log_i, log_f):
    n, nh, s, dk = q.shape
    dv = v.shape[-1]
    nc = s // A_CHUNK

    def chunks(a):
        return jnp.moveaxis(a.reshape(n, nh, nc, A_CHUNK, *a.shape[3:]), 2, 0)

    lower = jnp.tril(jnp.ones((A_CHUNK, A_CHUNK), dtype=bool))

    def step(carry, inp):
        c_st, n_st, m_st = carry
        qb, kb, vb, ib, fb = inp
        b = jnp.cumsum(fb, axis=-1)
        d = b[..., :, None] - b[..., None, :] + ib[..., None, :]
        d = jnp.where(lower, d, -jnp.inf)
        inter = b + m_st[..., None]
        m_t = jnp.maximum(inter, d.max(-1))
        w_inter = jnp.exp(inter - m_t)
        qk = jnp.einsum('nhtd,nhsd->nhts', qb, kb) * jnp.exp(d - m_t[..., None])
        num = jnp.einsum('nhts,nhsv->nhtv', qk, vb) + w_inter[..., None] * jnp.einsum('nhtd,nhdv->nhtv', qb, c_st)
        den = qk.sum(-1) + w_inter * jnp.einsum('nhtd,nhd->nht', qb, n_st)
        h = num / jnp.maximum(jnp.abs(den), jnp.exp(-m_t))[..., None]
        b_last = b[..., -1]
        g_s = b_last[..., None] - b + ib
        m_new = jnp.maximum(b_last + m_st, g_s.max(-1))
        w_s = jnp.exp(g_s - m_new[..., None])
        carry_decay = jnp.exp(b_last + m_st - m_new)
        c_new = carry_decay[..., None, None] * c_st + jnp.einsum('nhs,nhsd,nhsv->nhdv', w_s, kb, vb)
        n_new = carry_decay[..., None] * n_st + jnp.einsum('nhs,nhsd->nhd', w_s, kb)
        return (c_new, n_new, m_new), h

    init = (jnp.zeros((n, nh, dk, dv), jnp.float32),
            jnp.zeros((n, nh, dk), jnp.float32),
            jnp.full((n, nh), A_M_INIT, jnp.float32))
    _, hs = lax.scan(step, init, tuple(map(chunks, (q, k, v, log_i, log_f))))
    return jnp.moveaxis(hs, 0, 2).reshape(n, nh, s, dv)


def mlstm_mixer(x, w_in, b_in, head_gain, w_out):
    bsz, s, _ = x.shape
    p = (x @ w_in + b_in).astype(jnp.float32)
    q, k, v, o, gates = jnp.split(p, [A_QK, 2 * A_QK, 2 * A_QK + D_MODEL, 2 * A_QK + 2 * D_MODEL], axis=-1)

    def heads(a, dh):
        return a.reshape(bsz, s, A_HEADS, dh).transpose(0, 2, 1, 3)

    q = heads(q, A_DQK) * A_DQK ** -0.5
    k = heads(k, A_DQK)
    v = heads(v, A_DV)
    g = gates.reshape(bsz, s, 4, A_HEADS).transpose(2, 0, 3, 1)
    log_i = stack_directions(g[0], g[2], 2)
    log_f = jax.nn.log_sigmoid(stack_directions(g[1], g[3], 2))
    h = mlstm_chunkwise(stack_directions(q, q, 2), stack_directions(k, k, 2),
                        stack_directions(v, v, 2), log_i, log_f)
    h = merge_directions(h, 2).transpose(0, 2, 1, 3)
    h = rms_norm(h, head_gain.reshape(A_HEADS, A_DV)).reshape(bsz, s, D_MODEL)
    y = h * jax.nn.sigmoid(o)
    return y.astype(x.dtype) @ w_out


def centred_token_shift(p, mu):
    prev = jnp.pad(p[:, :-1], ((0, 0), (1, 0), (0, 0)))
    nxt = jnp.pad(p[:, 1:], ((0, 0), (0, 1), (0, 0)))
    return p + mu * (0.5 * (prev + nxt) - p)


def rwkv7_step(state, inp):
    r_t, w_t, k_t, v_t, kk_t, a_t = inp
    sa = jnp.einsum('nhvk,nhk->nhv', state, kk_t)
    state = (state * w_t[:, :, None, :] - sa[..., None] * (kk_t * a_t)[:, :, None, :]
             + v_t[..., None] * k_t[:, :, None, :])
    return state, jnp.einsum('nhvk,nhk->nhv', state, r_t)


def rwkv7_mixer(x, w_in, mu, w0, w_up, a0, a_up, k_k, k_a, r_k, g_up, lnx_g, lnx_b, w_out):
    bsz, s, _ = x.shape
    p = centred_token_shift((x @ w_in).astype(jnp.float32), mu)
    cut = [D_MODEL, 2 * D_MODEL, 3 * D_MODEL, 3 * D_MODEL + 2 * B_DECAY_LORA,
           3 * D_MODEL + 2 * B_DECAY_LORA + 2 * B_AAA_LORA]
    r, k, v, wl, al, gl = jnp.split(p, cut, axis=-1)
    wl = wl.reshape(bsz, s, 2, B_DECAY_LORA)
    al = al.reshape(bsz, s, 2, B_AAA_LORA)
    w_raw = w0 + jnp.einsum('bsdr,drc->bsdc', jnp.tanh(wl), w_up)
    decay = jnp.exp(-jnp.exp(-jax.nn.softplus(-w_raw) - 0.5))
    a = jax.nn.sigmoid(a0 + jnp.einsum('bsdr,drc->bsdc', al, a_up))
    g = jax.nn.sigmoid(gl) @ g_up
    kk = (k * k_k).reshape(bsz, s, B_HEADS, B_HEAD)
    kk = kk / jnp.maximum(jnp.linalg.norm(kk, axis=-1, keepdims=True), 1e-12)
    k_dir = k[:, :, None, :] * (1.0 + (a - 1.0) * k_a)

    def dirs(t_fwd, t_bwd):
        t = stack_directions(t_fwd, t_bwd, 1).reshape(2 * bsz, s, B_HEADS, B_HEAD)
        return t.transpose(1, 0, 2, 3)

    xs = (dirs(r, r), dirs(decay[:, :, 0], decay[:, :, 1]), dirs(k_dir[:, :, 0], k_dir[:, :, 1]),
          dirs(v, v), dirs(kk, kk), dirs(a[:, :, 0], a[:, :, 1]))
    state0 = jnp.zeros((2 * bsz, B_HEADS, B_HEAD, B_HEAD), jnp.float32)
    _, ys = lax.scan(rwkv7_step, state0, xs)
    y = merge_directions(ys.transpose(1, 0, 2, 3), 1)
    mu_y = y.mean(-1, keepdims=True)
    var_y = jnp.square(y - mu_y).mean(-1, keepdims=True)
    y = ((y - mu_y) * lax.rsqrt(var_y + B_GN_EPS) * lnx_g.reshape(B_HEADS, B_HEAD)
         + lnx_b.reshape(B_HEADS, B_HEAD))
    r_h = r.reshape(bsz, s, B_HEADS, B_HEAD)
    v_h = v.reshape(bsz, s, B_HEADS, B_HEAD)
    bonus = jnp.einsum('bshn,bsdhn,hn->bsh', r_h, k_dir.reshape(bsz, s, 2, B_HEADS, B_HEAD), r_k)[..., None] * v_h
    out = (y + bonus).reshape(bsz, s, D_MODEL) * g
    return out.astype(x.dtype) @ w_out


def axial_rope_tables(s):
    rows = s // GRID_W
    row = jnp.repeat(jnp.arange(rows, dtype=jnp.float32), GRID_W)
    col = jnp.tile(jnp.arange(GRID_W, dtype=jnp.float32), rows)
    n_freq = C_HEAD_DIM // 4
    inv_freq = ROPE_THETA ** (-jnp.arange(n_freq, dtype=jnp.float32) / n_freq)
    ang = jnp.concatenate([row[:, None] * inv_freq, col[:, None] * inv_freq], axis=-1)
    return jnp.cos(ang), jnp.sin(ang)


def apply_rope(x, cos, sin):
    xp = x.astype(jnp.float32).reshape(*x.shape[:-1], -1, 2)
    x0, x1 = xp[..., 0], xp[..., 1]
    return jnp.stack([x0 * cos - x1 * sin, x0 * sin + x1 * cos], axis=-1).reshape(x.shape)


def axial_gqa_mixer(x, w_in, q_gain, k_gain, w_out):
    bsz, s, _ = x.shape
    p = x @ w_in
    q, k, v = jnp.split(p, [C_Q_HEADS * C_HEAD_DIM, (C_Q_HEADS + C_KV_HEADS) * C_HEAD_DIM], axis=-1)
    q = rms_norm(q.reshape(bsz, s, C_KV_HEADS, C_GROUP, C_HEAD_DIM), q_gain)
    k = rms_norm(k.reshape(bsz, s, C_KV_HEADS, C_HEAD_DIM), k_gain)
    v = v.reshape(bsz, s, C_KV_HEADS, C_HEAD_DIM).astype(jnp.float32)
    cos, sin = axial_rope_tables(s)
    q = apply_rope(q, cos[:, None, None, :], sin[:, None, None, :]) * C_HEAD_DIM ** -0.5
    k = apply_rope(k, cos[:, None, :], sin[:, None, :])
    nb = s // C_Q_BLOCK
    q_blocks = jnp.moveaxis(q.reshape(bsz, nb, C_Q_BLOCK, C_KV_HEADS, C_GROUP, C_HEAD_DIM), 1, 0)

    def block(qi):
        scores = jnp.einsum('blhgd,bshd->bhgls', qi, k)
        probs = jax.nn.softmax(scores, axis=-1)
        return jnp.einsum('bhgls,bshd->blhgd', probs, v)

    o = lax.map(block, q_blocks)
    o = jnp.moveaxis(o, 0, 1).reshape(bsz, s, D_MODEL)
    return o.astype(x.dtype) @ w_out


def clamped_swiglu(h):
    x_glu, x_lin = h[..., ::2], h[..., 1::2]
    x_glu = jnp.minimum(x_glu, SWIGLU_LIMIT)
    x_lin = jnp.clip(x_lin, -SWIGLU_LIMIT, SWIGLU_LIMIT)
    return x_glu * jax.nn.sigmoid(SWIGLU_ALPHA * x_glu) * (x_lin + 1.0)


def moe_ffn(x, w_router, b_router, w_gu, b_gu, w_dn, b_dn):
    bsz, s, d = x.shape
    xt = x.reshape(-1, d)
    n = xt.shape[0]
    logits = (xt @ w_router + b_router).astype(jnp.float32)
    top_val, top_idx = lax.top_k(logits, TOP_K)
    gate = jax.nn.softmax(top_val, axis=-1)
    flat_e = top_idx.reshape(-1)
    order = jnp.argsort(flat_e)
    e_sorted = flat_e[order]
    tok_sorted = order // TOP_K
    gate_sorted = gate.reshape(-1)[order]
    counts = jnp.bincount(flat_e, length=N_EXPERTS)
    padded = (counts + MOE_BLOCK - 1) // MOE_BLOCK * MOE_BLOCK
    start = jnp.cumsum(counts) - counts
    pad_end = jnp.cumsum(padded)
    pad_start = pad_end - padded
    slot = pad_start[e_sorted] + (jnp.arange(n * TOP_K) - start[e_sorted])
    cap = -(-(n * TOP_K) // MOE_BLOCK) * MOE_BLOCK + N_EXPERTS * MOE_BLOCK
    n_blocks = cap // MOE_BLOCK
    buf = jnp.zeros((cap, d), x.dtype).at[slot].set(xt[tok_sorted])
    block_expert = jnp.minimum(
        jnp.searchsorted(pad_end, jnp.arange(n_blocks) * MOE_BLOCK, side='right'), N_EXPERTS - 1)

    def expert_block(args):
        xb, e = args
        h = xb @ w_gu[e] + b_gu[e]
        return clamped_swiglu(h) @ w_dn[e] + b_dn[e]

    yb = lax.map(expert_block, (buf.reshape(n_blocks, MOE_BLOCK, d), block_expert))
    y_sorted = yb.reshape(cap, d)[slot] * gate_sorted[:, None].astype(yb.dtype)
    y = jax.ops.segment_sum(y_sorted, tok_sorted, num_segments=n)
    return y.reshape(bsz, s, d).astype(x.dtype)


def setup_inputs(seed: int = 0) -> dict:
    key = jax.random.key(seed)
    ks = iter(jax.random.split(key, 40))
    D = D_MODEL
    n_a, n_b, n_c = (len(range(m, DEPTH, N_MIXERS)) for m in range(N_MIXERS))

    def nrm(shape, scale):
        return jax.random.normal(next(ks), shape, jnp.float32) * scale

    def unif(shape, lo, hi):
        return jax.random.uniform(next(ks), shape, jnp.float32, minval=lo, maxval=hi)

    a_gate_offset = jnp.concatenate([
        jnp.zeros((A_PROJ - 4 * A_HEADS,), jnp.float32),
        jnp.repeat(jnp.array([0.0, A_FGATE_BIAS, 0.0, A_FGATE_BIAS], jnp.float32), A_HEADS)])
    return {
        'x': nrm((BATCH, SEQ, D), 1.0),
        'a_w_in': nrm((n_a, D, A_PROJ), D ** -0.5),
        'a_b_in': nrm((n_a, A_PROJ), 0.02) + a_gate_offset,
        'a_head_gain': 1.0 + nrm((n_a, D), 0.02),
        'a_w_out': nrm((n_a, D, D), D ** -0.5 * DEEPNORM_BETA),
        'b_w_in': nrm((n_b, D, B_PROJ), D ** -0.5),
        'b_mu': unif((n_b, B_PROJ), 0.0, 1.0),
        'b_w0': unif((n_b, 2, D), -6.0, 1.0),
        'b_w_up': nrm((n_b, 2, B_DECAY_LORA, D), 0.1 * B_DECAY_LORA ** -0.5),
        'b_a0': nrm((n_b, 2, D), 0.1),
        'b_a_up': nrm((n_b, 2, B_AAA_LORA, D), 0.5 * B_AAA_LORA ** -0.5),
        'b_k_k': 0.85 + nrm((n_b, D), 0.02),
        'b_k_a': 1.0 + nrm((n_b, D), 0.02),
        'b_r_k': nrm((n_b, B_HEADS, B_HEAD), 0.1),
        'b_g_up': nrm((n_b, B_GATE_LORA, D), B_GATE_LORA ** -0.5),
        'b_lnx_g': 1.0 + nrm((n_b, D), 0.02),
        'b_lnx_b': nrm((n_b, D), 0.02),
        'b_w_out': nrm((n_b, D, D), D ** -0.5 * DEEPNORM_BETA),
        'c_w_in': nrm((n_c, D, C_PROJ), D ** -0.5),
        'c_q_gain': 1.0 + nrm((n_c, C_HEAD_DIM), 0.02),
        'c_k_gain': 1.0 + nrm((n_c, C_HEAD_DIM), 0.02),
        'c_w_out': nrm((n_c, D, D), D ** -0.5 * DEEPNORM_BETA),
        'ln1_g': 1.0 + nrm((DEPTH, D), 0.02),
        'ln1_b': nrm((DEPTH, D), 0.02),
        'moe_w_router': nrm((DEPTH, D, N_EXPERTS), D ** -0.5),
        'moe_b_router': nrm((DEPTH, N_EXPERTS), 0.01),
        'moe_w_gu': nrm((DEPTH, N_EXPERTS, D, 2 * D_EXPERT), D ** -0.5),
        'moe_b_gu': nrm((DEPTH, N_EXPERTS, 2 * D_EXPERT), 0.02),
        'moe_w_dn': nrm((DEPTH, N_EXPERTS, D_EXPERT, D), D_EXPERT ** -0.5 * DEEPNORM_BETA),
        'moe_b_dn': nrm((DEPTH, N_EXPERTS, D), 0.02),
        'ln2_g': 1.0 + nrm((DEPTH, D), 0.02),
        'ln2_b': nrm((DEPTH, D), 0.02),
    }


def reference(x, a_w_in, a_b_in, a_head_gain, a_w_out,
              b_w_in, b_mu, b_w0, b_w_up, b_a0, b_a_up, b_k_k, b_k_a, b_r_k, b_g_up,
              b_lnx_g, b_lnx_b, b_w_out,
              c_w_in, c_q_gain, c_k_gain, c_w_out,
              ln1_g, ln1_b,
              moe_w_router, moe_b_router, moe_w_gu, moe_b_gu, moe_w_dn, moe_b_dn,
              ln2_g, ln2_b):
    for i in range(DEPTH):
        kind = i % N_MIXERS
        j = i // N_MIXERS
        if kind == 0:
            mix = mlstm_mixer(x, a_w_in[j], a_b_in[j], a_head_gain[j], a_w_out[j])
        elif kind == 1:
            mix = rwkv7_mixer(x, b_w_in[j], b_mu[j], b_w0[j], b_w_up[j], b_a0[j], b_a_up[j],
                              b_k_k[j], b_k_a[j], b_r_k[j], b_g_up[j], b_lnx_g[j], b_lnx_b[j], b_w_out[j])
        else:
            mix = axial_gqa_mixer(x, c_w_in[j], c_q_gain[j], c_k_gain[j], c_w_out[j])
        x = layer_norm(DEEPNORM_ALPHA * x + mix, ln1_g[i], ln1_b[i])
        ffn = moe_ffn(x, moe_w_router[i], moe_b_router[i], moe_w_gu[i], moe_b_gu[i], moe_w_dn[i], moe_b_dn[i])
        x = layer_norm(DEEPNORM_ALPHA * x + ffn, ln2_g[i], ln2_b[i])
    return x
```

```python
import functools

import jax
import jax.numpy as jnp
from jax import lax
from jax.experimental import pallas as pl
from jax.experimental.pallas import tpu as pltpu

DEPTH = 4
N_MIXERS = 3
GRID_W = 64
DEEPNORM_ALPHA = (2 * DEPTH) ** 0.25
LN_EPS = 1e-5
RMS_EPS = 1e-6

A_HEADS = 8
A_DQK = 64
A_CHUNK = 64
A_M_INIT = -1e30

B_HEAD = 64
B_DECAY_LORA = 64
B_AAA_LORA = 64
B_GATE_LORA = 128
B_GN_EPS = 64e-5

C_HEAD_DIM = 128
C_KV_HEADS = 2
C_Q_BLOCK = 128
ROPE_THETA = 10000.0

N_EXPERTS = 32
TOP_K = 4
SWIGLU_LIMIT = 7.0
SWIGLU_ALPHA = 1.702

LANES = 128
VMEM_LIMIT_BYTES = 56 * 1024 * 1024

ROW_TILE = 512
MOE_ROW_TILE = 512


def _compiler_params(semantics):
    return pltpu.CompilerParams(dimension_semantics=semantics, vmem_limit_bytes=VMEM_LIMIT_BYTES)


def _row_tile(m):
    t = min(ROW_TILE, m)
    assert m % t == 0, (m, t)
    return t


def _proj_kernel(x_ref, w_ref, b_ref, o_ref, *, col_chunk):
    xb = x_ref[...].astype(jnp.bfloat16)
    n = o_ref.shape[1]
    for j in range(0, n, col_chunk):
        acc = jnp.dot(xb, w_ref[:, j:j + col_chunk], preferred_element_type=jnp.float32)
        o_ref[:, j:j + col_chunk] = (acc + b_ref[:, j:j + col_chunk]).astype(o_ref.dtype)


def _project(x, w, b, out_dtype):
    m, k = x.shape
    n = w.shape[1]
    assert n % LANES == 0
    tm = _row_tile(m)
    col_chunk = max(c for c in range(LANES, 5 * LANES + 1, LANES) if n % c == 0)
    return pl.pallas_call(
        functools.partial(_proj_kernel, col_chunk=col_chunk),
        out_shape=jax.ShapeDtypeStruct((m, n), out_dtype),
        grid=(m // tm,),
        in_specs=[pl.BlockSpec((tm, k), lambda i: (i, 0)),
                  pl.BlockSpec((k, n), lambda i: (0, 0)),
                  pl.BlockSpec((1, n), lambda i: (0, 0))],
        out_specs=pl.BlockSpec((tm, n), lambda i: (i, 0)),
        compiler_params=_compiler_params(("parallel",)),
        name="project",
    )(x, w.astype(jnp.bfloat16), b.reshape(1, n).astype(jnp.float32))


def _pad_cols(w, b, n_pad):
    k, n = w.shape
    if b is None:
        b = jnp.zeros((n,), jnp.float32)
    return jnp.pad(w, ((0, 0), (0, n_pad - n))), jnp.pad(b, (0, n_pad - n))


def _layer_norm_rows(z, g, b):
    mu = jnp.mean(z, axis=-1, keepdims=True)
    zc = z - mu
    var = jnp.mean(zc * zc, axis=-1, keepdims=True)
    return zc * lax.rsqrt(var + LN_EPS) * g + b


def _outproj_ln_kernel(h_ref, w_ref, x_ref, g_ref, b_ref, o_ref):
    mix = jnp.dot(h_ref[...].astype(jnp.bfloat16), w_ref[...], preferred_element_type=jnp.float32)
    z = DEEPNORM_ALPHA * x_ref[...] + mix
    o_ref[...] = _layer_norm_rows(z, g_ref[...], b_ref[...])


def _outproj_ln(h, w_out, x, g, b):
    m, d = x.shape
    tm = _row_tile(m)
    row = pl.BlockSpec((tm, d), lambda i: (i, 0))
    vec = pl.BlockSpec((1, d), lambda i: (0, 0))
    return pl.pallas_call(
        _outproj_ln_kernel,
        out_shape=jax.ShapeDtypeStruct((m, d), jnp.float32),
        grid=(m // tm,),
        in_specs=[row, pl.BlockSpec((d, d), lambda i: (0, 0)), row, vec, vec],
        out_specs=row,
        compiler_params=_compiler_params(("parallel",)),
        name="outproj_ln",
    )(h, w_out.astype(jnp.bfloat16), x, g.reshape(1, d), b.reshape(1, d))


def _residual_ln_kernel(y_ref, x_ref, g_ref, b_ref, o_ref):
    z = DEEPNORM_ALPHA * x_ref[...] + y_ref[...].astype(jnp.float32)
    o_ref[...] = _layer_norm_rows(z, g_ref[...], b_ref[...])


def _residual_ln(y, x, g, b):
    m, d = x.shape
    tm = _row_tile(m)
    row = pl.BlockSpec((tm, d), lambda i: (i, 0))
    vec = pl.BlockSpec((1, d), lambda i: (0, 0))
    return pl.pallas_call(
        _residual_ln_kernel,
        out_shape=jax.ShapeDtypeStruct((m, d), jnp.float32),
        grid=(m // tm,),
        in_specs=[row, row, vec, vec],
        out_specs=row,
        compiler_params=_compiler_params(("parallel",)),
        name="residual_ln",
    )(y, x, g.reshape(1, d), b.reshape(1, d))


def _router_kernel(x_ref, w_ref, b_ref, o_ref):
    o_ref[...] = jnp.dot(x_ref[...], w_ref[...], preferred_element_type=jnp.float32,
                         precision=lax.Precision.HIGHEST) + b_ref[...]


def _router_logits(xt, w_router, b_router):
    m, d = xt.shape
    tm = _row_tile(m)
    w, b = _pad_cols(w_router, b_router, LANES)
    out = pl.pallas_call(
        _router_kernel,
        out_shape=jax.ShapeDtypeStruct((m, LANES), jnp.float32),
        grid=(m // tm,),
        in_specs=[pl.BlockSpec((tm, d), lambda i: (i, 0)),
                  pl.BlockSpec((d, LANES), lambda i: (0, 0)),
                  pl.BlockSpec((1, LANES), lambda i: (0, 0))],
        out_specs=pl.BlockSpec((tm, LANES), lambda i: (i, 0)),
        compiler_params=_compiler_params(("parallel",)),
        name="router",
    )(xt, w, b.reshape(1, LANES))
    return out[:, :N_EXPERTS]


def _expert_kernel(be_ref, nb_ref, x_ref, wg_ref, wl_ref, bg_ref, bl_ref, wd_ref, bd_ref, gate_ref, o_ref):
    @pl.when(pl.program_id(0) < nb_ref[0])
    def _():
        xb = x_ref[...]
        hg = jnp.dot(xb, wg_ref[0], preferred_element_type=jnp.float32) + bg_ref[0]
        hl = jnp.dot(xb, wl_ref[0], preferred_element_type=jnp.float32) + bl_ref[0]
        hg = jnp.minimum(hg, SWIGLU_LIMIT)
        hl = jnp.clip(hl, -SWIGLU_LIMIT, SWIGLU_LIMIT)
        act = hg * jax.nn.sigmoid(SWIGLU_ALPHA * hg) * (hl + 1.0)
        y = jnp.dot(act.astype(jnp.bfloat16), wd_ref[0], preferred_element_type=jnp.float32) + bd_ref[0]
        o_ref[...] = (y * gate_ref[...]).astype(o_ref.dtype)


def _expert_ffn(x_sorted, gate_sorted, block_expert, n_used, w_g, w_l, b_g, b_l, w_dn, b_dn):
    cap, d = x_sorted.shape
    de = w_g.shape[2]
    tm = MOE_ROW_TILE
    n_blocks = cap // tm

    def blk(i, be, nb):
        return (jnp.minimum(i, nb[0] - 1), 0)

    def wsel(i, be, nb):
        return (be[i], 0, 0)

    grid_spec = pltpu.PrefetchScalarGridSpec(
        num_scalar_prefetch=2,
        grid=(n_blocks,),
        in_specs=[pl.BlockSpec((tm, d), blk),
                  pl.BlockSpec((1, d, de), wsel),
                  pl.BlockSpec((1, d, de), wsel),
                  pl.BlockSpec((1, 1, de), wsel),
                  pl.BlockSpec((1, 1, de), wsel),
                  pl.BlockSpec((1, de, d), wsel),
                  pl.BlockSpec((1, 1, d), wsel),
                  pl.BlockSpec((tm, 1), blk)],
        out_specs=pl.BlockSpec((tm, d), blk),
    )
    return pl.pallas_call(
        _expert_kernel,
        out_shape=jax.ShapeDtypeStruct((cap, d), jnp.bfloat16),
        grid_spec=grid_spec,
        compiler_params=_compiler_params(("arbitrary",)),
        name="expert_ffn",
    )(block_expert, n_used, x_sorted, w_g, w_l, b_g, b_l, w_dn, b_dn, gate_sorted)


def _moe_ffn(x, w_router, b_router, w_gu, b_gu, w_dn, b_dn):
    n, d = x.shape
    tm = MOE_ROW_TILE
    logits = _router_logits(x, w_router, b_router)
    top_val, top_idx = lax.top_k(logits, TOP_K)
    gate = jax.nn.softmax(top_val, axis=-1)
    flat_e = top_idx.reshape(-1)
    order = jnp.argsort(flat_e)
    e_sorted = flat_e[order]
    counts = jnp.bincount(flat_e, length=N_EXPERTS)
    padded = (counts + tm - 1) // tm * tm
    start = jnp.cumsum(counts) - counts
    pad_end = jnp.cumsum(padded)
    pad_start = pad_end - padded
    slot_sorted = pad_start[e_sorted] + (jnp.arange(n * TOP_K) - start[e_sorted])
    cap = n * TOP_K + N_EXPERTS * tm
    n_blocks = cap // tm
    tok_of_slot = jnp.zeros((cap,), jnp.int32).at[slot_sorted].set((order // TOP_K).astype(jnp.int32))
    gate_of_slot = jnp.zeros((cap,), jnp.float32).at[slot_sorted].set(gate.reshape(-1)[order])
    slot_of_assign = jnp.zeros((n * TOP_K,), jnp.int32).at[order].set(slot_sorted.astype(jnp.int32))
    block_expert = jnp.minimum(
        jnp.searchsorted(pad_end, jnp.arange(n_blocks) * tm, side='right'), N_EXPERTS - 1).astype(jnp.int32)
    n_used = (pad_end[-1] // tm).astype(jnp.int32).reshape(1)

    x_sorted = jnp.take(x.astype(jnp.bfloat16), tok_of_slot, axis=0)
    w_g = w_gu[:, :, 0::2].astype(jnp.bfloat16)
    w_l = w_gu[:, :, 1::2].astype(jnp.bfloat16)
    b_g = b_gu[:, None, 0::2]
    b_l = b_gu[:, None, 1::2]
    yb = _expert_ffn(x_sorted, gate_of_slot.reshape(cap, 1), block_expert, n_used,
                     w_g, w_l, b_g, b_l, w_dn.astype(jnp.bfloat16), b_dn[:, None, :])
    y = jnp.take(yb, slot_of_assign, axis=0).reshape(n, TOP_K, d).astype(jnp.float32).sum(axis=1)
    return y


def _stack_directions(t_fwd, t_bwd, axis):
    return jnp.concatenate([t_fwd, jnp.flip(t_bwd, axis=axis)], axis=0)


def _merge_directions(y, axis):
    half = y.shape[0] // 2
    return y[:half] + jnp.flip(y[half:], axis=axis)


def _rms_norm(x, g):
    return x * lax.rsqrt(jnp.square(x).mean(-1, keepdims=True) + RMS_EPS) * g


def _mlstm_chunkwise(q, k, v, log_i, log_f):
    n, nh, s, dk = q.shape
    dv = v.shape[-1]
    nc = s // A_CHUNK

    def chunks(a):
        return jnp.moveaxis(a.reshape(n, nh, nc, A_CHUNK, *a.shape[3:]), 2, 0)

    lower = jnp.tril(jnp.ones((A_CHUNK, A_CHUNK), dtype=bool))

    def step(carry, inp):
        c_st, n_st, m_st = carry
        qb, kb, vb, ib, fb = inp
        b = jnp.cumsum(fb, axis=-1)
        d = b[..., :, None] - b[..., None, :] + ib[..., None, :]
        d = jnp.where(lower, d, -jnp.inf)
        inter = b + m_st[..., None]
        m_t = jnp.maximum(inter, d.max(-1))
        w_inter = jnp.exp(inter - m_t)
        qk = jnp.einsum('nhtd,nhsd->nhts', qb, kb) * jnp.exp(d - m_t[..., None])
        num = jnp.einsum('nhts,nhsv->nhtv', qk, vb) + w_inter[..., None] * jnp.einsum('nhtd,nhdv->nhtv', qb, c_st)
        den = qk.sum(-1) + w_inter * jnp.einsum('nhtd,nhd->nht', qb, n_st)
        h = num / jnp.maximum(jnp.abs(den), jnp.exp(-m_t))[..., None]
        b_last = b[..., -1]
        g_s = b_last[..., None] - b + ib
        m_new = jnp.maximum(b_last + m_st, g_s.max(-1))
        w_s = jnp.exp(g_s - m_new[..., None])
        carry_decay = jnp.exp(b_last + m_st - m_new)
        c_new = carry_decay[..., None, None] * c_st + jnp.einsum('nhs,nhsd,nhsv->nhdv', w_s, kb, vb)
        n_new = carry_decay[..., None] * n_st + jnp.einsum('nhs,nhsd->nhd', w_s, kb)
        return (c_new, n_new, m_new), h

    init = (jnp.zeros((n, nh, dk, dv), jnp.float32),
            jnp.zeros((n, nh, dk), jnp.float32),
            jnp.full((n, nh), A_M_INIT, jnp.float32))
    _, hs = lax.scan(step, init, tuple(map(chunks, (q, k, v, log_i, log_f))))
    return jnp.moveaxis(hs, 0, 2).reshape(n, nh, s, dv)


def _mlstm_mixer(x, w_in, b_in, head_gain):
    bsz, s, d = x.shape
    a_qk = A_HEADS * A_DQK
    a_dv = d // A_HEADS
    n_proj = w_in.shape[1]
    n_pad = -(-n_proj // LANES) * LANES
    w, b = _pad_cols(w_in, b_in, n_pad)
    p = _project(x.reshape(bsz * s, d), w, b, jnp.float32).reshape(bsz, s, n_pad)[..., :n_proj]
    q, k, v, o, gates = jnp.split(p, [a_qk, 2 * a_qk, 2 * a_qk + d, 2 * a_qk + 2 * d], axis=-1)

    def heads(a, dh):
        return a.reshape(bsz, s, A_HEADS, dh).transpose(0, 2, 1, 3)

    q = heads(q, A_DQK) * A_DQK ** -0.5
    k = heads(k, A_DQK)
    v = heads(v, a_dv)
    g = gates.reshape(bsz, s, 4, A_HEADS).transpose(2, 0, 3, 1)
    log_i = _stack_directions(g[0], g[2], 2)
    log_f = jax.nn.log_sigmoid(_stack_directions(g[1], g[3], 2))
    h = _mlstm_chunkwise(_stack_directions(q, q, 2), _stack_directions(k, k, 2),
                         _stack_directions(v, v, 2), log_i, log_f)
    h = _merge_directions(h, 2).transpose(0, 2, 1, 3)
    h = _rms_norm(h, head_gain.reshape(A_HEADS, a_dv)).reshape(bsz, s, d)
    return h * jax.nn.sigmoid(o)


def _centred_token_shift(p, mu):
    prev = jnp.pad(p[:, :-1], ((0, 0), (1, 0), (0, 0)))
    nxt = jnp.pad(p[:, 1:], ((0, 0), (0, 1), (0, 0)))
    return p + mu * (0.5 * (prev + nxt) - p)


def _rwkv7_step(state, inp):
    r_t, w_t, k_t, v_t, kk_t, a_t = inp
    sa = jnp.einsum('nhvk,nhk->nhv', state, kk_t)
    state = (state * w_t[:, :, None, :] - sa[..., None] * (kk_t * a_t)[:, :, None, :]
             + v_t[..., None] * k_t[:, :, None, :])
    return state, jnp.einsum('nhvk,nhk->nhv', state, r_t)


def _rwkv7_mixer(x, w_in, mu, w0, w_up, a0, a_up, k_k, k_a, r_k, g_up, lnx_g, lnx_b):
    bsz, s, d = x.shape
    nh = d // B_HEAD
    n_proj = w_in.shape[1]
    p = _project(x.reshape(bsz * s, d), w_in, jnp.zeros((n_proj,), jnp.float32), jnp.float32)
    p = _centred_token_shift(p.reshape(bsz, s, n_proj), mu)
    cut = [d, 2 * d, 3 * d, 3 * d + 2 * B_DECAY_LORA, 3 * d + 2 * B_DECAY_LORA + 2 * B_AAA_LORA]
    r, k, v, wl, al, gl = jnp.split(p, cut, axis=-1)
    wl = wl.reshape(bsz, s, 2, B_DECAY_LORA)
    al = al.reshape(bsz, s, 2, B_AAA_LORA)
    w_raw = w0 + jnp.einsum('bsdr,drc->bsdc', jnp.tanh(wl), w_up)
    decay = jnp.exp(-jnp.exp(-jax.nn.softplus(-w_raw) - 0.5))
    a = jax.nn.sigmoid(a0 + jnp.einsum('bsdr,drc->bsdc', al, a_up))
    g = jax.nn.sigmoid(gl) @ g_up
    kk = (k * k_k).reshape(bsz, s, nh, B_HEAD)
    kk = kk / jnp.maximum(jnp.linalg.norm(kk, axis=-1, keepdims=True), 1e-12)
    k_dir = k[:, :, None, :] * (1.0 + (a - 1.0) * k_a)

    def dirs(t_fwd, t_bwd):
        t = _stack_directions(t_fwd, t_bwd, 1).reshape(2 * bsz, s, nh, B_HEAD)
        return t.transpose(1, 0, 2, 3)

    xs = (dirs(r, r), dirs(decay[:, :, 0], decay[:, :, 1]), dirs(k_dir[:, :, 0], k_dir[:, :, 1]),
          dirs(v, v), dirs(kk, kk), dirs(a[:, :, 0], a[:, :, 1]))
    state0 = jnp.zeros((2 * bsz, nh, B_HEAD, B_HEAD), jnp.float32)
    _, ys = lax.scan(_rwkv7_step, state0, xs)
    y = _merge_directions(ys.transpose(1, 0, 2, 3), 1)
    mu_y = y.mean(-1, keepdims=True)
    var_y = jnp.square(y - mu_y).mean(-1, keepdims=True)
    y = ((y - mu_y) * lax.rsqrt(var_y + B_GN_EPS) * lnx_g.reshape(nh, B_HEAD) + lnx_b.reshape(nh, B_HEAD))
    r_h = r.reshape(bsz, s, nh, B_HEAD)
    v_h = v.reshape(bsz, s, nh, B_HEAD)
    bonus = jnp.einsum('bshn,bsdhn,hn->bsh', r_h, k_dir.reshape(bsz, s, 2, nh, B_HEAD), r_k)[..., None] * v_h
    return (y + bonus).reshape(bsz, s, d) * g


def _axial_rope_tables(s):
    rows = s // GRID_W
    row = jnp.repeat(jnp.arange(rows, dtype=jnp.float32), GRID_W)
    col = jnp.tile(jnp.arange(GRID_W, dtype=jnp.float32), rows)
    n_freq = C_HEAD_DIM // 4
    inv_freq = ROPE_THETA ** (-jnp.arange(n_freq, dtype=jnp.float32) / n_freq)
    ang = jnp.concatenate([row[:, None] * inv_freq, col[:, None] * inv_freq], axis=-1)
    return jnp.cos(ang), jnp.sin(ang)


def _apply_rope(x, cos, sin):
    xp = x.reshape(*x.shape[:-1], -1, 2)
    x0, x1 = xp[..., 0], xp[..., 1]
    return jnp.stack([x0 * cos - x1 * sin, x0 * sin + x1 * cos], axis=-1).reshape(x.shape)


def _axial_gqa_mixer(x, w_in, q_gain, k_gain):
    bsz, s, d = x.shape
    qh = d // C_HEAD_DIM
    group = qh // C_KV_HEADS
    n_proj = w_in.shape[1]
    p = _project(x.reshape(bsz * s, d), w_in, jnp.zeros((n_proj,), jnp.float32), jnp.float32)
    p = p.reshape(bsz, s, n_proj)
    q, k, v = jnp.split(p, [qh * C_HEAD_DIM, (qh + C_KV_HEADS) * C_HEAD_DIM], axis=-1)
    q = _rms_norm(q.reshape(bsz, s, C_KV_HEADS, group, C_HEAD_DIM), q_gain)
    k = _rms_norm(k.reshape(bsz, s, C_KV_HEADS, C_HEAD_DIM), k_gain)
    v = v.reshape(bsz, s, C_KV_HEADS, C_HEAD_DIM)
    cos, sin = _axial_rope_tables(s)
    q = _apply_rope(q, cos[:, None, None, :], sin[:, None, None, :]) * C_HEAD_DIM ** -0.5
    k = _apply_rope(k, cos[:, None, :], sin[:, None, :])
    nb = s // C_Q_BLOCK
    q_blocks = jnp.moveaxis(q.reshape(bsz, nb, C_Q_BLOCK, C_KV_HEADS, group, C_HEAD_DIM), 1, 0)

    def block(qi):
        scores = jnp.einsum('blhgd,bshd->bhgls', qi, k)
        probs = jax.nn.softmax(scores, axis=-1)
        return jnp.einsum('bhgls,bshd->blhgd', probs, v)

    o = lax.map(block, q_blocks)
    return jnp.moveaxis(o, 0, 1).reshape(bsz, s, d)


def kernel(x, a_w_in, a_b_in, a_head_gain, a_w_out, b_w_in, b_mu, b_w0, b_w_up, b_a0, b_a_up, b_k_k, b_k_a, b_r_k, b_g_up, b_lnx_g, b_lnx_b, b_w_out, c_w_in, c_q_gain, c_k_gain, c_w_out, ln1_g, ln1_b, moe_w_router, moe_b_router, moe_w_gu, moe_b_gu, moe_w_dn, moe_b_dn, ln2_g, ln2_b):
    bsz, s, d = x.shape
    depth = ln1_g.shape[0]
    xt = x.reshape(bsz * s, d)
    for i in range(depth):
        kind = i % N_MIXERS
        j = i // N_MIXERS
        x3 = xt.reshape(bsz, s, d)
        if kind == 0:
            h = _mlstm_mixer(x3, a_w_in[j], a_b_in[j], a_head_gain[j])
            w_out = a_w_out[j]
        elif kind == 1:
            h = _rwkv7_mixer(x3, b_w_in[j], b_mu[j], b_w0[j], b_w_up[j], b_a0[j], b_a_up[j],
                             b_k_k[j], b_k_a[j], b_r_k[j], b_g_up[j], b_lnx_g[j], b_lnx_b[j])
            w_out = b_w_out[j]
        else:
            h = _axial_gqa_mixer(x3, c_w_in[j], c_q_gain[j], c_k_gain[j])
            w_out = c_w_out[j]
        xt = _outproj_ln(h.reshape(bsz * s, d), w_out, xt, ln1_g[i], ln1_b[i])
        ffn = _moe_ffn(xt, moe_w_router[i], moe_b_router[i], moe_w_gu[i], moe_b_gu[i], moe_w_dn[i], moe_b_dn[i])
        xt = _residual_ln(ffn, xt, ln2_g[i], ln2_b[i])
    return xt.reshape(bsz, s, d)
```

```python
import functools

import jax
import jax.numpy as jnp
from jax import lax
from jax.experimental import pallas as pl
from jax.experimental.pallas import tpu as pltpu

DEPTH = 4
N_MIXERS = 3
GRID_W = 64
DEEPNORM_ALPHA = (2 * DEPTH) ** 0.25
LN_EPS = 1e-5
RMS_EPS = 1e-6

A_HEADS = 8
A_DQK = 64
A_CHUNK = 64
A_M_INIT = -1e30

B_HEAD = 64
B_DECAY_LORA = 64
B_AAA_LORA = 64
B_GATE_LORA = 128
B_GN_EPS = 64e-5

C_HEAD_DIM = 128
C_KV_HEADS = 2
C_Q_BLOCK = 128
ROPE_THETA = 10000.0

N_EXPERTS = 32
TOP_K = 4
SWIGLU_LIMIT = 7.0
SWIGLU_ALPHA = 1.702

LANES = 128
VMEM_LIMIT_BYTES = 56 * 1024 * 1024

ROW_TILE = 512
MOE_ROW_TILE = 512


def _compiler_params(semantics):
    return pltpu.CompilerParams(dimension_semantics=semantics, vmem_limit_bytes=VMEM_LIMIT_BYTES)


def _row_tile(m):
    t = min(ROW_TILE, m)
    assert m % t == 0, (m, t)
    return t


def _col_chunk(n):
    return max(c for c in range(LANES, 5 * LANES + 1, LANES) if n % c == 0)


def _proj_kernel(x_ref, w_ref, b_ref, *o_refs):
    xb = x_ref[...].astype(jnp.bfloat16)
    col = 0
    for o_ref in o_refs:
        n = o_ref.shape[1]
        step = _col_chunk(n)
        for j in range(0, n, step):
            acc = jnp.dot(xb, w_ref[:, col + j:col + j + step], preferred_element_type=jnp.float32)
            o_ref[:, j:j + step] = (acc + b_ref[:, col + j:col + j + step]).astype(o_ref.dtype)
        col += n


def _project(x, w, b, out_dtype, f32_tail=0):
    m, k = x.shape
    n = w.shape[1]
    widths = [n - f32_tail, f32_tail] if f32_tail else [n]
    dtypes = [out_dtype, jnp.float32]
    assert all(c % LANES == 0 for c in widths)
    tm = _row_tile(m)
    outs = pl.pallas_call(
        _proj_kernel,
        out_shape=[jax.ShapeDtypeStruct((m, c), dt) for c, dt in zip(widths, dtypes)],
        grid=(m // tm,),
        in_specs=[pl.BlockSpec((tm, k), lambda i: (i, 0)),
                  pl.BlockSpec((k, n), lambda i: (0, 0)),
                  pl.BlockSpec((1, n), lambda i: (0, 0))],
        out_specs=[pl.BlockSpec((tm, c), lambda i: (i, 0)) for c in widths],
        compiler_params=_compiler_params(("parallel",)),
        name="project",
    )(x, w.astype(jnp.bfloat16), b.reshape(1, n).astype(jnp.float32))
    return tuple(outs) if f32_tail else outs[0]


def _pad_cols(w, b, n_pad):
    k, n = w.shape
    if b is None:
        b = jnp.zeros((n,), jnp.float32)
    return jnp.pad(w, ((0, 0), (0, n_pad - n))), jnp.pad(b, (0, n_pad - n))


def _layer_norm_rows(z, g, b):
    mu = jnp.mean(z, axis=-1, keepdims=True)
    zc = z - mu
    var = jnp.mean(zc * zc, axis=-1, keepdims=True)
    return zc * lax.rsqrt(var + LN_EPS) * g + b


def _outproj_ln_kernel(h_ref, w_ref, x_ref, g_ref, b_ref, o_ref):
    mix = jnp.dot(h_ref[...].astype(jnp.bfloat16), w_ref[...], preferred_element_type=jnp.float32)
    z = DEEPNORM_ALPHA * x_ref[...] + mix
    o_ref[...] = _layer_norm_rows(z, g_ref[...], b_ref[...])


def _outproj_ln(h, w_out, x, g, b):
    m, d = x.shape
    tm = _row_tile(m)
    row = pl.BlockSpec((tm, d), lambda i: (i, 0))
    vec = pl.BlockSpec((1, d), lambda i: (0, 0))
    return pl.pallas_call(
        _outproj_ln_kernel,
        out_shape=jax.ShapeDtypeStruct((m, d), jnp.float32),
        grid=(m // tm,),
        in_specs=[row, pl.BlockSpec((d, d), lambda i: (0, 0)), row, vec, vec],
        out_specs=row,
        compiler_params=_compiler_params(("parallel",)),
        name="outproj_ln",
    )(h, w_out.astype(jnp.bfloat16), x, g.reshape(1, d), b.reshape(1, d))


def _residual_ln_kernel(y_ref, x_ref, g_ref, b_ref, o_ref):
    z = DEEPNORM_ALPHA * x_ref[...] + y_ref[...].astype(jnp.float32)
    o_ref[...] = _layer_norm_rows(z, g_ref[...], b_ref[...])


def _residual_ln(y, x, g, b):
    m, d = x.shape
    tm = _row_tile(m)
    row = pl.BlockSpec((tm, d), lambda i: (i, 0))
    vec = pl.BlockSpec((1, d), lambda i: (0, 0))
    return pl.pallas_call(
        _residual_ln_kernel,
        out_shape=jax.ShapeDtypeStruct((m, d), jnp.float32),
        grid=(m // tm,),
        in_specs=[row, row, vec, vec],
        out_specs=row,
        compiler_params=_compiler_params(("parallel",)),
        name="residual_ln",
    )(y, x, g.reshape(1, d), b.reshape(1, d))


def _router_kernel(x_ref, w_ref, b_ref, o_ref):
    o_ref[...] = jnp.dot(x_ref[...], w_ref[...], preferred_element_type=jnp.float32,
                         precision=lax.Precision.HIGHEST) + b_ref[...]


def _router_logits(xt, w_router, b_router):
    m, d = xt.shape
    tm = _row_tile(m)
    w, b = _pad_cols(w_router, b_router, LANES)
    out = pl.pallas_call(
        _router_kernel,
        out_shape=jax.ShapeDtypeStruct((m, LANES), jnp.float32),
        grid=(m // tm,),
        in_specs=[pl.BlockSpec((tm, d), lambda i: (i, 0)),
                  pl.BlockSpec((d, LANES), lambda i: (0, 0)),
                  pl.BlockSpec((1, LANES), lambda i: (0, 0))],
        out_specs=pl.BlockSpec((tm, LANES), lambda i: (i, 0)),
        compiler_params=_compiler_params(("parallel",)),
        name="router",
    )(xt, w, b.reshape(1, LANES))
    return out[:, :N_EXPERTS]


def _expert_kernel(be_ref, nb_ref, x_ref, wg_ref, wl_ref, bg_ref, bl_ref, wd_ref, bd_ref, gate_ref, o_ref):
    @pl.when(pl.program_id(0) < nb_ref[0])
    def _():
        xb = x_ref[...]
        hg = jnp.dot(xb, wg_ref[0], preferred_element_type=jnp.float32) + bg_ref[0]
        hl = jnp.dot(xb, wl_ref[0], preferred_element_type=jnp.float32) + bl_ref[0]
        hg = jnp.minimum(hg, SWIGLU_LIMIT)
        hl = jnp.clip(hl, -SWIGLU_LIMIT, SWIGLU_LIMIT)
        act = hg * jax.nn.sigmoid(SWIGLU_ALPHA * hg) * (hl + 1.0)
        y = jnp.dot(act.astype(jnp.bfloat16), wd_ref[0], preferred_element_type=jnp.float32) + bd_ref[0]
        o_ref[...] = (y * gate_ref[...]).astype(o_ref.dtype)


def _expert_ffn(x_sorted, gate_sorted, block_expert, n_used, w_g, w_l, b_g, b_l, w_dn, b_dn):
    cap, d = x_sorted.shape
    de = w_g.shape[2]
    tm = MOE_ROW_TILE
    n_blocks = cap // tm

    def blk(i, be, nb):
        return (jnp.minimum(i, nb[0] - 1), 0)

    def wsel(i, be, nb):
        return (be[i], 0, 0)

    grid_spec = pltpu.PrefetchScalarGridSpec(
        num_scalar_prefetch=2,
        grid=(n_blocks,),
        in_specs=[pl.BlockSpec((tm, d), blk),
                  pl.BlockSpec((1, d, de), wsel),
                  pl.BlockSpec((1, d, de), wsel),
                  pl.BlockSpec((1, 1, de), wsel),
                  pl.BlockSpec((1, 1, de), wsel),
                  pl.BlockSpec((1, de, d), wsel),
                  pl.BlockSpec((1, 1, d), wsel),
                  pl.BlockSpec((tm, 1), blk)],
        out_specs=pl.BlockSpec((tm, d), blk),
    )
    return pl.pallas_call(
        _expert_kernel,
        out_shape=jax.ShapeDtypeStruct((cap, d), jnp.bfloat16),
        grid_spec=grid_spec,
        compiler_params=_compiler_params(("arbitrary",)),
        name="expert_ffn",
    )(block_expert, n_used, x_sorted, w_g, w_l, b_g, b_l, w_dn, b_dn, gate_sorted)


def _moe_ffn(x, w_router, b_router, w_gu, b_gu, w_dn, b_dn):
    n, d = x.shape
    tm = MOE_ROW_TILE
    logits = _router_logits(x, w_router, b_router)
    top_val, top_idx = lax.top_k(logits, TOP_K)
    gate = jax.nn.softmax(top_val, axis=-1)
    flat_e = top_idx.reshape(-1)
    order = jnp.argsort(flat_e)
    e_sorted = flat_e[order]
    counts = jnp.bincount(flat_e, length=N_EXPERTS)
    padded = (counts + tm - 1) // tm * tm
    start = jnp.cumsum(counts) - counts
    pad_end = jnp.cumsum(padded)
    pad_start = pad_end - padded
    slot_sorted = pad_start[e_sorted] + (jnp.arange(n * TOP_K) - start[e_sorted])
    cap = n * TOP_K + N_EXPERTS * tm
    n_blocks = cap // tm
    tok_of_slot = jnp.zeros((cap,), jnp.int32).at[slot_sorted].set((order // TOP_K).astype(jnp.int32))
    gate_of_slot = jnp.zeros((cap,), jnp.float32).at[slot_sorted].set(gate.reshape(-1)[order])
    slot_of_assign = jnp.zeros((n * TOP_K,), jnp.int32).at[order].set(slot_sorted.astype(jnp.int32))
    block_expert = jnp.minimum(
        jnp.searchsorted(pad_end, jnp.arange(n_blocks) * tm, side='right'), N_EXPERTS - 1).astype(jnp.int32)
    n_used = (pad_end[-1] // tm).astype(jnp.int32).reshape(1)

    x_sorted = jnp.take(x.astype(jnp.bfloat16), tok_of_slot, axis=0)
    w_g = w_gu[:, :, 0::2].astype(jnp.bfloat16)
    w_l = w_gu[:, :, 1::2].astype(jnp.bfloat16)
    b_g = b_gu[:, None, 0::2]
    b_l = b_gu[:, None, 1::2]
    yb = _expert_ffn(x_sorted, gate_of_slot.reshape(cap, 1), block_expert, n_used,
                     w_g, w_l, b_g, b_l, w_dn.astype(jnp.bfloat16), b_dn[:, None, :])
    y = jnp.take(yb, slot_of_assign, axis=0).reshape(n, TOP_K, d).astype(jnp.float32).sum(axis=1)
    return y


MLSTM_CHUNK = 128
MLSTM_PAIR = 2
MLSTM_NEG = -1e30


def _cumsum_rows(x, reverse):
    n = x.shape[0]
    row = lax.broadcasted_iota(jnp.int32, x.shape, 0)
    sh = 1
    while sh < n:
        if reverse:
            x = x + jnp.where(row < n - sh, pltpu.roll(x, n - sh, 0), 0.0)
        else:
            x = x + jnp.where(row >= sh, pltpu.roll(x, sh, 0), 0.0)
        sh *= 2
    return x


def _mlstm_kernel(q_ref, k_ref, v_ref, og_ref, g_ref, gain_ref, o_ref, h_s, c_s):
    f32, bf16 = jnp.float32, jnp.bfloat16
    seq = q_ref.shape[0]
    L = MLSTM_CHUNK
    n_chunks = seq // L
    dv = v_ref.shape[1] // MLSTM_PAIR
    dqk = q_ref.shape[1] // MLSTM_PAIR

    lane_q = lax.broadcasted_iota(jnp.int32, (1, q_ref.shape[1]), 1) // dqk
    lane_g = lax.broadcasted_iota(jnp.int32, (1, LANES), 1)
    is_forget_lane = (lane_g // MLSTM_PAIR) % 2 == 1
    row = lax.broadcasted_iota(jnp.int32, (L, L), 0)
    col = lax.broadcasted_iota(jnp.int32, (L, L), 1)
    causal = (col <= row, col >= row)
    ones_v = jnp.ones((L, dv), bf16)

    c_s[...] = jnp.zeros_like(c_s)

    def direction(start, d, m_prev):
        rows = pl.ds(start, L)
        gates = g_ref[rows, :]
        log_f = jnp.minimum(gates, 0.0) - jnp.log(1.0 + jnp.exp(-jnp.abs(gates)))
        cum = _cumsum_rows(log_f, reverse=(d == 1))
        z = jnp.where(is_forget_lane, cum, gates)
        z_t = z.T
        q_all = q_ref[rows, :]
        k_all = k_ref[rows, :]
        m_out = []
        for j in range(MLSTM_PAIR):
            lane_i = 2 * MLSTM_PAIR * d + j
            lane_f = lane_i + MLSTM_PAIR
            b_col, i_col = z[:, lane_f:lane_f + 1], z[:, lane_i:lane_i + 1]
            b_row, i_row = z_t[lane_f:lane_f + 1, :], z_t[lane_i:lane_i + 1, :]
            m_st = m_prev[j]
            dmat = jnp.where(causal[d], b_col + (i_row - b_row), MLSTM_NEG)
            inter = b_col + m_st
            m_t = jnp.maximum(inter, jnp.max(dmat, axis=-1, keepdims=True))
            qz = jnp.where(lane_q == j, q_all, jnp.zeros_like(q_all))
            kz = jnp.where(lane_q == j, k_all, jnp.zeros_like(k_all))
            v_aug = jnp.concatenate([v_ref[rows, j * dv:(j + 1) * dv], ones_v], axis=1)
            qk = _nt_dot(qz, kz) * (A_DQK ** -0.5) * jnp.exp(dmat - m_t)
            state = c_s[2 * d + j]
            nd = (jnp.dot(qk.astype(bf16), v_aug, preferred_element_type=f32)
                  + jnp.exp(inter - m_t) * (A_DQK ** -0.5)
                  * jnp.dot(qz, state.astype(bf16), preferred_element_type=f32))
            h = nd[:, :dv] / jnp.maximum(jnp.abs(nd[:, dv:]), jnp.exp(-m_t))
            h_s[d, rows, j * dv:(j + 1) * dv] = h
            b_last = b_col[L - 1:L, :] if d == 0 else b_col[0:1, :]
            g_s = b_last - b_col + i_col
            m_new = jnp.maximum(b_last + m_st, jnp.max(g_s, axis=0, keepdims=True))
            w_s = jnp.exp(g_s - m_new)
            c_s[2 * d + j] = (jnp.exp(b_last + m_st - m_new) * state
                              + _tn_dot((w_s * kz.astype(f32)).astype(bf16), v_aug))
            m_out.append(m_new)
        return m_out

    def body(c, m_all):
        m_f = direction(pl.multiple_of(c * L, L), 0, m_all[:MLSTM_PAIR])
        m_b = direction(pl.multiple_of((n_chunks - 1 - c) * L, L), 1, m_all[MLSTM_PAIR:])
        return tuple(m_f) + tuple(m_b)

    m_init = tuple(jnp.full((1, 1), A_M_INIT, f32) for _ in range(2 * MLSTM_PAIR))
    lax.fori_loop(0, n_chunks, body, m_init)

    def finish(c, carry):
        rows = pl.ds(pl.multiple_of(c * L, L), L)
        for j in range(MLSTM_PAIR):
            cols = slice(j * dv, (j + 1) * dv)
            h = h_s[0, rows, cols] + h_s[1, rows, cols]
            hn = h * lax.rsqrt(jnp.mean(h * h, axis=-1, keepdims=True) + RMS_EPS) * gain_ref[:, cols]
            o_ref[rows, cols] = (hn * jax.nn.sigmoid(og_ref[rows, cols].astype(f32))).astype(o_ref.dtype)
        return carry

    lax.fori_loop(0, n_chunks, finish, 0)


def _mlstm_mixer(x, w_in, b_in, head_gain):
    bsz, s, d = x.shape
    a_qk = A_HEADS * A_DQK
    dv = d // A_HEADS
    n_main = 2 * a_qk + 2 * d
    n_pairs = A_HEADS // MLSTM_PAIR
    pw_qk = MLSTM_PAIR * A_DQK
    pw_v = MLSTM_PAIR * dv
    assert pw_qk == LANES and s % MLSTM_CHUNK == 0
    gate_cols = jnp.asarray([[n_main + t * A_HEADS + MLSTM_PAIR * hp + j for t in range(4) for j in range(MLSTM_PAIR)]
                             for hp in range(n_pairs)])
    n_gate = gate_cols.shape[1]
    w_tail = jnp.pad(w_in[:, gate_cols], ((0, 0), (0, 0), (0, LANES - n_gate))).reshape(d, n_pairs * LANES)
    b_tail = jnp.pad(b_in[gate_cols], ((0, 0), (0, LANES - n_gate))).reshape(n_pairs * LANES)
    w = jnp.concatenate([w_in[:, :n_main], w_tail], axis=1)
    b = jnp.concatenate([b_in[:n_main], b_tail])
    p, gates = _project(x.reshape(bsz * s, d), w, b, jnp.bfloat16, f32_tail=n_pairs * LANES)
    p = p.reshape(bsz, s, n_main)
    gates = gates.reshape(bsz, s, n_pairs * LANES)
    k0 = a_qk // pw_qk
    v0 = 2 * a_qk // pw_v
    o0 = (2 * a_qk + d) // pw_v
    return pl.pallas_call(
        _mlstm_kernel,
        out_shape=jax.ShapeDtypeStruct((bsz, s, d), jnp.bfloat16),
        grid=(bsz, n_pairs),
        in_specs=[pl.BlockSpec((None, s, pw_qk), lambda b, h: (b, 0, h)),
                  pl.BlockSpec((None, s, pw_qk), lambda b, h: (b, 0, k0 + h)),
                  pl.BlockSpec((None, s, pw_v), lambda b, h: (b, 0, v0 + h)),
                  pl.BlockSpec((None, s, pw_v), lambda b, h: (b, 0, o0 + h)),
                  pl.BlockSpec((None, s, LANES), lambda b, h: (b, 0, h)),
                  pl.BlockSpec((1, pw_v), lambda b, h: (0, h))],
        out_specs=pl.BlockSpec((None, s, pw_v), lambda b, h: (b, 0, h)),
        scratch_shapes=[pltpu.VMEM((2, s, pw_v), jnp.float32),
                        pltpu.VMEM((2 * MLSTM_PAIR, pw_qk, 2 * dv), jnp.float32)],
        compiler_params=_compiler_params(("parallel", "parallel")),
        name="mlstm_scan",
    )(p, p, p, p, gates, head_gain.reshape(1, d))


RWKV_CHUNK = 64
RWKV_GROUP = 4
RWKV_LANES = RWKV_GROUP * B_HEAD
RWKV_PREP_ROWS = 256


def _f32_dot(a, b_bf16):
    hi = a.astype(jnp.bfloat16)
    lo = (a - hi.astype(jnp.float32)).astype(jnp.bfloat16)
    return (jnp.dot(hi, b_bf16, preferred_element_type=jnp.float32)
            + jnp.dot(lo, b_bf16, preferred_element_type=jnp.float32))


def _nt_dot(a, b):
    return lax.dot_general(a, b, (((1,), (1,)), ((), ())), preferred_element_type=jnp.float32)


def _tn_dot(a, b):
    return lax.dot_general(a, b, (((0,), (0,)), ((), ())), preferred_element_type=jnp.float32)


def _rwkv_scan_kernel(r_ref, k_ref, v_ref, wl_ref, al_ref, gl_ref, mu_r_ref, mu_k_ref, mu_v_ref, mu_wl_ref,
                      mu_al_ref, mu_gl_ref, w0_ref, wup_ref, a0_ref, aup_ref, kk_ref, ka_ref, rk_ref, gup_ref,
                      lng_ref, lnb_ref, o_ref,
                      r_s, v_s, kap_s, g_s, logw_s, kh_s, beta_s, y_s):
    f32, bf16 = jnp.float32, jnp.bfloat16
    seq = r_ref.shape[0]
    L, W, RT = RWKV_CHUNK, RWKV_LANES, RWKV_PREP_ROWS
    n_chunks = seq // L
    n_prep = seq // RT

    lane = lax.broadcasted_iota(jnp.int32, (1, W), 1)
    head_masks = [(lane // B_HEAD) == h for h in range(RWKV_GROUP)]
    ones_bd = ((lax.broadcasted_iota(jnp.int32, (W, W), 0) // B_HEAD)
               == (lax.broadcasted_iota(jnp.int32, (W, W), 1) // B_HEAD)).astype(bf16)

    def block_diag(a):
        zero = jnp.zeros_like(a)
        return jnp.concatenate([jnp.where(m, a, zero) for m in head_masks], axis=0)

    def seg_sum(a):
        return _f32_dot(a, ones_bd)

    def shifted(ref, mu_ref, i, rows):
        o = pl.multiple_of(i * RT, RT)
        x = ref[pl.ds(o, RT), :].astype(f32)
        before = ref[pl.ds(pl.multiple_of(jnp.maximum(o - 16, 0), 16), 16), :].astype(f32)[15:16, :]
        after = ref[pl.ds(pl.multiple_of(jnp.minimum(o + RT, seq - 16), 16), 16), :].astype(f32)[0:1, :]
        before = jnp.where(i > 0, before, 0.0)
        after = jnp.where(i < n_prep - 1, after, 0.0)
        prev = jnp.where(rows == 0, before, pltpu.roll(x, 1, 0))
        nxt = jnp.where(rows == RT - 1, after, pltpu.roll(x, RT - 1, 0))
        return x + mu_ref[...] * (0.5 * (prev + nxt) - x)

    def prep(i, carry):
        o = pl.multiple_of(i * RT, RT)
        rows_w = lax.broadcasted_iota(jnp.int32, (RT, W), 0)
        rows_n = lax.broadcasted_iota(jnp.int32, (RT, wl_ref.shape[1]), 0)
        r = shifted(r_ref, mu_r_ref, i, rows_w)
        k = shifted(k_ref, mu_k_ref, i, rows_w)
        v = shifted(v_ref, mu_v_ref, i, rows_w)
        wl = shifted(wl_ref, mu_wl_ref, i, rows_n)
        al = shifted(al_ref, mu_al_ref, i, rows_n)
        gl = shifted(gl_ref, mu_gl_ref, i, rows_n)
        kk0 = k * kk_ref[...]
        kap = kk0 / jnp.maximum(jnp.sqrt(seg_sum(kk0 * kk0)), 1e-12)
        r_s[pl.ds(o, RT), :] = r
        v_s[pl.ds(o, RT), :] = v
        kap_s[pl.ds(o, RT), :] = kap
        g_s[pl.ds(o, RT), :] = jnp.dot(jax.nn.sigmoid(gl).astype(bf16), gup_ref[...],
                                       preferred_element_type=f32)
        for d in range(2):
            wl_d = jnp.tanh(wl[:, d * B_DECAY_LORA:(d + 1) * B_DECAY_LORA]).astype(bf16)
            al_d = al[:, d * B_AAA_LORA:(d + 1) * B_AAA_LORA].astype(bf16)
            w_raw = w0_ref[d:d + 1, :] + jnp.dot(wl_d, wup_ref[d], preferred_element_type=f32)
            a = jax.nn.sigmoid(a0_ref[d:d + 1, :] + jnp.dot(al_d, aup_ref[d], preferred_element_type=f32))
            logw_s[d, pl.ds(o, RT), :] = -jnp.exp(-0.5) * jax.nn.sigmoid(w_raw)
            kh_s[d, pl.ds(o, RT), :] = k * (1.0 + (a - 1.0) * ka_ref[...])
            beta_s[d, pl.ds(o, RT), :] = kap * a
        return carry

    lax.fori_loop(0, n_prep, prep, 0)

    row = lax.broadcasted_iota(jnp.int32, (L, W), 0)
    col = lax.broadcasted_iota(jnp.int32, (L, W), 1) % B_HEAD
    eye_cat = (row == col).astype(f32)
    strict = (row > col, row < col)
    incl = (row >= col, row <= col)

    def cumsum_rows(x, reverse):
        sh = 1
        while sh < L:
            if reverse:
                x = x + jnp.where(row < L - sh, pltpu.roll(x, L - sh, 0), 0.0)
            else:
                x = x + jnp.where(row >= sh, pltpu.roll(x, sh, 0), 0.0)
            sh *= 2
        return x

    def chunk(start, state, d):
        rows = pl.ds(start, L)
        r, v, kap = r_s[rows, :], v_s[rows, :], kap_s[rows, :]
        logw, kh, beta = logw_s[d, rows, :], kh_s[d, rows, :], beta_s[d, rows, :]
        cum = cumsum_rows(logw, reverse=(d == 1))
        tot = cum[L - 1:L, :] if d == 0 else cum[0:1, :]
        p_in, p_inv, p_end = jnp.exp(cum), jnp.exp(-cum), jnp.exp(tot - cum)
        kap_t = kap * jnp.exp(cum - logw)
        r_t = r * p_in
        v_bd = block_diag(v.astype(bf16))
        g_all = _nt_dot(jnp.concatenate([kap_t, r_t], axis=0).astype(bf16),
                        jnp.concatenate([block_diag((beta * p_inv).astype(bf16)),
                                         block_diag((kh * p_inv).astype(bf16))], axis=0))
        x_pow = -jnp.where(strict[d], g_all[:L, :W], 0.0)
        a_ak = jnp.where(strict[d], g_all[:L, W:], 0.0)
        a_qb = jnp.where(incl[d], g_all[L:, :W], 0.0)
        a_qk = jnp.where(incl[d], g_all[L:, W:], 0.0)
        t_inv = eye_cat + x_pow
        x_pow = jnp.dot(x_pow.astype(bf16), block_diag(x_pow.astype(bf16)), preferred_element_type=f32)
        n_steps = L.bit_length() - 2
        for step in range(n_steps):
            if step < n_steps - 1:
                both = jnp.dot(jnp.concatenate([x_pow, t_inv], axis=0).astype(bf16),
                               block_diag(x_pow.astype(bf16)), preferred_element_type=f32)
                x_pow, t_inv = both[:L], t_inv + both[L:]
            else:
                t_inv = t_inv + jnp.dot(t_inv.astype(bf16), block_diag(x_pow.astype(bf16)),
                                        preferred_element_type=f32)
        av = jnp.dot(jnp.concatenate([a_ak, a_qk], axis=0).astype(bf16), v_bd, preferred_element_type=f32)
        t_b = t_inv.astype(bf16)
        w_mat = jnp.dot(t_b, block_diag(kap_t.astype(bf16)), preferred_element_type=f32)
        u_loc = jnp.dot(t_b, block_diag(av[:L].astype(bf16)), preferred_element_type=f32)
        ws = _nt_dot(jnp.concatenate([w_mat, r_t], axis=0).astype(bf16), block_diag(state.astype(bf16)))
        u = ws[:L] + u_loc
        y = ws[L:] + av[L:] - jnp.dot(a_qb.astype(bf16), block_diag(u.astype(bf16)), preferred_element_type=f32)
        full = _tn_dot(jnp.concatenate([v, -u], axis=0).astype(bf16),
                       jnp.concatenate([kh * p_end, beta * p_end], axis=0).astype(bf16))
        new_state = state * jnp.exp(tot)
        for h, m in enumerate(head_masks):
            new_state = new_state + jnp.where(m, full[h * B_HEAD:(h + 1) * B_HEAD, :], 0.0)
        return y, new_state

    def body(c, states):
        s_f, s_b = states
        start_f = pl.multiple_of(c * L, L)
        start_b = pl.multiple_of((n_chunks - 1 - c) * L, L)
        y_f, s_f = chunk(start_f, s_f, 0)
        y_b, s_b = chunk(start_b, s_b, 1)
        y_s[0, pl.ds(start_f, L), :] = y_f
        y_s[1, pl.ds(start_b, L), :] = y_b
        return s_f, s_b

    zero_state = jnp.zeros((B_HEAD, W), f32)
    lax.fori_loop(0, n_chunks, body, (zero_state, zero_state))

    def finish(i, carry):
        rows = pl.ds(pl.multiple_of(i * RT, RT), RT)
        y = y_s[0, rows, :] + y_s[1, rows, :]
        mean = seg_sum(y) * (1.0 / B_HEAD)
        yc = y - mean
        var = seg_sum(yc * yc) * (1.0 / B_HEAD)
        yn = yc * lax.rsqrt(var + B_GN_EPS) * lng_ref[...] + lnb_ref[...]
        bonus = seg_sum(r_s[rows, :] * (kh_s[0, rows, :] + kh_s[1, rows, :]) * rk_ref[...]) * v_s[rows, :]
        o_ref[rows, :] = ((yn + bonus) * g_s[rows, :]).astype(o_ref.dtype)
        return carry

    lax.fori_loop(0, n_prep, finish, 0)


def _rwkv7_mixer(x, w_in, mu, w0, w_up, a0, a_up, k_k, k_a, r_k, g_up, lnx_g, lnx_b):
    bsz, s, d = x.shape
    n_proj = w_in.shape[1]
    W = RWKV_LANES
    assert d % W == 0 and s % RWKV_PREP_ROWS == 0 and B_GATE_LORA == LANES
    assert 2 * B_DECAY_LORA == LANES and 2 * B_AAA_LORA == LANES
    p = _project(x.reshape(bsz * s, d), w_in, jnp.zeros((n_proj,), jnp.float32), jnp.bfloat16)
    p = p.reshape(bsz, s, n_proj)
    n_groups = d // W
    gw = d // W
    lora0 = 3 * d // LANES

    def seq_w(off):
        return pl.BlockSpec((None, s, W), lambda b, g, off=off: (b, 0, off + g))

    def seq_n(idx):
        return pl.BlockSpec((None, s, LANES), lambda b, g, idx=idx: (b, 0, idx))

    def vec_w(off):
        return pl.BlockSpec((1, W), lambda b, g, off=off: (0, off + g))

    def vec_n(idx):
        return pl.BlockSpec((1, LANES), lambda b, g, idx=idx: (0, idx))

    par_w = pl.BlockSpec((1, W), lambda b, g: (0, g))
    two_w = pl.BlockSpec((2, W), lambda b, g: (0, g))
    up_w = pl.BlockSpec((2, B_DECAY_LORA, W), lambda b, g: (0, 0, g))
    mu2 = mu.reshape(1, n_proj)
    row = lambda a: a.reshape(1, d)
    scratch_w = pltpu.VMEM((s, W), jnp.float32)
    scratch_2w = pltpu.VMEM((2, s, W), jnp.float32)
    return pl.pallas_call(
        _rwkv_scan_kernel,
        out_shape=jax.ShapeDtypeStruct((bsz, s, d), jnp.bfloat16),
        grid=(bsz, n_groups),
        in_specs=[seq_w(0), seq_w(gw), seq_w(2 * gw), seq_n(lora0), seq_n(lora0 + 1), seq_n(lora0 + 2),
                  vec_w(0), vec_w(gw), vec_w(2 * gw), vec_n(lora0), vec_n(lora0 + 1), vec_n(lora0 + 2),
                  two_w, up_w, two_w, up_w, par_w, par_w, par_w,
                  pl.BlockSpec((B_GATE_LORA, W), lambda b, g: (0, g)), par_w, par_w],
        out_specs=pl.BlockSpec((None, s, W), lambda b, g: (b, 0, g)),
        scratch_shapes=[scratch_w, scratch_w, scratch_w, scratch_w, scratch_2w, scratch_2w, scratch_2w, scratch_2w],
        compiler_params=_compiler_params(("parallel", "parallel")),
        name="rwkv7_scan",
    )(p, p, p, p, p, p, mu2, mu2, mu2, mu2, mu2, mu2,
      w0, w_up.astype(jnp.bfloat16), a0, a_up.astype(jnp.bfloat16), row(k_k), row(k_a), row(r_k),
      g_up.astype(jnp.bfloat16), row(lnx_g), row(lnx_b))


def _axial_rope_tables(s):
    rows = s // GRID_W
    row = jnp.repeat(jnp.arange(rows, dtype=jnp.float32), GRID_W)
    col = jnp.tile(jnp.arange(GRID_W, dtype=jnp.float32), rows)
    n_freq = C_HEAD_DIM // 4
    inv_freq = ROPE_THETA ** (-jnp.arange(n_freq, dtype=jnp.float32) / n_freq)
    ang = jnp.concatenate([row[:, None] * inv_freq, col[:, None] * inv_freq], axis=-1)
    return jnp.cos(ang), jnp.sin(ang)


ATTN_Q_TILE = 256


def _rms_rope(x, gain, cos_f, sin_f):
    xn = x * lax.rsqrt(jnp.mean(x * x, axis=-1, keepdims=True) + RMS_EPS) * gain
    return xn * cos_f + pltpu.roll(xn, C_HEAD_DIM // 2, 1) * sin_f


def _attn_kernel(q_ref, k_ref, v_ref, cq_ref, sq_ref, ck_ref, sk_ref, qg_ref, kg_ref, o_ref, kr_s):
    f32, bf16 = jnp.float32, jnp.bfloat16

    @pl.when(pl.program_id(2) == 0)
    def _():
        kr_s[...] = _rms_rope(k_ref[...].astype(f32), kg_ref[...], ck_ref[...], sk_ref[...]).astype(bf16)

    group = q_ref.shape[1] // C_HEAD_DIM
    for g in range(group):
        cols = slice(g * C_HEAD_DIM, (g + 1) * C_HEAD_DIM)
        q = _rms_rope(q_ref[:, cols].astype(f32), qg_ref[...], cq_ref[...], sq_ref[...]) * C_HEAD_DIM ** -0.5
        scores = _nt_dot(q.astype(bf16), kr_s[...])
        p = jnp.exp(scores - jnp.max(scores, axis=-1, keepdims=True))
        denom = jnp.sum(p, axis=-1, keepdims=True)
        o = jnp.dot(p.astype(bf16), v_ref[...], preferred_element_type=f32)
        o_ref[:, cols] = (o / denom).astype(o_ref.dtype)


def _axial_gqa_mixer(x, w_in, q_gain, k_gain):
    bsz, s, d = x.shape
    dh = C_HEAD_DIM
    qh = d // dh
    group = qh // C_KV_HEADS
    n_proj = w_in.shape[1]
    half = jnp.concatenate([jnp.arange(0, dh, 2), jnp.arange(1, dh, 2)])
    rot_heads = qh + C_KV_HEADS
    perm = jnp.concatenate([(jnp.arange(rot_heads)[:, None] * dh + half[None, :]).reshape(-1),
                            jnp.arange(rot_heads * dh, n_proj)])
    p = _project(x.reshape(bsz * s, d), w_in[:, perm], jnp.zeros((n_proj,), jnp.float32), jnp.bfloat16)
    p = p.reshape(bsz, s, n_proj)
    cos, sin = _axial_rope_tables(s)
    cos_f = jnp.concatenate([cos, cos], axis=-1)
    sin_f = jnp.concatenate([-sin, sin], axis=-1)
    tq = min(ATTN_Q_TILE, s)
    gw = group * dh
    q_rows = pl.BlockSpec((tq, dh), lambda b, h, i: (i, 0))
    k_rows = pl.BlockSpec((s, dh), lambda b, h, i: (0, 0))
    vec = pl.BlockSpec((1, dh), lambda b, h, i: (0, 0))
    return pl.pallas_call(
        _attn_kernel,
        out_shape=jax.ShapeDtypeStruct((bsz, s, d), jnp.bfloat16),
        grid=(bsz, C_KV_HEADS, s // tq),
        in_specs=[pl.BlockSpec((None, tq, gw), lambda b, h, i: (b, i, h)),
                  pl.BlockSpec((None, s, dh), lambda b, h, i: (b, 0, qh + h)),
                  pl.BlockSpec((None, s, dh), lambda b, h, i: (b, 0, qh + C_KV_HEADS + h)),
                  q_rows, q_rows, k_rows, k_rows, vec, vec],
        out_specs=pl.BlockSpec((None, tq, gw), lambda b, h, i: (b, i, h)),
        scratch_shapes=[pltpu.VMEM((s, dh), jnp.bfloat16)],
        compiler_params=_compiler_params(("parallel", "parallel", "arbitrary")),
        name="axial_attention",
    )(p, p, p, cos_f, sin_f, cos_f, sin_f, q_gain[half].reshape(1, dh), k_gain[half].reshape(1, dh))


def kernel(x, a_w_in, a_b_in, a_head_gain, a_w_out, b_w_in, b_mu, b_w0, b_w_up, b_a0, b_a_up, b_k_k, b_k_a, b_r_k, b_g_up, b_lnx_g, b_lnx_b, b_w_out, c_w_in, c_q_gain, c_k_gain, c_w_out, ln1_g, ln1_b, moe_w_router, moe_b_router, moe_w_gu, moe_b_gu, moe_w_dn, moe_b_dn, ln2_g, ln2_b):
    bsz, s, d = x.shape
    depth = ln1_g.shape[0]
    xt = x.reshape(bsz * s, d)
    for i in range(depth):
        kind = i % N_MIXERS
        j = i // N_MIXERS
        x3 = xt.reshape(bsz, s, d)
        if kind == 0:
            h = _mlstm_mixer(x3, a_w_in[j], a_b_in[j], a_head_gain[j])
            w_out = a_w_out[j]
        elif kind == 1:
            h = _rwkv7_mixer(x3, b_w_in[j], b_mu[j], b_w0[j], b_w_up[j], b_a0[j], b_a_up[j],
                             b_k_k[j], b_k_a[j], b_r_k[j], b_g_up[j], b_lnx_g[j], b_lnx_b[j])
            w_out = b_w_out[j]
        else:
            h = _axial_gqa_mixer(x3, c_w_in[j], c_q_gain[j], c_k_gain[j])
            w_out = c_w_out[j]
        xt = _outproj_ln(h.reshape(bsz * s, d), w_out, xt, ln1_g[i], ln1_b[i])
        ffn = _moe_ffn(xt, moe_w_router[i], moe_b_router[i], moe_w_gu[i], moe_b_gu[i], moe_w_dn[i], moe_b_dn[i])
        xt = _residual_ln(ffn, xt, ln2_g[i], ln2_b[i])
    return xt.reshape(bsz, s, d)
```

```python
import functools

import jax
import jax.numpy as jnp
from jax import lax
from jax.experimental import pallas as pl
from jax.experimental.pallas import tpu as pltpu

DEPTH = 4
N_MIXERS = 3
GRID_W = 64
DEEPNORM_ALPHA = (2 * DEPTH) ** 0.25
LN_EPS = 1e-5
RMS_EPS = 1e-6

A_HEADS = 8
A_DQK = 64
A_CHUNK = 64
A_M_INIT = -1e30

B_HEAD = 64
B_DECAY_LORA = 64
B_AAA_LORA = 64
B_GATE_LORA = 128
B_GN_EPS = 64e-5

C_HEAD_DIM = 128
C_KV_HEADS = 2
C_Q_BLOCK = 128
ROPE_THETA = 10000.0

N_EXPERTS = 32
TOP_K = 4
SWIGLU_LIMIT = 7.0
SWIGLU_ALPHA = 1.702

LANES = 128
VMEM_LIMIT_BYTES = 56 * 1024 * 1024

ROW_TILE = 512
MOE_ROW_TILE = 512


def _compiler_params(semantics):
    return pltpu.CompilerParams(dimension_semantics=semantics, vmem_limit_bytes=VMEM_LIMIT_BYTES)


def _row_tile(m):
    t = min(ROW_TILE, m)
    assert m % t == 0, (m, t)
    return t


def _col_chunk(n):
    return max(c for c in range(LANES, 5 * LANES + 1, LANES) if n % c == 0)


def _proj_kernel(x_ref, w_ref, b_ref, *o_refs):
    xb = x_ref[...].astype(jnp.bfloat16)
    col = 0
    for o_ref in o_refs:
        n = o_ref.shape[1]
        step = _col_chunk(n)
        for j in range(0, n, step):
            acc = jnp.dot(xb, w_ref[:, col + j:col + j + step], preferred_element_type=jnp.float32)
            o_ref[:, j:j + step] = (acc + b_ref[:, col + j:col + j + step]).astype(o_ref.dtype)
        col += n


def _project(x, w, b, out_dtype, f32_tail=0):
    m, k = x.shape
    n = w.shape[1]
    widths = [n - f32_tail, f32_tail] if f32_tail else [n]
    dtypes = [out_dtype, jnp.float32]
    assert all(c % LANES == 0 for c in widths)
    tm = _row_tile(m)
    outs = pl.pallas_call(
        _proj_kernel,
        out_shape=[jax.ShapeDtypeStruct((m, c), dt) for c, dt in zip(widths, dtypes)],
        grid=(m // tm,),
        in_specs=[pl.BlockSpec((tm, k), lambda i: (i, 0)),
                  pl.BlockSpec((k, n), lambda i: (0, 0)),
                  pl.BlockSpec((1, n), lambda i: (0, 0))],
        out_specs=[pl.BlockSpec((tm, c), lambda i: (i, 0)) for c in widths],
        compiler_params=_compiler_params(("parallel",)),
        name="project",
    )(x, w.astype(jnp.bfloat16), b.reshape(1, n).astype(jnp.float32))
    return tuple(outs) if f32_tail else outs[0]


def _pad_cols(w, b, n_pad):
    k, n = w.shape
    if b is None:
        b = jnp.zeros((n,), jnp.float32)
    return jnp.pad(w, ((0, 0), (0, n_pad - n))), jnp.pad(b, (0, n_pad - n))


def _layer_norm_rows(z, g, b):
    mu = jnp.mean(z, axis=-1, keepdims=True)
    zc = z - mu
    var = jnp.mean(zc * zc, axis=-1, keepdims=True)
    return zc * lax.rsqrt(var + LN_EPS) * g + b


def _outproj_ln_kernel(h_ref, w_ref, x_ref, g_ref, b_ref, o_ref, ob_ref):
    mix = jnp.dot(h_ref[...].astype(jnp.bfloat16), w_ref[...], preferred_element_type=jnp.float32)
    z = DEEPNORM_ALPHA * x_ref[...] + mix
    y = _layer_norm_rows(z, g_ref[...], b_ref[...])
    o_ref[...] = y
    ob_ref[...] = y.astype(ob_ref.dtype)


def _outproj_ln(h, w_out, x, g, b):
    m, d = x.shape
    tm = _row_tile(m)
    row = pl.BlockSpec((tm, d), lambda i: (i, 0))
    vec = pl.BlockSpec((1, d), lambda i: (0, 0))
    return pl.pallas_call(
        _outproj_ln_kernel,
        out_shape=[jax.ShapeDtypeStruct((m, d), jnp.float32), jax.ShapeDtypeStruct((m, d), jnp.bfloat16)],
        grid=(m // tm,),
        in_specs=[row, pl.BlockSpec((d, d), lambda i: (0, 0)), row, vec, vec],
        out_specs=[row, row],
        compiler_params=_compiler_params(("parallel",)),
        name="outproj_ln",
    )(h, w_out.astype(jnp.bfloat16), x, g.reshape(1, d), b.reshape(1, d))


ROUTE_IDX_LANE = 0
ROUTE_RANK_LANE = TOP_K
ROUTE_GATE_LANE = 2 * TOP_K


def _combine_ln_kernel(y_ref, r_ref, x_ref, g_ref, b_ref, o_ref):
    y = jnp.zeros(x_ref.shape, jnp.float32)
    for k in range(TOP_K):
        gate = r_ref[:, ROUTE_GATE_LANE + k:ROUTE_GATE_LANE + k + 1]
        y = y + gate * y_ref[k].astype(jnp.float32)
    z = DEEPNORM_ALPHA * x_ref[...] + y
    o_ref[...] = _layer_norm_rows(z, g_ref[...], b_ref[...])


def _combine_ln(y_rows, route, x, g, b):
    m, d = x.shape
    tm = _row_tile(m)
    row = pl.BlockSpec((tm, d), lambda i: (i, 0))
    vec = pl.BlockSpec((1, d), lambda i: (0, 0))
    return pl.pallas_call(
        _combine_ln_kernel,
        out_shape=jax.ShapeDtypeStruct((m, d), jnp.float32),
        grid=(m // tm,),
        in_specs=[pl.BlockSpec((TOP_K, tm, d), lambda i: (0, i, 0)),
                  pl.BlockSpec((tm, LANES), lambda i: (i, 0)), row, vec, vec],
        out_specs=row,
        compiler_params=_compiler_params(("parallel",)),
        name="combine_ln",
    )(y_rows, route, x, g.reshape(1, d), b.reshape(1, d))


ROUTER_PAD_BIAS = -1e30


def _router_kernel(x_ref, w_ref, b_ref, o_ref, cnt_ref, base_s):
    f32 = jnp.float32

    @pl.when(pl.program_id(0) == 0)
    def _():
        base_s[...] = jnp.zeros_like(base_s)

    t = x_ref.shape[0]
    logits = jnp.dot(x_ref[...], w_ref[...], preferred_element_type=f32,
                     precision=lax.Precision.HIGHEST) + b_ref[...]
    lane = lax.broadcasted_iota(jnp.int32, (t, LANES), 1)
    vals = logits
    tops, idxs, sels = [], [], []
    for _ in range(TOP_K):
        top = jnp.max(vals, axis=-1, keepdims=True)
        idx = jnp.min(jnp.where(vals == top, lane, LANES), axis=-1, keepdims=True)
        sel = lane == idx
        vals = jnp.where(sel, -jnp.inf, vals)
        tops.append(top)
        idxs.append(idx)
        sels.append(sel)
    exps = [jnp.exp(top - tops[0]) for top in tops]
    total = exps[0]
    for e in exps[1:]:
        total = total + e
    chosen = jnp.zeros((t, LANES), f32)
    for sel in sels:
        chosen = chosen + sel.astype(f32)
    earlier = (lax.broadcasted_iota(jnp.int32, (t, t), 0) > lax.broadcasted_iota(jnp.int32, (t, t), 1))
    prefix = jnp.dot(earlier.astype(jnp.bfloat16), chosen.astype(jnp.bfloat16),
                     preferred_element_type=f32) + base_s[...]
    packed = jnp.zeros((t, LANES), f32)
    for k in range(TOP_K):
        rank = jnp.sum(jnp.where(sels[k], prefix, 0.0), axis=-1, keepdims=True)
        packed = jnp.where(lane == ROUTE_IDX_LANE + k, idxs[k].astype(f32), packed)
        packed = jnp.where(lane == ROUTE_RANK_LANE + k, rank, packed)
        packed = jnp.where(lane == ROUTE_GATE_LANE + k, exps[k] / total, packed)
    o_ref[...] = packed
    base_s[...] = base_s[...] + jnp.sum(chosen, axis=0, keepdims=True)
    cnt_ref[...] = base_s[...]


def _route(xt, w_router, b_router):
    m, d = xt.shape
    tm = _row_tile(m)
    n_e = w_router.shape[1]
    w = jnp.pad(w_router, ((0, 0), (0, LANES - n_e)))
    b = jnp.pad(b_router, (0, LANES - n_e), constant_values=ROUTER_PAD_BIAS)
    return pl.pallas_call(
        _router_kernel,
        out_shape=[jax.ShapeDtypeStruct((m, LANES), jnp.float32), jax.ShapeDtypeStruct((1, LANES), jnp.float32)],
        grid=(m // tm,),
        in_specs=[pl.BlockSpec((tm, d), lambda i: (i, 0)),
                  pl.BlockSpec((d, LANES), lambda i: (0, 0)),
                  pl.BlockSpec((1, LANES), lambda i: (0, 0))],
        out_specs=[pl.BlockSpec((tm, LANES), lambda i: (i, 0)), pl.BlockSpec((1, LANES), lambda i: (0, 0))],
        scratch_shapes=[pltpu.VMEM((1, LANES), jnp.float32)],
        compiler_params=_compiler_params(("arbitrary",)),
        name="router",
    )(xt, w, b.reshape(1, LANES))


GLU_BLOCK = 2 * LANES


def _regroup_kernel(w_ref, p_ref, o_ref):
    w = w_ref[0].astype(jnp.bfloat16)
    for c in range(0, w.shape[1], GLU_BLOCK):
        o_ref[0, :, c:c + GLU_BLOCK] = jnp.dot(w[:, c:c + GLU_BLOCK], p_ref[...],
                                               preferred_element_type=jnp.float32).astype(o_ref.dtype)


def _regroup_glu_columns(w_gu):
    n_e, d, f2 = w_gu.shape
    src = jnp.arange(GLU_BLOCK)
    perm = (src[:, None] == (2 * (src % LANES) + src // LANES)[None, :]).astype(jnp.bfloat16)
    tk = _row_tile(d)
    return pl.pallas_call(
        _regroup_kernel,
        out_shape=jax.ShapeDtypeStruct((n_e, d, f2), jnp.bfloat16),
        grid=(n_e, d // tk),
        in_specs=[pl.BlockSpec((1, tk, f2), lambda e, i: (e, i, 0)),
                  pl.BlockSpec((GLU_BLOCK, GLU_BLOCK), lambda e, i: (0, 0))],
        out_specs=pl.BlockSpec((1, tk, f2), lambda e, i: (e, i, 0)),
        compiler_params=_compiler_params(("parallel", "parallel")),
        name="regroup_glu",
    )(w_gu, perm)


def _regroup_glu_bias(b_gu):
    n_e, f2 = b_gu.shape
    return b_gu.reshape(n_e, f2 // GLU_BLOCK, LANES, 2).transpose(0, 1, 3, 2).reshape(n_e, 1, f2)


def _expert_kernel(be_ref, nb_ref, x_ref, wgu_ref, bgu_ref, wd_ref, bd_ref, o_ref):
    @pl.when(pl.program_id(0) < nb_ref[0])
    def _():
        xb = x_ref[...]
        acts = []
        for c in range(0, wgu_ref.shape[2], GLU_BLOCK):
            h = (jnp.dot(xb, wgu_ref[0, :, c:c + GLU_BLOCK], preferred_element_type=jnp.float32)
                 + bgu_ref[0, :, c:c + GLU_BLOCK])
            hg = jnp.minimum(h[:, :LANES], SWIGLU_LIMIT)
            hl = jnp.clip(h[:, LANES:], -SWIGLU_LIMIT, SWIGLU_LIMIT)
            acts.append((hg * jax.nn.sigmoid(SWIGLU_ALPHA * hg) * (hl + 1.0)).astype(jnp.bfloat16))
        act = jnp.concatenate(acts, axis=1)
        y = jnp.dot(act, wd_ref[0], preferred_element_type=jnp.float32) + bd_ref[0]
        o_ref[...] = y.astype(o_ref.dtype)


def _expert_ffn(x_sorted, block_expert, n_used, w_gu, b_gu, w_dn, b_dn):
    cap, d = x_sorted.shape
    f2 = w_gu.shape[2]
    tm = MOE_ROW_TILE
    n_blocks = cap // tm

    def blk(i, be, nb):
        return (jnp.minimum(i, nb[0] - 1), 0)

    def wsel(i, be, nb):
        return (be[i], 0, 0)

    grid_spec = pltpu.PrefetchScalarGridSpec(
        num_scalar_prefetch=2,
        grid=(n_blocks,),
        in_specs=[pl.BlockSpec((tm, d), blk),
                  pl.BlockSpec((1, d, f2), wsel),
                  pl.BlockSpec((1, 1, f2), wsel),
                  pl.BlockSpec((1, f2 // 2, d), wsel),
                  pl.BlockSpec((1, 1, d), wsel)],
        out_specs=pl.BlockSpec((tm, d), blk),
    )
    return pl.pallas_call(
        _expert_kernel,
        out_shape=jax.ShapeDtypeStruct((cap, d), jnp.bfloat16),
        grid_spec=grid_spec,
        compiler_params=_compiler_params(("arbitrary",)),
        name="expert_ffn",
    )(block_expert, n_used, x_sorted, w_gu, b_gu, w_dn, b_dn)


def _moe_ffn(x, x_bf16, w_router, b_router, w_gu, b_gu, w_dn, b_dn):
    n, d = x.shape
    tm = MOE_ROW_TILE
    n_e = w_router.shape[1]
    route, counts = _route(x, w_router, b_router)
    top_idx = route[:, ROUTE_IDX_LANE:ROUTE_IDX_LANE + TOP_K].astype(jnp.int32)
    rank = route[:, ROUTE_RANK_LANE:ROUTE_RANK_LANE + TOP_K].astype(jnp.int32)
    counts = counts[0, :n_e].astype(jnp.int32)
    padded = (counts + tm - 1) // tm * tm
    pad_end = jnp.cumsum(padded)
    pad_start = pad_end - padded
    cap = n * TOP_K + n_e * tm
    n_blocks = cap // tm
    slot = rank + jnp.sum(jnp.where(top_idx[..., None] == jnp.arange(n_e), pad_start, 0), axis=-1)
    slot_kmajor = slot.T.reshape(-1)
    tok_of_slot = jnp.zeros((cap,), jnp.int32).at[slot_kmajor].set(jnp.tile(jnp.arange(n, dtype=jnp.int32), TOP_K))
    block_expert = jnp.minimum(
        jnp.searchsorted(pad_end, jnp.arange(n_blocks) * tm, side='right'), n_e - 1).astype(jnp.int32)
    n_used = (pad_end[-1] // tm).astype(jnp.int32).reshape(1)

    x_sorted = jnp.take(x_bf16, tok_of_slot, axis=0)
    yb = _expert_ffn(x_sorted, block_expert, n_used, _regroup_glu_columns(w_gu), _regroup_glu_bias(b_gu),
                     w_dn.astype(jnp.bfloat16), b_dn[:, None, :])
    return jnp.take(yb, slot_kmajor, axis=0).reshape(TOP_K, n, d), route


MLSTM_CHUNK = 128
MLSTM_PAIR = 2
MLSTM_NEG = -1e30


def _cumsum_rows(x, reverse):
    n = x.shape[0]
    row = lax.broadcasted_iota(jnp.int32, x.shape, 0)
    sh = 1
    while sh < n:
        if reverse:
            x = x + jnp.where(row < n - sh, pltpu.roll(x, n - sh, 0), 0.0)
        else:
            x = x + jnp.where(row >= sh, pltpu.roll(x, sh, 0), 0.0)
        sh *= 2
    return x


def _mlstm_kernel(q_ref, k_ref, v_ref, og_ref, g_ref, gain_ref, o_ref, h_s, c_s):
    f32, bf16 = jnp.float32, jnp.bfloat16
    seq = q_ref.shape[0]
    L = MLSTM_CHUNK
    n_chunks = seq // L
    dv = v_ref.shape[1] // MLSTM_PAIR
    dqk = q_ref.shape[1] // MLSTM_PAIR

    lane_q = lax.broadcasted_iota(jnp.int32, (1, q_ref.shape[1]), 1) // dqk
    lane_g = lax.broadcasted_iota(jnp.int32, (1, LANES), 1)
    is_forget_lane = (lane_g // MLSTM_PAIR) % 2 == 1
    row = lax.broadcasted_iota(jnp.int32, (L, L), 0)
    col = lax.broadcasted_iota(jnp.int32, (L, L), 1)
    causal = (col <= row, col >= row)
    ones_v = jnp.ones((L, dv), bf16)

    c_s[...] = jnp.zeros_like(c_s)

    def direction(start, d, m_prev):
        rows = pl.ds(start, L)
        gates = g_ref[rows, :]
        log_f = jnp.minimum(gates, 0.0) - jnp.log(1.0 + jnp.exp(-jnp.abs(gates)))
        cum = _cumsum_rows(log_f, reverse=(d == 1))
        z = jnp.where(is_forget_lane, cum, gates)
        z_t = z.T
        q_all = q_ref[rows, :]
        k_all = k_ref[rows, :]
        m_out = []
        for j in range(MLSTM_PAIR):
            lane_i = 2 * MLSTM_PAIR * d + j
            lane_f = lane_i + MLSTM_PAIR
            b_col, i_col = z[:, lane_f:lane_f + 1], z[:, lane_i:lane_i + 1]
            b_row, i_row = z_t[lane_f:lane_f + 1, :], z_t[lane_i:lane_i + 1, :]
            m_st = m_prev[j]
            dmat = jnp.where(causal[d], b_col + (i_row - b_row), MLSTM_NEG)
            inter = b_col + m_st
            m_t = jnp.maximum(inter, jnp.max(dmat, axis=-1, keepdims=True))
            qz = jnp.where(lane_q == j, q_all, jnp.zeros_like(q_all))
            kz = jnp.where(lane_q == j, k_all, jnp.zeros_like(k_all))
            v_aug = jnp.concatenate([v_ref[rows, j * dv:(j + 1) * dv], ones_v], axis=1)
            qk = _nt_dot(qz, kz) * (A_DQK ** -0.5) * jnp.exp(dmat - m_t)
            state = c_s[2 * d + j]
            nd = (jnp.dot(qk.astype(bf16), v_aug, preferred_element_type=f32)
                  + jnp.exp(inter - m_t) * (A_DQK ** -0.5)
                  * jnp.dot(qz, state.astype(bf16), preferred_element_type=f32))
            h = nd[:, :dv] / jnp.maximum(jnp.abs(nd[:, dv:]), jnp.exp(-m_t))
            h_s[d, rows, j * dv:(j + 1) * dv] = h
            b_last = b_col[L - 1:L, :] if d == 0 else b_col[0:1, :]
            g_s = b_last - b_col + i_col
            m_new = jnp.maximum(b_last + m_st, jnp.max(g_s, axis=0, keepdims=True))
            w_s = jnp.exp(g_s - m_new)
            c_s[2 * d + j] = (jnp.exp(b_last + m_st - m_new) * state
                              + _tn_dot((w_s * kz.astype(f32)).astype(bf16), v_aug))
            m_out.append(m_new)
        return m_out

    def body(c, m_all):
        m_f = direction(pl.multiple_of(c * L, L), 0, m_all[:MLSTM_PAIR])
        m_b = direction(pl.multiple_of((n_chunks - 1 - c) * L, L), 1, m_all[MLSTM_PAIR:])
        return tuple(m_f) + tuple(m_b)

    m_init = tuple(jnp.full((1, 1), A_M_INIT, f32) for _ in range(2 * MLSTM_PAIR))
    lax.fori_loop(0, n_chunks, body, m_init)

    def finish(c, carry):
        rows = pl.ds(pl.multiple_of(c * L, L), L)
        for j in range(MLSTM_PAIR):
            cols = slice(j * dv, (j + 1) * dv)
            h = h_s[0, rows, cols] + h_s[1, rows, cols]
            hn = h * lax.rsqrt(jnp.mean(h * h, axis=-1, keepdims=True) + RMS_EPS) * gain_ref[:, cols]
            o_ref[rows, cols] = (hn * jax.nn.sigmoid(og_ref[rows, cols].astype(f32))).astype(o_ref.dtype)
        return carry

    lax.fori_loop(0, n_chunks, finish, 0)


def _mlstm_mixer(x, w_in, b_in, head_gain):
    bsz, s, d = x.shape
    a_qk = A_HEADS * A_DQK
    dv = d // A_HEADS
    n_main = 2 * a_qk + 2 * d
    n_pairs = A_HEADS // MLSTM_PAIR
    pw_qk = MLSTM_PAIR * A_DQK
    pw_v = MLSTM_PAIR * dv
    assert pw_qk == LANES and s % MLSTM_CHUNK == 0
    gate_cols = jnp.asarray([[n_main + t * A_HEADS + MLSTM_PAIR * hp + j for t in range(4) for j in range(MLSTM_PAIR)]
                             for hp in range(n_pairs)])
    n_gate = gate_cols.shape[1]
    w_tail = jnp.pad(w_in[:, gate_cols], ((0, 0), (0, 0), (0, LANES - n_gate))).reshape(d, n_pairs * LANES)
    b_tail = jnp.pad(b_in[gate_cols], ((0, 0), (0, LANES - n_gate))).reshape(n_pairs * LANES)
    w = jnp.concatenate([w_in[:, :n_main], w_tail], axis=1)
    b = jnp.concatenate([b_in[:n_main], b_tail])
    p, gates = _project(x.reshape(bsz * s, d), w, b, jnp.bfloat16, f32_tail=n_pairs * LANES)
    p = p.reshape(bsz, s, n_main)
    gates = gates.reshape(bsz, s, n_pairs * LANES)
    k0 = a_qk // pw_qk
    v0 = 2 * a_qk // pw_v
    o0 = (2 * a_qk + d) // pw_v
    return pl.pallas_call(
        _mlstm_kernel,
        out_shape=jax.ShapeDtypeStruct((bsz, s, d), jnp.bfloat16),
        grid=(bsz, n_pairs),
        in_specs=[pl.BlockSpec((None, s, pw_qk), lambda b, h: (b, 0, h)),
                  pl.BlockSpec((None, s, pw_qk), lambda b, h: (b, 0, k0 + h)),
                  pl.BlockSpec((None, s, pw_v), lambda b, h: (b, 0, v0 + h)),
                  pl.BlockSpec((None, s, pw_v), lambda b, h: (b, 0, o0 + h)),
                  pl.BlockSpec((None, s, LANES), lambda b, h: (b, 0, h)),
                  pl.BlockSpec((1, pw_v), lambda b, h: (0, h))],
        out_specs=pl.BlockSpec((None, s, pw_v), lambda b, h: (b, 0, h)),
        scratch_shapes=[pltpu.VMEM((2, s, pw_v), jnp.float32),
                        pltpu.VMEM((2 * MLSTM_PAIR, pw_qk, 2 * dv), jnp.float32)],
        compiler_params=_compiler_params(("parallel", "parallel")),
        name="mlstm_scan",
    )(p, p, p, p, gates, head_gain.reshape(1, d))


RWKV_CHUNK = 64
RWKV_GROUP = 4
RWKV_LANES = RWKV_GROUP * B_HEAD
RWKV_PREP_ROWS = 256


def _f32_dot(a, b_bf16):
    hi = a.astype(jnp.bfloat16)
    lo = (a - hi.astype(jnp.float32)).astype(jnp.bfloat16)
    return (jnp.dot(hi, b_bf16, preferred_element_type=jnp.float32)
            + jnp.dot(lo, b_bf16, preferred_element_type=jnp.float32))


def _nt_dot(a, b):
    return lax.dot_general(a, b, (((1,), (1,)), ((), ())), preferred_element_type=jnp.float32)


def _tn_dot(a, b):
    return lax.dot_general(a, b, (((0,), (0,)), ((), ())), preferred_element_type=jnp.float32)


def _rwkv_scan_kernel(r_ref, k_ref, v_ref, wl_ref, al_ref, gl_ref, mu_r_ref, mu_k_ref, mu_v_ref, mu_wl_ref,
                      mu_al_ref, mu_gl_ref, w0_ref, wup_ref, a0_ref, aup_ref, kk_ref, ka_ref, rk_ref, gup_ref,
                      lng_ref, lnb_ref, o_ref,
                      r_s, v_s, kap_s, g_s, logw_s, kh_s, beta_s, y_s):
    f32, bf16 = jnp.float32, jnp.bfloat16
    seq = r_ref.shape[0]
    L, W, RT = RWKV_CHUNK, RWKV_LANES, RWKV_PREP_ROWS
    n_chunks = seq // L
    n_prep = seq // RT

    lane = lax.broadcasted_iota(jnp.int32, (1, W), 1)
    head_masks = [(lane // B_HEAD) == h for h in range(RWKV_GROUP)]
    ones_bd = ((lax.broadcasted_iota(jnp.int32, (W, W), 0) // B_HEAD)
               == (lax.broadcasted_iota(jnp.int32, (W, W), 1) // B_HEAD)).astype(bf16)

    def block_diag(a):
        zero = jnp.zeros_like(a)
        return jnp.concatenate([jnp.where(m, a, zero) for m in head_masks], axis=0)

    def seg_sum(a):
        return _f32_dot(a, ones_bd)

    def shifted(ref, mu_ref, i, rows):
        o = pl.multiple_of(i * RT, RT)
        x = ref[pl.ds(o, RT), :].astype(f32)
        before = ref[pl.ds(pl.multiple_of(jnp.maximum(o - 16, 0), 16), 16), :].astype(f32)[15:16, :]
        after = ref[pl.ds(pl.multiple_of(jnp.minimum(o + RT, seq - 16), 16), 16), :].astype(f32)[0:1, :]
        before = jnp.where(i > 0, before, 0.0)
        after = jnp.where(i < n_prep - 1, after, 0.0)
        prev = jnp.where(rows == 0, before, pltpu.roll(x, 1, 0))
        nxt = jnp.where(rows == RT - 1, after, pltpu.roll(x, RT - 1, 0))
        return x + mu_ref[...] * (0.5 * (prev + nxt) - x)

    def prep(i, carry):
        o = pl.multiple_of(i * RT, RT)
        rows_w = lax.broadcasted_iota(jnp.int32, (RT, W), 0)
        rows_n = lax.broadcasted_iota(jnp.int32, (RT, wl_ref.shape[1]), 0)
        r = shifted(r_ref, mu_r_ref, i, rows_w)
        k = shifted(k_ref, mu_k_ref, i, rows_w)
        v = shifted(v_ref, mu_v_ref, i, rows_w)
        wl = shifted(wl_ref, mu_wl_ref, i, rows_n)
        al = shifted(al_ref, mu_al_ref, i, rows_n)
        gl = shifted(gl_ref, mu_gl_ref, i, rows_n)
        kk0 = k * kk_ref[...]
        kap = kk0 / jnp.maximum(jnp.sqrt(seg_sum(kk0 * kk0)), 1e-12)
        r_s[pl.ds(o, RT), :] = r
        v_s[pl.ds(o, RT), :] = v
        kap_s[pl.ds(o, RT), :] = kap
        g_s[pl.ds(o, RT), :] = jnp.dot(jax.nn.sigmoid(gl).astype(bf16), gup_ref[...],
                                       preferred_element_type=f32)
        for d in range(2):
            wl_d = jnp.tanh(wl[:, d * B_DECAY_LORA:(d + 1) * B_DECAY_LORA]).astype(bf16)
            al_d = al[:, d * B_AAA_LORA:(d + 1) * B_AAA_LORA].astype(bf16)
            w_raw = w0_ref[d:d + 1, :] + jnp.dot(wl_d, wup_ref[d], preferred_element_type=f32)
            a = jax.nn.sigmoid(a0_ref[d:d + 1, :] + jnp.dot(al_d, aup_ref[d], preferred_element_type=f32))
            logw_s[d, pl.ds(o, RT), :] = -jnp.exp(-0.5) * jax.nn.sigmoid(w_raw)
            kh_s[d, pl.ds(o, RT), :] = k * (1.0 + (a - 1.0) * ka_ref[...])
            beta_s[d, pl.ds(o, RT), :] = kap * a
        return carry

    lax.fori_loop(0, n_prep, prep, 0)

    row = lax.broadcasted_iota(jnp.int32, (L, W), 0)
    col = lax.broadcasted_iota(jnp.int32, (L, W), 1) % B_HEAD
    eye_cat = (row == col).astype(f32)
    strict = (row > col, row < col)
    incl = (row >= col, row <= col)

    def cumsum_rows(x, reverse):
        sh = 1
        while sh < L:
            if reverse:
                x = x + jnp.where(row < L - sh, pltpu.roll(x, L - sh, 0), 0.0)
            else:
                x = x + jnp.where(row >= sh, pltpu.roll(x, sh, 0), 0.0)
            sh *= 2
        return x

    def chunk(start, state, d):
        rows = pl.ds(start, L)
        r, v, kap = r_s[rows, :], v_s[rows, :], kap_s[rows, :]
        logw, kh, beta = logw_s[d, rows, :], kh_s[d, rows, :], beta_s[d, rows, :]
        cum = cumsum_rows(logw, reverse=(d == 1))
        tot = cum[L - 1:L, :] if d == 0 else cum[0:1, :]
        p_in, p_inv, p_end = jnp.exp(cum), jnp.exp(-cum), jnp.exp(tot - cum)
        kap_t = kap * jnp.exp(cum - logw)
        r_t = r * p_in
        v_bd = block_diag(v.astype(bf16))
        g_all = _nt_dot(jnp.concatenate([kap_t, r_t], axis=0).astype(bf16),
                        jnp.concatenate([block_diag((beta * p_inv).astype(bf16)),
                                         block_diag((kh * p_inv).astype(bf16))], axis=0))
        x_pow = -jnp.where(strict[d], g_all[:L, :W], 0.0)
        a_ak = jnp.where(strict[d], g_all[:L, W:], 0.0)
        a_qb = jnp.where(incl[d], g_all[L:, :W], 0.0)
        a_qk = jnp.where(incl[d], g_all[L:, W:], 0.0)
        t_inv = eye_cat + x_pow
        x_pow = jnp.dot(x_pow.astype(bf16), block_diag(x_pow.astype(bf16)), preferred_element_type=f32)
        n_steps = L.bit_length() - 2
        for step in range(n_steps):
            if step < n_steps - 1:
                both = jnp.dot(jnp.concatenate([x_pow, t_inv], axis=0).astype(bf16),
                               block_diag(x_pow.astype(bf16)), preferred_element_type=f32)
                x_pow, t_inv = both[:L], t_inv + both[L:]
            else:
                t_inv = t_inv + jnp.dot(t_inv.astype(bf16), block_diag(x_pow.astype(bf16)),
                                        preferred_element_type=f32)
        av = jnp.dot(jnp.concatenate([a_ak, a_qk], axis=0).astype(bf16), v_bd, preferred_element_type=f32)
        t_b = t_inv.astype(bf16)
        w_mat = jnp.dot(t_b, block_diag(kap_t.astype(bf16)), preferred_element_type=f32)
        u_loc = jnp.dot(t_b, block_diag(av[:L].astype(bf16)), preferred_element_type=f32)
        ws = _nt_dot(jnp.concatenate([w_mat, r_t], axis=0).astype(bf16), block_diag(state.astype(bf16)))
        u = ws[:L] + u_loc
        y = ws[L:] + av[L:] - jnp.dot(a_qb.astype(bf16), block_diag(u.astype(bf16)), preferred_element_type=f32)
        full = _tn_dot(jnp.concatenate([v, -u], axis=0).astype(bf16),
                       jnp.concatenate([kh * p_end, beta * p_end], axis=0).astype(bf16))
        new_state = state * jnp.exp(tot)
        for h, m in enumerate(head_masks):
            new_state = new_state + jnp.where(m, full[h * B_HEAD:(h + 1) * B_HEAD, :], 0.0)
        return y, new_state

    def body(c, states):
        s_f, s_b = states
        start_f = pl.multiple_of(c * L, L)
        start_b = pl.multiple_of((n_chunks - 1 - c) * L, L)
        y_f, s_f = chunk(start_f, s_f, 0)
        y_b, s_b = chunk(start_b, s_b, 1)
        y_s[0, pl.ds(start_f, L), :] = y_f
        y_s[1, pl.ds(start_b, L), :] = y_b
        return s_f, s_b

    zero_state = jnp.zeros((B_HEAD, W), f32)
    lax.fori_loop(0, n_chunks, body, (zero_state, zero_state))

    def finish(i, carry):
        rows = pl.ds(pl.multiple_of(i * RT, RT), RT)
        y = y_s[0, rows, :] + y_s[1, rows, :]
        mean = seg_sum(y) * (1.0 / B_HEAD)
        yc = y - mean
        var = seg_sum(yc * yc) * (1.0 / B_HEAD)
        yn = yc * lax.rsqrt(var + B_GN_EPS) * lng_ref[...] + lnb_ref[...]
        bonus = seg_sum(r_s[rows, :] * (kh_s[0, rows, :] + kh_s[1, rows, :]) * rk_ref[...]) * v_s[rows, :]
        o_ref[rows, :] = ((yn + bonus) * g_s[rows, :]).astype(o_ref.dtype)
        return carry

    lax.fori_loop(0, n_prep, finish, 0)


def _rwkv7_mixer(x, w_in, mu, w0, w_up, a0, a_up, k_k, k_a, r_k, g_up, lnx_g, lnx_b):
    bsz, s, d = x.shape
    n_proj = w_in.shape[1]
    W = RWKV_LANES
    assert d % W == 0 and s % RWKV_PREP_ROWS == 0 and B_GATE_LORA == LANES
    assert 2 * B_DECAY_LORA == LANES and 2 * B_AAA_LORA == LANES
    p = _project(x.reshape(bsz * s, d), w_in, jnp.zeros((n_proj,), jnp.float32), jnp.bfloat16)
    p = p.reshape(bsz, s, n_proj)
    n_groups = d // W
    gw = d // W
    lora0 = 3 * d // LANES

    def seq_w(off):
        return pl.BlockSpec((None, s, W), lambda b, g, off=off: (b, 0, off + g))

    def seq_n(idx):
        return pl.BlockSpec((None, s, LANES), lambda b, g, idx=idx: (b, 0, idx))

    def vec_w(off):
        return pl.BlockSpec((1, W), lambda b, g, off=off: (0, off + g))

    def vec_n(idx):
        return pl.BlockSpec((1, LANES), lambda b, g, idx=idx: (0, idx))

    par_w = pl.BlockSpec((1, W), lambda b, g: (0, g))
    two_w = pl.BlockSpec((2, W), lambda b, g: (0, g))
    up_w = pl.BlockSpec((2, B_DECAY_LORA, W), lambda b, g: (0, 0, g))
    mu2 = mu.reshape(1, n_proj)
    row = lambda a: a.reshape(1, d)
    scratch_w = pltpu.VMEM((s, W), jnp.float32)
    scratch_2w = pltpu.VMEM((2, s, W), jnp.float32)
    return pl.pallas_call(
        _rwkv_scan_kernel,
        out_shape=jax.ShapeDtypeStruct((bsz, s, d), jnp.bfloat16),
        grid=(bsz, n_groups),
        in_specs=[seq_w(0), seq_w(gw), seq_w(2 * gw), seq_n(lora0), seq_n(lora0 + 1), seq_n(lora0 + 2),
                  vec_w(0), vec_w(gw), vec_w(2 * gw), vec_n(lora0), vec_n(lora0 + 1), vec_n(lora0 + 2),
                  two_w, up_w, two_w, up_w, par_w, par_w, par_w,
                  pl.BlockSpec((B_GATE_LORA, W), lambda b, g: (0, g)), par_w, par_w],
        out_specs=pl.BlockSpec((None, s, W), lambda b, g: (b, 0, g)),
        scratch_shapes=[scratch_w, scratch_w, scratch_w, scratch_w, scratch_2w, scratch_2w, scratch_2w, scratch_2w],
        compiler_params=_compiler_params(("parallel", "parallel")),
        name="rwkv7_scan",
    )(p, p, p, p, p, p, mu2, mu2, mu2, mu2, mu2, mu2,
      w0, w_up.astype(jnp.bfloat16), a0, a_up.astype(jnp.bfloat16), row(k_k), row(k_a), row(r_k),
      g_up.astype(jnp.bfloat16), row(lnx_g), row(lnx_b))


def _axial_rope_tables(s):
    rows = s // GRID_W
    row = jnp.repeat(jnp.arange(rows, dtype=jnp.float32), GRID_W)
    col = jnp.tile(jnp.arange(GRID_W, dtype=jnp.float32), rows)
    n_freq = C_HEAD_DIM // 4
    inv_freq = ROPE_THETA ** (-jnp.arange(n_freq, dtype=jnp.float32) / n_freq)
    ang = jnp.concatenate([row[:, None] * inv_freq, col[:, None] * inv_freq], axis=-1)
    return jnp.cos(ang), jnp.sin(ang)


ATTN_Q_TILE = 256


def _rms_rope(x, gain, cos_f, sin_f):
    xn = x * lax.rsqrt(jnp.mean(x * x, axis=-1, keepdims=True) + RMS_EPS) * gain
    return xn * cos_f + pltpu.roll(xn, C_HEAD_DIM // 2, 1) * sin_f


def _attn_kernel(q_ref, k_ref, v_ref, cq_ref, sq_ref, ck_ref, sk_ref, qg_ref, kg_ref, o_ref, kr_s):
    f32, bf16 = jnp.float32, jnp.bfloat16

    @pl.when(pl.program_id(2) == 0)
    def _():
        kr_s[...] = _rms_rope(k_ref[...].astype(f32), kg_ref[...], ck_ref[...], sk_ref[...]).astype(bf16)

    group = q_ref.shape[1] // C_HEAD_DIM
    for g in range(group):
        cols = slice(g * C_HEAD_DIM, (g + 1) * C_HEAD_DIM)
        q = _rms_rope(q_ref[:, cols].astype(f32), qg_ref[...], cq_ref[...], sq_ref[...]) * C_HEAD_DIM ** -0.5
        scores = _nt_dot(q.astype(bf16), kr_s[...])
        p = jnp.exp(scores - jnp.max(scores, axis=-1, keepdims=True))
        denom = jnp.sum(p, axis=-1, keepdims=True)
        o = jnp.dot(p.astype(bf16), v_ref[...], preferred_element_type=f32)
        o_ref[:, cols] = (o / denom).astype(o_ref.dtype)


def _axial_gqa_mixer(x, w_in, q_gain, k_gain):
    bsz, s, d = x.shape
    dh = C_HEAD_DIM
    qh = d // dh
    group = qh // C_KV_HEADS
    n_proj = w_in.shape[1]
    half = jnp.concatenate([jnp.arange(0, dh, 2), jnp.arange(1, dh, 2)])
    n_rot = (qh + C_KV_HEADS) * dh
    w_rot = w_in[:, :n_rot].reshape(d, n_rot // dh, dh // 2, 2).transpose(0, 1, 3, 2).reshape(d, n_rot)
    w = jnp.concatenate([w_rot, w_in[:, n_rot:]], axis=1)
    p = _project(x.reshape(bsz * s, d), w, jnp.zeros((n_proj,), jnp.float32), jnp.bfloat16)
    p = p.reshape(bsz, s, n_proj)
    cos, sin = _axial_rope_tables(s)
    cos_f = jnp.concatenate([cos, cos], axis=-1)
    sin_f = jnp.concatenate([-sin, sin], axis=-1)
    tq = min(ATTN_Q_TILE, s)
    gw = group * dh
    q_rows = pl.BlockSpec((tq, dh), lambda b, h, i: (i, 0))
    k_rows = pl.BlockSpec((s, dh), lambda b, h, i: (0, 0))
    vec = pl.BlockSpec((1, dh), lambda b, h, i: (0, 0))
    return pl.pallas_call(
        _attn_kernel,
        out_shape=jax.ShapeDtypeStruct((bsz, s, d), jnp.bfloat16),
        grid=(bsz, C_KV_HEADS, s // tq),
        in_specs=[pl.BlockSpec((None, tq, gw), lambda b, h, i: (b, i, h)),
                  pl.BlockSpec((None, s, dh), lambda b, h, i: (b, 0, qh + h)),
                  pl.BlockSpec((None, s, dh), lambda b, h, i: (b, 0, qh + C_KV_HEADS + h)),
                  q_rows, q_rows, k_rows, k_rows, vec, vec],
        out_specs=pl.BlockSpec((None, tq, gw), lambda b, h, i: (b, i, h)),
        scratch_shapes=[pltpu.VMEM((s, dh), jnp.bfloat16)],
        compiler_params=_compiler_params(("parallel", "parallel", "arbitrary")),
        name="axial_attention",
    )(p, p, p, cos_f, sin_f, cos_f, sin_f, q_gain[half].reshape(1, dh), k_gain[half].reshape(1, dh))


def kernel(x, a_w_in, a_b_in, a_head_gain, a_w_out, b_w_in, b_mu, b_w0, b_w_up, b_a0, b_a_up, b_k_k, b_k_a, b_r_k, b_g_up, b_lnx_g, b_lnx_b, b_w_out, c_w_in, c_q_gain, c_k_gain, c_w_out, ln1_g, ln1_b, moe_w_router, moe_b_router, moe_w_gu, moe_b_gu, moe_w_dn, moe_b_dn, ln2_g, ln2_b):
    bsz, s, d = x.shape
    depth = ln1_g.shape[0]
    xt = x.reshape(bsz * s, d)
    for i in range(depth):
        kind = i % N_MIXERS
        j = i // N_MIXERS
        x3 = xt.reshape(bsz, s, d)
        if kind == 0:
            h = _mlstm_mixer(x3, a_w_in[j], a_b_in[j], a_head_gain[j])
            w_out = a_w_out[j]
        elif kind == 1:
            h = _rwkv7_mixer(x3, b_w_in[j], b_mu[j], b_w0[j], b_w_up[j], b_a0[j], b_a_up[j],
                             b_k_k[j], b_k_a[j], b_r_k[j], b_g_up[j], b_lnx_g[j], b_lnx_b[j])
            w_out = b_w_out[j]
        else:
            h = _axial_gqa_mixer(x3, c_w_in[j], c_q_gain[j], c_k_gain[j])
            w_out = c_w_out[j]
        xt, xt_bf16 = _outproj_ln(h.reshape(bsz * s, d), w_out, xt, ln1_g[i], ln1_b[i])
        y_rows, route = _moe_ffn(xt, xt_bf16, moe_w_router[i], moe_b_router[i], moe_w_gu[i], moe_b_gu[i],
                                 moe_w_dn[i], moe_b_dn[i])
        xt = _combine_ln(y_rows, route, xt, ln2_g[i], ln2_b[i])
    return xt.reshape(bsz, s, d)
```

```python
import functools

import jax
import jax.numpy as jnp
from jax import lax
from jax.experimental import pallas as pl
from jax.experimental.pallas import tpu as pltpu

DEPTH = 4
N_MIXERS = 3
GRID_W = 64
DEEPNORM_ALPHA = (2 * DEPTH) ** 0.25
LN_EPS = 1e-5
RMS_EPS = 1e-6

A_HEADS = 8
A_DQK = 64
A_CHUNK = 64
A_M_INIT = -1e30

B_HEAD = 64
B_DECAY_LORA = 64
B_AAA_LORA = 64
B_GATE_LORA = 128
B_GN_EPS = 64e-5

C_HEAD_DIM = 128
C_KV_HEADS = 2
C_Q_BLOCK = 128
ROPE_THETA = 10000.0

N_EXPERTS = 32
TOP_K = 4
SWIGLU_LIMIT = 7.0
SWIGLU_ALPHA = 1.702

LANES = 128
VMEM_LIMIT_BYTES = 56 * 1024 * 1024

ROW_TILE = 512
MOE_ROW_TILE = 512


def _compiler_params(semantics):
    return pltpu.CompilerParams(dimension_semantics=semantics, vmem_limit_bytes=VMEM_LIMIT_BYTES)


def _row_tile(m):
    t = min(ROW_TILE, m)
    assert m % t == 0, (m, t)
    return t


def _col_chunk(n):
    return max(c for c in range(LANES, 5 * LANES + 1, LANES) if n % c == 0)


def _proj_kernel(x_ref, w_ref, b_ref, *o_refs):
    xb = x_ref[...].astype(jnp.bfloat16)
    col = 0
    for o_ref in o_refs:
        n = o_ref.shape[1]
        step = _col_chunk(n)
        for j in range(0, n, step):
            acc = jnp.dot(xb, w_ref[:, col + j:col + j + step], preferred_element_type=jnp.float32)
            o_ref[:, j:j + step] = (acc + b_ref[:, col + j:col + j + step]).astype(o_ref.dtype)
        col += n


def _project(x, w, b, out_dtype, f32_tail=0):
    m, k = x.shape
    n = w.shape[1]
    widths = [n - f32_tail, f32_tail] if f32_tail else [n]
    dtypes = [out_dtype, jnp.float32]
    assert all(c % LANES == 0 for c in widths)
    tm = _row_tile(m)
    outs = pl.pallas_call(
        _proj_kernel,
        out_shape=[jax.ShapeDtypeStruct((m, c), dt) for c, dt in zip(widths, dtypes)],
        grid=(m // tm,),
        in_specs=[pl.BlockSpec((tm, k), lambda i: (i, 0)),
                  pl.BlockSpec((k, n), lambda i: (0, 0)),
                  pl.BlockSpec((1, n), lambda i: (0, 0))],
        out_specs=[pl.BlockSpec((tm, c), lambda i: (i, 0)) for c in widths],
        compiler_params=_compiler_params(("parallel",)),
        name="project",
    )(x, w.astype(jnp.bfloat16), b.reshape(1, n).astype(jnp.float32))
    return tuple(outs) if f32_tail else outs[0]


def _pad_cols(w, b, n_pad):
    k, n = w.shape
    if b is None:
        b = jnp.zeros((n,), jnp.float32)
    return jnp.pad(w, ((0, 0), (0, n_pad - n))), jnp.pad(b, (0, n_pad - n))


def _layer_norm_rows(z, g, b):
    mu = jnp.mean(z, axis=-1, keepdims=True)
    zc = z - mu
    var = jnp.mean(zc * zc, axis=-1, keepdims=True)
    return zc * lax.rsqrt(var + LN_EPS) * g + b


def _outproj_ln_kernel(h_ref, w_ref, x_ref, g_ref, b_ref, o_ref, ob_ref):
    mix = jnp.dot(h_ref[...].astype(jnp.bfloat16), w_ref[...], preferred_element_type=jnp.float32)
    z = DEEPNORM_ALPHA * x_ref[...] + mix
    y = _layer_norm_rows(z, g_ref[...], b_ref[...])
    o_ref[...] = y
    ob_ref[...] = y.astype(ob_ref.dtype)


def _outproj_ln(h, w_out, x, g, b):
    m, d = x.shape
    tm = _row_tile(m)
    row = pl.BlockSpec((tm, d), lambda i: (i, 0))
    vec = pl.BlockSpec((1, d), lambda i: (0, 0))
    return pl.pallas_call(
        _outproj_ln_kernel,
        out_shape=[jax.ShapeDtypeStruct((m, d), jnp.float32), jax.ShapeDtypeStruct((m, d), jnp.bfloat16)],
        grid=(m // tm,),
        in_specs=[row, pl.BlockSpec((d, d), lambda i: (0, 0)), row, vec, vec],
        out_specs=[row, row],
        compiler_params=_compiler_params(("parallel",)),
        name="outproj_ln",
    )(h, w_out.astype(jnp.bfloat16), x, g.reshape(1, d), b.reshape(1, d))


ROUTE_IDX_LANE = 0
ROUTE_RANK_LANE = TOP_K
ROUTE_GATE_LANE = 2 * TOP_K


def _combine_ln_kernel(y_ref, r_ref, x_ref, g_ref, b_ref, o_ref):
    y = jnp.zeros(x_ref.shape, jnp.float32)
    for k in range(TOP_K):
        gate = r_ref[:, ROUTE_GATE_LANE + k:ROUTE_GATE_LANE + k + 1]
        y = y + gate * y_ref[k].astype(jnp.float32)
    z = DEEPNORM_ALPHA * x_ref[...] + y
    o_ref[...] = _layer_norm_rows(z, g_ref[...], b_ref[...])


def _combine_ln(y_rows, route, x, g, b):
    m, d = x.shape
    tm = _row_tile(m)
    row = pl.BlockSpec((tm, d), lambda i: (i, 0))
    vec = pl.BlockSpec((1, d), lambda i: (0, 0))
    return pl.pallas_call(
        _combine_ln_kernel,
        out_shape=jax.ShapeDtypeStruct((m, d), jnp.float32),
        grid=(m // tm,),
        in_specs=[pl.BlockSpec((TOP_K, tm, d), lambda i: (0, i, 0)),
                  pl.BlockSpec((tm, LANES), lambda i: (i, 0)), row, vec, vec],
        out_specs=row,
        compiler_params=_compiler_params(("parallel",)),
        name="combine_ln",
    )(y_rows, route, x, g.reshape(1, d), b.reshape(1, d))


ROUTER_PAD_BIAS = -1e30


def _router_kernel(x_ref, w_ref, b_ref, o_ref, cnt_ref, base_s):
    f32 = jnp.float32

    @pl.when(pl.program_id(0) == 0)
    def _():
        base_s[...] = jnp.zeros_like(base_s)

    t = x_ref.shape[0]
    logits = jnp.dot(x_ref[...], w_ref[...], preferred_element_type=f32,
                     precision=lax.Precision.HIGHEST) + b_ref[...]
    lane = lax.broadcasted_iota(jnp.int32, (t, LANES), 1)
    vals = logits
    tops, idxs, sels = [], [], []
    for _ in range(TOP_K):
        top = jnp.max(vals, axis=-1, keepdims=True)
        idx = jnp.min(jnp.where(vals == top, lane, LANES), axis=-1, keepdims=True)
        sel = lane == idx
        vals = jnp.where(sel, -jnp.inf, vals)
        tops.append(top)
        idxs.append(idx)
        sels.append(sel)
    exps = [jnp.exp(top - tops[0]) for top in tops]
    total = exps[0]
    for e in exps[1:]:
        total = total + e
    chosen = jnp.zeros((t, LANES), f32)
    for sel in sels:
        chosen = chosen + sel.astype(f32)
    earlier = (lax.broadcasted_iota(jnp.int32, (t, t), 0) > lax.broadcasted_iota(jnp.int32, (t, t), 1))
    prefix = jnp.dot(earlier.astype(jnp.bfloat16), chosen.astype(jnp.bfloat16),
                     preferred_element_type=f32) + base_s[...]
    packed = jnp.zeros((t, LANES), f32)
    for k in range(TOP_K):
        rank = jnp.sum(jnp.where(sels[k], prefix, 0.0), axis=-1, keepdims=True)
        packed = jnp.where(lane == ROUTE_IDX_LANE + k, idxs[k].astype(f32), packed)
        packed = jnp.where(lane == ROUTE_RANK_LANE + k, rank, packed)
        packed = jnp.where(lane == ROUTE_GATE_LANE + k, exps[k] / total, packed)
    o_ref[...] = packed
    base_s[...] = base_s[...] + jnp.sum(chosen, axis=0, keepdims=True)
    cnt_ref[...] = base_s[...]


def _route(xt, w_router, b_router):
    m, d = xt.shape
    tm = _row_tile(m)
    n_e = w_router.shape[1]
    w = jnp.pad(w_router, ((0, 0), (0, LANES - n_e)))
    b = jnp.pad(b_router, (0, LANES - n_e), constant_values=ROUTER_PAD_BIAS)
    return pl.pallas_call(
        _router_kernel,
        out_shape=[jax.ShapeDtypeStruct((m, LANES), jnp.float32), jax.ShapeDtypeStruct((1, LANES), jnp.float32)],
        grid=(m // tm,),
        in_specs=[pl.BlockSpec((tm, d), lambda i: (i, 0)),
                  pl.BlockSpec((d, LANES), lambda i: (0, 0)),
                  pl.BlockSpec((1, LANES), lambda i: (0, 0))],
        out_specs=[pl.BlockSpec((tm, LANES), lambda i: (i, 0)), pl.BlockSpec((1, LANES), lambda i: (0, 0))],
        scratch_shapes=[pltpu.VMEM((1, LANES), jnp.float32)],
        compiler_params=_compiler_params(("arbitrary",)),
        name="router",
    )(xt, w, b.reshape(1, LANES))


GLU_BLOCK = 2 * LANES


def _regroup_kernel(w_ref, p_ref, o_ref):
    w = w_ref[0].astype(jnp.bfloat16)
    for c in range(0, w.shape[1], GLU_BLOCK):
        o_ref[0, :, c:c + GLU_BLOCK] = jnp.dot(w[:, c:c + GLU_BLOCK], p_ref[...],
                                               preferred_element_type=jnp.float32).astype(o_ref.dtype)


def _regroup_glu_columns(w_gu_layers, layer):
    _, n_e, d, f2 = w_gu_layers.shape
    src = jnp.arange(GLU_BLOCK)
    perm = (src[:, None] == (2 * (src % LANES) + src // LANES)[None, :]).astype(jnp.bfloat16)
    tk = _row_tile(d)
    return pl.pallas_call(
        _regroup_kernel,
        out_shape=jax.ShapeDtypeStruct((n_e, d, f2), jnp.bfloat16),
        grid=(n_e, d // tk),
        in_specs=[pl.BlockSpec((None, 1, tk, f2), lambda e, i: (layer, e, i, 0)),
                  pl.BlockSpec((GLU_BLOCK, GLU_BLOCK), lambda e, i: (0, 0))],
        out_specs=pl.BlockSpec((1, tk, f2), lambda e, i: (e, i, 0)),
        compiler_params=_compiler_params(("parallel", "parallel")),
        name="regroup_glu",
    )(w_gu_layers, perm)


def _slot_kernel(r_ref, start_ref, o_ref):
    lane = lax.broadcasted_iota(jnp.int32, r_ref.shape, 1)
    route = r_ref[...]
    slots = jnp.zeros(r_ref.shape, jnp.float32)
    for k in range(TOP_K):
        expert = route[:, ROUTE_IDX_LANE + k:ROUTE_IDX_LANE + k + 1].astype(jnp.int32)
        first = jnp.sum(jnp.where(lane == expert, start_ref[...], 0.0), axis=-1, keepdims=True)
        slots = jnp.where(lane == k, first + route[:, ROUTE_RANK_LANE + k:ROUTE_RANK_LANE + k + 1], slots)
    o_ref[...] = slots.astype(jnp.int32)


def _slots(route, expert_start):
    m = route.shape[0]
    tm = _row_tile(m)
    return pl.pallas_call(
        _slot_kernel,
        out_shape=jax.ShapeDtypeStruct((m, LANES), jnp.int32),
        grid=(m // tm,),
        in_specs=[pl.BlockSpec((tm, LANES), lambda i: (i, 0)), pl.BlockSpec((1, LANES), lambda i: (0, 0))],
        out_specs=pl.BlockSpec((tm, LANES), lambda i: (i, 0)),
        compiler_params=_compiler_params(("parallel",)),
        name="moe_slots",
    )(route, expert_start)


def _regroup_glu_bias(b_gu):
    n_e, f2 = b_gu.shape
    return b_gu.reshape(n_e, f2 // GLU_BLOCK, LANES, 2).transpose(0, 1, 3, 2).reshape(n_e, 1, f2)


def _expert_kernel(be_ref, nb_ref, x_ref, wgu_ref, bgu_ref, wd_ref, bd_ref, o_ref):
    @pl.when(pl.program_id(0) < nb_ref[0])
    def _():
        xb = x_ref[...]
        acts = []
        for c in range(0, wgu_ref.shape[2], GLU_BLOCK):
            h = (jnp.dot(xb, wgu_ref[0, :, c:c + GLU_BLOCK], preferred_element_type=jnp.float32)
                 + bgu_ref[0, :, c:c + GLU_BLOCK])
            hg = jnp.minimum(h[:, :LANES], SWIGLU_LIMIT)
            hl = jnp.clip(h[:, LANES:], -SWIGLU_LIMIT, SWIGLU_LIMIT)
            acts.append((hg * jax.nn.sigmoid(SWIGLU_ALPHA * hg) * (hl + 1.0)).astype(jnp.bfloat16))
        act = jnp.concatenate(acts, axis=1)
        y = jnp.dot(act, wd_ref[0], preferred_element_type=jnp.float32) + bd_ref[0]
        o_ref[...] = y.astype(o_ref.dtype)


def _expert_ffn(x_sorted, block_expert, n_used, w_gu, b_gu, w_dn, b_dn):
    cap, d = x_sorted.shape
    f2 = w_gu.shape[2]
    tm = MOE_ROW_TILE
    n_blocks = cap // tm

    def blk(i, be, nb):
        return (jnp.minimum(i, nb[0] - 1), 0)

    def wsel(i, be, nb):
        return (be[i], 0, 0)

    grid_spec = pltpu.PrefetchScalarGridSpec(
        num_scalar_prefetch=2,
        grid=(n_blocks,),
        in_specs=[pl.BlockSpec((tm, d), blk),
                  pl.BlockSpec((1, d, f2), wsel),
                  pl.BlockSpec((1, 1, f2), wsel),
                  pl.BlockSpec((1, f2 // 2, d), wsel),
                  pl.BlockSpec((1, 1, d), wsel)],
        out_specs=pl.BlockSpec((tm, d), blk),
    )
    return pl.pallas_call(
        _expert_kernel,
        out_shape=jax.ShapeDtypeStruct((cap, d), jnp.bfloat16),
        grid_spec=grid_spec,
        compiler_params=_compiler_params(("arbitrary",)),
        name="expert_ffn",
    )(block_expert, n_used, x_sorted, w_gu, b_gu, w_dn, b_dn)


def _moe_ffn(x, x_bf16, w_router, b_router, w_gu_layers, layer, b_gu, w_dn, b_dn):
    n, d = x.shape
    tm = MOE_ROW_TILE
    n_e = w_router.shape[1]
    route, counts = _route(x, w_router, b_router)
    counts = counts[0].astype(jnp.int32)
    padded = (counts + tm - 1) // tm * tm
    pad_end = jnp.cumsum(padded)
    pad_start = pad_end - padded
    cap = n * TOP_K + n_e * tm
    n_blocks = cap // tm
    slot = _slots(route, pad_start.astype(jnp.float32).reshape(1, LANES))[:, :TOP_K]
    slot_kmajor = slot.T.reshape(-1)
    tok_of_slot = jnp.zeros((cap,), jnp.int32).at[slot_kmajor].set(jnp.tile(jnp.arange(n, dtype=jnp.int32), TOP_K))
    block_first_row = jnp.arange(n_blocks, dtype=jnp.int32) * tm
    block_expert = jnp.minimum(jnp.sum(pad_end[None, :n_e] <= block_first_row[:, None], axis=1), n_e - 1)
    n_used = (pad_end[n_e - 1] // tm).reshape(1)

    x_sorted = jnp.take(x_bf16, tok_of_slot, axis=0)
    yb = _expert_ffn(x_sorted, block_expert.astype(jnp.int32), n_used.astype(jnp.int32),
                     _regroup_glu_columns(w_gu_layers, layer), _regroup_glu_bias(b_gu),
                     w_dn.astype(jnp.bfloat16), b_dn[:, None, :])
    return jnp.take(yb, slot_kmajor, axis=0).reshape(TOP_K, n, d), route


MLSTM_CHUNK = 128
MLSTM_PAIR = 2
MLSTM_NEG = -1e30


def _cumsum_rows(x, reverse):
    n = x.shape[0]
    row = lax.broadcasted_iota(jnp.int32, x.shape, 0)
    sh = 1
    while sh < n:
        if reverse:
            x = x + jnp.where(row < n - sh, pltpu.roll(x, n - sh, 0), 0.0)
        else:
            x = x + jnp.where(row >= sh, pltpu.roll(x, sh, 0), 0.0)
        sh *= 2
    return x


def _mlstm_kernel(q_ref, k_ref, v_ref, og_ref, g_ref, gain_ref, o_ref, h_s, c_s):
    f32, bf16 = jnp.float32, jnp.bfloat16
    seq = q_ref.shape[0]
    L = MLSTM_CHUNK
    n_chunks = seq // L
    dv = v_ref.shape[1] // MLSTM_PAIR
    dqk = q_ref.shape[1] // MLSTM_PAIR

    lane_q = lax.broadcasted_iota(jnp.int32, (1, q_ref.shape[1]), 1) // dqk
    lane_g = lax.broadcasted_iota(jnp.int32, (1, LANES), 1)
    is_forget_lane = (lane_g // MLSTM_PAIR) % 2 == 1
    row = lax.broadcasted_iota(jnp.int32, (L, L), 0)
    col = lax.broadcasted_iota(jnp.int32, (L, L), 1)
    causal = (col <= row, col >= row)
    ones_v = jnp.ones((L, dv), bf16)

    c_s[...] = jnp.zeros_like(c_s)

    def direction(start, d, m_prev):
        rows = pl.ds(start, L)
        gates = g_ref[rows, :]
        log_f = jnp.minimum(gates, 0.0) - jnp.log(1.0 + jnp.exp(-jnp.abs(gates)))
        cum = _cumsum_rows(log_f, reverse=(d == 1))
        z = jnp.where(is_forget_lane, cum, gates)
        z_t = z.T
        q_all = q_ref[rows, :]
        k_all = k_ref[rows, :]
        m_out = []
        for j in range(MLSTM_PAIR):
            lane_i = 2 * MLSTM_PAIR * d + j
            lane_f = lane_i + MLSTM_PAIR
            b_col, i_col = z[:, lane_f:lane_f + 1], z[:, lane_i:lane_i + 1]
            b_row, i_row = z_t[lane_f:lane_f + 1, :], z_t[lane_i:lane_i + 1, :]
            m_st = m_prev[j]
            dmat = jnp.where(causal[d], b_col + (i_row - b_row), MLSTM_NEG)
            inter = b_col + m_st
            m_t = jnp.maximum(inter, jnp.max(dmat, axis=-1, keepdims=True))
            qz = jnp.where(lane_q == j, q_all, jnp.zeros_like(q_all))
            kz = jnp.where(lane_q == j, k_all, jnp.zeros_like(k_all))
            v_aug = jnp.concatenate([v_ref[rows, j * dv:(j + 1) * dv], ones_v], axis=1)
            qk = _nt_dot(qz, kz) * (A_DQK ** -0.5) * jnp.exp(dmat - m_t)
            state = c_s[2 * d + j]
            nd = (jnp.dot(qk.astype(bf16), v_aug, preferred_element_type=f32)
                  + jnp.exp(inter - m_t) * (A_DQK ** -0.5)
                  * jnp.dot(qz, state.astype(bf16), preferred_element_type=f32))
            h = nd[:, :dv] / jnp.maximum(jnp.abs(nd[:, dv:]), jnp.exp(-m_t))
            h_s[d, rows, j * dv:(j + 1) * dv] = h
            b_last = b_col[L - 1:L, :] if d == 0 else b_col[0:1, :]
            g_s = b_last - b_col + i_col
            m_new = jnp.maximum(b_last + m_st, jnp.max(g_s, axis=0, keepdims=True))
            w_s = jnp.exp(g_s - m_new)
            c_s[2 * d + j] = (jnp.exp(b_last + m_st - m_new) * state
                              + _tn_dot((w_s * kz.astype(f32)).astype(bf16), v_aug))
            m_out.append(m_new)
        return m_out

    def body(c, m_all):
        m_f = direction(pl.multiple_of(c * L, L), 0, m_all[:MLSTM_PAIR])
        m_b = direction(pl.multiple_of((n_chunks - 1 - c) * L, L), 1, m_all[MLSTM_PAIR:])
        return tuple(m_f) + tuple(m_b)

    m_init = tuple(jnp.full((1, 1), A_M_INIT, f32) for _ in range(2 * MLSTM_PAIR))
    lax.fori_loop(0, n_chunks, body, m_init)

    def finish(c, carry):
        rows = pl.ds(pl.multiple_of(c * L, L), L)
        for j in range(MLSTM_PAIR):
            cols = slice(j * dv, (j + 1) * dv)
            h = h_s[0, rows, cols] + h_s[1, rows, cols]
            hn = h * lax.rsqrt(jnp.mean(h * h, axis=-1, keepdims=True) + RMS_EPS) * gain_ref[:, cols]
            o_ref[rows, cols] = (hn * jax.nn.sigmoid(og_ref[rows, cols].astype(f32))).astype(o_ref.dtype)
        return carry

    lax.fori_loop(0, n_chunks, finish, 0)


def _mlstm_mixer(x, w_in, b_in, head_gain):
    bsz, s, d = x.shape
    a_qk = A_HEADS * A_DQK
    dv = d // A_HEADS
    n_main = 2 * a_qk + 2 * d
    n_pairs = A_HEADS // MLSTM_PAIR
    pw_qk = MLSTM_PAIR * A_DQK
    pw_v = MLSTM_PAIR * dv
    assert pw_qk == LANES and s % MLSTM_CHUNK == 0
    gate_cols = jnp.asarray([[n_main + t * A_HEADS + MLSTM_PAIR * hp + j for t in range(4) for j in range(MLSTM_PAIR)]
                             for hp in range(n_pairs)])
    n_gate = gate_cols.shape[1]
    w_tail = jnp.pad(w_in[:, gate_cols], ((0, 0), (0, 0), (0, LANES - n_gate))).reshape(d, n_pairs * LANES)
    b_tail = jnp.pad(b_in[gate_cols], ((0, 0), (0, LANES - n_gate))).reshape(n_pairs * LANES)
    w = jnp.concatenate([w_in[:, :n_main], w_tail], axis=1)
    b = jnp.concatenate([b_in[:n_main], b_tail])
    p, gates = _project(x.reshape(bsz * s, d), w, b, jnp.bfloat16, f32_tail=n_pairs * LANES)
    p = p.reshape(bsz, s, n_main)
    gates = gates.reshape(bsz, s, n_pairs * LANES)
    k0 = a_qk // pw_qk
    v0 = 2 * a_qk // pw_v
    o0 = (2 * a_qk + d) // pw_v
    return pl.pallas_call(
        _mlstm_kernel,
        out_shape=jax.ShapeDtypeStruct((bsz, s, d), jnp.bfloat16),
        grid=(bsz, n_pairs),
        in_specs=[pl.BlockSpec((None, s, pw_qk), lambda b, h: (b, 0, h)),
                  pl.BlockSpec((None, s, pw_qk), lambda b, h: (b, 0, k0 + h)),
                  pl.BlockSpec((None, s, pw_v), lambda b, h: (b, 0, v0 + h)),
                  pl.BlockSpec((None, s, pw_v), lambda b, h: (b, 0, o0 + h)),
                  pl.BlockSpec((None, s, LANES), lambda b, h: (b, 0, h)),
                  pl.BlockSpec((1, pw_v), lambda b, h: (0, h))],
        out_specs=pl.BlockSpec((None, s, pw_v), lambda b, h: (b, 0, h)),
        scratch_shapes=[pltpu.VMEM((2, s, pw_v), jnp.float32),
                        pltpu.VMEM((2 * MLSTM_PAIR, pw_qk, 2 * dv), jnp.float32)],
        compiler_params=_compiler_params(("parallel", "parallel")),
        name="mlstm_scan",
    )(p, p, p, p, gates, head_gain.reshape(1, d))


RWKV_CHUNK = 64
RWKV_GROUP = 4
RWKV_LANES = RWKV_GROUP * B_HEAD
RWKV_PREP_ROWS = 256
RWKV_LOCAL_UNROLL = 2


def _f32_dot(a, b_bf16):
    hi = a.astype(jnp.bfloat16)
    lo = (a - hi.astype(jnp.float32)).astype(jnp.bfloat16)
    return (jnp.dot(hi, b_bf16, preferred_element_type=jnp.float32)
            + jnp.dot(lo, b_bf16, preferred_element_type=jnp.float32))


def _nt_dot(a, b):
    return lax.dot_general(a, b, (((1,), (1,)), ((), ())), preferred_element_type=jnp.float32)


def _tn_dot(a, b):
    return lax.dot_general(a, b, (((0,), (0,)), ((), ())), preferred_element_type=jnp.float32)


def _rwkv_scan_kernel(r_ref, k_ref, v_ref, wl_ref, al_ref, gl_ref, mu_r_ref, mu_k_ref, mu_v_ref, mu_wl_ref,
                      mu_al_ref, mu_gl_ref, w0_ref, wup_ref, a0_ref, aup_ref, kk_ref, ka_ref, rk_ref, gup_ref,
                      lng_ref, lnb_ref, o_ref,
                      r_s, v_s, kap_s, g_s, logw_s, kh_s, beta_s, y_s,
                      wmat_s, uloc_s, avq_s, rt_s, aqb_s, kend_s, bend_s, ptot_s):
    f32, bf16 = jnp.float32, jnp.bfloat16
    seq = r_ref.shape[0]
    L, W, RT = RWKV_CHUNK, RWKV_LANES, RWKV_PREP_ROWS
    n_chunks = seq // L
    n_prep = seq // RT

    lane = lax.broadcasted_iota(jnp.int32, (1, W), 1)
    head_masks = [(lane // B_HEAD) == h for h in range(RWKV_GROUP)]
    ones_bd = ((lax.broadcasted_iota(jnp.int32, (W, W), 0) // B_HEAD)
               == (lax.broadcasted_iota(jnp.int32, (W, W), 1) // B_HEAD)).astype(bf16)

    def block_diag(a):
        zero = jnp.zeros_like(a)
        return jnp.concatenate([jnp.where(m, a, zero) for m in head_masks], axis=0)

    def seg_sum(a):
        return _f32_dot(a, ones_bd)

    def shifted(ref, mu_ref, i, rows):
        o = pl.multiple_of(i * RT, RT)
        x = ref[pl.ds(o, RT), :].astype(f32)
        before = ref[pl.ds(pl.multiple_of(jnp.maximum(o - 16, 0), 16), 16), :].astype(f32)[15:16, :]
        after = ref[pl.ds(pl.multiple_of(jnp.minimum(o + RT, seq - 16), 16), 16), :].astype(f32)[0:1, :]
        before = jnp.where(i > 0, before, 0.0)
        after = jnp.where(i < n_prep - 1, after, 0.0)
        prev = jnp.where(rows == 0, before, pltpu.roll(x, 1, 0))
        nxt = jnp.where(rows == RT - 1, after, pltpu.roll(x, RT - 1, 0))
        return x + mu_ref[...] * (0.5 * (prev + nxt) - x)

    def prep(i, carry):
        o = pl.multiple_of(i * RT, RT)
        rows_w = lax.broadcasted_iota(jnp.int32, (RT, W), 0)
        rows_n = lax.broadcasted_iota(jnp.int32, (RT, wl_ref.shape[1]), 0)
        r = shifted(r_ref, mu_r_ref, i, rows_w)
        k = shifted(k_ref, mu_k_ref, i, rows_w)
        v = shifted(v_ref, mu_v_ref, i, rows_w)
        wl = shifted(wl_ref, mu_wl_ref, i, rows_n)
        al = shifted(al_ref, mu_al_ref, i, rows_n)
        gl = shifted(gl_ref, mu_gl_ref, i, rows_n)
        kk0 = k * kk_ref[...]
        kap = kk0 / jnp.maximum(jnp.sqrt(seg_sum(kk0 * kk0)), 1e-12)
        r_s[pl.ds(o, RT), :] = r
        v_s[pl.ds(o, RT), :] = v.astype(bf16)
        kap_s[pl.ds(o, RT), :] = kap
        g_s[pl.ds(o, RT), :] = jnp.dot(jax.nn.sigmoid(gl).astype(bf16), gup_ref[...],
                                       preferred_element_type=f32).astype(bf16)
        for d in range(2):
            wl_d = jnp.tanh(wl[:, d * B_DECAY_LORA:(d + 1) * B_DECAY_LORA]).astype(bf16)
            al_d = al[:, d * B_AAA_LORA:(d + 1) * B_AAA_LORA].astype(bf16)
            w_raw = w0_ref[d:d + 1, :] + jnp.dot(wl_d, wup_ref[d], preferred_element_type=f32)
            a = jax.nn.sigmoid(a0_ref[d:d + 1, :] + jnp.dot(al_d, aup_ref[d], preferred_element_type=f32))
            logw_s[d, pl.ds(o, RT), :] = -jnp.exp(-0.5) * jax.nn.sigmoid(w_raw)
            kh_s[d, pl.ds(o, RT), :] = (k * (1.0 + (a - 1.0) * ka_ref[...])).astype(bf16)
            beta_s[d, pl.ds(o, RT), :] = (kap * a).astype(bf16)
        return carry

    lax.fori_loop(0, n_prep, prep, 0)

    row = lax.broadcasted_iota(jnp.int32, (L, W), 0)
    col = lax.broadcasted_iota(jnp.int32, (L, W), 1) % B_HEAD
    eye_cat = (row == col).astype(f32)
    strict = (row > col, row < col)
    incl = (row >= col, row <= col)

    def cumsum_rows(x, reverse):
        sh = 1
        while sh < L:
            if reverse:
                x = x + jnp.where(row < L - sh, pltpu.roll(x, L - sh, 0), 0.0)
            else:
                x = x + jnp.where(row >= sh, pltpu.roll(x, sh, 0), 0.0)
            sh *= 2
        return x

    def local_part(c, d):
        rows = pl.ds(pl.multiple_of(c * L, L), L)
        r, kap = r_s[rows, :], kap_s[rows, :]
        logw, kh, beta = logw_s[d, rows, :], kh_s[d, rows, :].astype(f32), beta_s[d, rows, :].astype(f32)
        cum = cumsum_rows(logw, reverse=(d == 1))
        tot = cum[L - 1:L, :] if d == 0 else cum[0:1, :]
        p_in, p_inv, p_end = jnp.exp(cum), jnp.exp(-cum), jnp.exp(tot - cum)
        kap_t = (kap * jnp.exp(cum - logw)).astype(bf16)
        r_t = (r * p_in).astype(bf16)
        v_bd = block_diag(v_s[rows, :])
        g_all = _nt_dot(jnp.concatenate([kap_t, r_t], axis=0),
                        jnp.concatenate([block_diag((beta * p_inv).astype(bf16)),
                                         block_diag((kh * p_inv).astype(bf16))], axis=0))
        x_pow = -jnp.where(strict[d], g_all[:L, :W], 0.0)
        a_ak = jnp.where(strict[d], g_all[:L, W:], 0.0)
        a_qb = jnp.where(incl[d], g_all[L:, :W], 0.0)
        a_qk = jnp.where(incl[d], g_all[L:, W:], 0.0)
        t_inv = eye_cat + x_pow
        x_pow = jnp.dot(x_pow.astype(bf16), block_diag(x_pow.astype(bf16)), preferred_element_type=f32)
        n_steps = L.bit_length() - 2
        for step in range(n_steps):
            if step < n_steps - 1:
                both = jnp.dot(jnp.concatenate([x_pow, t_inv], axis=0).astype(bf16),
                               block_diag(x_pow.astype(bf16)), preferred_element_type=f32)
                x_pow, t_inv = both[:L], t_inv + both[L:]
            else:
                t_inv = t_inv + jnp.dot(t_inv.astype(bf16), block_diag(x_pow.astype(bf16)),
                                        preferred_element_type=f32)
        av = jnp.dot(jnp.concatenate([a_ak, a_qk], axis=0).astype(bf16), v_bd, preferred_element_type=f32)
        t_b = t_inv.astype(bf16)
        wmat_s[d, rows, :] = jnp.dot(t_b, block_diag(kap_t), preferred_element_type=f32).astype(bf16)
        uloc_s[d, rows, :] = jnp.dot(t_b, block_diag(av[:L].astype(bf16)), preferred_element_type=f32).astype(bf16)
        avq_s[d, rows, :] = av[L:].astype(bf16)
        rt_s[d, rows, :] = r_t
        aqb_s[d, rows, :] = a_qb.astype(bf16)
        kend_s[d, rows, :] = (kh * p_end).astype(bf16)
        bend_s[d, rows, :] = (beta * p_end).astype(bf16)
        ptot_s[d, pl.ds(pl.multiple_of(c * 8, 8), 8), :] = jnp.broadcast_to(jnp.exp(tot), (8, W))

    def local_body(i, carry):
        for j in range(RWKV_LOCAL_UNROLL):
            for d in range(2):
                local_part(i * RWKV_LOCAL_UNROLL + j, d)
        return carry

    lax.fori_loop(0, n_chunks // RWKV_LOCAL_UNROLL, local_body, 0)

    def carried_part(c, state, d):
        rows = pl.ds(pl.multiple_of(c * L, L), L)
        ws = _nt_dot(jnp.concatenate([wmat_s[d, rows, :], rt_s[d, rows, :]], axis=0), block_diag(state.astype(bf16)))
        u = ws[:L] + uloc_s[d, rows, :].astype(f32)
        u_b = u.astype(bf16)
        y_s[d, rows, :] = (ws[L:] + avq_s[d, rows, :].astype(f32)
                           - jnp.dot(aqb_s[d, rows, :], block_diag(u_b), preferred_element_type=f32))
        full = _tn_dot(jnp.concatenate([v_s[rows, :], -u_b], axis=0),
                       jnp.concatenate([kend_s[d, rows, :], bend_s[d, rows, :]], axis=0))
        new_state = state * ptot_s[d, pl.ds(pl.multiple_of(c * 8, 8), 8), :][0:1, :]
        for h, m in enumerate(head_masks):
            new_state = new_state + jnp.where(m, full[h * B_HEAD:(h + 1) * B_HEAD, :], 0.0)
        return new_state

    def carried_body(c, states):
        return carried_part(c, states[0], 0), carried_part(n_chunks - 1 - c, states[1], 1)

    zero_state = jnp.zeros((B_HEAD, W), f32)
    lax.fori_loop(0, n_chunks, carried_body, (zero_state, zero_state))

    def finish(i, carry):
        rows = pl.ds(pl.multiple_of(i * RT, RT), RT)
        y = y_s[0, rows, :] + y_s[1, rows, :]
        mean = seg_sum(y) * (1.0 / B_HEAD)
        yc = y - mean
        var = seg_sum(yc * yc) * (1.0 / B_HEAD)
        yn = yc * lax.rsqrt(var + B_GN_EPS) * lng_ref[...] + lnb_ref[...]
        kh_both = kh_s[0, rows, :].astype(f32) + kh_s[1, rows, :].astype(f32)
        bonus = seg_sum(r_s[rows, :] * kh_both * rk_ref[...]) * v_s[rows, :].astype(f32)
        o_ref[rows, :] = ((yn + bonus) * g_s[rows, :].astype(f32)).astype(o_ref.dtype)
        return carry

    lax.fori_loop(0, n_prep, finish, 0)


def _rwkv7_mixer(x, w_in, mu, w0, w_up, a0, a_up, k_k, k_a, r_k, g_up, lnx_g, lnx_b):
    bsz, s, d = x.shape
    n_proj = w_in.shape[1]
    W = RWKV_LANES
    assert d % W == 0 and s % RWKV_PREP_ROWS == 0 and B_GATE_LORA == LANES
    assert 2 * B_DECAY_LORA == LANES and 2 * B_AAA_LORA == LANES
    p = _project(x.reshape(bsz * s, d), w_in, jnp.zeros((n_proj,), jnp.float32), jnp.bfloat16)
    p = p.reshape(bsz, s, n_proj)
    n_groups = d // W
    gw = d // W
    lora0 = 3 * d // LANES

    def seq_w(off):
        return pl.BlockSpec((None, s, W), lambda b, g, off=off: (b, 0, off + g))

    def seq_n(idx):
        return pl.BlockSpec((None, s, LANES), lambda b, g, idx=idx: (b, 0, idx))

    def vec_w(off):
        return pl.BlockSpec((1, W), lambda b, g, off=off: (0, off + g))

    def vec_n(idx):
        return pl.BlockSpec((1, LANES), lambda b, g, idx=idx: (0, idx))

    par_w = pl.BlockSpec((1, W), lambda b, g: (0, g))
    two_w = pl.BlockSpec((2, W), lambda b, g: (0, g))
    up_w = pl.BlockSpec((2, B_DECAY_LORA, W), lambda b, g: (0, 0, g))
    mu2 = mu.reshape(1, n_proj)
    row = lambda a: a.reshape(1, d)
    f32_w = pltpu.VMEM((s, W), jnp.float32)
    b16_w = pltpu.VMEM((s, W), jnp.bfloat16)
    f32_2w = pltpu.VMEM((2, s, W), jnp.float32)
    b16_2w = pltpu.VMEM((2, s, W), jnp.bfloat16)
    decay_rows = pltpu.VMEM((2, 8 * (s // RWKV_CHUNK), W), jnp.float32)
    return pl.pallas_call(
        _rwkv_scan_kernel,
        out_shape=jax.ShapeDtypeStruct((bsz, s, d), jnp.bfloat16),
        grid=(bsz, n_groups),
        in_specs=[seq_w(0), seq_w(gw), seq_w(2 * gw), seq_n(lora0), seq_n(lora0 + 1), seq_n(lora0 + 2),
                  vec_w(0), vec_w(gw), vec_w(2 * gw), vec_n(lora0), vec_n(lora0 + 1), vec_n(lora0 + 2),
                  two_w, up_w, two_w, up_w, par_w, par_w, par_w,
                  pl.BlockSpec((B_GATE_LORA, W), lambda b, g: (0, g)), par_w, par_w],
        out_specs=pl.BlockSpec((None, s, W), lambda b, g: (b, 0, g)),
        scratch_shapes=[f32_w, b16_w, f32_w, b16_w, f32_2w, b16_2w, b16_2w, f32_2w,
                        b16_2w, b16_2w, b16_2w, b16_2w, b16_2w, b16_2w, b16_2w, decay_rows],
        compiler_params=_compiler_params(("parallel", "parallel")),
        name="rwkv7_scan",
    )(p, p, p, p, p, p, mu2, mu2, mu2, mu2, mu2, mu2,
      w0, w_up.astype(jnp.bfloat16), a0, a_up.astype(jnp.bfloat16), row(k_k), row(k_a), row(r_k),
      g_up.astype(jnp.bfloat16), row(lnx_g), row(lnx_b))


def _axial_rope_tables(s):
    rows = s // GRID_W
    row = jnp.repeat(jnp.arange(rows, dtype=jnp.float32), GRID_W)
    col = jnp.tile(jnp.arange(GRID_W, dtype=jnp.float32), rows)
    n_freq = C_HEAD_DIM // 4
    inv_freq = ROPE_THETA ** (-jnp.arange(n_freq, dtype=jnp.float32) / n_freq)
    ang = jnp.concatenate([row[:, None] * inv_freq, col[:, None] * inv_freq], axis=-1)
    return jnp.cos(ang), jnp.sin(ang)


ATTN_Q_TILE = 256


def _rms_rope(x, gain, cos_f, sin_f):
    xn = x * lax.rsqrt(jnp.mean(x * x, axis=-1, keepdims=True) + RMS_EPS) * gain
    return xn * cos_f + pltpu.roll(xn, C_HEAD_DIM // 2, 1) * sin_f


def _attn_kernel(q_ref, k_ref, v_ref, cq_ref, sq_ref, ck_ref, sk_ref, qg_ref, kg_ref, o_ref, kr_s):
    f32, bf16 = jnp.float32, jnp.bfloat16

    @pl.when(pl.program_id(2) == 0)
    def _():
        kr_s[...] = _rms_rope(k_ref[...].astype(f32), kg_ref[...], ck_ref[...], sk_ref[...]).astype(bf16)

    group = q_ref.shape[1] // C_HEAD_DIM
    for g in range(group):
        cols = slice(g * C_HEAD_DIM, (g + 1) * C_HEAD_DIM)
        q = _rms_rope(q_ref[:, cols].astype(f32), qg_ref[...], cq_ref[...], sq_ref[...]) * C_HEAD_DIM ** -0.5
        scores = _nt_dot(q.astype(bf16), kr_s[...])
        p = jnp.exp(scores - jnp.max(scores, axis=-1, keepdims=True))
        denom = jnp.sum(p, axis=-1, keepdims=True)
        o = jnp.dot(p.astype(bf16), v_ref[...], preferred_element_type=f32)
        o_ref[:, cols] = (o / denom).astype(o_ref.dtype)


def _axial_gqa_mixer(x, w_in, q_gain, k_gain):
    bsz, s, d = x.shape
    dh = C_HEAD_DIM
    qh = d // dh
    group = qh // C_KV_HEADS
    n_proj = w_in.shape[1]
    half = jnp.concatenate([jnp.arange(0, dh, 2), jnp.arange(1, dh, 2)])
    n_rot = (qh + C_KV_HEADS) * dh
    w_rot = w_in[:, :n_rot].reshape(d, n_rot // dh, dh // 2, 2).transpose(0, 1, 3, 2).reshape(d, n_rot)
    w = jnp.concatenate([w_rot, w_in[:, n_rot:]], axis=1)
    p = _project(x.reshape(bsz * s, d), w, jnp.zeros((n_proj,), jnp.float32), jnp.bfloat16)
    p = p.reshape(bsz, s, n_proj)
    cos, sin = _axial_rope_tables(s)
    cos_f = jnp.concatenate([cos, cos], axis=-1)
    sin_f = jnp.concatenate([-sin, sin], axis=-1)
    tq = min(ATTN_Q_TILE, s)
    gw = group * dh
    q_rows = pl.BlockSpec((tq, dh), lambda b, h, i: (i, 0))
    k_rows = pl.BlockSpec((s, dh), lambda b, h, i: (0, 0))
    vec = pl.BlockSpec((1, dh), lambda b, h, i: (0, 0))
    return pl.pallas_call(
        _attn_kernel,
        out_shape=jax.ShapeDtypeStruct((bsz, s, d), jnp.bfloat16),
        grid=(bsz, C_KV_HEADS, s // tq),
        in_specs=[pl.BlockSpec((None, tq, gw), lambda b, h, i: (b, i, h)),
                  pl.BlockSpec((None, s, dh), lambda b, h, i: (b, 0, qh + h)),
                  pl.BlockSpec((None, s, dh), lambda b, h, i: (b, 0, qh + C_KV_HEADS + h)),
                  q_rows, q_rows, k_rows, k_rows, vec, vec],
        out_specs=pl.BlockSpec((None, tq, gw), lambda b, h, i: (b, i, h)),
        scratch_shapes=[pltpu.VMEM((s, dh), jnp.bfloat16)],
        compiler_params=_compiler_params(("parallel", "parallel", "arbitrary")),
        name="axial_attention",
    )(p, p, p, cos_f, sin_f, cos_f, sin_f, q_gain[half].reshape(1, dh), k_gain[half].reshape(1, dh))


def kernel(x, a_w_in, a_b_in, a_head_gain, a_w_out, b_w_in, b_mu, b_w0, b_w_up, b_a0, b_a_up, b_k_k, b_k_a, b_r_k, b_g_up, b_lnx_g, b_lnx_b, b_w_out, c_w_in, c_q_gain, c_k_gain, c_w_out, ln1_g, ln1_b, moe_w_router, moe_b_router, moe_w_gu, moe_b_gu, moe_w_dn, moe_b_dn, ln2_g, ln2_b):
    bsz, s, d = x.shape
    depth = ln1_g.shape[0]
    xt = x.reshape(bsz * s, d)
    for i in range(depth):
        kind = i % N_MIXERS
        j = i // N_MIXERS
        x3 = xt.reshape(bsz, s, d)
        if kind == 0:
            h = _mlstm_mixer(x3, a_w_in[j], a_b_in[j], a_head_gain[j])
            w_out = a_w_out[j]
        elif kind == 1:
            h = _rwkv7_mixer(x3, b_w_in[j], b_mu[j], b_w0[j], b_w_up[j], b_a0[j], b_a_up[j],
                             b_k_k[j], b_k_a[j], b_r_k[j], b_g_up[j], b_lnx_g[j], b_lnx_b[j])
            w_out = b_w_out[j]
        else:
            h = _axial_gqa_mixer(x3, c_w_in[j], c_q_gain[j], c_k_gain[j])
            w_out = c_w_out[j]
        xt, xt_bf16 = _outproj_ln(h.reshape(bsz * s, d), w_out, xt, ln1_g[i], ln1_b[i])
        y_rows, route = _moe_ffn(xt, xt_bf16, moe_w_router[i], moe_b_router[i], moe_w_gu, i, moe_b_gu[i],
                                 moe_w_dn[i], moe_b_dn[i])
        xt = _combine_ln(y_rows, route, xt, ln2_g[i], ln2_b[i])
    return xt.reshape(bsz, s, d)
```

```python
import functools

import jax
import jax.numpy as jnp
from jax import lax
from jax.experimental import pallas as pl
from jax.experimental.pallas import tpu as pltpu

DEPTH = 4
N_MIXERS = 3
GRID_W = 64
DEEPNORM_ALPHA = (2 * DEPTH) ** 0.25
LN_EPS = 1e-5
RMS_EPS = 1e-6

A_HEADS = 8
A_DQK = 64
A_CHUNK = 64
A_M_INIT = -1e30

B_HEAD = 64
B_DECAY_LORA = 64
B_AAA_LORA = 64
B_GATE_LORA = 128
B_GN_EPS = 64e-5

C_HEAD_DIM = 128
C_KV_HEADS = 2
C_Q_BLOCK = 128
ROPE_THETA = 10000.0

N_EXPERTS = 32
TOP_K = 4
SWIGLU_LIMIT = 7.0
SWIGLU_ALPHA = 1.702

LANES = 128
VMEM_LIMIT_BYTES = 56 * 1024 * 1024

ROW_TILE = 512
MOE_ROW_TILE = 512
MOE_TOKEN_GROUPS = 2


def _compiler_params(semantics):
    return pltpu.CompilerParams(dimension_semantics=semantics, vmem_limit_bytes=VMEM_LIMIT_BYTES)


def _row_tile(m):
    t = min(ROW_TILE, m)
    assert m % t == 0, (m, t)
    return t


def _col_chunk(n):
    return max(c for c in range(LANES, 5 * LANES + 1, LANES) if n % c == 0)


def _proj_kernel(x_ref, w_ref, b_ref, *o_refs):
    xb = x_ref[...].astype(jnp.bfloat16)
    col = 0
    for o_ref in o_refs:
        n = o_ref.shape[1]
        step = _col_chunk(n)
        for j in range(0, n, step):
            acc = jnp.dot(xb, w_ref[:, col + j:col + j + step], preferred_element_type=jnp.float32)
            o_ref[:, j:j + step] = (acc + b_ref[:, col + j:col + j + step]).astype(o_ref.dtype)
        col += n


def _project(x, w, b, out_dtype, f32_tail=0):
    m, k = x.shape
    n = w.shape[1]
    widths = [n - f32_tail, f32_tail] if f32_tail else [n]
    dtypes = [out_dtype, jnp.float32]
    assert all(c % LANES == 0 for c in widths)
    tm = _row_tile(m)
    outs = pl.pallas_call(
        _proj_kernel,
        out_shape=[jax.ShapeDtypeStruct((m, c), dt) for c, dt in zip(widths, dtypes)],
        grid=(m // tm,),
        in_specs=[pl.BlockSpec((tm, k), lambda i: (i, 0)),
                  pl.BlockSpec((k, n), lambda i: (0, 0)),
                  pl.BlockSpec((1, n), lambda i: (0, 0))],
        out_specs=[pl.BlockSpec((tm, c), lambda i: (i, 0)) for c in widths],
        compiler_params=_compiler_params(("parallel",)),
        name="project",
    )(x, w.astype(jnp.bfloat16), b.reshape(1, n).astype(jnp.float32))
    return tuple(outs) if f32_tail else outs[0]


def _pad_cols(w, b, n_pad):
    k, n = w.shape
    if b is None:
        b = jnp.zeros((n,), jnp.float32)
    return jnp.pad(w, ((0, 0), (0, n_pad - n))), jnp.pad(b, (0, n_pad - n))


def _layer_norm_rows(z, g, b):
    mu = jnp.mean(z, axis=-1, keepdims=True)
    zc = z - mu
    var = jnp.mean(zc * zc, axis=-1, keepdims=True)
    return zc * lax.rsqrt(var + LN_EPS) * g + b


def _outproj_ln_kernel(h_ref, w_ref, x_ref, g_ref, b_ref, o_ref, ob_ref):
    mix = jnp.dot(h_ref[...].astype(jnp.bfloat16), w_ref[...], preferred_element_type=jnp.float32)
    z = DEEPNORM_ALPHA * x_ref[...] + mix
    y = _layer_norm_rows(z, g_ref[...], b_ref[...])
    o_ref[...] = y
    ob_ref[...] = y.astype(ob_ref.dtype)


def _outproj_ln(h, w_out, x, g, b):
    m, d = x.shape
    tm = _row_tile(m)
    row = pl.BlockSpec((tm, d), lambda i: (i, 0))
    vec = pl.BlockSpec((1, d), lambda i: (0, 0))
    return pl.pallas_call(
        _outproj_ln_kernel,
        out_shape=[jax.ShapeDtypeStruct((m, d), jnp.float32), jax.ShapeDtypeStruct((m, d), jnp.bfloat16)],
        grid=(m // tm,),
        in_specs=[row, pl.BlockSpec((d, d), lambda i: (0, 0)), row, vec, vec],
        out_specs=[row, row],
        compiler_params=_compiler_params(("parallel",)),
        name="outproj_ln",
    )(h, w_out.astype(jnp.bfloat16), x, g.reshape(1, d), b.reshape(1, d))


ROUTE_IDX_LANE = 0
ROUTE_RANK_LANE = TOP_K
ROUTE_GATE_LANE = 2 * TOP_K


def _combine_ln_kernel(y_ref, r_ref, x_ref, g_ref, b_ref, *rest):
    o_ref = rest[-1]
    y = jnp.zeros(x_ref.shape, jnp.float32)
    for k in range(TOP_K):
        gate = r_ref[:, ROUTE_GATE_LANE + k:ROUTE_GATE_LANE + k + 1]
        y = y + gate * y_ref[k].astype(jnp.float32)
    z = DEEPNORM_ALPHA * x_ref[...] + y
    o_ref[...] = _layer_norm_rows(z, g_ref[...], b_ref[...])


def _combine_ln(y_rows, route, x, group, g, b, earlier=None):
    m = y_rows.shape[1]
    d = x.shape[1]
    tm = _row_tile(m)
    first = group * (m // tm)
    row = pl.BlockSpec((tm, d), lambda i: (first + i, 0))
    vec = pl.BlockSpec((1, d), lambda i: (0, 0))
    in_specs = [pl.BlockSpec((TOP_K, tm, d), lambda i: (0, i, 0)),
                pl.BlockSpec((tm, LANES), lambda i: (i, 0)), row, vec, vec]
    operands = [y_rows, route, x, g.reshape(1, d), b.reshape(1, d)]
    aliases = {}
    if earlier is not None:
        in_specs.append(pl.BlockSpec(memory_space=pl.ANY))
        operands.append(earlier)
        aliases = {len(operands) - 1: 0}
    return pl.pallas_call(
        _combine_ln_kernel,
        out_shape=jax.ShapeDtypeStruct(x.shape, jnp.float32),
        grid=(m // tm,),
        in_specs=in_specs,
        out_specs=row,
        input_output_aliases=aliases,
        compiler_params=_compiler_params(("parallel",)),
        name="combine_ln",
    )(*operands)


ROUTER_PAD_BIAS = -1e30


def _router_kernel(x_ref, whi_ref, wlo_ref, b_ref, o_ref, cnt_ref, base_s):
    f32 = jnp.float32

    @pl.when(pl.program_id(0) == 0)
    def _():
        base_s[...] = jnp.zeros_like(base_s)

    t = x_ref.shape[0]
    x = x_ref[...]
    x_hi = x.astype(jnp.bfloat16)
    x_lo = (x - x_hi.astype(f32)).astype(jnp.bfloat16)
    logits = (jnp.dot(x_hi, whi_ref[...], preferred_element_type=f32)
              + jnp.dot(x_hi, wlo_ref[...], preferred_element_type=f32)
              + jnp.dot(x_lo, whi_ref[...], preferred_element_type=f32)) + b_ref[...]
    lane = lax.broadcasted_iota(jnp.int32, (t, LANES), 1)
    vals = logits
    tops, idxs, sels = [], [], []
    for _ in range(TOP_K):
        top = jnp.max(vals, axis=-1, keepdims=True)
        idx = jnp.min(jnp.where(vals == top, lane, LANES), axis=-1, keepdims=True)
        sel = lane == idx
        vals = jnp.where(sel, -jnp.inf, vals)
        tops.append(top)
        idxs.append(idx)
        sels.append(sel)
    exps = [jnp.exp(top - tops[0]) for top in tops]
    total = exps[0]
    for e in exps[1:]:
        total = total + e
    chosen = jnp.zeros((t, LANES), f32)
    for sel in sels:
        chosen = chosen + sel.astype(f32)
    earlier = (lax.broadcasted_iota(jnp.int32, (t, t), 0) > lax.broadcasted_iota(jnp.int32, (t, t), 1))
    prefix = jnp.dot(earlier.astype(jnp.bfloat16), chosen.astype(jnp.bfloat16),
                     preferred_element_type=f32) + base_s[...]
    packed = jnp.zeros((t, LANES), f32)
    for k in range(TOP_K):
        rank = jnp.sum(jnp.where(sels[k], prefix, 0.0), axis=-1, keepdims=True)
        packed = jnp.where(lane == ROUTE_IDX_LANE + k, idxs[k].astype(f32), packed)
        packed = jnp.where(lane == ROUTE_RANK_LANE + k, rank, packed)
        packed = jnp.where(lane == ROUTE_GATE_LANE + k, exps[k] / total, packed)
    o_ref[...] = packed
    base_s[...] = base_s[...] + jnp.sum(chosen, axis=0, keepdims=True)
    cnt_ref[...] = base_s[...]


def _route(xt, group, m, w_router, b_router):
    d = xt.shape[1]
    tm = _row_tile(m)
    first = group * (m // tm)
    n_e = w_router.shape[1]
    w = jnp.pad(w_router, ((0, 0), (0, LANES - n_e)))
    b = jnp.pad(b_router, (0, LANES - n_e), constant_values=ROUTER_PAD_BIAS)
    w_hi = w.astype(jnp.bfloat16)
    return pl.pallas_call(
        _router_kernel,
        out_shape=[jax.ShapeDtypeStruct((m, LANES), jnp.float32), jax.ShapeDtypeStruct((1, LANES), jnp.float32)],
        grid=(m // tm,),
        in_specs=[pl.BlockSpec((tm, d), lambda i: (first + i, 0)),
                  pl.BlockSpec((d, LANES), lambda i: (0, 0)),
                  pl.BlockSpec((d, LANES), lambda i: (0, 0)),
                  pl.BlockSpec((1, LANES), lambda i: (0, 0))],
        out_specs=[pl.BlockSpec((tm, LANES), lambda i: (i, 0)), pl.BlockSpec((1, LANES), lambda i: (0, 0))],
        scratch_shapes=[pltpu.VMEM((1, LANES), jnp.float32)],
        compiler_params=_compiler_params(("arbitrary",)),
        name="router",
    )(xt, w_hi, (w - w_hi.astype(jnp.float32)).astype(jnp.bfloat16), b.reshape(1, LANES))


GLU_BLOCK = 2 * LANES


def _regroup_kernel(w_ref, p_ref, o_ref):
    w = w_ref[0].astype(jnp.bfloat16)
    for c in range(0, w.shape[1], GLU_BLOCK):
        o_ref[0, :, c:c + GLU_BLOCK] = jnp.dot(w[:, c:c + GLU_BLOCK], p_ref[...],
                                               preferred_element_type=jnp.float32).astype(o_ref.dtype)


def _regroup_glu_columns(w_gu_layers, layer):
    _, n_e, d, f2 = w_gu_layers.shape
    src = jnp.arange(GLU_BLOCK)
    perm = (src[:, None] == (2 * (src % LANES) + src // LANES)[None, :]).astype(jnp.bfloat16)
    tk = _row_tile(d)
    return pl.pallas_call(
        _regroup_kernel,
        out_shape=jax.ShapeDtypeStruct((n_e, d, f2), jnp.bfloat16),
        grid=(n_e, d // tk),
        in_specs=[pl.BlockSpec((None, 1, tk, f2), lambda e, i: (layer, e, i, 0)),
                  pl.BlockSpec((GLU_BLOCK, GLU_BLOCK), lambda e, i: (0, 0))],
        out_specs=pl.BlockSpec((1, tk, f2), lambda e, i: (e, i, 0)),
        compiler_params=_compiler_params(("parallel", "parallel")),
        name="regroup_glu",
    )(w_gu_layers, perm)


def _slot_kernel(r_ref, start_ref, o_ref):
    lane = lax.broadcasted_iota(jnp.int32, r_ref.shape, 1)
    route = r_ref[...]
    slots = jnp.zeros(r_ref.shape, jnp.float32)
    for k in range(TOP_K):
        expert = route[:, ROUTE_IDX_LANE + k:ROUTE_IDX_LANE + k + 1].astype(jnp.int32)
        first = jnp.sum(jnp.where(lane == expert, start_ref[...], 0.0), axis=-1, keepdims=True)
        slots = jnp.where(lane == k, first + route[:, ROUTE_RANK_LANE + k:ROUTE_RANK_LANE + k + 1], slots)
    o_ref[...] = slots.astype(jnp.int32)


def _slots(route, expert_start):
    m = route.shape[0]
    tm = _row_tile(m)
    return pl.pallas_call(
        _slot_kernel,
        out_shape=jax.ShapeDtypeStruct((m, LANES), jnp.int32),
        grid=(m // tm,),
        in_specs=[pl.BlockSpec((tm, LANES), lambda i: (i, 0)), pl.BlockSpec((1, LANES), lambda i: (0, 0))],
        out_specs=pl.BlockSpec((tm, LANES), lambda i: (i, 0)),
        compiler_params=_compiler_params(("parallel",)),
        name="moe_slots",
    )(route, expert_start)


def _regroup_glu_bias(b_gu):
    n_e, f2 = b_gu.shape
    return b_gu.reshape(n_e, f2 // GLU_BLOCK, LANES, 2).transpose(0, 1, 3, 2).reshape(n_e, 1, f2)


def _expert_kernel(be_ref, nb_ref, x_ref, wgu_ref, bgu_ref, wd_ref, bd_ref, o_ref):
    @pl.when(pl.program_id(0) < nb_ref[0])
    def _():
        xb = x_ref[...]
        acts = []
        for c in range(0, wgu_ref.shape[2], GLU_BLOCK):
            h = (jnp.dot(xb, wgu_ref[0, :, c:c + GLU_BLOCK], preferred_element_type=jnp.float32)
                 + bgu_ref[0, :, c:c + GLU_BLOCK])
            hg = jnp.minimum(h[:, :LANES], SWIGLU_LIMIT)
            hl = jnp.clip(h[:, LANES:], -SWIGLU_LIMIT, SWIGLU_LIMIT)
            acts.append((hg * jax.nn.sigmoid(SWIGLU_ALPHA * hg) * (hl + 1.0)).astype(jnp.bfloat16))
        act = jnp.concatenate(acts, axis=1)
        y = jnp.dot(act, wd_ref[0], preferred_element_type=jnp.float32) + bd_ref[0]
        o_ref[...] = y.astype(o_ref.dtype)


def _expert_ffn(x_sorted, block_expert, n_used, w_gu, b_gu, w_dn, b_dn):
    cap, d = x_sorted.shape
    f2 = w_gu.shape[2]
    tm = MOE_ROW_TILE
    n_blocks = cap // tm

    def blk(i, be, nb):
        return (jnp.minimum(i, nb[0] - 1), 0)

    def wsel(i, be, nb):
        return (be[i], 0, 0)

    grid_spec = pltpu.PrefetchScalarGridSpec(
        num_scalar_prefetch=2,
        grid=(n_blocks,),
        in_specs=[pl.BlockSpec((tm, d), blk),
                  pl.BlockSpec((1, d, f2), wsel),
                  pl.BlockSpec((1, 1, f2), wsel),
                  pl.BlockSpec((1, f2 // 2, d), wsel),
                  pl.BlockSpec((1, 1, d), wsel)],
        out_specs=pl.BlockSpec((tm, d), blk),
    )
    return pl.pallas_call(
        _expert_kernel,
        out_shape=jax.ShapeDtypeStruct((cap, d), jnp.bfloat16),
        grid_spec=grid_spec,
        compiler_params=_compiler_params(("arbitrary",)),
        name="expert_ffn",
    )(block_expert, n_used, x_sorted, w_gu, b_gu, w_dn, b_dn)


def _moe_ffn(x, x_bf16, group, n, w_router, b_router, w_gu, b_gu, w_dn, b_dn):
    d = x.shape[1]
    tm = MOE_ROW_TILE
    n_e = w_router.shape[1]
    route, counts = _route(x, group, n, w_router, b_router)
    counts = counts[0].astype(jnp.int32)
    padded = (counts + tm - 1) // tm * tm
    pad_end = jnp.cumsum(padded)
    pad_start = pad_end - padded
    cap = n * TOP_K + n_e * tm
    n_blocks = cap // tm
    slot = _slots(route, pad_start.astype(jnp.float32).reshape(1, LANES))[:, :TOP_K]
    slot_kmajor = slot.T.reshape(-1)
    tok_of_slot = jnp.full((cap,), group * n, jnp.int32).at[slot_kmajor].set(
        group * n + jnp.tile(jnp.arange(n, dtype=jnp.int32), TOP_K), mode="promise_in_bounds", unique_indices=True)
    block_first_row = jnp.arange(n_blocks, dtype=jnp.int32) * tm
    block_expert = jnp.minimum(jnp.sum(pad_end[None, :n_e] <= block_first_row[:, None], axis=1), n_e - 1)
    n_used = (pad_end[n_e - 1] // tm).reshape(1)

    x_sorted = x_bf16.at[tok_of_slot].get(mode="promise_in_bounds")
    yb = _expert_ffn(x_sorted, block_expert.astype(jnp.int32), n_used.astype(jnp.int32), w_gu, b_gu, w_dn, b_dn)
    return yb.at[slot_kmajor].get(mode="promise_in_bounds").reshape(TOP_K, n, d), route


MLSTM_CHUNK = 128
MLSTM_PAIR = 2
MLSTM_NEG = -1e30


def _interleave(generators):
    results = [None] * len(generators)
    live = list(enumerate(generators))
    while live:
        still = []
        for idx, gen in live:
            try:
                next(gen)
                still.append((idx, gen))
            except StopIteration as stop:
                results[idx] = stop.value
        live = still
    return results


def _one_by_one(generators):
    return [_interleave([gen])[0] for gen in generators]


def _cumsum_rows(x, reverse):
    n = x.shape[0]
    row = lax.broadcasted_iota(jnp.int32, x.shape, 0)
    sh = 1
    while sh < n:
        if reverse:
            x = x + jnp.where(row < n - sh, pltpu.roll(x, n - sh, 0), 0.0)
        else:
            x = x + jnp.where(row >= sh, pltpu.roll(x, sh, 0), 0.0)
        sh *= 2
    return x


def _cummax_rows(x, reverse):
    n = x.shape[0]
    row = lax.broadcasted_iota(jnp.int32, x.shape, 0)
    sh = 1
    while sh < n:
        if reverse:
            x = jnp.maximum(x, jnp.where(row < n - sh, pltpu.roll(x, n - sh, 0), -jnp.inf))
        else:
            x = jnp.maximum(x, jnp.where(row >= sh, pltpu.roll(x, sh, 0), -jnp.inf))
        sh *= 2
    return x


def _mlstm_kernel(q_ref, k_ref, v_ref, og_ref, g_ref, gain_ref, o_ref, h_s, c_s):
    f32, bf16 = jnp.float32, jnp.bfloat16
    seq = q_ref.shape[0]
    L = MLSTM_CHUNK
    n_chunks = seq // L
    dv = v_ref.shape[1] // MLSTM_PAIR
    dqk = q_ref.shape[1] // MLSTM_PAIR

    lane_q = lax.broadcasted_iota(jnp.int32, (1, q_ref.shape[1]), 1) // dqk
    lane_g = lax.broadcasted_iota(jnp.int32, (1, LANES), 1)
    is_forget_lane = (lane_g // MLSTM_PAIR) % 2 == 1
    row = lax.broadcasted_iota(jnp.int32, (L, L), 0)
    col = lax.broadcasted_iota(jnp.int32, (L, L), 1)
    causal = (col <= row, col >= row)
    ones_v = jnp.ones((L, dv), bf16)

    c_s[...] = jnp.zeros_like(c_s)

    def gate_terms(start, d):
        rows = pl.ds(start, L)
        gates = g_ref[rows, :]
        log_f = jnp.minimum(gates, 0.0) - jnp.log(1.0 + jnp.exp(-jnp.abs(gates)))
        cum = _cumsum_rows(log_f, reverse=(d == 1))
        src = gates - pltpu.roll(cum, LANES - MLSTM_PAIR, 1)
        run_max = _cummax_rows(src, reverse=(d == 1))
        return rows, gates, cum, src.T, run_max, q_ref[rows, :], k_ref[rows, :]

    def head_unit(terms, d, j, m_st):
        rows, gates, cum, src_t, run_max, q_all, k_all = terms
        lane_i = 2 * MLSTM_PAIR * d + j
        lane_f = lane_i + MLSTM_PAIR
        b_col, i_col = cum[:, lane_f:lane_f + 1], gates[:, lane_i:lane_i + 1]
        src_row = src_t[lane_i:lane_i + 1, :]
        qz = jnp.where(lane_q == j, q_all, jnp.zeros_like(q_all))
        kz = jnp.where(lane_q == j, k_all, jnp.zeros_like(k_all))
        state = c_s[2 * d + j]
        scores = _nt_dot(qz, kz)
        carried = jnp.dot(qz, state.astype(bf16), preferred_element_type=f32)
        yield
        mu = jnp.maximum(m_st, run_max[:, lane_i:lane_i + 1])
        v_aug = jnp.concatenate([v_ref[rows, j * dv:(j + 1) * dv], ones_v], axis=1)
        qk = scores * (A_DQK ** -0.5) * jnp.exp(jnp.where(causal[d], src_row - mu, MLSTM_NEG))
        edge = L - 1 if d == 0 else 0
        b_last = b_col[edge:edge + 1, :]
        m_new = b_last + mu[edge:edge + 1, :]
        w_s = jnp.exp(b_last - b_col + i_col - m_new)
        local = jnp.dot(qk.astype(bf16), v_aug, preferred_element_type=f32)
        update = _tn_dot((w_s * kz.astype(f32)).astype(bf16), v_aug)
        yield
        nd = local + jnp.exp(m_st - mu) * (A_DQK ** -0.5) * carried
        h_s[d, rows, j * dv:(j + 1) * dv] = nd[:, :dv] / jnp.maximum(jnp.abs(nd[:, dv:]), jnp.exp(-b_col - mu))
        c_s[2 * d + j] = jnp.exp(b_last + m_st - m_new) * state + update
        return m_new

    def body(c, m_all):
        terms = (gate_terms(pl.multiple_of(c * L, L), 0), gate_terms(pl.multiple_of((n_chunks - 1 - c) * L, L), 1))
        return tuple(_one_by_one([head_unit(terms[d], d, j, m_all[MLSTM_PAIR * d + j])
                                  for d in range(2) for j in range(MLSTM_PAIR)]))

    m_init = tuple(jnp.full((1, 1), A_M_INIT, f32) for _ in range(2 * MLSTM_PAIR))
    lax.fori_loop(0, n_chunks, body, m_init)

    def finish(c, carry):
        rows = pl.ds(pl.multiple_of(c * L, L), L)
        for j in range(MLSTM_PAIR):
            cols = slice(j * dv, (j + 1) * dv)
            h = h_s[0, rows, cols] + h_s[1, rows, cols]
            hn = h * lax.rsqrt(jnp.mean(h * h, axis=-1, keepdims=True) + RMS_EPS) * gain_ref[:, cols]
            o_ref[rows, cols] = (hn * jax.nn.sigmoid(og_ref[rows, cols].astype(f32))).astype(o_ref.dtype)
        return carry

    lax.fori_loop(0, n_chunks, finish, 0)


def _mlstm_mixer(x, w_in, b_in, head_gain):
    bsz, s, d = x.shape
    a_qk = A_HEADS * A_DQK
    dv = d // A_HEADS
    n_main = 2 * a_qk + 2 * d
    n_pairs = A_HEADS // MLSTM_PAIR
    pw_qk = MLSTM_PAIR * A_DQK
    pw_v = MLSTM_PAIR * dv
    assert pw_qk == LANES and s % MLSTM_CHUNK == 0
    gate_cols = jnp.asarray([[n_main + t * A_HEADS + MLSTM_PAIR * hp + j for t in range(4) for j in range(MLSTM_PAIR)]
                             for hp in range(n_pairs)])
    n_gate = gate_cols.shape[1]
    w_tail = jnp.pad(w_in[:, gate_cols], ((0, 0), (0, 0), (0, LANES - n_gate))).reshape(d, n_pairs * LANES)
    b_tail = jnp.pad(b_in[gate_cols], ((0, 0), (0, LANES - n_gate))).reshape(n_pairs * LANES)
    w = jnp.concatenate([w_in[:, :n_main], w_tail], axis=1)
    b = jnp.concatenate([b_in[:n_main], b_tail])
    p, gates = _project(x.reshape(bsz * s, d), w, b, jnp.bfloat16, f32_tail=n_pairs * LANES)
    p = p.reshape(bsz, s, n_main)
    gates = gates.reshape(bsz, s, n_pairs * LANES)
    k0 = a_qk // pw_qk
    v0 = 2 * a_qk // pw_v
    o0 = (2 * a_qk + d) // pw_v
    return pl.pallas_call(
        _mlstm_kernel,
        out_shape=jax.ShapeDtypeStruct((bsz, s, d), jnp.bfloat16),
        grid=(bsz, n_pairs),
        in_specs=[pl.BlockSpec((None, s, pw_qk), lambda b, h: (b, 0, h)),
                  pl.BlockSpec((None, s, pw_qk), lambda b, h: (b, 0, k0 + h)),
                  pl.BlockSpec((None, s, pw_v), lambda b, h: (b, 0, v0 + h)),
                  pl.BlockSpec((None, s, pw_v), lambda b, h: (b, 0, o0 + h)),
                  pl.BlockSpec((None, s, LANES), lambda b, h: (b, 0, h)),
                  pl.BlockSpec((1, pw_v), lambda b, h: (0, h))],
        out_specs=pl.BlockSpec((None, s, pw_v), lambda b, h: (b, 0, h)),
        scratch_shapes=[pltpu.VMEM((2, s, pw_v), jnp.float32),
                        pltpu.VMEM((2 * MLSTM_PAIR, pw_qk, 2 * dv), jnp.float32)],
        compiler_params=_compiler_params(("parallel", "parallel")),
        name="mlstm_scan",
    )(p, p, p, p, gates, head_gain.reshape(1, d))


RWKV_CHUNK = 64
RWKV_GROUP = 4
RWKV_LANES = RWKV_GROUP * B_HEAD
RWKV_PREP_ROWS = 256
RWKV_LOCAL_UNROLL = 4


def _f32_dot(a, b_bf16):
    hi = a.astype(jnp.bfloat16)
    lo = (a - hi.astype(jnp.float32)).astype(jnp.bfloat16)
    return (jnp.dot(hi, b_bf16, preferred_element_type=jnp.float32)
            + jnp.dot(lo, b_bf16, preferred_element_type=jnp.float32))


def _nt_dot(a, b):
    return lax.dot_general(a, b, (((1,), (1,)), ((), ())), preferred_element_type=jnp.float32)


def _tn_dot(a, b):
    return lax.dot_general(a, b, (((0,), (0,)), ((), ())), preferred_element_type=jnp.float32)


def _rwkv_scan_kernel(r_ref, k_ref, v_ref, wl_ref, al_ref, gl_ref, mu_r_ref, mu_k_ref, mu_v_ref, mu_wl_ref,
                      mu_al_ref, mu_gl_ref, w0_ref, wup_ref, a0_ref, aup_ref, kk_ref, ka_ref, rk_ref, gup_ref,
                      lng_ref, lnb_ref, o_ref,
                      r_s, v_s, kap_s, g_s, logw_s, kh_s, beta_s, y_s,
                      wmat_s, uloc_s, avq_s, rt_s, aqb_s, kend_s, bend_s, ptot_s):
    f32, bf16 = jnp.float32, jnp.bfloat16
    seq = r_ref.shape[0]
    L, W, RT = RWKV_CHUNK, RWKV_LANES, RWKV_PREP_ROWS
    n_chunks = seq // L
    n_prep = seq // RT

    lane = lax.broadcasted_iota(jnp.int32, (1, W), 1)
    head_masks = [(lane // B_HEAD) == h for h in range(RWKV_GROUP)]
    ones_bd = ((lax.broadcasted_iota(jnp.int32, (W, W), 0) // B_HEAD)
               == (lax.broadcasted_iota(jnp.int32, (W, W), 1) // B_HEAD)).astype(bf16)

    def block_diag(a):
        zero = jnp.zeros_like(a)
        return jnp.concatenate([jnp.where(m, a, zero) for m in head_masks], axis=0)

    def seg_sum(a):
        return _f32_dot(a, ones_bd)

    def shifted(ref, mu_ref, i, rows):
        o = pl.multiple_of(i * RT, RT)
        x = ref[pl.ds(o, RT), :].astype(f32)
        before = ref[pl.ds(pl.multiple_of(jnp.maximum(o - 16, 0), 16), 16), :].astype(f32)[15:16, :]
        after = ref[pl.ds(pl.multiple_of(jnp.minimum(o + RT, seq - 16), 16), 16), :].astype(f32)[0:1, :]
        before = jnp.where(i > 0, before, 0.0)
        after = jnp.where(i < n_prep - 1, after, 0.0)
        prev = jnp.where(rows == 0, before, pltpu.roll(x, 1, 0))
        nxt = jnp.where(rows == RT - 1, after, pltpu.roll(x, RT - 1, 0))
        return x + mu_ref[...] * (0.5 * (prev + nxt) - x)

    def prep(i, carry):
        o = pl.multiple_of(i * RT, RT)
        rows_w = lax.broadcasted_iota(jnp.int32, (RT, W), 0)
        rows_n = lax.broadcasted_iota(jnp.int32, (RT, wl_ref.shape[1]), 0)
        r = shifted(r_ref, mu_r_ref, i, rows_w)
        k = shifted(k_ref, mu_k_ref, i, rows_w)
        v = shifted(v_ref, mu_v_ref, i, rows_w)
        wl = shifted(wl_ref, mu_wl_ref, i, rows_n)
        al = shifted(al_ref, mu_al_ref, i, rows_n)
        gl = shifted(gl_ref, mu_gl_ref, i, rows_n)
        kk0 = k * kk_ref[...]
        kap = kk0 / jnp.maximum(jnp.sqrt(seg_sum(kk0 * kk0)), 1e-12)
        r_s[pl.ds(o, RT), :] = r
        v_s[pl.ds(o, RT), :] = v.astype(bf16)
        kap_s[pl.ds(o, RT), :] = kap
        g_s[pl.ds(o, RT), :] = jnp.dot(jax.nn.sigmoid(gl).astype(bf16), gup_ref[...],
                                       preferred_element_type=f32).astype(bf16)
        for d in range(2):
            wl_d = jnp.tanh(wl[:, d * B_DECAY_LORA:(d + 1) * B_DECAY_LORA]).astype(bf16)
            al_d = al[:, d * B_AAA_LORA:(d + 1) * B_AAA_LORA].astype(bf16)
            w_raw = w0_ref[d:d + 1, :] + jnp.dot(wl_d, wup_ref[d], preferred_element_type=f32)
            a = jax.nn.sigmoid(a0_ref[d:d + 1, :] + jnp.dot(al_d, aup_ref[d], preferred_element_type=f32))
            logw_s[d, pl.ds(o, RT), :] = -jnp.exp(-0.5) * jax.nn.sigmoid(w_raw)
            kh_s[d, pl.ds(o, RT), :] = (k * (1.0 + (a - 1.0) * ka_ref[...])).astype(bf16)
            beta_s[d, pl.ds(o, RT), :] = (kap * a).astype(bf16)
        return carry

    lax.fori_loop(0, n_prep, prep, 0)

    row = lax.broadcasted_iota(jnp.int32, (L, W), 0)
    col = lax.broadcasted_iota(jnp.int32, (L, W), 1) % B_HEAD
    eye_cat = (row == col).astype(f32)
    strict = (row > col, row < col)
    incl = (row >= col, row <= col)

    def cumsum_rows(x, reverse):
        sh = 1
        while sh < L:
            if reverse:
                x = x + jnp.where(row < L - sh, pltpu.roll(x, L - sh, 0), 0.0)
            else:
                x = x + jnp.where(row >= sh, pltpu.roll(x, sh, 0), 0.0)
            sh *= 2
        return x

    interleave = _interleave

    def local_part(c, d):
        rows = pl.ds(pl.multiple_of(c * L, L), L)
        r, kap = r_s[rows, :], kap_s[rows, :]
        logw, kh, beta = logw_s[d, rows, :], kh_s[d, rows, :].astype(f32), beta_s[d, rows, :].astype(f32)
        cum = cumsum_rows(logw, reverse=(d == 1))
        tot = cum[L - 1:L, :] if d == 0 else cum[0:1, :]
        p_in, p_inv, p_end = jnp.exp(cum), jnp.exp(-cum), jnp.exp(tot - cum)
        kap_t = (kap * jnp.exp(cum - logw)).astype(bf16)
        r_t = (r * p_in).astype(bf16)
        v_bd = block_diag(v_s[rows, :])
        g_all = _nt_dot(jnp.concatenate([kap_t, r_t], axis=0),
                        jnp.concatenate([block_diag((beta * p_inv).astype(bf16)),
                                         block_diag((kh * p_inv).astype(bf16))], axis=0))
        yield
        x_pow = -jnp.where(strict[d], g_all[:L, :W], 0.0)
        a_ak = jnp.where(strict[d], g_all[:L, W:], 0.0)
        a_qb = jnp.where(incl[d], g_all[L:, :W], 0.0)
        a_qk = jnp.where(incl[d], g_all[L:, W:], 0.0)
        t_inv = eye_cat + x_pow
        av = jnp.dot(jnp.concatenate([a_ak, a_qk], axis=0).astype(bf16), v_bd, preferred_element_type=f32)
        x_pow = jnp.dot(x_pow.astype(bf16), block_diag(x_pow.astype(bf16)), preferred_element_type=f32)
        yield
        n_steps = L.bit_length() - 2
        for step in range(n_steps):
            if step < n_steps - 1:
                both = jnp.dot(jnp.concatenate([x_pow, t_inv], axis=0).astype(bf16),
                               block_diag(x_pow.astype(bf16)), preferred_element_type=f32)
                yield
                x_pow, t_inv = both[:L], t_inv + both[L:]
            else:
                last = jnp.dot(t_inv.astype(bf16), block_diag(x_pow.astype(bf16)), preferred_element_type=f32)
                yield
                t_inv = t_inv + last
        t_b = t_inv.astype(bf16)
        w_mat = jnp.dot(t_b, block_diag(kap_t), preferred_element_type=f32)
        u_loc = jnp.dot(t_b, block_diag(av[:L].astype(bf16)), preferred_element_type=f32)
        yield
        wmat_s[d, rows, :] = w_mat.astype(bf16)
        uloc_s[d, rows, :] = u_loc.astype(bf16)
        avq_s[d, rows, :] = av[L:].astype(bf16)
        rt_s[d, rows, :] = r_t
        aqb_s[d, rows, :] = a_qb.astype(bf16)
        kend_s[d, rows, :] = (kh * p_end).astype(bf16)
        bend_s[d, rows, :] = (beta * p_end).astype(bf16)
        ptot_s[d, pl.ds(pl.multiple_of(c * 8, 8), 8), :] = jnp.broadcast_to(jnp.exp(tot), (8, W))

    def local_body(i, carry):
        interleave([local_part(i * RWKV_LOCAL_UNROLL + j, d) for j in range(RWKV_LOCAL_UNROLL) for d in range(2)])
        return carry

    lax.fori_loop(0, n_chunks // RWKV_LOCAL_UNROLL, local_body, 0)

    def carried_part(c, state, d):
        rows = pl.ds(pl.multiple_of(c * L, L), L)
        ws = _nt_dot(jnp.concatenate([wmat_s[d, rows, :], rt_s[d, rows, :]], axis=0), block_diag(state.astype(bf16)))
        yield
        u = ws[:L] + uloc_s[d, rows, :].astype(f32)
        u_b = u.astype(bf16)
        full = _tn_dot(jnp.concatenate([v_s[rows, :], -u_b], axis=0),
                       jnp.concatenate([kend_s[d, rows, :], bend_s[d, rows, :]], axis=0))
        y_loc = jnp.dot(aqb_s[d, rows, :], block_diag(u_b), preferred_element_type=f32)
        yield
        new_state = state * ptot_s[d, pl.ds(pl.multiple_of(c * 8, 8), 8), :][0:1, :]
        for h, m in enumerate(head_masks):
            new_state = new_state + jnp.where(m, full[h * B_HEAD:(h + 1) * B_HEAD, :], 0.0)
        y_s[d, rows, :] = ws[L:] + avq_s[d, rows, :].astype(f32) - y_loc
        return new_state

    def carried_body(c, states):
        return tuple(interleave([carried_part(c, states[0], 0), carried_part(n_chunks - 1 - c, states[1], 1)]))

    zero_state = jnp.zeros((B_HEAD, W), f32)
    lax.fori_loop(0, n_chunks, carried_body, (zero_state, zero_state))

    def finish(i, carry):
        rows = pl.ds(pl.multiple_of(i * RT, RT), RT)
        y = y_s[0, rows, :] + y_s[1, rows, :]
        mean = seg_sum(y) * (1.0 / B_HEAD)
        yc = y - mean
        var = seg_sum(yc * yc) * (1.0 / B_HEAD)
        yn = yc * lax.rsqrt(var + B_GN_EPS) * lng_ref[...] + lnb_ref[...]
        kh_both = kh_s[0, rows, :].astype(f32) + kh_s[1, rows, :].astype(f32)
        bonus = seg_sum(r_s[rows, :] * kh_both * rk_ref[...]) * v_s[rows, :].astype(f32)
        o_ref[rows, :] = ((yn + bonus) * g_s[rows, :].astype(f32)).astype(o_ref.dtype)
        return carry

    lax.fori_loop(0, n_prep, finish, 0)


def _rwkv7_mixer(x, w_in, mu, w0, w_up, a0, a_up, k_k, k_a, r_k, g_up, lnx_g, lnx_b):
    bsz, s, d = x.shape
    n_proj = w_in.shape[1]
    W = RWKV_LANES
    assert d % W == 0 and s % RWKV_PREP_ROWS == 0 and B_GATE_LORA == LANES
    assert 2 * B_DECAY_LORA == LANES and 2 * B_AAA_LORA == LANES
    p = _project(x.reshape(bsz * s, d), w_in, jnp.zeros((n_proj,), jnp.float32), jnp.bfloat16)
    p = p.reshape(bsz, s, n_proj)
    n_groups = d // W
    gw = d // W
    lora0 = 3 * d // LANES

    def seq_w(off):
        return pl.BlockSpec((None, s, W), lambda b, g, off=off: (b, 0, off + g))

    def seq_n(idx):
        return pl.BlockSpec((None, s, LANES), lambda b, g, idx=idx: (b, 0, idx))

    def vec_w(off):
        return pl.BlockSpec((1, W), lambda b, g, off=off: (0, off + g))

    def vec_n(idx):
        return pl.BlockSpec((1, LANES), lambda b, g, idx=idx: (0, idx))

    par_w = pl.BlockSpec((1, W), lambda b, g: (0, g))
    two_w = pl.BlockSpec((2, W), lambda b, g: (0, g))
    up_w = pl.BlockSpec((2, B_DECAY_LORA, W), lambda b, g: (0, 0, g))
    mu2 = mu.reshape(1, n_proj)
    row = lambda a: a.reshape(1, d)
    f32_w = pltpu.VMEM((s, W), jnp.float32)
    b16_w = pltpu.VMEM((s, W), jnp.bfloat16)
    f32_2w = pltpu.VMEM((2, s, W), jnp.float32)
    b16_2w = pltpu.VMEM((2, s, W), jnp.bfloat16)
    decay_rows = pltpu.VMEM((2, 8 * (s // RWKV_CHUNK), W), jnp.float32)
    return pl.pallas_call(
        _rwkv_scan_kernel,
        out_shape=jax.ShapeDtypeStruct((bsz, s, d), jnp.bfloat16),
        grid=(bsz, n_groups),
        in_specs=[seq_w(0), seq_w(gw), seq_w(2 * gw), seq_n(lora0), seq_n(lora0 + 1), seq_n(lora0 + 2),
                  vec_w(0), vec_w(gw), vec_w(2 * gw), vec_n(lora0), vec_n(lora0 + 1), vec_n(lora0 + 2),
                  two_w, up_w, two_w, up_w, par_w, par_w, par_w,
                  pl.BlockSpec((B_GATE_LORA, W), lambda b, g: (0, g)), par_w, par_w],
        out_specs=pl.BlockSpec((None, s, W), lambda b, g: (b, 0, g)),
        scratch_shapes=[f32_w, b16_w, f32_w, b16_w, f32_2w, b16_2w, b16_2w, f32_2w,
                        b16_2w, b16_2w, b16_2w, b16_2w, b16_2w, b16_2w, b16_2w, decay_rows],
        compiler_params=_compiler_params(("parallel", "parallel")),
        name="rwkv7_scan",
    )(p, p, p, p, p, p, mu2, mu2, mu2, mu2, mu2, mu2,
      w0, w_up.astype(jnp.bfloat16), a0, a_up.astype(jnp.bfloat16), row(k_k), row(k_a), row(r_k),
      g_up.astype(jnp.bfloat16), row(lnx_g), row(lnx_b))


def _axial_rope_tables(s):
    rows = s // GRID_W
    row = jnp.repeat(jnp.arange(rows, dtype=jnp.float32), GRID_W)
    col = jnp.tile(jnp.arange(GRID_W, dtype=jnp.float32), rows)
    n_freq = C_HEAD_DIM // 4
    inv_freq = ROPE_THETA ** (-jnp.arange(n_freq, dtype=jnp.float32) / n_freq)
    ang = jnp.concatenate([row[:, None] * inv_freq, col[:, None] * inv_freq], axis=-1)
    return jnp.cos(ang), jnp.sin(ang)


ATTN_Q_TILE = 256


def _rms_rope(x, gain, cos_f, sin_f):
    xn = x * lax.rsqrt(jnp.mean(x * x, axis=-1, keepdims=True) + RMS_EPS) * gain
    return xn * cos_f + pltpu.roll(xn, C_HEAD_DIM // 2, 1) * sin_f


def _attn_kernel(q_ref, k_ref, v_ref, cq_ref, sq_ref, ck_ref, sk_ref, qg_ref, kg_ref, o_ref, kr_s):
    f32, bf16 = jnp.float32, jnp.bfloat16

    @pl.when(pl.program_id(2) == 0)
    def _():
        kr_s[...] = _rms_rope(k_ref[...].astype(f32), kg_ref[...], ck_ref[...], sk_ref[...]).astype(bf16)

    group = q_ref.shape[1] // C_HEAD_DIM
    for g in range(group):
        cols = slice(g * C_HEAD_DIM, (g + 1) * C_HEAD_DIM)
        q = _rms_rope(q_ref[:, cols].astype(f32), qg_ref[...], cq_ref[...], sq_ref[...]) * C_HEAD_DIM ** -0.5
        scores = _nt_dot(q.astype(bf16), kr_s[...])
        p = jnp.exp(scores - jnp.max(scores, axis=-1, keepdims=True))
        denom = jnp.sum(p, axis=-1, keepdims=True)
        o = jnp.dot(p.astype(bf16), v_ref[...], preferred_element_type=f32)
        o_ref[:, cols] = (o / denom).astype(o_ref.dtype)


def _axial_gqa_mixer(x, w_in, q_gain, k_gain):
    bsz, s, d = x.shape
    dh = C_HEAD_DIM
    qh = d // dh
    group = qh // C_KV_HEADS
    n_proj = w_in.shape[1]
    half = jnp.concatenate([jnp.arange(0, dh, 2), jnp.arange(1, dh, 2)])
    n_rot = (qh + C_KV_HEADS) * dh
    w_rot = w_in[:, :n_rot].reshape(d, n_rot // dh, dh // 2, 2).transpose(0, 1, 3, 2).reshape(d, n_rot)
    w = jnp.concatenate([w_rot, w_in[:, n_rot:]], axis=1)
    p = _project(x.reshape(bsz * s, d), w, jnp.zeros((n_proj,), jnp.float32), jnp.bfloat16)
    p = p.reshape(bsz, s, n_proj)
    cos, sin = _axial_rope_tables(s)
    cos_f = jnp.concatenate([cos, cos], axis=-1)
    sin_f = jnp.concatenate([-sin, sin], axis=-1)
    tq = min(ATTN_Q_TILE, s)
    gw = group * dh
    q_rows = pl.BlockSpec((tq, dh), lambda b, h, i: (i, 0))
    k_rows = pl.BlockSpec((s, dh), lambda b, h, i: (0, 0))
    vec = pl.BlockSpec((1, dh), lambda b, h, i: (0, 0))
    return pl.pallas_call(
        _attn_kernel,
        out_shape=jax.ShapeDtypeStruct((bsz, s, d), jnp.bfloat16),
        grid=(bsz, C_KV_HEADS, s // tq),
        in_specs=[pl.BlockSpec((None, tq, gw), lambda b, h, i: (b, i, h)),
                  pl.BlockSpec((None, s, dh), lambda b, h, i: (b, 0, qh + h)),
                  pl.BlockSpec((None, s, dh), lambda b, h, i: (b, 0, qh + C_KV_HEADS + h)),
                  q_rows, q_rows, k_rows, k_rows, vec, vec],
        out_specs=pl.BlockSpec((None, tq, gw), lambda b, h, i: (b, i, h)),
        scratch_shapes=[pltpu.VMEM((s, dh), jnp.bfloat16)],
        compiler_params=_compiler_params(("parallel", "parallel", "arbitrary")),
        name="axial_attention",
    )(p, p, p, cos_f, sin_f, cos_f, sin_f, q_gain[half].reshape(1, dh), k_gain[half].reshape(1, dh))


def kernel(x, a_w_in, a_b_in, a_head_gain, a_w_out, b_w_in, b_mu, b_w0, b_w_up, b_a0, b_a_up, b_k_k, b_k_a, b_r_k, b_g_up, b_lnx_g, b_lnx_b, b_w_out, c_w_in, c_q_gain, c_k_gain, c_w_out, ln1_g, ln1_b, moe_w_router, moe_b_router, moe_w_gu, moe_b_gu, moe_w_dn, moe_b_dn, ln2_g, ln2_b):
    bsz, s, d = x.shape
    depth = ln1_g.shape[0]
    xt = x.reshape(bsz * s, d)
    for i in range(depth):
        kind = i % N_MIXERS
        j = i // N_MIXERS
        x3 = xt.reshape(bsz, s, d)
        if kind == 0:
            h = _mlstm_mixer(x3, a_w_in[j], a_b_in[j], a_head_gain[j])
            w_out = a_w_out[j]
        elif kind == 1:
            h = _rwkv7_mixer(x3, b_w_in[j], b_mu[j], b_w0[j], b_w_up[j], b_a0[j], b_a_up[j],
                             b_k_k[j], b_k_a[j], b_r_k[j], b_g_up[j], b_lnx_g[j], b_lnx_b[j])
            w_out = b_w_out[j]
        else:
            h = _axial_gqa_mixer(x3, c_w_in[j], c_q_gain[j], c_k_gain[j])
            w_out = c_w_out[j]
        xt, xt_bf16 = _outproj_ln(h.reshape(bsz * s, d), w_out, xt, ln1_g[i], ln1_b[i])
        w_gu = _regroup_glu_columns(moe_w_gu, i)
        b_gu = _regroup_glu_bias(moe_b_gu[i])
        w_dn = moe_w_dn[i].astype(jnp.bfloat16)
        b_dn = moe_b_dn[i][:, None, :]
        n_group = bsz * s // MOE_TOKEN_GROUPS
        x_next = None
        for grp in range(MOE_TOKEN_GROUPS):
            y_rows, route = _moe_ffn(xt, xt_bf16, grp, n_group, moe_w_router[i], moe_b_router[i],
                                     w_gu, b_gu, w_dn, b_dn)
            x_next = _combine_ln(y_rows, route, xt, grp, ln2_g[i], ln2_b[i], earlier=x_next)
        xt = x_next
    return xt.reshape(bsz, s, d)
```

```python
import functools

import jax
import jax.numpy as jnp
from jax import lax
from jax.experimental import pallas as pl
from jax.experimental.pallas import tpu as pltpu

DEPTH = 4
N_MIXERS = 3
GRID_W = 64
DEEPNORM_ALPHA = (2 * DEPTH) ** 0.25
LN_EPS = 1e-5
RMS_EPS = 1e-6

A_HEADS = 8
A_DQK = 64
A_CHUNK = 64
A_M_INIT = -1e30

B_HEAD = 64
B_DECAY_LORA = 64
B_AAA_LORA = 64
B_GATE_LORA = 128
B_GN_EPS = 64e-5

C_HEAD_DIM = 128
C_KV_HEADS = 2
C_Q_BLOCK = 128
ROPE_THETA = 10000.0

N_EXPERTS = 32
TOP_K = 4
SWIGLU_LIMIT = 7.0
SWIGLU_ALPHA = 1.702

LANES = 128
VMEM_LIMIT_BYTES = 56 * 1024 * 1024

ROW_TILE = 512
MOE_ROW_TILE = 512
MOE_TOKEN_GROUPS = 2


def _compiler_params(semantics):
    return pltpu.CompilerParams(dimension_semantics=semantics, vmem_limit_bytes=VMEM_LIMIT_BYTES)


def _row_tile(m):
    t = min(ROW_TILE, m)
    assert m % t == 0, (m, t)
    return t


def _col_chunk(n):
    return max(c for c in range(LANES, 5 * LANES + 1, LANES) if n % c == 0)


def _proj_kernel(x_ref, w_ref, b_ref, *o_refs):
    xb = x_ref[...].astype(jnp.bfloat16)
    col = 0
    for o_ref in o_refs:
        n = o_ref.shape[1]
        step = _col_chunk(n)
        for j in range(0, n, step):
            acc = jnp.dot(xb, w_ref[:, col + j:col + j + step], preferred_element_type=jnp.float32)
            o_ref[:, j:j + step] = (acc + b_ref[:, col + j:col + j + step]).astype(o_ref.dtype)
        col += n


def _project(x, w, b, out_dtype, f32_tail=0):
    m, k = x.shape
    n = w.shape[1]
    widths = [n - f32_tail, f32_tail] if f32_tail else [n]
    dtypes = [out_dtype, jnp.float32]
    assert all(c % LANES == 0 for c in widths)
    tm = _row_tile(m)
    outs = pl.pallas_call(
        _proj_kernel,
        out_shape=[jax.ShapeDtypeStruct((m, c), dt) for c, dt in zip(widths, dtypes)],
        grid=(m // tm,),
        in_specs=[pl.BlockSpec((tm, k), lambda i: (i, 0)),
                  pl.BlockSpec((k, n), lambda i: (0, 0)),
                  pl.BlockSpec((1, n), lambda i: (0, 0))],
        out_specs=[pl.BlockSpec((tm, c), lambda i: (i, 0)) for c in widths],
        compiler_params=_compiler_params(("parallel",)),
        name="project",
    )(x, w.astype(jnp.bfloat16), b.reshape(1, n).astype(jnp.float32))
    return tuple(outs) if f32_tail else outs[0]


def _pad_cols(w, b, n_pad):
    k, n = w.shape
    if b is None:
        b = jnp.zeros((n,), jnp.float32)
    return jnp.pad(w, ((0, 0), (0, n_pad - n))), jnp.pad(b, (0, n_pad - n))


def _layer_norm_rows(z, g, b):
    mu = jnp.mean(z, axis=-1, keepdims=True)
    zc = z - mu
    var = jnp.mean(zc * zc, axis=-1, keepdims=True)
    return zc * lax.rsqrt(var + LN_EPS) * g + b


def _outproj_ln_kernel(h_ref, w_ref, x_ref, g_ref, b_ref, o_ref, ob_ref):
    mix = jnp.dot(h_ref[...].astype(jnp.bfloat16), w_ref[...], preferred_element_type=jnp.float32)
    z = DEEPNORM_ALPHA * x_ref[...] + mix
    y = _layer_norm_rows(z, g_ref[...], b_ref[...])
    o_ref[...] = y
    ob_ref[...] = y.astype(ob_ref.dtype)


def _outproj_ln(h, w_out, x, g, b):
    m, d = x.shape
    tm = _row_tile(m)
    row = pl.BlockSpec((tm, d), lambda i: (i, 0))
    vec = pl.BlockSpec((1, d), lambda i: (0, 0))
    return pl.pallas_call(
        _outproj_ln_kernel,
        out_shape=[jax.ShapeDtypeStruct((m, d), jnp.float32), jax.ShapeDtypeStruct((m, d), jnp.bfloat16)],
        grid=(m // tm,),
        in_specs=[row, pl.BlockSpec((d, d), lambda i: (0, 0)), row, vec, vec],
        out_specs=[row, row],
        compiler_params=_compiler_params(("parallel",)),
        name="outproj_ln",
    )(h, w_out.astype(jnp.bfloat16), x, g.reshape(1, d), b.reshape(1, d))


ROUTE_IDX_LANE = 0
ROUTE_RANK_LANE = TOP_K
ROUTE_GATE_LANE = 2 * TOP_K


def _combine_ln_kernel(y_ref, r_ref, x_ref, g_ref, b_ref, *rest):
    o_ref = rest[-1]
    y = jnp.zeros(x_ref.shape, jnp.float32)
    for k in range(TOP_K):
        gate = r_ref[:, ROUTE_GATE_LANE + k:ROUTE_GATE_LANE + k + 1]
        y = y + gate * y_ref[k].astype(jnp.float32)
    z = DEEPNORM_ALPHA * x_ref[...] + y
    o_ref[...] = _layer_norm_rows(z, g_ref[...], b_ref[...])


def _combine_ln(y_rows, route, x, group, g, b, earlier=None):
    m = y_rows.shape[1]
    d = x.shape[1]
    tm = _row_tile(m)
    first = group * (m // tm)
    row = pl.BlockSpec((tm, d), lambda i: (first + i, 0))
    vec = pl.BlockSpec((1, d), lambda i: (0, 0))
    in_specs = [pl.BlockSpec((TOP_K, tm, d), lambda i: (0, i, 0)),
                pl.BlockSpec((tm, LANES), lambda i: (i, 0)), row, vec, vec]
    operands = [y_rows, route, x, g.reshape(1, d), b.reshape(1, d)]
    aliases = {}
    if earlier is not None:
        in_specs.append(pl.BlockSpec(memory_space=pl.ANY))
        operands.append(earlier)
        aliases = {len(operands) - 1: 0}
    return pl.pallas_call(
        _combine_ln_kernel,
        out_shape=jax.ShapeDtypeStruct(x.shape, jnp.float32),
        grid=(m // tm,),
        in_specs=in_specs,
        out_specs=row,
        input_output_aliases=aliases,
        compiler_params=_compiler_params(("parallel",)),
        name="combine_ln",
    )(*operands)


ROUTER_PAD_BIAS = -1e30


def _router_kernel(x_ref, whi_ref, wlo_ref, b_ref, o_ref, cnt_ref, base_s):
    f32 = jnp.float32

    @pl.when(pl.program_id(0) == 0)
    def _():
        base_s[...] = jnp.zeros_like(base_s)

    t = x_ref.shape[0]
    x = x_ref[...]
    x_hi = x.astype(jnp.bfloat16)
    x_lo = (x - x_hi.astype(f32)).astype(jnp.bfloat16)
    logits = (jnp.dot(x_hi, whi_ref[...], preferred_element_type=f32)
              + jnp.dot(x_hi, wlo_ref[...], preferred_element_type=f32)
              + jnp.dot(x_lo, whi_ref[...], preferred_element_type=f32)) + b_ref[...]
    lane = lax.broadcasted_iota(jnp.int32, (t, LANES), 1)
    vals = logits
    tops, idxs, sels = [], [], []
    for _ in range(TOP_K):
        top = jnp.max(vals, axis=-1, keepdims=True)
        idx = jnp.min(jnp.where(vals == top, lane, LANES), axis=-1, keepdims=True)
        sel = lane == idx
        vals = jnp.where(sel, -jnp.inf, vals)
        tops.append(top)
        idxs.append(idx)
        sels.append(sel)
    exps = [jnp.exp(top - tops[0]) for top in tops]
    total = exps[0]
    for e in exps[1:]:
        total = total + e
    chosen = jnp.zeros((t, LANES), f32)
    for sel in sels:
        chosen = chosen + sel.astype(f32)
    earlier = (lax.broadcasted_iota(jnp.int32, (t, t), 0) > lax.broadcasted_iota(jnp.int32, (t, t), 1))
    prefix = jnp.dot(earlier.astype(jnp.bfloat16), chosen.astype(jnp.bfloat16),
                     preferred_element_type=f32) + base_s[...]
    packed = jnp.zeros((t, LANES), f32)
    for k in range(TOP_K):
        rank = jnp.sum(jnp.where(sels[k], prefix, 0.0), axis=-1, keepdims=True)
        packed = jnp.where(lane == ROUTE_IDX_LANE + k, idxs[k].astype(f32), packed)
        packed = jnp.where(lane == ROUTE_RANK_LANE + k, rank, packed)
        packed = jnp.where(lane == ROUTE_GATE_LANE + k, exps[k] / total, packed)
    o_ref[...] = packed
    base_s[...] = base_s[...] + jnp.sum(chosen, axis=0, keepdims=True)
    cnt_ref[...] = base_s[...]


def _route(xt, group, m, w_router, b_router):
    d = xt.shape[1]
    tm = _row_tile(m)
    first = group * (m // tm)
    n_e = w_router.shape[1]
    w = jnp.pad(w_router, ((0, 0), (0, LANES - n_e)))
    b = jnp.pad(b_router, (0, LANES - n_e), constant_values=ROUTER_PAD_BIAS)
    w_hi = w.astype(jnp.bfloat16)
    return pl.pallas_call(
        _router_kernel,
        out_shape=[jax.ShapeDtypeStruct((m, LANES), jnp.float32), jax.ShapeDtypeStruct((1, LANES), jnp.float32)],
        grid=(m // tm,),
        in_specs=[pl.BlockSpec((tm, d), lambda i: (first + i, 0)),
                  pl.BlockSpec((d, LANES), lambda i: (0, 0)),
                  pl.BlockSpec((d, LANES), lambda i: (0, 0)),
                  pl.BlockSpec((1, LANES), lambda i: (0, 0))],
        out_specs=[pl.BlockSpec((tm, LANES), lambda i: (i, 0)), pl.BlockSpec((1, LANES), lambda i: (0, 0))],
        scratch_shapes=[pltpu.VMEM((1, LANES), jnp.float32)],
        compiler_params=_compiler_params(("arbitrary",)),
        name="router",
    )(xt, w_hi, (w - w_hi.astype(jnp.float32)).astype(jnp.bfloat16), b.reshape(1, LANES))


GLU_BLOCK = 2 * LANES


def _regroup_kernel(w_ref, p_ref, o_ref):
    w = w_ref[0].astype(jnp.bfloat16)
    for c in range(0, w.shape[1], GLU_BLOCK):
        o_ref[0, :, c:c + GLU_BLOCK] = jnp.dot(w[:, c:c + GLU_BLOCK], p_ref[...],
                                               preferred_element_type=jnp.float32).astype(o_ref.dtype)


def _regroup_glu_columns(w_gu_layers, layer):
    _, n_e, d, f2 = w_gu_layers.shape
    src = jnp.arange(GLU_BLOCK)
    perm = (src[:, None] == (2 * (src % LANES) + src // LANES)[None, :]).astype(jnp.bfloat16)
    tk = _row_tile(d)
    return pl.pallas_call(
        _regroup_kernel,
        out_shape=jax.ShapeDtypeStruct((n_e, d, f2), jnp.bfloat16),
        grid=(n_e, d // tk),
        in_specs=[pl.BlockSpec((None, 1, tk, f2), lambda e, i: (layer, e, i, 0)),
                  pl.BlockSpec((GLU_BLOCK, GLU_BLOCK), lambda e, i: (0, 0))],
        out_specs=pl.BlockSpec((1, tk, f2), lambda e, i: (e, i, 0)),
        compiler_params=_compiler_params(("parallel", "parallel")),
        name="regroup_glu",
    )(w_gu_layers, perm)


def _cast_kernel(w_ref, o_ref):
    o_ref[...] = w_ref[...].astype(o_ref.dtype)


def _layer_to_bf16(w_layers, layer):
    _, n_e, r, c = w_layers.shape
    tr = _row_tile(r)
    return pl.pallas_call(
        _cast_kernel,
        out_shape=jax.ShapeDtypeStruct((n_e, r, c), jnp.bfloat16),
        grid=(n_e, r // tr),
        in_specs=[pl.BlockSpec((None, 1, tr, c), lambda e, i: (layer, e, i, 0))],
        out_specs=pl.BlockSpec((1, tr, c), lambda e, i: (e, i, 0)),
        compiler_params=_compiler_params(("parallel", "parallel")),
        name="layer_to_bf16",
    )(w_layers)


def _slot_kernel(r_ref, start_ref, o_ref):
    lane = lax.broadcasted_iota(jnp.int32, r_ref.shape, 1)
    route = r_ref[...]
    slots = jnp.zeros(r_ref.shape, jnp.float32)
    for k in range(TOP_K):
        expert = route[:, ROUTE_IDX_LANE + k:ROUTE_IDX_LANE + k + 1].astype(jnp.int32)
        first = jnp.sum(jnp.where(lane == expert, start_ref[...], 0.0), axis=-1, keepdims=True)
        slots = jnp.where(lane == k, first + route[:, ROUTE_RANK_LANE + k:ROUTE_RANK_LANE + k + 1], slots)
    o_ref[...] = slots.astype(jnp.int32)


def _slots(route, expert_start):
    m = route.shape[0]
    tm = _row_tile(m)
    return pl.pallas_call(
        _slot_kernel,
        out_shape=jax.ShapeDtypeStruct((m, LANES), jnp.int32),
        grid=(m // tm,),
        in_specs=[pl.BlockSpec((tm, LANES), lambda i: (i, 0)), pl.BlockSpec((1, LANES), lambda i: (0, 0))],
        out_specs=pl.BlockSpec((tm, LANES), lambda i: (i, 0)),
        compiler_params=_compiler_params(("parallel",)),
        name="moe_slots",
    )(route, expert_start)


def _regroup_glu_bias(b_gu):
    n_e, f2 = b_gu.shape
    return b_gu.reshape(n_e, f2 // GLU_BLOCK, LANES, 2).transpose(0, 1, 3, 2).reshape(n_e, 1, f2)


def _expert_kernel(be_ref, nb_ref, x_ref, wgu_ref, bgu_ref, wd_ref, bd_ref, o_ref):
    @pl.when(pl.program_id(0) < nb_ref[0])
    def _():
        xb = x_ref[...]
        acts = []
        for c in range(0, wgu_ref.shape[2], GLU_BLOCK):
            h = (jnp.dot(xb, wgu_ref[0, :, c:c + GLU_BLOCK], preferred_element_type=jnp.float32)
                 + bgu_ref[0, :, c:c + GLU_BLOCK])
            hg = jnp.minimum(h[:, :LANES], SWIGLU_LIMIT)
            hl = jnp.clip(h[:, LANES:], -SWIGLU_LIMIT, SWIGLU_LIMIT)
            acts.append((hg * jax.nn.sigmoid(SWIGLU_ALPHA * hg) * (hl + 1.0)).astype(jnp.bfloat16))
        act = jnp.concatenate(acts, axis=1)
        y = jnp.dot(act, wd_ref[0], preferred_element_type=jnp.float32) + bd_ref[0]
        o_ref[...] = y.astype(o_ref.dtype)


def _expert_ffn(x_sorted, block_expert, n_used, w_gu, b_gu, w_dn, b_dn):
    cap, d = x_sorted.shape
    f2 = w_gu.shape[2]
    tm = MOE_ROW_TILE
    n_blocks = cap // tm

    def blk(i, be, nb):
        return (jnp.minimum(i, nb[0] - 1), 0)

    def wsel(i, be, nb):
        return (be[i], 0, 0)

    grid_spec = pltpu.PrefetchScalarGridSpec(
        num_scalar_prefetch=2,
        grid=(n_blocks,),
        in_specs=[pl.BlockSpec((tm, d), blk),
                  pl.BlockSpec((1, d, f2), wsel),
                  pl.BlockSpec((1, 1, f2), wsel),
                  pl.BlockSpec((1, f2 // 2, d), wsel),
                  pl.BlockSpec((1, 1, d), wsel)],
        out_specs=pl.BlockSpec((tm, d), blk),
    )
    return pl.pallas_call(
        _expert_kernel,
        out_shape=jax.ShapeDtypeStruct((cap, d), jnp.bfloat16),
        grid_spec=grid_spec,
        compiler_params=_compiler_params(("arbitrary",)),
        name="expert_ffn",
    )(block_expert, n_used, x_sorted, w_gu, b_gu, w_dn, b_dn)


def _moe_ffn(x, x_bf16, group, n, w_router, b_router, w_gu, b_gu, w_dn, b_dn):
    d = x.shape[1]
    tm = MOE_ROW_TILE
    n_e = w_router.shape[1]
    route, counts = _route(x, group, n, w_router, b_router)
    counts = counts[0].astype(jnp.int32)
    padded = (counts + tm - 1) // tm * tm
    pad_end = jnp.cumsum(padded)
    pad_start = pad_end - padded
    cap = n * TOP_K + n_e * tm
    n_blocks = cap // tm
    slot = _slots(route, pad_start.astype(jnp.float32).reshape(1, LANES))[:, :TOP_K]
    slot_kmajor = slot.T.reshape(-1)
    block_first_row = jnp.arange(n_blocks, dtype=jnp.int32) * tm
    block_expert = jnp.minimum(jnp.sum(pad_end[None, :n_e] <= block_first_row[:, None], axis=1), n_e - 1)
    n_used = (pad_end[n_e - 1] // tm).reshape(1)
    top_idx = route[:, ROUTE_IDX_LANE:ROUTE_IDX_LANE + TOP_K].astype(jnp.int32).reshape(-1)
    tok_by_expert = jnp.argsort(top_idx, stable=True).astype(jnp.int32) // TOP_K
    unpadded_start = jnp.cumsum(counts) - counts
    row_in_expert = (block_first_row - pad_start[block_expert])[:, None] + jnp.arange(tm, dtype=jnp.int32)[None, :]
    src = jnp.minimum(unpadded_start[block_expert][:, None] + row_in_expert, n * TOP_K - 1).reshape(-1)
    valid = (row_in_expert < counts[block_expert][:, None]).reshape(-1)
    tok_of_slot = group * n + jnp.where(valid, tok_by_expert.at[src].get(mode="promise_in_bounds"), 0)

    x_sorted = x_bf16.at[tok_of_slot].get(mode="promise_in_bounds")
    yb = _expert_ffn(x_sorted, block_expert.astype(jnp.int32), n_used.astype(jnp.int32), w_gu, b_gu, w_dn, b_dn)
    return yb.at[slot_kmajor].get(mode="promise_in_bounds").reshape(TOP_K, n, d), route


MLSTM_CHUNK = 128
MLSTM_PAIR = 2
MLSTM_NEG = -1e30


def _interleave(generators):
    results = [None] * len(generators)
    live = list(enumerate(generators))
    while live:
        still = []
        for idx, gen in live:
            try:
                next(gen)
                still.append((idx, gen))
            except StopIteration as stop:
                results[idx] = stop.value
        live = still
    return results


def _one_by_one(generators):
    return [_interleave([gen])[0] for gen in generators]


def _cumsum_rows(x, reverse):
    n = x.shape[0]
    row = lax.broadcasted_iota(jnp.int32, x.shape, 0)
    sh = 1
    while sh < n:
        if reverse:
            x = x + jnp.where(row < n - sh, pltpu.roll(x, n - sh, 0), 0.0)
        else:
            x = x + jnp.where(row >= sh, pltpu.roll(x, sh, 0), 0.0)
        sh *= 2
    return x


def _cummax_rows(x, reverse):
    n = x.shape[0]
    row = lax.broadcasted_iota(jnp.int32, x.shape, 0)
    sh = 1
    while sh < n:
        if reverse:
            x = jnp.maximum(x, jnp.where(row < n - sh, pltpu.roll(x, n - sh, 0), -jnp.inf))
        else:
            x = jnp.maximum(x, jnp.where(row >= sh, pltpu.roll(x, sh, 0), -jnp.inf))
        sh *= 2
    return x


def _mlstm_kernel(q_ref, k_ref, v_ref, og_ref, g_ref, gain_ref, o_ref, h_s, c_s):
    f32, bf16 = jnp.float32, jnp.bfloat16
    seq = q_ref.shape[0]
    L = MLSTM_CHUNK
    n_chunks = seq // L
    dv = v_ref.shape[1] // MLSTM_PAIR
    dqk = q_ref.shape[1] // MLSTM_PAIR

    lane_q = lax.broadcasted_iota(jnp.int32, (1, q_ref.shape[1]), 1) // dqk
    lane_g = lax.broadcasted_iota(jnp.int32, (1, LANES), 1)
    is_forget_lane = (lane_g // MLSTM_PAIR) % 2 == 1
    row = lax.broadcasted_iota(jnp.int32, (L, L), 0)
    col = lax.broadcasted_iota(jnp.int32, (L, L), 1)
    causal = (col <= row, col >= row)
    ones_v = jnp.ones((L, dv), bf16)

    c_s[...] = jnp.zeros_like(c_s)

    def gate_terms(start, d):
        rows = pl.ds(start, L)
        gates = g_ref[rows, :]
        log_f = jnp.minimum(gates, 0.0) - jnp.log(1.0 + jnp.exp(-jnp.abs(gates)))
        cum = _cumsum_rows(log_f, reverse=(d == 1))
        src = gates - pltpu.roll(cum, LANES - MLSTM_PAIR, 1)
        run_max = _cummax_rows(src, reverse=(d == 1))
        return rows, gates, cum, src.T, run_max, q_ref[rows, :], k_ref[rows, :]

    def head_unit(terms, d, j, m_st):
        rows, gates, cum, src_t, run_max, q_all, k_all = terms
        lane_i = 2 * MLSTM_PAIR * d + j
        lane_f = lane_i + MLSTM_PAIR
        b_col, i_col = cum[:, lane_f:lane_f + 1], gates[:, lane_i:lane_i + 1]
        src_row = src_t[lane_i:lane_i + 1, :]
        qz = jnp.where(lane_q == j, q_all, jnp.zeros_like(q_all))
        kz = jnp.where(lane_q == j, k_all, jnp.zeros_like(k_all))
        state = c_s[2 * d + j]
        scores = _nt_dot(qz, kz)
        carried = jnp.dot(qz, state.astype(bf16), preferred_element_type=f32)
        yield
        mu = jnp.maximum(m_st, run_max[:, lane_i:lane_i + 1])
        v_aug = jnp.concatenate([v_ref[rows, j * dv:(j + 1) * dv], ones_v], axis=1)
        qk = scores * (A_DQK ** -0.5) * jnp.exp(jnp.where(causal[d], src_row - mu, MLSTM_NEG))
        edge = L - 1 if d == 0 else 0
        b_last = b_col[edge:edge + 1, :]
        m_new = b_last + mu[edge:edge + 1, :]
        w_s = jnp.exp(b_last - b_col + i_col - m_new)
        local = jnp.dot(qk.astype(bf16), v_aug, preferred_element_type=f32)
        update = _tn_dot((w_s * kz.astype(f32)).astype(bf16), v_aug)
        yield
        nd = local + jnp.exp(m_st - mu) * (A_DQK ** -0.5) * carried
        h_s[d, rows, j * dv:(j + 1) * dv] = nd[:, :dv] / jnp.maximum(jnp.abs(nd[:, dv:]), jnp.exp(-b_col - mu))
        c_s[2 * d + j] = jnp.exp(b_last + m_st - m_new) * state + update
        return m_new

    def body(c, m_all):
        terms = (gate_terms(pl.multiple_of(c * L, L), 0), gate_terms(pl.multiple_of((n_chunks - 1 - c) * L, L), 1))
        return tuple(_one_by_one([head_unit(terms[d], d, j, m_all[MLSTM_PAIR * d + j])
                                  for d in range(2) for j in range(MLSTM_PAIR)]))

    m_init = tuple(jnp.full((1, 1), A_M_INIT, f32) for _ in range(2 * MLSTM_PAIR))
    lax.fori_loop(0, n_chunks, body, m_init)

    def finish(c, carry):
        rows = pl.ds(pl.multiple_of(c * L, L), L)
        for j in range(MLSTM_PAIR):
            cols = slice(j * dv, (j + 1) * dv)
            h = h_s[0, rows, cols] + h_s[1, rows, cols]
            hn = h * lax.rsqrt(jnp.mean(h * h, axis=-1, keepdims=True) + RMS_EPS) * gain_ref[:, cols]
            o_ref[rows, cols] = (hn * jax.nn.sigmoid(og_ref[rows, cols].astype(f32))).astype(o_ref.dtype)
        return carry

    lax.fori_loop(0, n_chunks, finish, 0)


def _mlstm_mixer(x, w_in, b_in, head_gain):
    bsz, s, d = x.shape
    a_qk = A_HEADS * A_DQK
    dv = d // A_HEADS
    n_main = 2 * a_qk + 2 * d
    n_pairs = A_HEADS // MLSTM_PAIR
    pw_qk = MLSTM_PAIR * A_DQK
    pw_v = MLSTM_PAIR * dv
    assert pw_qk == LANES and s % MLSTM_CHUNK == 0
    gate_cols = jnp.asarray([[n_main + t * A_HEADS + MLSTM_PAIR * hp + j for t in range(4) for j in range(MLSTM_PAIR)]
                             for hp in range(n_pairs)])
    n_gate = gate_cols.shape[1]
    w_tail = jnp.pad(w_in[:, gate_cols], ((0, 0), (0, 0), (0, LANES - n_gate))).reshape(d, n_pairs * LANES)
    b_tail = jnp.pad(b_in[gate_cols], ((0, 0), (0, LANES - n_gate))).reshape(n_pairs * LANES)
    w = jnp.concatenate([w_in[:, :n_main], w_tail], axis=1)
    b = jnp.concatenate([b_in[:n_main], b_tail])
    p, gates = _project(x.reshape(bsz * s, d), w, b, jnp.bfloat16, f32_tail=n_pairs * LANES)
    p = p.reshape(bsz, s, n_main)
    gates = gates.reshape(bsz, s, n_pairs * LANES)
    k0 = a_qk // pw_qk
    v0 = 2 * a_qk // pw_v
    o0 = (2 * a_qk + d) // pw_v
    return pl.pallas_call(
        _mlstm_kernel,
        out_shape=jax.ShapeDtypeStruct((bsz, s, d), jnp.bfloat16),
        grid=(bsz, n_pairs),
        in_specs=[pl.BlockSpec((None, s, pw_qk), lambda b, h: (b, 0, h)),
                  pl.BlockSpec((None, s, pw_qk), lambda b, h: (b, 0, k0 + h)),
                  pl.BlockSpec((None, s, pw_v), lambda b, h: (b, 0, v0 + h)),
                  pl.BlockSpec((None, s, pw_v), lambda b, h: (b, 0, o0 + h)),
                  pl.BlockSpec((None, s, LANES), lambda b, h: (b, 0, h)),
                  pl.BlockSpec((1, pw_v), lambda b, h: (0, h))],
        out_specs=pl.BlockSpec((None, s, pw_v), lambda b, h: (b, 0, h)),
        scratch_shapes=[pltpu.VMEM((2, s, pw_v), jnp.float32),
                        pltpu.VMEM((2 * MLSTM_PAIR, pw_qk, 2 * dv), jnp.float32)],
        compiler_params=_compiler_params(("parallel", "parallel")),
        name="mlstm_scan",
    )(p, p, p, p, gates, head_gain.reshape(1, d))


RWKV_CHUNK = 64
RWKV_GROUP = 4
RWKV_LANES = RWKV_GROUP * B_HEAD
RWKV_PREP_ROWS = 256
RWKV_LOCAL_UNROLL = 4


def _f32_dot(a, b_bf16):
    hi = a.astype(jnp.bfloat16)
    lo = (a - hi.astype(jnp.float32)).astype(jnp.bfloat16)
    return (jnp.dot(hi, b_bf16, preferred_element_type=jnp.float32)
            + jnp.dot(lo, b_bf16, preferred_element_type=jnp.float32))


def _nt_dot(a, b):
    return lax.dot_general(a, b, (((1,), (1,)), ((), ())), preferred_element_type=jnp.float32)


def _tn_dot(a, b):
    return lax.dot_general(a, b, (((0,), (0,)), ((), ())), preferred_element_type=jnp.float32)


def _rwkv_scan_kernel(r_ref, k_ref, v_ref, wl_ref, al_ref, gl_ref, mu_r_ref, mu_k_ref, mu_v_ref, mu_wl_ref,
                      mu_al_ref, mu_gl_ref, w0_ref, wup_ref, a0_ref, aup_ref, kk_ref, ka_ref, rk_ref, gup_ref,
                      lng_ref, lnb_ref, o_ref,
                      r_s, v_s, kap_s, g_s, logw_s, kh_s, beta_s, y_s,
                      wmat_s, uloc_s, avq_s, rt_s, aqb_s, kend_s, bend_s, ptot_s):
    f32, bf16 = jnp.float32, jnp.bfloat16
    seq = r_ref.shape[0]
    L, W, RT = RWKV_CHUNK, RWKV_LANES, RWKV_PREP_ROWS
    n_chunks = seq // L
    n_prep = seq // RT

    lane = lax.broadcasted_iota(jnp.int32, (1, W), 1)
    head_masks = [(lane // B_HEAD) == h for h in range(RWKV_GROUP)]
    ones_bd = ((lax.broadcasted_iota(jnp.int32, (W, W), 0) // B_HEAD)
               == (lax.broadcasted_iota(jnp.int32, (W, W), 1) // B_HEAD)).astype(bf16)

    def block_diag(a):
        zero = jnp.zeros_like(a)
        return jnp.concatenate([jnp.where(m, a, zero) for m in head_masks], axis=0)

    def seg_sum(a):
        return _f32_dot(a, ones_bd)

    def shifted(ref, mu_ref, i, rows):
        o = pl.multiple_of(i * RT, RT)
        x = ref[pl.ds(o, RT), :].astype(f32)
        before = ref[pl.ds(pl.multiple_of(jnp.maximum(o - 16, 0), 16), 16), :].astype(f32)[15:16, :]
        after = ref[pl.ds(pl.multiple_of(jnp.minimum(o + RT, seq - 16), 16), 16), :].astype(f32)[0:1, :]
        before = jnp.where(i > 0, before, 0.0)
        after = jnp.where(i < n_prep - 1, after, 0.0)
        prev = jnp.where(rows == 0, before, pltpu.roll(x, 1, 0))
        nxt = jnp.where(rows == RT - 1, after, pltpu.roll(x, RT - 1, 0))
        return x + mu_ref[...] * (0.5 * (prev + nxt) - x)

    def prep(i, carry):
        o = pl.multiple_of(i * RT, RT)
        rows_w = lax.broadcasted_iota(jnp.int32, (RT, W), 0)
        rows_n = lax.broadcasted_iota(jnp.int32, (RT, wl_ref.shape[1]), 0)
        r = shifted(r_ref, mu_r_ref, i, rows_w)
        k = shifted(k_ref, mu_k_ref, i, rows_w)
        v = shifted(v_ref, mu_v_ref, i, rows_w)
        wl = shifted(wl_ref, mu_wl_ref, i, rows_n)
        al = shifted(al_ref, mu_al_ref, i, rows_n)
        gl = shifted(gl_ref, mu_gl_ref, i, rows_n)
        kk0 = k * kk_ref[...]
        kap = kk0 / jnp.maximum(jnp.sqrt(seg_sum(kk0 * kk0)), 1e-12)
        r_s[pl.ds(o, RT), :] = r
        v_s[pl.ds(o, RT), :] = v.astype(bf16)
        kap_s[pl.ds(o, RT), :] = kap
        g_s[pl.ds(o, RT), :] = jnp.dot(jax.nn.sigmoid(gl).astype(bf16), gup_ref[...],
                                       preferred_element_type=f32).astype(bf16)
        for d in range(2):
            wl_d = jnp.tanh(wl[:, d * B_DECAY_LORA:(d + 1) * B_DECAY_LORA]).astype(bf16)
            al_d = al[:, d * B_AAA_LORA:(d + 1) * B_AAA_LORA].astype(bf16)
            w_raw = w0_ref[d:d + 1, :] + jnp.dot(wl_d, wup_ref[d], preferred_element_type=f32)
            a = jax.nn.sigmoid(a0_ref[d:d + 1, :] + jnp.dot(al_d, aup_ref[d], preferred_element_type=f32))
            logw_s[d, pl.ds(o, RT), :] = -jnp.exp(-0.5) * jax.nn.sigmoid(w_raw)
            kh_s[d, pl.ds(o, RT), :] = (k * (1.0 + (a - 1.0) * ka_ref[...])).astype(bf16)
            beta_s[d, pl.ds(o, RT), :] = (kap * a).astype(bf16)
        return carry

    lax.fori_loop(0, n_prep, prep, 0)

    row = lax.broadcasted_iota(jnp.int32, (L, W), 0)
    col = lax.broadcasted_iota(jnp.int32, (L, W), 1) % B_HEAD
    eye_cat = (row == col).astype(f32)
    strict = (row > col, row < col)
    incl = (row >= col, row <= col)

    def cumsum_rows(x, reverse):
        sh = 1
        while sh < L:
            if reverse:
                x = x + jnp.where(row < L - sh, pltpu.roll(x, L - sh, 0), 0.0)
            else:
                x = x + jnp.where(row >= sh, pltpu.roll(x, sh, 0), 0.0)
            sh *= 2
        return x

    interleave = _interleave

    def local_part(c, d):
        rows = pl.ds(pl.multiple_of(c * L, L), L)
        r, kap = r_s[rows, :], kap_s[rows, :]
        logw, kh, beta = logw_s[d, rows, :], kh_s[d, rows, :].astype(f32), beta_s[d, rows, :].astype(f32)
        cum = cumsum_rows(logw, reverse=(d == 1))
        tot = cum[L - 1:L, :] if d == 0 else cum[0:1, :]
        p_in, p_inv, p_end = jnp.exp(cum), jnp.exp(-cum), jnp.exp(tot - cum)
        kap_t = (kap * jnp.exp(cum - logw)).astype(bf16)
        r_t = (r * p_in).astype(bf16)
        v_bd = block_diag(v_s[rows, :])
        g_all = _nt_dot(jnp.concatenate([kap_t, r_t], axis=0),
                        jnp.concatenate([block_diag((beta * p_inv).astype(bf16)),
                                         block_diag((kh * p_inv).astype(bf16))], axis=0))
        yield
        x_pow = -jnp.where(strict[d], g_all[:L, :W], 0.0)
        a_ak = jnp.where(strict[d], g_all[:L, W:], 0.0)
        a_qb = jnp.where(incl[d], g_all[L:, :W], 0.0)
        a_qk = jnp.where(incl[d], g_all[L:, W:], 0.0)
        t_inv = eye_cat + x_pow
        av = jnp.dot(jnp.concatenate([a_ak, a_qk], axis=0).astype(bf16), v_bd, preferred_element_type=f32)
        x_pow = jnp.dot(x_pow.astype(bf16), block_diag(x_pow.astype(bf16)), preferred_element_type=f32)
        yield
        n_steps = L.bit_length() - 2
        for step in range(n_steps):
            if step < n_steps - 1:
                both = jnp.dot(jnp.concatenate([x_pow, t_inv], axis=0).astype(bf16),
                               block_diag(x_pow.astype(bf16)), preferred_element_type=f32)
                yield
                x_pow, t_inv = both[:L], t_inv + both[L:]
            else:
                last = jnp.dot(t_inv.astype(bf16), block_diag(x_pow.astype(bf16)), preferred_element_type=f32)
                yield
                t_inv = t_inv + last
        t_b = t_inv.astype(bf16)
        w_mat = jnp.dot(t_b, block_diag(kap_t), preferred_element_type=f32)
        u_loc = jnp.dot(t_b, block_diag(av[:L].astype(bf16)), preferred_element_type=f32)
        yield
        wmat_s[d, rows, :] = w_mat.astype(bf16)
        uloc_s[d, rows, :] = u_loc.astype(bf16)
        avq_s[d, rows, :] = av[L:].astype(bf16)
        rt_s[d, rows, :] = r_t
        aqb_s[d, rows, :] = a_qb.astype(bf16)
        kend_s[d, rows, :] = (kh * p_end).astype(bf16)
        bend_s[d, rows, :] = (beta * p_end).astype(bf16)
        ptot_s[d, pl.ds(pl.multiple_of(c * 8, 8), 8), :] = jnp.broadcast_to(jnp.exp(tot), (8, W))

    def local_body(i, carry):
        interleave([local_part(i * RWKV_LOCAL_UNROLL + j, d) for j in range(RWKV_LOCAL_UNROLL) for d in range(2)])
        return carry

    lax.fori_loop(0, n_chunks // RWKV_LOCAL_UNROLL, local_body, 0)

    def carried_part(c, state, d):
        rows = pl.ds(pl.multiple_of(c * L, L), L)
        ws = _nt_dot(jnp.concatenate([wmat_s[d, rows, :], rt_s[d, rows, :]], axis=0), block_diag(state.astype(bf16)))
        yield
        u = ws[:L] + uloc_s[d, rows, :].astype(f32)
        u_b = u.astype(bf16)
        full = _tn_dot(jnp.concatenate([v_s[rows, :], -u_b], axis=0),
                       jnp.concatenate([kend_s[d, rows, :], bend_s[d, rows, :]], axis=0))
        y_loc = jnp.dot(aqb_s[d, rows, :], block_diag(u_b), preferred_element_type=f32)
        yield
        new_state = state * ptot_s[d, pl.ds(pl.multiple_of(c * 8, 8), 8), :][0:1, :]
        for h, m in enumerate(head_masks):
            new_state = new_state + jnp.where(m, full[h * B_HEAD:(h + 1) * B_HEAD, :], 0.0)
        y_s[d, rows, :] = ws[L:] + avq_s[d, rows, :].astype(f32) - y_loc
        return new_state

    def carried_body(c, states):
        return tuple(interleave([carried_part(c, states[0], 0), carried_part(n_chunks - 1 - c, states[1], 1)]))

    zero_state = jnp.zeros((B_HEAD, W), f32)
    lax.fori_loop(0, n_chunks, carried_body, (zero_state, zero_state))

    def finish(i, carry):
        rows = pl.ds(pl.multiple_of(i * RT, RT), RT)
        y = y_s[0, rows, :] + y_s[1, rows, :]
        mean = seg_sum(y) * (1.0 / B_HEAD)
        yc = y - mean
        var = seg_sum(yc * yc) * (1.0 / B_HEAD)
        yn = yc * lax.rsqrt(var + B_GN_EPS) * lng_ref[...] + lnb_ref[...]
        kh_both = kh_s[0, rows, :].astype(f32) + kh_s[1, rows, :].astype(f32)
        bonus = seg_sum(r_s[rows, :] * kh_both * rk_ref[...]) * v_s[rows, :].astype(f32)
        o_ref[rows, :] = ((yn + bonus) * g_s[rows, :].astype(f32)).astype(o_ref.dtype)
        return carry

    lax.fori_loop(0, n_prep, finish, 0)


def _rwkv7_mixer(x, w_in, mu, w0, w_up, a0, a_up, k_k, k_a, r_k, g_up, lnx_g, lnx_b):
    bsz, s, d = x.shape
    n_proj = w_in.shape[1]
    W = RWKV_LANES
    assert d % W == 0 and s % RWKV_PREP_ROWS == 0 and B_GATE_LORA == LANES
    assert 2 * B_DECAY_LORA == LANES and 2 * B_AAA_LORA == LANES
    p = _project(x.reshape(bsz * s, d), w_in, jnp.zeros((n_proj,), jnp.float32), jnp.bfloat16)
    p = p.reshape(bsz, s, n_proj)
    n_groups = d // W
    gw = d // W
    lora0 = 3 * d // LANES

    def seq_w(off):
        return pl.BlockSpec((None, s, W), lambda b, g, off=off: (b, 0, off + g))

    def seq_n(idx):
        return pl.BlockSpec((None, s, LANES), lambda b, g, idx=idx: (b, 0, idx))

    def vec_w(off):
        return pl.BlockSpec((1, W), lambda b, g, off=off: (0, off + g))

    def vec_n(idx):
        return pl.BlockSpec((1, LANES), lambda b, g, idx=idx: (0, idx))

    par_w = pl.BlockSpec((1, W), lambda b, g: (0, g))
    two_w = pl.BlockSpec((2, W), lambda b, g: (0, g))
    up_w = pl.BlockSpec((2, B_DECAY_LORA, W), lambda b, g: (0, 0, g))
    mu2 = mu.reshape(1, n_proj)
    row = lambda a: a.reshape(1, d)
    f32_w = pltpu.VMEM((s, W), jnp.float32)
    b16_w = pltpu.VMEM((s, W), jnp.bfloat16)
    f32_2w = pltpu.VMEM((2, s, W), jnp.float32)
    b16_2w = pltpu.VMEM((2, s, W), jnp.bfloat16)
    decay_rows = pltpu.VMEM((2, 8 * (s // RWKV_CHUNK), W), jnp.float32)
    return pl.pallas_call(
        _rwkv_scan_kernel,
        out_shape=jax.ShapeDtypeStruct((bsz, s, d), jnp.bfloat16),
        grid=(bsz, n_groups),
        in_specs=[seq_w(0), seq_w(gw), seq_w(2 * gw), seq_n(lora0), seq_n(lora0 + 1), seq_n(lora0 + 2),
                  vec_w(0), vec_w(gw), vec_w(2 * gw), vec_n(lora0), vec_n(lora0 + 1), vec_n(lora0 + 2),
                  two_w, up_w, two_w, up_w, par_w, par_w, par_w,
                  pl.BlockSpec((B_GATE_LORA, W), lambda b, g: (0, g)), par_w, par_w],
        out_specs=pl.BlockSpec((None, s, W), lambda b, g: (b, 0, g)),
        scratch_shapes=[f32_w, b16_w, f32_w, b16_w, f32_2w, b16_2w, b16_2w, f32_2w,
                        b16_2w, b16_2w, b16_2w, b16_2w, b16_2w, b16_2w, b16_2w, decay_rows],
        compiler_params=_compiler_params(("parallel", "parallel")),
        name="rwkv7_scan",
    )(p, p, p, p, p, p, mu2, mu2, mu2, mu2, mu2, mu2,
      w0, w_up.astype(jnp.bfloat16), a0, a_up.astype(jnp.bfloat16), row(k_k), row(k_a), row(r_k),
      g_up.astype(jnp.bfloat16), row(lnx_g), row(lnx_b))


def _axial_rope_tables(s):
    rows = s // GRID_W
    row = jnp.repeat(jnp.arange(rows, dtype=jnp.float32), GRID_W)
    col = jnp.tile(jnp.arange(GRID_W, dtype=jnp.float32), rows)
    n_freq = C_HEAD_DIM // 4
    inv_freq = ROPE_THETA ** (-jnp.arange(n_freq, dtype=jnp.float32) / n_freq)
    ang = jnp.concatenate([row[:, None] * inv_freq, col[:, None] * inv_freq], axis=-1)
    return jnp.cos(ang), jnp.sin(ang)


ATTN_Q_TILE = 256


def _rms_rope(x, gain, cos_f, sin_f):
    xn = x * lax.rsqrt(jnp.mean(x * x, axis=-1, keepdims=True) + RMS_EPS) * gain
    return xn * cos_f + pltpu.roll(xn, C_HEAD_DIM // 2, 1) * sin_f


def _attn_kernel(q_ref, k_ref, v_ref, cq_ref, sq_ref, ck_ref, sk_ref, qg_ref, kg_ref, o_ref, kr_s, va_s):
    f32, bf16 = jnp.float32, jnp.bfloat16
    dh = C_HEAD_DIM

    @pl.when(pl.program_id(2) == 0)
    def _():
        kr_s[...] = _rms_rope(k_ref[...].astype(f32), kg_ref[...], ck_ref[...], sk_ref[...]).astype(bf16)
        va_s[:, :dh] = v_ref[...]
        va_s[:, dh:] = jnp.ones((va_s.shape[0], dh), bf16)

    group = q_ref.shape[1] // dh
    for g in range(group):
        cols = slice(g * dh, (g + 1) * dh)
        q = _rms_rope(q_ref[:, cols].astype(f32), qg_ref[...], cq_ref[...], sq_ref[...]) * dh ** -0.5
        scores = _nt_dot(q.astype(bf16), kr_s[...])
        p = jnp.exp((scores - jnp.max(scores, axis=-1, keepdims=True)).astype(bf16))
        od = jnp.dot(p, va_s[...], preferred_element_type=f32)
        o_ref[:, cols] = (od[:, :dh] / od[:, dh:]).astype(o_ref.dtype)


def _axial_gqa_mixer(x, w_in, q_gain, k_gain):
    bsz, s, d = x.shape
    dh = C_HEAD_DIM
    qh = d // dh
    group = qh // C_KV_HEADS
    n_proj = w_in.shape[1]
    half = jnp.concatenate([jnp.arange(0, dh, 2), jnp.arange(1, dh, 2)])
    n_rot = (qh + C_KV_HEADS) * dh
    w_rot = w_in[:, :n_rot].reshape(d, n_rot // dh, dh // 2, 2).transpose(0, 1, 3, 2).reshape(d, n_rot)
    w = jnp.concatenate([w_rot, w_in[:, n_rot:]], axis=1)
    p = _project(x.reshape(bsz * s, d), w, jnp.zeros((n_proj,), jnp.float32), jnp.bfloat16)
    p = p.reshape(bsz, s, n_proj)
    cos, sin = _axial_rope_tables(s)
    cos_f = jnp.concatenate([cos, cos], axis=-1)
    sin_f = jnp.concatenate([-sin, sin], axis=-1)
    tq = min(ATTN_Q_TILE, s)
    gw = group * dh
    q_rows = pl.BlockSpec((tq, dh), lambda b, h, i: (i, 0))
    k_rows = pl.BlockSpec((s, dh), lambda b, h, i: (0, 0))
    vec = pl.BlockSpec((1, dh), lambda b, h, i: (0, 0))
    return pl.pallas_call(
        _attn_kernel,
        out_shape=jax.ShapeDtypeStruct((bsz, s, d), jnp.bfloat16),
        grid=(bsz, C_KV_HEADS, s // tq),
        in_specs=[pl.BlockSpec((None, tq, gw), lambda b, h, i: (b, i, h)),
                  pl.BlockSpec((None, s, dh), lambda b, h, i: (b, 0, qh + h)),
                  pl.BlockSpec((None, s, dh), lambda b, h, i: (b, 0, qh + C_KV_HEADS + h)),
                  q_rows, q_rows, k_rows, k_rows, vec, vec],
        out_specs=pl.BlockSpec((None, tq, gw), lambda b, h, i: (b, i, h)),
        scratch_shapes=[pltpu.VMEM((s, dh), jnp.bfloat16), pltpu.VMEM((s, 2 * dh), jnp.bfloat16)],
        compiler_params=_compiler_params(("parallel", "parallel", "arbitrary")),
        name="axial_attention",
    )(p, p, p, cos_f, sin_f, cos_f, sin_f, q_gain[half].reshape(1, dh), k_gain[half].reshape(1, dh))


def kernel(x, a_w_in, a_b_in, a_head_gain, a_w_out, b_w_in, b_mu, b_w0, b_w_up, b_a0, b_a_up, b_k_k, b_k_a, b_r_k, b_g_up, b_lnx_g, b_lnx_b, b_w_out, c_w_in, c_q_gain, c_k_gain, c_w_out, ln1_g, ln1_b, moe_w_router, moe_b_router, moe_w_gu, moe_b_gu, moe_w_dn, moe_b_dn, ln2_g, ln2_b):
    bsz, s, d = x.shape
    depth = ln1_g.shape[0]
    xt = x.reshape(bsz * s, d)
    for i in range(depth):
        kind = i % N_MIXERS
        j = i // N_MIXERS
        x3 = xt.reshape(bsz, s, d)
        if kind == 0:
            h = _mlstm_mixer(x3, a_w_in[j], a_b_in[j], a_head_gain[j])
            w_out = a_w_out[j]
        elif kind == 1:
            h = _rwkv7_mixer(x3, b_w_in[j], b_mu[j], b_w0[j], b_w_up[j], b_a0[j], b_a_up[j],
                             b_k_k[j], b_k_a[j], b_r_k[j], b_g_up[j], b_lnx_g[j], b_lnx_b[j])
            w_out = b_w_out[j]
        else:
            h = _axial_gqa_mixer(x3, c_w_in[j], c_q_gain[j], c_k_gain[j])
            w_out = c_w_out[j]
        xt, xt_bf16 = _outproj_ln(h.reshape(bsz * s, d), w_out, xt, ln1_g[i], ln1_b[i])
        w_gu = _regroup_glu_columns(moe_w_gu, i)
        b_gu = _regroup_glu_bias(moe_b_gu[i])
        w_dn = _layer_to_bf16(moe_w_dn, i)
        b_dn = moe_b_dn[i][:, None, :]
        n_group = bsz * s // MOE_TOKEN_GROUPS
        x_next = None
        for grp in range(MOE_TOKEN_GROUPS):
            y_rows, route = _moe_ffn(xt, xt_bf16, grp, n_group, moe_w_router[i], moe_b_router[i],
                                     w_gu, b_gu, w_dn, b_dn)
            x_next = _combine_ln(y_rows, route, xt, grp, ln2_g[i], ln2_b[i], earlier=x_next)
        xt = x_next
    return xt.reshape(bsz, s, d)
```

```python
import functools

import jax
import jax.numpy as jnp
from jax import lax
from jax.experimental import pallas as pl
from jax.experimental.pallas import tpu as pltpu

DEPTH = 4
N_MIXERS = 3
GRID_W = 64
DEEPNORM_ALPHA = (2 * DEPTH) ** 0.25
LN_EPS = 1e-5
RMS_EPS = 1e-6

A_HEADS = 8
A_DQK = 64
A_CHUNK = 64
A_M_INIT = -1e30

B_HEAD = 64
B_DECAY_LORA = 64
B_AAA_LORA = 64
B_GATE_LORA = 128
B_GN_EPS = 64e-5

C_HEAD_DIM = 128
C_KV_HEADS = 2
C_Q_BLOCK = 128
ROPE_THETA = 10000.0

N_EXPERTS = 32
TOP_K = 4
SWIGLU_LIMIT = 7.0
SWIGLU_ALPHA = 1.702

LANES = 128
VMEM_LIMIT_BYTES = 56 * 1024 * 1024

ROW_TILE = 512
MOE_ROW_TILE = 512
MOE_TOKEN_GROUPS = 4


def _compiler_params(semantics):
    return pltpu.CompilerParams(dimension_semantics=semantics, vmem_limit_bytes=VMEM_LIMIT_BYTES)


def _row_tile(m):
    t = min(ROW_TILE, m)
    assert m % t == 0, (m, t)
    return t


def _col_chunk(n):
    return max(c for c in range(LANES, 5 * LANES + 1, LANES) if n % c == 0)


def _proj_kernel(x_ref, w_ref, b_ref, *o_refs):
    xb = x_ref[...].astype(jnp.bfloat16)
    col = 0
    for o_ref in o_refs:
        n = o_ref.shape[1]
        step = _col_chunk(n)
        for j in range(0, n, step):
            acc = jnp.dot(xb, w_ref[:, col + j:col + j + step], preferred_element_type=jnp.float32)
            o_ref[:, j:j + step] = (acc + b_ref[:, col + j:col + j + step]).astype(o_ref.dtype)
        col += n


def _project(x, w, b, out_dtype, f32_tail=0):
    m, k = x.shape
    n = w.shape[1]
    widths = [n - f32_tail, f32_tail] if f32_tail else [n]
    dtypes = [out_dtype, jnp.float32]
    assert all(c % LANES == 0 for c in widths)
    tm = _row_tile(m)
    outs = pl.pallas_call(
        _proj_kernel,
        out_shape=[jax.ShapeDtypeStruct((m, c), dt) for c, dt in zip(widths, dtypes)],
        grid=(m // tm,),
        in_specs=[pl.BlockSpec((tm, k), lambda i: (i, 0)),
                  pl.BlockSpec((k, n), lambda i: (0, 0)),
                  pl.BlockSpec((1, n), lambda i: (0, 0))],
        out_specs=[pl.BlockSpec((tm, c), lambda i: (i, 0)) for c in widths],
        compiler_params=_compiler_params(("parallel",)),
        name="project",
    )(x, w.astype(jnp.bfloat16), b.reshape(1, n).astype(jnp.float32))
    return tuple(outs) if f32_tail else outs[0]


def _pad_cols(w, b, n_pad):
    k, n = w.shape
    if b is None:
        b = jnp.zeros((n,), jnp.float32)
    return jnp.pad(w, ((0, 0), (0, n_pad - n))), jnp.pad(b, (0, n_pad - n))


def _layer_norm_rows(z, g, b):
    mu = jnp.mean(z, axis=-1, keepdims=True)
    zc = z - mu
    var = jnp.mean(zc * zc, axis=-1, keepdims=True)
    return zc * lax.rsqrt(var + LN_EPS) * g + b


def _outproj_ln_kernel(h_ref, w_ref, x_ref, g_ref, b_ref, o_ref, ob_ref):
    mix = jnp.dot(h_ref[...].astype(jnp.bfloat16), w_ref[...], preferred_element_type=jnp.float32)
    z = DEEPNORM_ALPHA * x_ref[...] + mix
    y = _layer_norm_rows(z, g_ref[...], b_ref[...])
    o_ref[...] = y
    ob_ref[...] = y.astype(ob_ref.dtype)


def _outproj_ln(h, w_out, x, g, b):
    m, d = x.shape
    tm = _row_tile(m)
    row = pl.BlockSpec((tm, d), lambda i: (i, 0))
    vec = pl.BlockSpec((1, d), lambda i: (0, 0))
    return pl.pallas_call(
        _outproj_ln_kernel,
        out_shape=[jax.ShapeDtypeStruct((m, d), jnp.float32), jax.ShapeDtypeStruct((m, d), jnp.bfloat16)],
        grid=(m // tm,),
        in_specs=[row, pl.BlockSpec((d, d), lambda i: (0, 0)), row, vec, vec],
        out_specs=[row, row],
        compiler_params=_compiler_params(("parallel",)),
        name="outproj_ln",
    )(h, w_out.astype(jnp.bfloat16), x, g.reshape(1, d), b.reshape(1, d))


ROUTE_IDX_LANE = 0
ROUTE_RANK_LANE = TOP_K
ROUTE_GATE_LANE = 2 * TOP_K


def _combine_ln_kernel(x_ref, g_ref, b_ref, *rest, tiles_per_group):
    o_ref = rest[-1]
    n_groups = (len(rest) - 1) // 2
    group = pl.program_id(0) // tiles_per_group
    for grp in range(n_groups):
        y_ref, r_ref = rest[2 * grp], rest[2 * grp + 1]

        @pl.when(group == grp)
        def _():
            y = jnp.zeros(x_ref.shape, jnp.float32)
            for k in range(TOP_K):
                gate = r_ref[:, ROUTE_GATE_LANE + k:ROUTE_GATE_LANE + k + 1]
                y = y + gate * y_ref[k].astype(jnp.float32)
            z = DEEPNORM_ALPHA * x_ref[...] + y
            o_ref[...] = _layer_norm_rows(z, g_ref[...], b_ref[...])


def _combine_ln(y_rows_groups, route_groups, x, g, b):
    n_groups = len(y_rows_groups)
    m = y_rows_groups[0].shape[1]
    d = x.shape[1]
    tm = _row_tile(m)
    tiles = m // tm
    row = pl.BlockSpec((tm, d), lambda i: (i, 0))
    vec = pl.BlockSpec((1, d), lambda i: (0, 0))
    in_specs = [row, vec, vec]
    operands = [x, g.reshape(1, d), b.reshape(1, d)]
    for grp in range(n_groups):
        local = lambda i, grp=grp: jnp.clip(i - grp * tiles, 0, tiles - 1)
        in_specs += [pl.BlockSpec((TOP_K, tm, d), lambda i, local=local: (0, local(i), 0)),
                     pl.BlockSpec((tm, LANES), lambda i, local=local: (local(i), 0))]
        operands += [y_rows_groups[grp], route_groups[grp]]
    return pl.pallas_call(
        functools.partial(_combine_ln_kernel, tiles_per_group=tiles),
        out_shape=jax.ShapeDtypeStruct(x.shape, jnp.float32),
        grid=(n_groups * tiles,),
        in_specs=in_specs,
        out_specs=row,
        compiler_params=_compiler_params(("parallel",)),
        name="combine_ln",
    )(*operands)


ROUTER_PAD_BIAS = -1e30


def _router_kernel(x_ref, whi_ref, wlo_ref, b_ref, o_ref, cnt_ref, base_s):
    f32 = jnp.float32

    @pl.when(pl.program_id(0) == 0)
    def _():
        base_s[...] = jnp.zeros_like(base_s)

    t = x_ref.shape[0]
    x = x_ref[...]
    x_hi = x.astype(jnp.bfloat16)
    x_lo = (x - x_hi.astype(f32)).astype(jnp.bfloat16)
    logits = (jnp.dot(x_hi, whi_ref[...], preferred_element_type=f32)
              + jnp.dot(x_hi, wlo_ref[...], preferred_element_type=f32)
              + jnp.dot(x_lo, whi_ref[...], preferred_element_type=f32)) + b_ref[...]
    lane = lax.broadcasted_iota(jnp.int32, (t, LANES), 1)
    vals = logits
    tops, idxs, sels = [], [], []
    for _ in range(TOP_K):
        top = jnp.max(vals, axis=-1, keepdims=True)
        idx = jnp.min(jnp.where(vals == top, lane, LANES), axis=-1, keepdims=True)
        sel = lane == idx
        vals = jnp.where(sel, -jnp.inf, vals)
        tops.append(top)
        idxs.append(idx)
        sels.append(sel)
    exps = [jnp.exp(top - tops[0]) for top in tops]
    total = exps[0]
    for e in exps[1:]:
        total = total + e
    chosen = jnp.zeros((t, LANES), f32)
    for sel in sels:
        chosen = chosen + sel.astype(f32)
    earlier = (lax.broadcasted_iota(jnp.int32, (t, t), 0) > lax.broadcasted_iota(jnp.int32, (t, t), 1))
    prefix = jnp.dot(earlier.astype(jnp.bfloat16), chosen.astype(jnp.bfloat16),
                     preferred_element_type=f32) + base_s[...]
    packed = jnp.zeros((t, LANES), f32)
    for k in range(TOP_K):
        rank = jnp.sum(jnp.where(sels[k], prefix, 0.0), axis=-1, keepdims=True)
        packed = jnp.where(lane == ROUTE_IDX_LANE + k, idxs[k].astype(f32), packed)
        packed = jnp.where(lane == ROUTE_RANK_LANE + k, rank, packed)
        packed = jnp.where(lane == ROUTE_GATE_LANE + k, exps[k] / total, packed)
    o_ref[...] = packed
    base_s[...] = base_s[...] + jnp.sum(chosen, axis=0, keepdims=True)
    cnt_ref[...] = base_s[...]


def _route(xt, group, m, w_router, b_router):
    d = xt.shape[1]
    tm = _row_tile(m)
    first = group * (m // tm)
    n_e = w_router.shape[1]
    w = jnp.pad(w_router, ((0, 0), (0, LANES - n_e)))
    b = jnp.pad(b_router, (0, LANES - n_e), constant_values=ROUTER_PAD_BIAS)
    w_hi = w.astype(jnp.bfloat16)
    return pl.pallas_call(
        _router_kernel,
        out_shape=[jax.ShapeDtypeStruct((m, LANES), jnp.float32), jax.ShapeDtypeStruct((1, LANES), jnp.float32)],
        grid=(m // tm,),
        in_specs=[pl.BlockSpec((tm, d), lambda i: (first + i, 0)),
                  pl.BlockSpec((d, LANES), lambda i: (0, 0)),
                  pl.BlockSpec((d, LANES), lambda i: (0, 0)),
                  pl.BlockSpec((1, LANES), lambda i: (0, 0))],
        out_specs=[pl.BlockSpec((tm, LANES), lambda i: (i, 0)), pl.BlockSpec((1, LANES), lambda i: (0, 0))],
        scratch_shapes=[pltpu.VMEM((1, LANES), jnp.float32)],
        compiler_params=_compiler_params(("arbitrary",)),
        name="router",
    )(xt, w_hi, (w - w_hi.astype(jnp.float32)).astype(jnp.bfloat16), b.reshape(1, LANES))


GLU_BLOCK = 2 * LANES


def _regroup_kernel(w_ref, p_ref, o_ref):
    w = w_ref[0].astype(jnp.bfloat16)
    for c in range(0, w.shape[1], GLU_BLOCK):
        o_ref[0, :, c:c + GLU_BLOCK] = jnp.dot(w[:, c:c + GLU_BLOCK], p_ref[...],
                                               preferred_element_type=jnp.float32).astype(o_ref.dtype)


def _regroup_glu_columns(w_gu_layers, layer):
    _, n_e, d, f2 = w_gu_layers.shape
    src = jnp.arange(GLU_BLOCK)
    perm = (src[:, None] == (2 * (src % LANES) + src // LANES)[None, :]).astype(jnp.bfloat16)
    tk = _row_tile(d)
    return pl.pallas_call(
        _regroup_kernel,
        out_shape=jax.ShapeDtypeStruct((n_e, d, f2), jnp.bfloat16),
        grid=(n_e, d // tk),
        in_specs=[pl.BlockSpec((None, 1, tk, f2), lambda e, i: (layer, e, i, 0)),
                  pl.BlockSpec((GLU_BLOCK, GLU_BLOCK), lambda e, i: (0, 0))],
        out_specs=pl.BlockSpec((1, tk, f2), lambda e, i: (e, i, 0)),
        compiler_params=_compiler_params(("parallel", "parallel")),
        name="regroup_glu",
    )(w_gu_layers, perm)


def _cast_kernel(w_ref, o_ref):
    o_ref[...] = w_ref[...].astype(o_ref.dtype)


def _layer_to_bf16(w_layers, layer):
    _, n_e, r, c = w_layers.shape
    tr = _row_tile(r)
    return pl.pallas_call(
        _cast_kernel,
        out_shape=jax.ShapeDtypeStruct((n_e, r, c), jnp.bfloat16),
        grid=(n_e, r // tr),
        in_specs=[pl.BlockSpec((None, 1, tr, c), lambda e, i: (layer, e, i, 0))],
        out_specs=pl.BlockSpec((1, tr, c), lambda e, i: (e, i, 0)),
        compiler_params=_compiler_params(("parallel", "parallel")),
        name="layer_to_bf16",
    )(w_layers)


def _slot_kernel(r_ref, start_ref, o_ref):
    lane = lax.broadcasted_iota(jnp.int32, r_ref.shape, 1)
    route = r_ref[...]
    slots = jnp.zeros(r_ref.shape, jnp.float32)
    for k in range(TOP_K):
        expert = route[:, ROUTE_IDX_LANE + k:ROUTE_IDX_LANE + k + 1].astype(jnp.int32)
        first = jnp.sum(jnp.where(lane == expert, start_ref[...], 0.0), axis=-1, keepdims=True)
        slots = jnp.where(lane == k, first + route[:, ROUTE_RANK_LANE + k:ROUTE_RANK_LANE + k + 1], slots)
    o_ref[...] = slots.astype(jnp.int32)


def _slots(route, expert_start):
    m = route.shape[0]
    tm = _row_tile(m)
    return pl.pallas_call(
        _slot_kernel,
        out_shape=jax.ShapeDtypeStruct((m, LANES), jnp.int32),
        grid=(m // tm,),
        in_specs=[pl.BlockSpec((tm, LANES), lambda i: (i, 0)), pl.BlockSpec((1, LANES), lambda i: (0, 0))],
        out_specs=pl.BlockSpec((tm, LANES), lambda i: (i, 0)),
        compiler_params=_compiler_params(("parallel",)),
        name="moe_slots",
    )(route, expert_start)


def _regroup_glu_bias(b_gu):
    n_e, f2 = b_gu.shape
    return b_gu.reshape(n_e, f2 // GLU_BLOCK, LANES, 2).transpose(0, 1, 3, 2).reshape(n_e, 1, f2)


def _expert_kernel(be_ref, nb_ref, x_ref, wgu_ref, bgu_ref, wd_ref, bd_ref, o_ref):
    @pl.when(pl.program_id(0) < nb_ref[0])
    def _():
        xb = x_ref[...]
        acts = []
        for c in range(0, wgu_ref.shape[2], GLU_BLOCK):
            h = (jnp.dot(xb, wgu_ref[0, :, c:c + GLU_BLOCK], preferred_element_type=jnp.float32)
                 + bgu_ref[0, :, c:c + GLU_BLOCK])
            hg = jnp.minimum(h[:, :LANES], SWIGLU_LIMIT)
            hl = jnp.clip(h[:, LANES:], -SWIGLU_LIMIT, SWIGLU_LIMIT)
            acts.append((hg * jax.nn.sigmoid(SWIGLU_ALPHA * hg) * (hl + 1.0)).astype(jnp.bfloat16))
        act = jnp.concatenate(acts, axis=1)
        y = jnp.dot(act, wd_ref[0], preferred_element_type=jnp.float32) + bd_ref[0]
        o_ref[...] = y.astype(o_ref.dtype)


def _expert_ffn(x_sorted, block_expert, n_used, w_gu, b_gu, w_dn, b_dn):
    cap, d = x_sorted.shape
    f2 = w_gu.shape[2]
    tm = MOE_ROW_TILE
    n_blocks = cap // tm

    def blk(i, be, nb):
        return (jnp.minimum(i, nb[0] - 1), 0)

    def wsel(i, be, nb):
        return (be[i], 0, 0)

    grid_spec = pltpu.PrefetchScalarGridSpec(
        num_scalar_prefetch=2,
        grid=(n_blocks,),
        in_specs=[pl.BlockSpec((tm, d), blk),
                  pl.BlockSpec((1, d, f2), wsel),
                  pl.BlockSpec((1, 1, f2), wsel),
                  pl.BlockSpec((1, f2 // 2, d), wsel),
                  pl.BlockSpec((1, 1, d), wsel)],
        out_specs=pl.BlockSpec((tm, d), blk),
    )
    return pl.pallas_call(
        _expert_kernel,
        out_shape=jax.ShapeDtypeStruct((cap, d), jnp.bfloat16),
        grid_spec=grid_spec,
        compiler_params=_compiler_params(("arbitrary",)),
        name="expert_ffn",
    )(block_expert, n_used, x_sorted, w_gu, b_gu, w_dn, b_dn)


def _moe_ffn(x, x_bf16, group, n, w_router, b_router, w_gu, b_gu, w_dn, b_dn):
    d = x.shape[1]
    tm = MOE_ROW_TILE
    n_e = w_router.shape[1]
    route, counts = _route(x, group, n, w_router, b_router)
    counts = counts[0].astype(jnp.int32)
    padded = (counts + tm - 1) // tm * tm
    pad_end = jnp.cumsum(padded)
    pad_start = pad_end - padded
    cap = n * TOP_K + n_e * tm
    n_blocks = cap // tm
    slot = _slots(route, pad_start.astype(jnp.float32).reshape(1, LANES))[:, :TOP_K]
    slot_kmajor = slot.T.reshape(-1)
    block_first_row = jnp.arange(n_blocks, dtype=jnp.int32) * tm
    block_expert = jnp.minimum(jnp.sum(pad_end[None, :n_e] <= block_first_row[:, None], axis=1), n_e - 1)
    n_used = (pad_end[n_e - 1] // tm).reshape(1)
    top_idx = route[:, ROUTE_IDX_LANE:ROUTE_IDX_LANE + TOP_K].astype(jnp.int32).reshape(-1)
    tok_by_expert = jnp.argsort(top_idx, stable=True).astype(jnp.int32) // TOP_K
    unpadded_start = jnp.cumsum(counts) - counts
    row_in_expert = (block_first_row - pad_start[block_expert])[:, None] + jnp.arange(tm, dtype=jnp.int32)[None, :]
    src = jnp.minimum(unpadded_start[block_expert][:, None] + row_in_expert, n * TOP_K - 1).reshape(-1)
    valid = (row_in_expert < counts[block_expert][:, None]).reshape(-1)
    tok_of_slot = group * n + jnp.where(valid, tok_by_expert.at[src].get(mode="promise_in_bounds"), 0)

    x_sorted = x_bf16.at[tok_of_slot].get(mode="promise_in_bounds")
    yb = _expert_ffn(x_sorted, block_expert.astype(jnp.int32), n_used.astype(jnp.int32), w_gu, b_gu, w_dn, b_dn)
    return yb.at[slot_kmajor].get(mode="promise_in_bounds").reshape(TOP_K, n, d), route


MLSTM_CHUNK = 128
MLSTM_PAIR = 2
MLSTM_NEG = -1e30


def _interleave(generators):
    results = [None] * len(generators)
    live = list(enumerate(generators))
    while live:
        still = []
        for idx, gen in live:
            try:
                next(gen)
                still.append((idx, gen))
            except StopIteration as stop:
                results[idx] = stop.value
        live = still
    return results


def _one_by_one(generators):
    return [_interleave([gen])[0] for gen in generators]


def _cumsum_rows(x, reverse):
    n = x.shape[0]
    row = lax.broadcasted_iota(jnp.int32, x.shape, 0)
    sh = 1
    while sh < n:
        if reverse:
            x = x + jnp.where(row < n - sh, pltpu.roll(x, n - sh, 0), 0.0)
        else:
            x = x + jnp.where(row >= sh, pltpu.roll(x, sh, 0), 0.0)
        sh *= 2
    return x


def _cummax_rows(x, reverse):
    n = x.shape[0]
    row = lax.broadcasted_iota(jnp.int32, x.shape, 0)
    sh = 1
    while sh < n:
        if reverse:
            x = jnp.maximum(x, jnp.where(row < n - sh, pltpu.roll(x, n - sh, 0), -jnp.inf))
        else:
            x = jnp.maximum(x, jnp.where(row >= sh, pltpu.roll(x, sh, 0), -jnp.inf))
        sh *= 2
    return x


def _mlstm_kernel(q_ref, k_ref, v_ref, og_ref, g_ref, gain_ref, o_ref, h_s, c_s):
    f32, bf16 = jnp.float32, jnp.bfloat16
    seq = q_ref.shape[0]
    L = MLSTM_CHUNK
    n_chunks = seq // L
    dv = v_ref.shape[1] // MLSTM_PAIR
    dqk = q_ref.shape[1] // MLSTM_PAIR

    lane_q = lax.broadcasted_iota(jnp.int32, (1, q_ref.shape[1]), 1) // dqk
    lane_g = lax.broadcasted_iota(jnp.int32, (1, LANES), 1)
    is_forget_lane = (lane_g // MLSTM_PAIR) % 2 == 1
    row = lax.broadcasted_iota(jnp.int32, (L, L), 0)
    col = lax.broadcasted_iota(jnp.int32, (L, L), 1)
    causal = (col <= row, col >= row)
    ones_v = jnp.ones((L, dv), bf16)

    c_s[...] = jnp.zeros_like(c_s)

    def gate_terms(start, d):
        rows = pl.ds(start, L)
        gates = g_ref[rows, :]
        log_f = jnp.minimum(gates, 0.0) - jnp.log(1.0 + jnp.exp(-jnp.abs(gates)))
        cum = _cumsum_rows(log_f, reverse=(d == 1))
        src = gates - pltpu.roll(cum, LANES - MLSTM_PAIR, 1)
        run_max = _cummax_rows(src, reverse=(d == 1))
        return rows, gates, cum, src.T, run_max, q_ref[rows, :], k_ref[rows, :]

    def head_unit(terms, d, j, m_st):
        rows, gates, cum, src_t, run_max, q_all, k_all = terms
        lane_i = 2 * MLSTM_PAIR * d + j
        lane_f = lane_i + MLSTM_PAIR
        b_col, i_col = cum[:, lane_f:lane_f + 1], gates[:, lane_i:lane_i + 1]
        src_row = src_t[lane_i:lane_i + 1, :]
        qz = jnp.where(lane_q == j, q_all, jnp.zeros_like(q_all))
        kz = jnp.where(lane_q == j, k_all, jnp.zeros_like(k_all))
        state = c_s[2 * d + j]
        scores = _nt_dot(qz, kz)
        carried = jnp.dot(qz, state.astype(bf16), preferred_element_type=f32)
        yield
        mu = jnp.maximum(m_st, run_max[:, lane_i:lane_i + 1])
        v_aug = jnp.concatenate([v_ref[rows, j * dv:(j + 1) * dv], ones_v], axis=1)
        qk = scores * (A_DQK ** -0.5) * jnp.exp(jnp.where(causal[d], src_row - mu, MLSTM_NEG))
        edge = L - 1 if d == 0 else 0
        b_last = b_col[edge:edge + 1, :]
        m_new = b_last + mu[edge:edge + 1, :]
        w_s = jnp.exp(b_last - b_col + i_col - m_new)
        local = jnp.dot(qk.astype(bf16), v_aug, preferred_element_type=f32)
        update = _tn_dot((w_s * kz.astype(f32)).astype(bf16), v_aug)
        yield
        nd = local + jnp.exp(m_st - mu) * (A_DQK ** -0.5) * carried
        h_s[d, rows, j * dv:(j + 1) * dv] = nd[:, :dv] / jnp.maximum(jnp.abs(nd[:, dv:]), jnp.exp(-b_col - mu))
        c_s[2 * d + j] = jnp.exp(b_last + m_st - m_new) * state + update
        return m_new

    def body(c, m_all):
        terms = (gate_terms(pl.multiple_of(c * L, L), 0), gate_terms(pl.multiple_of((n_chunks - 1 - c) * L, L), 1))
        return tuple(_one_by_one([head_unit(terms[d], d, j, m_all[MLSTM_PAIR * d + j])
                                  for d in range(2) for j in range(MLSTM_PAIR)]))

    m_init = tuple(jnp.full((1, 1), A_M_INIT, f32) for _ in range(2 * MLSTM_PAIR))
    lax.fori_loop(0, n_chunks, body, m_init)

    def finish(c, carry):
        rows = pl.ds(pl.multiple_of(c * L, L), L)
        for j in range(MLSTM_PAIR):
            cols = slice(j * dv, (j + 1) * dv)
            h = h_s[0, rows, cols] + h_s[1, rows, cols]
            hn = h * lax.rsqrt(jnp.mean(h * h, axis=-1, keepdims=True) + RMS_EPS) * gain_ref[:, cols]
            o_ref[rows, cols] = (hn * jax.nn.sigmoid(og_ref[rows, cols].astype(f32))).astype(o_ref.dtype)
        return carry

    lax.fori_loop(0, n_chunks, finish, 0)


def _mlstm_mixer(x, w_in, b_in, head_gain):
    bsz, s, d = x.shape
    a_qk = A_HEADS * A_DQK
    dv = d // A_HEADS
    n_main = 2 * a_qk + 2 * d
    n_pairs = A_HEADS // MLSTM_PAIR
    pw_qk = MLSTM_PAIR * A_DQK
    pw_v = MLSTM_PAIR * dv
    assert pw_qk == LANES and s % MLSTM_CHUNK == 0
    gate_cols = jnp.asarray([[n_main + t * A_HEADS + MLSTM_PAIR * hp + j for t in range(4) for j in range(MLSTM_PAIR)]
                             for hp in range(n_pairs)])
    n_gate = gate_cols.shape[1]
    w_tail = jnp.pad(w_in[:, gate_cols], ((0, 0), (0, 0), (0, LANES - n_gate))).reshape(d, n_pairs * LANES)
    b_tail = jnp.pad(b_in[gate_cols], ((0, 0), (0, LANES - n_gate))).reshape(n_pairs * LANES)
    w = jnp.concatenate([w_in[:, :n_main], w_tail], axis=1)
    b = jnp.concatenate([b_in[:n_main], b_tail])
    p, gates = _project(x.reshape(bsz * s, d), w, b, jnp.bfloat16, f32_tail=n_pairs * LANES)
    p = p.reshape(bsz, s, n_main)
    gates = gates.reshape(bsz, s, n_pairs * LANES)
    k0 = a_qk // pw_qk
    v0 = 2 * a_qk // pw_v
    o0 = (2 * a_qk + d) // pw_v
    return pl.pallas_call(
        _mlstm_kernel,
        out_shape=jax.ShapeDtypeStruct((bsz, s, d), jnp.bfloat16),
        grid=(bsz, n_pairs),
        in_specs=[pl.BlockSpec((None, s, pw_qk), lambda b, h: (b, 0, h)),
                  pl.BlockSpec((None, s, pw_qk), lambda b, h: (b, 0, k0 + h)),
                  pl.BlockSpec((None, s, pw_v), lambda b, h: (b, 0, v0 + h)),
                  pl.BlockSpec((None, s, pw_v), lambda b, h: (b, 0, o0 + h)),
                  pl.BlockSpec((None, s, LANES), lambda b, h: (b, 0, h)),
                  pl.BlockSpec((1, pw_v), lambda b, h: (0, h))],
        out_specs=pl.BlockSpec((None, s, pw_v), lambda b, h: (b, 0, h)),
        scratch_shapes=[pltpu.VMEM((2, s, pw_v), jnp.float32),
                        pltpu.VMEM((2 * MLSTM_PAIR, pw_qk, 2 * dv), jnp.float32)],
        compiler_params=_compiler_params(("parallel", "parallel")),
        name="mlstm_scan",
    )(p, p, p, p, gates, head_gain.reshape(1, d))


RWKV_CHUNK = 64
RWKV_GROUP = 4
RWKV_LANES = RWKV_GROUP * B_HEAD
RWKV_PREP_ROWS = 256
RWKV_LOCAL_UNROLL = 4


def _f32_dot(a, b_bf16):
    hi = a.astype(jnp.bfloat16)
    lo = (a - hi.astype(jnp.float32)).astype(jnp.bfloat16)
    return (jnp.dot(hi, b_bf16, preferred_element_type=jnp.float32)
            + jnp.dot(lo, b_bf16, preferred_element_type=jnp.float32))


def _nt_dot(a, b):
    return lax.dot_general(a, b, (((1,), (1,)), ((), ())), preferred_element_type=jnp.float32)


def _tn_dot(a, b):
    return lax.dot_general(a, b, (((0,), (0,)), ((), ())), preferred_element_type=jnp.float32)


def _rwkv_scan_kernel(r_ref, k_ref, v_ref, wl_ref, al_ref, gl_ref, mu_r_ref, mu_k_ref, mu_v_ref, mu_wl_ref,
                      mu_al_ref, mu_gl_ref, w0_ref, wup_ref, a0_ref, aup_ref, kk_ref, ka_ref, rk_ref, gup_ref,
                      lng_ref, lnb_ref, o_ref,
                      r_s, v_s, kap_s, g_s, logw_s, kh_s, beta_s, y_s,
                      wmat_s, uloc_s, avq_s, rt_s, aqb_s, kend_s, bend_s, ptot_s):
    f32, bf16 = jnp.float32, jnp.bfloat16
    seq = r_ref.shape[0]
    L, W, RT = RWKV_CHUNK, RWKV_LANES, RWKV_PREP_ROWS
    n_chunks = seq // L
    n_prep = seq // RT

    lane = lax.broadcasted_iota(jnp.int32, (1, W), 1)
    head_masks = [(lane // B_HEAD) == h for h in range(RWKV_GROUP)]
    ones_bd = ((lax.broadcasted_iota(jnp.int32, (W, W), 0) // B_HEAD)
               == (lax.broadcasted_iota(jnp.int32, (W, W), 1) // B_HEAD)).astype(bf16)

    def block_diag(a):
        zero = jnp.zeros_like(a)
        return jnp.concatenate([jnp.where(m, a, zero) for m in head_masks], axis=0)

    def seg_sum(a):
        return _f32_dot(a, ones_bd)

    def shifted(ref, mu_ref, i, rows):
        o = pl.multiple_of(i * RT, RT)
        x = ref[pl.ds(o, RT), :].astype(f32)
        before = ref[pl.ds(pl.multiple_of(jnp.maximum(o - 16, 0), 16), 16), :].astype(f32)[15:16, :]
        after = ref[pl.ds(pl.multiple_of(jnp.minimum(o + RT, seq - 16), 16), 16), :].astype(f32)[0:1, :]
        before = jnp.where(i > 0, before, 0.0)
        after = jnp.where(i < n_prep - 1, after, 0.0)
        prev = jnp.where(rows == 0, before, pltpu.roll(x, 1, 0))
        nxt = jnp.where(rows == RT - 1, after, pltpu.roll(x, RT - 1, 0))
        return x + mu_ref[...] * (0.5 * (prev + nxt) - x)

    def prep(i, carry):
        o = pl.multiple_of(i * RT, RT)
        rows_w = lax.broadcasted_iota(jnp.int32, (RT, W), 0)
        rows_n = lax.broadcasted_iota(jnp.int32, (RT, wl_ref.shape[1]), 0)
        r = shifted(r_ref, mu_r_ref, i, rows_w)
        k = shifted(k_ref, mu_k_ref, i, rows_w)
        v = shifted(v_ref, mu_v_ref, i, rows_w)
        wl = shifted(wl_ref, mu_wl_ref, i, rows_n)
        al = shifted(al_ref, mu_al_ref, i, rows_n)
        gl = shifted(gl_ref, mu_gl_ref, i, rows_n)
        kk0 = k * kk_ref[...]
        kap = kk0 / jnp.maximum(jnp.sqrt(seg_sum(kk0 * kk0)), 1e-12)
        r_s[pl.ds(o, RT), :] = r
        v_s[pl.ds(o, RT), :] = v.astype(bf16)
        kap_s[pl.ds(o, RT), :] = kap
        g_s[pl.ds(o, RT), :] = jnp.dot(jax.nn.sigmoid(gl).astype(bf16), gup_ref[...],
                                       preferred_element_type=f32).astype(bf16)
        for d in range(2):
            wl_d = jnp.tanh(wl[:, d * B_DECAY_LORA:(d + 1) * B_DECAY_LORA]).astype(bf16)
            al_d = al[:, d * B_AAA_LORA:(d + 1) * B_AAA_LORA].astype(bf16)
            w_raw = w0_ref[d:d + 1, :] + jnp.dot(wl_d, wup_ref[d], preferred_element_type=f32)
            a = jax.nn.sigmoid(a0_ref[d:d + 1, :] + jnp.dot(al_d, aup_ref[d], preferred_element_type=f32))
            logw_s[d, pl.ds(o, RT), :] = -jnp.exp(-0.5) * jax.nn.sigmoid(w_raw)
            kh_s[d, pl.ds(o, RT), :] = (k * (1.0 + (a - 1.0) * ka_ref[...])).astype(bf16)
            beta_s[d, pl.ds(o, RT), :] = (kap * a).astype(bf16)
        return carry

    lax.fori_loop(0, n_prep, prep, 0)

    row = lax.broadcasted_iota(jnp.int32, (L, W), 0)
    col = lax.broadcasted_iota(jnp.int32, (L, W), 1) % B_HEAD
    eye_cat = (row == col).astype(f32)
    strict = (row > col, row < col)
    incl = (row >= col, row <= col)

    def cumsum_rows(x, reverse):
        sh = 1
        while sh < L:
            if reverse:
                x = x + jnp.where(row < L - sh, pltpu.roll(x, L - sh, 0), 0.0)
            else:
                x = x + jnp.where(row >= sh, pltpu.roll(x, sh, 0), 0.0)
            sh *= 2
        return x

    interleave = _interleave

    def local_part(c, d):
        rows = pl.ds(pl.multiple_of(c * L, L), L)
        r, kap = r_s[rows, :], kap_s[rows, :]
        logw, kh, beta = logw_s[d, rows, :], kh_s[d, rows, :].astype(f32), beta_s[d, rows, :].astype(f32)
        cum = cumsum_rows(logw, reverse=(d == 1))
        tot = cum[L - 1:L, :] if d == 0 else cum[0:1, :]
        p_in, p_inv, p_end = jnp.exp(cum), jnp.exp(-cum), jnp.exp(tot - cum)
        kap_t = (kap * jnp.exp(cum - logw)).astype(bf16)
        r_t = (r * p_in).astype(bf16)
        v_bd = block_diag(v_s[rows, :])
        g_all = _nt_dot(jnp.concatenate([kap_t, r_t], axis=0),
                        jnp.concatenate([block_diag((beta * p_inv).astype(bf16)),
                                         block_diag((kh * p_inv).astype(bf16))], axis=0))
        yield
        x_pow = -jnp.where(strict[d], g_all[:L, :W], 0.0)
        a_ak = jnp.where(strict[d], g_all[:L, W:], 0.0)
        a_qb = jnp.where(incl[d], g_all[L:, :W], 0.0)
        a_qk = jnp.where(incl[d], g_all[L:, W:], 0.0)
        t_inv = eye_cat + x_pow
        av = jnp.dot(jnp.concatenate([a_ak, a_qk], axis=0).astype(bf16), v_bd, preferred_element_type=f32)
        x_pow = jnp.dot(x_pow.astype(bf16), block_diag(x_pow.astype(bf16)), preferred_element_type=f32)
        yield
        n_steps = L.bit_length() - 2
        for step in range(n_steps):
            if step < n_steps - 1:
                both = jnp.dot(jnp.concatenate([x_pow, t_inv], axis=0).astype(bf16),
                               block_diag(x_pow.astype(bf16)), preferred_element_type=f32)
                yield
                x_pow, t_inv = both[:L], t_inv + both[L:]
            else:
                last = jnp.dot(t_inv.astype(bf16), block_diag(x_pow.astype(bf16)), preferred_element_type=f32)
                yield
                t_inv = t_inv + last
        t_b = t_inv.astype(bf16)
        w_mat = jnp.dot(t_b, block_diag(kap_t), preferred_element_type=f32)
        u_loc = jnp.dot(t_b, block_diag(av[:L].astype(bf16)), preferred_element_type=f32)
        yield
        wmat_s[d, rows, :] = w_mat.astype(bf16)
        uloc_s[d, rows, :] = u_loc.astype(bf16)
        avq_s[d, rows, :] = av[L:].astype(bf16)
        rt_s[d, rows, :] = r_t
        aqb_s[d, rows, :] = a_qb.astype(bf16)
        kend_s[d, rows, :] = (kh * p_end).astype(bf16)
        bend_s[d, rows, :] = (beta * p_end).astype(bf16)
        ptot_s[d, pl.ds(pl.multiple_of(c * 8, 8), 8), :] = jnp.broadcast_to(jnp.exp(tot), (8, W))

    def local_body(i, carry):
        interleave([local_part(i * RWKV_LOCAL_UNROLL + j, d) for j in range(RWKV_LOCAL_UNROLL) for d in range(2)])
        return carry

    lax.fori_loop(0, n_chunks // RWKV_LOCAL_UNROLL, local_body, 0)

    def carried_part(c, state, d):
        rows = pl.ds(pl.multiple_of(c * L, L), L)
        ws = _nt_dot(jnp.concatenate([wmat_s[d, rows, :], rt_s[d, rows, :]], axis=0), block_diag(state.astype(bf16)))
        yield
        u = ws[:L] + uloc_s[d, rows, :].astype(f32)
        u_b = u.astype(bf16)
        full = _tn_dot(jnp.concatenate([v_s[rows, :], -u_b], axis=0),
                       jnp.concatenate([kend_s[d, rows, :], bend_s[d, rows, :]], axis=0))
        y_loc = jnp.dot(aqb_s[d, rows, :], block_diag(u_b), preferred_element_type=f32)
        yield
        new_state = state * ptot_s[d, pl.ds(pl.multiple_of(c * 8, 8), 8), :][0:1, :]
        for h, m in enumerate(head_masks):
            new_state = new_state + jnp.where(m, full[h * B_HEAD:(h + 1) * B_HEAD, :], 0.0)
        y_s[d, rows, :] = ws[L:] + avq_s[d, rows, :].astype(f32) - y_loc
        return new_state

    def carried_body(c, states):
        return tuple(interleave([carried_part(c, states[0], 0), carried_part(n_chunks - 1 - c, states[1], 1)]))

    zero_state = jnp.zeros((B_HEAD, W), f32)
    lax.fori_loop(0, n_chunks, carried_body, (zero_state, zero_state))

    def finish(i, carry):
        rows = pl.ds(pl.multiple_of(i * RT, RT), RT)
        y = y_s[0, rows, :] + y_s[1, rows, :]
        mean = seg_sum(y) * (1.0 / B_HEAD)
        yc = y - mean
        var = seg_sum(yc * yc) * (1.0 / B_HEAD)
        yn = yc * lax.rsqrt(var + B_GN_EPS) * lng_ref[...] + lnb_ref[...]
        kh_both = kh_s[0, rows, :].astype(f32) + kh_s[1, rows, :].astype(f32)
        bonus = seg_sum(r_s[rows, :] * kh_both * rk_ref[...]) * v_s[rows, :].astype(f32)
        o_ref[rows, :] = ((yn + bonus) * g_s[rows, :].astype(f32)).astype(o_ref.dtype)
        return carry

    lax.fori_loop(0, n_prep, finish, 0)


def _rwkv7_mixer(x, w_in, mu, w0, w_up, a0, a_up, k_k, k_a, r_k, g_up, lnx_g, lnx_b):
    bsz, s, d = x.shape
    n_proj = w_in.shape[1]
    W = RWKV_LANES
    assert d % W == 0 and s % RWKV_PREP_ROWS == 0 and B_GATE_LORA == LANES
    assert 2 * B_DECAY_LORA == LANES and 2 * B_AAA_LORA == LANES
    p = _project(x.reshape(bsz * s, d), w_in, jnp.zeros((n_proj,), jnp.float32), jnp.bfloat16)
    p = p.reshape(bsz, s, n_proj)
    n_groups = d // W
    gw = d // W
    lora0 = 3 * d // LANES

    def seq_w(off):
        return pl.BlockSpec((None, s, W), lambda b, g, off=off: (b, 0, off + g))

    def seq_n(idx):
        return pl.BlockSpec((None, s, LANES), lambda b, g, idx=idx: (b, 0, idx))

    def vec_w(off):
        return pl.BlockSpec((1, W), lambda b, g, off=off: (0, off + g))

    def vec_n(idx):
        return pl.BlockSpec((1, LANES), lambda b, g, idx=idx: (0, idx))

    par_w = pl.BlockSpec((1, W), lambda b, g: (0, g))
    two_w = pl.BlockSpec((2, W), lambda b, g: (0, g))
    up_w = pl.BlockSpec((2, B_DECAY_LORA, W), lambda b, g: (0, 0, g))
    mu2 = mu.reshape(1, n_proj)
    row = lambda a: a.reshape(1, d)
    f32_w = pltpu.VMEM((s, W), jnp.float32)
    b16_w = pltpu.VMEM((s, W), jnp.bfloat16)
    f32_2w = pltpu.VMEM((2, s, W), jnp.float32)
    b16_2w = pltpu.VMEM((2, s, W), jnp.bfloat16)
    decay_rows = pltpu.VMEM((2, 8 * (s // RWKV_CHUNK), W), jnp.float32)
    return pl.pallas_call(
        _rwkv_scan_kernel,
        out_shape=jax.ShapeDtypeStruct((bsz, s, d), jnp.bfloat16),
        grid=(bsz, n_groups),
        in_specs=[seq_w(0), seq_w(gw), seq_w(2 * gw), seq_n(lora0), seq_n(lora0 + 1), seq_n(lora0 + 2),
                  vec_w(0), vec_w(gw), vec_w(2 * gw), vec_n(lora0), vec_n(lora0 + 1), vec_n(lora0 + 2),
                  two_w, up_w, two_w, up_w, par_w, par_w, par_w,
                  pl.BlockSpec((B_GATE_LORA, W), lambda b, g: (0, g)), par_w, par_w],
        out_specs=pl.BlockSpec((None, s, W), lambda b, g: (b, 0, g)),
        scratch_shapes=[f32_w, b16_w, f32_w, b16_w, f32_2w, b16_2w, b16_2w, f32_2w,
                        b16_2w, b16_2w, b16_2w, b16_2w, b16_2w, b16_2w, b16_2w, decay_rows],
        compiler_params=_compiler_params(("parallel", "parallel")),
        name="rwkv7_scan",
    )(p, p, p, p, p, p, mu2, mu2, mu2, mu2, mu2, mu2,
      w0, w_up.astype(jnp.bfloat16), a0, a_up.astype(jnp.bfloat16), row(k_k), row(k_a), row(r_k),
      g_up.astype(jnp.bfloat16), row(lnx_g), row(lnx_b))


def _axial_rope_tables(s):
    rows = s // GRID_W
    row = jnp.repeat(jnp.arange(rows, dtype=jnp.float32), GRID_W)
    col = jnp.tile(jnp.arange(GRID_W, dtype=jnp.float32), rows)
    n_freq = C_HEAD_DIM // 4
    inv_freq = ROPE_THETA ** (-jnp.arange(n_freq, dtype=jnp.float32) / n_freq)
    ang = jnp.concatenate([row[:, None] * inv_freq, col[:, None] * inv_freq], axis=-1)
    return jnp.cos(ang), jnp.sin(ang)


ATTN_Q_TILE = 256


def _rms_rope(x, gain, cos_f, sin_f):
    xn = x * lax.rsqrt(jnp.mean(x * x, axis=-1, keepdims=True) + RMS_EPS) * gain
    return xn * cos_f + pltpu.roll(xn, C_HEAD_DIM // 2, 1) * sin_f


def _attn_kernel(q_ref, k_ref, v_ref, cq_ref, sq_ref, ck_ref, sk_ref, qg_ref, kg_ref, o_ref, kr_s, va_s):
    f32, bf16 = jnp.float32, jnp.bfloat16
    dh = C_HEAD_DIM

    @pl.when(pl.program_id(2) == 0)
    def _():
        kr_s[...] = _rms_rope(k_ref[...].astype(f32), kg_ref[...], ck_ref[...], sk_ref[...]).astype(bf16)
        va_s[:, :dh] = v_ref[...]
        va_s[:, dh:] = jnp.ones((va_s.shape[0], dh), bf16)

    group = q_ref.shape[1] // dh
    for g in range(group):
        cols = slice(g * dh, (g + 1) * dh)
        q = _rms_rope(q_ref[:, cols].astype(f32), qg_ref[...], cq_ref[...], sq_ref[...]) * dh ** -0.5
        scores = _nt_dot(q.astype(bf16), kr_s[...])
        p = jnp.exp((scores - jnp.max(scores, axis=-1, keepdims=True)).astype(bf16))
        od = jnp.dot(p, va_s[...], preferred_element_type=f32)
        o_ref[:, cols] = (od[:, :dh] / od[:, dh:]).astype(o_ref.dtype)


def _axial_gqa_mixer(x, w_in, q_gain, k_gain):
    bsz, s, d = x.shape
    dh = C_HEAD_DIM
    qh = d // dh
    group = qh // C_KV_HEADS
    n_proj = w_in.shape[1]
    half = jnp.concatenate([jnp.arange(0, dh, 2), jnp.arange(1, dh, 2)])
    n_rot = (qh + C_KV_HEADS) * dh
    w_rot = w_in[:, :n_rot].reshape(d, n_rot // dh, dh // 2, 2).transpose(0, 1, 3, 2).reshape(d, n_rot)
    w = jnp.concatenate([w_rot, w_in[:, n_rot:]], axis=1)
    p = _project(x.reshape(bsz * s, d), w, jnp.zeros((n_proj,), jnp.float32), jnp.bfloat16)
    p = p.reshape(bsz, s, n_proj)
    cos, sin = _axial_rope_tables(s)
    cos_f = jnp.concatenate([cos, cos], axis=-1)
    sin_f = jnp.concatenate([-sin, sin], axis=-1)
    tq = min(ATTN_Q_TILE, s)
    gw = group * dh
    q_rows = pl.BlockSpec((tq, dh), lambda b, h, i: (i, 0))
    k_rows = pl.BlockSpec((s, dh), lambda b, h, i: (0, 0))
    vec = pl.BlockSpec((1, dh), lambda b, h, i: (0, 0))
    return pl.pallas_call(
        _attn_kernel,
        out_shape=jax.ShapeDtypeStruct((bsz, s, d), jnp.bfloat16),
        grid=(bsz, C_KV_HEADS, s // tq),
        in_specs=[pl.BlockSpec((None, tq, gw), lambda b, h, i: (b, i, h)),
                  pl.BlockSpec((None, s, dh), lambda b, h, i: (b, 0, qh + h)),
                  pl.BlockSpec((None, s, dh), lambda b, h, i: (b, 0, qh + C_KV_HEADS + h)),
                  q_rows, q_rows, k_rows, k_rows, vec, vec],
        out_specs=pl.BlockSpec((None, tq, gw), lambda b, h, i: (b, i, h)),
        scratch_shapes=[pltpu.VMEM((s, dh), jnp.bfloat16), pltpu.VMEM((s, 2 * dh), jnp.bfloat16)],
        compiler_params=_compiler_params(("parallel", "parallel", "arbitrary")),
        name="axial_attention",
    )(p, p, p, cos_f, sin_f, cos_f, sin_f, q_gain[half].reshape(1, dh), k_gain[half].reshape(1, dh))


def kernel(x, a_w_in, a_b_in, a_head_gain, a_w_out, b_w_in, b_mu, b_w0, b_w_up, b_a0, b_a_up, b_k_k, b_k_a, b_r_k, b_g_up, b_lnx_g, b_lnx_b, b_w_out, c_w_in, c_q_gain, c_k_gain, c_w_out, ln1_g, ln1_b, moe_w_router, moe_b_router, moe_w_gu, moe_b_gu, moe_w_dn, moe_b_dn, ln2_g, ln2_b):
    bsz, s, d = x.shape
    depth = ln1_g.shape[0]
    xt = x.reshape(bsz * s, d)
    for i in range(depth):
        kind = i % N_MIXERS
        j = i // N_MIXERS
        x3 = xt.reshape(bsz, s, d)
        if kind == 0:
            h = _mlstm_mixer(x3, a_w_in[j], a_b_in[j], a_head_gain[j])
            w_out = a_w_out[j]
        elif kind == 1:
            h = _rwkv7_mixer(x3, b_w_in[j], b_mu[j], b_w0[j], b_w_up[j], b_a0[j], b_a_up[j],
                             b_k_k[j], b_k_a[j], b_r_k[j], b_g_up[j], b_lnx_g[j], b_lnx_b[j])
            w_out = b_w_out[j]
        else:
            h = _axial_gqa_mixer(x3, c_w_in[j], c_q_gain[j], c_k_gain[j])
            w_out = c_w_out[j]
        xt, xt_bf16 = _outproj_ln(h.reshape(bsz * s, d), w_out, xt, ln1_g[i], ln1_b[i])
        w_gu = _regroup_glu_columns(moe_w_gu, i)
        b_gu = _regroup_glu_bias(moe_b_gu[i])
        w_dn = _layer_to_bf16(moe_w_dn, i)
        b_dn = moe_b_dn[i][:, None, :]
        n_group = bsz * s // MOE_TOKEN_GROUPS
        groups = [_moe_ffn(xt, xt_bf16, grp, n_group, moe_w_router[i], moe_b_router[i], w_gu, b_gu, w_dn, b_dn)
                  for grp in range(MOE_TOKEN_GROUPS)]
        xt = _combine_ln([y for y, _ in groups], [r for _, r in groups], xt, ln2_g[i], ln2_b[i])
    return xt.reshape(bsz, s, d)
```

```python
import functools

import jax
import jax.numpy as jnp
from jax import lax
from jax.experimental import pallas as pl
from jax.experimental.pallas import tpu as pltpu

DEPTH = 4
N_MIXERS = 3
GRID_W = 64
DEEPNORM_ALPHA = (2 * DEPTH) ** 0.25
LN_EPS = 1e-5
RMS_EPS = 1e-6

A_HEADS = 8
A_DQK = 64
A_CHUNK = 64
A_M_INIT = -1e30

B_HEAD = 64
B_DECAY_LORA = 64
B_AAA_LORA = 64
B_GATE_LORA = 128
B_GN_EPS = 64e-5

C_HEAD_DIM = 128
C_KV_HEADS = 2
C_Q_BLOCK = 128
ROPE_THETA = 10000.0

N_EXPERTS = 32
TOP_K = 4
SWIGLU_LIMIT = 7.0
SWIGLU_ALPHA = 1.702

LANES = 128
VMEM_LIMIT_BYTES = 56 * 1024 * 1024

ROW_TILE = 512
MOE_ROW_TILE = 512
MOE_TOKEN_GROUPS = 2


def _compiler_params(semantics):
    return pltpu.CompilerParams(dimension_semantics=semantics, vmem_limit_bytes=VMEM_LIMIT_BYTES)


def _row_tile(m):
    t = min(ROW_TILE, m)
    assert m % t == 0, (m, t)
    return t


def _col_chunk(n):
    return max(c for c in range(LANES, 5 * LANES + 1, LANES) if n % c == 0)


def _proj_kernel(x_ref, w_ref, b_ref, *o_refs):
    xb = x_ref[...].astype(jnp.bfloat16)
    col = 0
    for o_ref in o_refs:
        n = o_ref.shape[1]
        step = _col_chunk(n)
        for j in range(0, n, step):
            acc = jnp.dot(xb, w_ref[:, col + j:col + j + step], preferred_element_type=jnp.float32)
            o_ref[:, j:j + step] = (acc + b_ref[:, col + j:col + j + step]).astype(o_ref.dtype)
        col += n


def _project(x, w, b, out_dtype, f32_tail=0):
    m, k = x.shape
    n = w.shape[1]
    widths = [n - f32_tail, f32_tail] if f32_tail else [n]
    dtypes = [out_dtype, jnp.float32]
    assert all(c % LANES == 0 for c in widths)
    tm = _row_tile(m)
    outs = pl.pallas_call(
        _proj_kernel,
        out_shape=[jax.ShapeDtypeStruct((m, c), dt) for c, dt in zip(widths, dtypes)],
        grid=(m // tm,),
        in_specs=[pl.BlockSpec((tm, k), lambda i: (i, 0)),
                  pl.BlockSpec((k, n), lambda i: (0, 0)),
                  pl.BlockSpec((1, n), lambda i: (0, 0))],
        out_specs=[pl.BlockSpec((tm, c), lambda i: (i, 0)) for c in widths],
        compiler_params=_compiler_params(("parallel",)),
        name="project",
    )(x, w.astype(jnp.bfloat16), b.reshape(1, n).astype(jnp.float32))
    return tuple(outs) if f32_tail else outs[0]


def _pad_cols(w, b, n_pad):
    k, n = w.shape
    if b is None:
        b = jnp.zeros((n,), jnp.float32)
    return jnp.pad(w, ((0, 0), (0, n_pad - n))), jnp.pad(b, (0, n_pad - n))


def _layer_norm_rows(z, g, b):
    mu = jnp.mean(z, axis=-1, keepdims=True)
    zc = z - mu
    var = jnp.mean(zc * zc, axis=-1, keepdims=True)
    return zc * lax.rsqrt(var + LN_EPS) * g + b


def _outproj_ln_kernel(h_ref, w_ref, x_ref, g_ref, b_ref, o_ref, ob_ref):
    mix = jnp.dot(h_ref[...].astype(jnp.bfloat16), w_ref[...], preferred_element_type=jnp.float32)
    z = DEEPNORM_ALPHA * x_ref[...] + mix
    y = _layer_norm_rows(z, g_ref[...], b_ref[...])
    o_ref[...] = y
    ob_ref[...] = y.astype(ob_ref.dtype)


def _outproj_ln(h, w_out, x, g, b):
    m, d = x.shape
    tm = _row_tile(m)
    row = pl.BlockSpec((tm, d), lambda i: (i, 0))
    vec = pl.BlockSpec((1, d), lambda i: (0, 0))
    return pl.pallas_call(
        _outproj_ln_kernel,
        out_shape=[jax.ShapeDtypeStruct((m, d), jnp.float32), jax.ShapeDtypeStruct((m, d), jnp.bfloat16)],
        grid=(m // tm,),
        in_specs=[row, pl.BlockSpec((d, d), lambda i: (0, 0)), row, vec, vec],
        out_specs=[row, row],
        compiler_params=_compiler_params(("parallel",)),
        name="outproj_ln",
    )(h, w_out.astype(jnp.bfloat16), x, g.reshape(1, d), b.reshape(1, d))


ROUTE_IDX_LANE = 0
ROUTE_RANK_LANE = TOP_K
ROUTE_GATE_LANE = 2 * TOP_K


def _combine_ln_kernel(x_ref, g_ref, b_ref, *rest, tiles_per_group):
    o_ref = rest[-1]
    n_groups = (len(rest) - 1) // 2
    group = pl.program_id(0) // tiles_per_group
    for grp in range(n_groups):
        y_ref, r_ref = rest[2 * grp], rest[2 * grp + 1]

        @pl.when(group == grp)
        def _():
            y = jnp.zeros(x_ref.shape, jnp.float32)
            for k in range(TOP_K):
                gate = r_ref[:, ROUTE_GATE_LANE + k:ROUTE_GATE_LANE + k + 1]
                y = y + gate * y_ref[k].astype(jnp.float32)
            z = DEEPNORM_ALPHA * x_ref[...] + y
            o_ref[...] = _layer_norm_rows(z, g_ref[...], b_ref[...])


def _combine_ln(y_rows_groups, route_groups, x, g, b):
    n_groups = len(y_rows_groups)
    m = y_rows_groups[0].shape[1]
    d = x.shape[1]
    tm = _row_tile(m)
    tiles = m // tm
    row = pl.BlockSpec((tm, d), lambda i: (i, 0))
    vec = pl.BlockSpec((1, d), lambda i: (0, 0))
    in_specs = [row, vec, vec]
    operands = [x, g.reshape(1, d), b.reshape(1, d)]
    for grp in range(n_groups):
        local = lambda i, grp=grp: jnp.clip(i - grp * tiles, 0, tiles - 1)
        in_specs += [pl.BlockSpec((TOP_K, tm, d), lambda i, local=local: (0, local(i), 0)),
                     pl.BlockSpec((tm, LANES), lambda i, local=local: (local(i), 0))]
        operands += [y_rows_groups[grp], route_groups[grp]]
    return pl.pallas_call(
        functools.partial(_combine_ln_kernel, tiles_per_group=tiles),
        out_shape=jax.ShapeDtypeStruct(x.shape, jnp.float32),
        grid=(n_groups * tiles,),
        in_specs=in_specs,
        out_specs=row,
        compiler_params=_compiler_params(("parallel",)),
        name="combine_ln",
    )(*operands)


ROUTER_PAD_BIAS = -1e30


def _router_kernel(x_ref, whi_ref, wlo_ref, b_ref, o_ref, cnt_ref, base_s):
    f32 = jnp.float32

    @pl.when(pl.program_id(0) == 0)
    def _():
        base_s[...] = jnp.zeros_like(base_s)

    t = x_ref.shape[0]
    x = x_ref[...]
    x_hi = x.astype(jnp.bfloat16)
    x_lo = (x - x_hi.astype(f32)).astype(jnp.bfloat16)
    logits = (jnp.dot(x_hi, whi_ref[...], preferred_element_type=f32)
              + jnp.dot(x_hi, wlo_ref[...], preferred_element_type=f32)
              + jnp.dot(x_lo, whi_ref[...], preferred_element_type=f32)) + b_ref[...]
    lane = lax.broadcasted_iota(jnp.int32, (t, LANES), 1)
    vals = logits
    tops, idxs, sels = [], [], []
    for _ in range(TOP_K):
        top = jnp.max(vals, axis=-1, keepdims=True)
        idx = jnp.min(jnp.where(vals == top, lane, LANES), axis=-1, keepdims=True)
        sel = lane == idx
        vals = jnp.where(sel, -jnp.inf, vals)
        tops.append(top)
        idxs.append(idx)
        sels.append(sel)
    exps = [jnp.exp(top - tops[0]) for top in tops]
    total = exps[0]
    for e in exps[1:]:
        total = total + e
    chosen = jnp.zeros((t, LANES), f32)
    for sel in sels:
        chosen = chosen + sel.astype(f32)
    earlier = (lax.broadcasted_iota(jnp.int32, (t, t), 0) > lax.broadcasted_iota(jnp.int32, (t, t), 1))
    prefix = jnp.dot(earlier.astype(jnp.bfloat16), chosen.astype(jnp.bfloat16),
                     preferred_element_type=f32) + base_s[...]
    packed = jnp.zeros((t, LANES), f32)
    for k in range(TOP_K):
        rank = jnp.sum(jnp.where(sels[k], prefix, 0.0), axis=-1, keepdims=True)
        packed = jnp.where(lane == ROUTE_IDX_LANE + k, idxs[k].astype(f32), packed)
        packed = jnp.where(lane == ROUTE_RANK_LANE + k, rank, packed)
        packed = jnp.where(lane == ROUTE_GATE_LANE + k, exps[k] / total, packed)
    o_ref[...] = packed
    base_s[...] = base_s[...] + jnp.sum(chosen, axis=0, keepdims=True)
    cnt_ref[...] = base_s[...]


def _route(xt, group, m, w_router, b_router):
    d = xt.shape[1]
    tm = _row_tile(m)
    first = group * (m // tm)
    n_e = w_router.shape[1]
    w = jnp.pad(w_router, ((0, 0), (0, LANES - n_e)))
    b = jnp.pad(b_router, (0, LANES - n_e), constant_values=ROUTER_PAD_BIAS)
    w_hi = w.astype(jnp.bfloat16)
    return pl.pallas_call(
        _router_kernel,
        out_shape=[jax.ShapeDtypeStruct((m, LANES), jnp.float32), jax.ShapeDtypeStruct((1, LANES), jnp.float32)],
        grid=(m // tm,),
        in_specs=[pl.BlockSpec((tm, d), lambda i: (first + i, 0)),
                  pl.BlockSpec((d, LANES), lambda i: (0, 0)),
                  pl.BlockSpec((d, LANES), lambda i: (0, 0)),
                  pl.BlockSpec((1, LANES), lambda i: (0, 0))],
        out_specs=[pl.BlockSpec((tm, LANES), lambda i: (i, 0)), pl.BlockSpec((1, LANES), lambda i: (0, 0))],
        scratch_shapes=[pltpu.VMEM((1, LANES), jnp.float32)],
        compiler_params=_compiler_params(("arbitrary",)),
        name="router",
    )(xt, w_hi, (w - w_hi.astype(jnp.float32)).astype(jnp.bfloat16), b.reshape(1, LANES))


GLU_BLOCK = 2 * LANES


def _regroup_kernel(w_ref, p_ref, o_ref):
    w = w_ref[0].astype(jnp.bfloat16)
    for c in range(0, w.shape[1], GLU_BLOCK):
        o_ref[0, :, c:c + GLU_BLOCK] = jnp.dot(w[:, c:c + GLU_BLOCK], p_ref[...],
                                               preferred_element_type=jnp.float32).astype(o_ref.dtype)


def _regroup_glu_columns(w_gu_layers, layer):
    _, n_e, d, f2 = w_gu_layers.shape
    src = jnp.arange(GLU_BLOCK)
    perm = (src[:, None] == (2 * (src % LANES) + src // LANES)[None, :]).astype(jnp.bfloat16)
    tk = _row_tile(d)
    return pl.pallas_call(
        _regroup_kernel,
        out_shape=jax.ShapeDtypeStruct((n_e, d, f2), jnp.bfloat16),
        grid=(n_e, d // tk),
        in_specs=[pl.BlockSpec((None, 1, tk, f2), lambda e, i: (layer, e, i, 0)),
                  pl.BlockSpec((GLU_BLOCK, GLU_BLOCK), lambda e, i: (0, 0))],
        out_specs=pl.BlockSpec((1, tk, f2), lambda e, i: (e, i, 0)),
        compiler_params=_compiler_params(("parallel", "parallel")),
        name="regroup_glu",
    )(w_gu_layers, perm)


def _cast_kernel(w_ref, o_ref):
    o_ref[...] = w_ref[...].astype(o_ref.dtype)


def _layer_to_bf16(w_layers, layer):
    _, n_e, r, c = w_layers.shape
    tr = _row_tile(r)
    return pl.pallas_call(
        _cast_kernel,
        out_shape=jax.ShapeDtypeStruct((n_e, r, c), jnp.bfloat16),
        grid=(n_e, r // tr),
        in_specs=[pl.BlockSpec((None, 1, tr, c), lambda e, i: (layer, e, i, 0))],
        out_specs=pl.BlockSpec((1, tr, c), lambda e, i: (e, i, 0)),
        compiler_params=_compiler_params(("parallel", "parallel")),
        name="layer_to_bf16",
    )(w_layers)


def _slot_kernel(r_ref, start_ref, o_ref):
    lane = lax.broadcasted_iota(jnp.int32, r_ref.shape, 1)
    route = r_ref[...]
    slots = jnp.zeros(r_ref.shape, jnp.float32)
    for k in range(TOP_K):
        expert = route[:, ROUTE_IDX_LANE + k:ROUTE_IDX_LANE + k + 1].astype(jnp.int32)
        first = jnp.sum(jnp.where(lane == expert, start_ref[...], 0.0), axis=-1, keepdims=True)
        slots = jnp.where(lane == k, first + route[:, ROUTE_RANK_LANE + k:ROUTE_RANK_LANE + k + 1], slots)
    o_ref[...] = slots.astype(jnp.int32)


def _slots(route, expert_start):
    m = route.shape[0]
    tm = _row_tile(m)
    return pl.pallas_call(
        _slot_kernel,
        out_shape=jax.ShapeDtypeStruct((m, LANES), jnp.int32),
        grid=(m // tm,),
        in_specs=[pl.BlockSpec((tm, LANES), lambda i: (i, 0)), pl.BlockSpec((1, LANES), lambda i: (0, 0))],
        out_specs=pl.BlockSpec((tm, LANES), lambda i: (i, 0)),
        compiler_params=_compiler_params(("parallel",)),
        name="moe_slots",
    )(route, expert_start)


def _regroup_glu_bias(b_gu):
    n_e, f2 = b_gu.shape
    return b_gu.reshape(n_e, f2 // GLU_BLOCK, LANES, 2).transpose(0, 1, 3, 2).reshape(n_e, 1, f2)


def _expert_kernel(be_ref, nb_ref, x_ref, wgu_ref, bgu_ref, wd_ref, bd_ref, o_ref):
    @pl.when(pl.program_id(0) < nb_ref[0])
    def _():
        xb = x_ref[...]
        acts = []
        for c in range(0, wgu_ref.shape[2], GLU_BLOCK):
            h = (jnp.dot(xb, wgu_ref[0, :, c:c + GLU_BLOCK], preferred_element_type=jnp.float32)
                 + bgu_ref[0, :, c:c + GLU_BLOCK])
            hg = jnp.minimum(h[:, :LANES], SWIGLU_LIMIT)
            hl = jnp.clip(h[:, LANES:], -SWIGLU_LIMIT, SWIGLU_LIMIT)
            acts.append((hg * jax.nn.sigmoid(SWIGLU_ALPHA * hg) * (hl + 1.0)).astype(jnp.bfloat16))
        act = jnp.concatenate(acts, axis=1)
        y = jnp.dot(act, wd_ref[0], preferred_element_type=jnp.float32) + bd_ref[0]
        o_ref[...] = y.astype(o_ref.dtype)


def _expert_ffn(x_sorted, block_expert, n_used, w_gu, b_gu, w_dn, b_dn):
    cap, d = x_sorted.shape
    f2 = w_gu.shape[2]
    tm = MOE_ROW_TILE
    n_blocks = cap // tm

    def blk(i, be, nb):
        return (jnp.minimum(i, nb[0] - 1), 0)

    def wsel(i, be, nb):
        return (be[i], 0, 0)

    grid_spec = pltpu.PrefetchScalarGridSpec(
        num_scalar_prefetch=2,
        grid=(n_blocks,),
        in_specs=[pl.BlockSpec((tm, d), blk),
                  pl.BlockSpec((1, d, f2), wsel),
                  pl.BlockSpec((1, 1, f2), wsel),
                  pl.BlockSpec((1, f2 // 2, d), wsel),
                  pl.BlockSpec((1, 1, d), wsel)],
        out_specs=pl.BlockSpec((tm, d), blk),
    )
    return pl.pallas_call(
        _expert_kernel,
        out_shape=jax.ShapeDtypeStruct((cap, d), jnp.bfloat16),
        grid_spec=grid_spec,
        compiler_params=_compiler_params(("arbitrary",)),
        name="expert_ffn",
    )(block_expert, n_used, x_sorted, w_gu, b_gu, w_dn, b_dn)


def _moe_ffn(x, x_bf16, group, n, w_router, b_router, w_gu, b_gu, w_dn, b_dn):
    d = x.shape[1]
    tm = MOE_ROW_TILE
    n_e = w_router.shape[1]
    route, counts = _route(x, group, n, w_router, b_router)
    counts = counts[0].astype(jnp.int32)
    padded = (counts + tm - 1) // tm * tm
    pad_end = jnp.cumsum(padded)
    pad_start = pad_end - padded
    cap = n * TOP_K + n_e * tm
    n_blocks = cap // tm
    slot = _slots(route, pad_start.astype(jnp.float32).reshape(1, LANES))[:, :TOP_K]
    slot_kmajor = slot.T.reshape(-1)
    block_first_row = jnp.arange(n_blocks, dtype=jnp.int32) * tm
    block_expert = jnp.minimum(jnp.sum(pad_end[None, :n_e] <= block_first_row[:, None], axis=1), n_e - 1)
    n_used = (pad_end[n_e - 1] // tm).reshape(1)
    top_idx = route[:, ROUTE_IDX_LANE:ROUTE_IDX_LANE + TOP_K].astype(jnp.int32).reshape(-1)
    tok_by_expert = jnp.argsort(top_idx, stable=True).astype(jnp.int32) // TOP_K
    unpadded_start = jnp.cumsum(counts) - counts
    row_in_expert = (block_first_row - pad_start[block_expert])[:, None] + jnp.arange(tm, dtype=jnp.int32)[None, :]
    src = jnp.minimum(unpadded_start[block_expert][:, None] + row_in_expert, n * TOP_K - 1).reshape(-1)
    valid = (row_in_expert < counts[block_expert][:, None]).reshape(-1)
    tok_of_slot = group * n + jnp.where(valid, tok_by_expert.at[src].get(mode="promise_in_bounds"), 0)

    x_sorted = x_bf16.at[tok_of_slot].get(mode="promise_in_bounds")
    yb = _expert_ffn(x_sorted, block_expert.astype(jnp.int32), n_used.astype(jnp.int32), w_gu, b_gu, w_dn, b_dn)
    return yb.at[slot_kmajor].get(mode="promise_in_bounds").reshape(TOP_K, n, d), route


MLSTM_CHUNK = 128
MLSTM_PAIR = 2
MLSTM_NEG = -1e30


def _interleave(generators):
    results = [None] * len(generators)
    live = list(enumerate(generators))
    while live:
        still = []
        for idx, gen in live:
            try:
                next(gen)
                still.append((idx, gen))
            except StopIteration as stop:
                results[idx] = stop.value
        live = still
    return results


def _one_by_one(generators):
    return [_interleave([gen])[0] for gen in generators]


def _cumsum_rows(x, reverse):
    n = x.shape[0]
    row = lax.broadcasted_iota(jnp.int32, x.shape, 0)
    sh = 1
    while sh < n:
        if reverse:
            x = x + jnp.where(row < n - sh, pltpu.roll(x, n - sh, 0), 0.0)
        else:
            x = x + jnp.where(row >= sh, pltpu.roll(x, sh, 0), 0.0)
        sh *= 2
    return x


def _cummax_rows(x, reverse):
    n = x.shape[0]
    row = lax.broadcasted_iota(jnp.int32, x.shape, 0)
    sh = 1
    while sh < n:
        if reverse:
            x = jnp.maximum(x, jnp.where(row < n - sh, pltpu.roll(x, n - sh, 0), -jnp.inf))
        else:
            x = jnp.maximum(x, jnp.where(row >= sh, pltpu.roll(x, sh, 0), -jnp.inf))
        sh *= 2
    return x


def _mlstm_kernel(q_ref, k_ref, v_ref, og_ref, g_ref, gain_ref, o_ref, h_s, c_s):
    f32, bf16 = jnp.float32, jnp.bfloat16
    seq = q_ref.shape[0]
    L = MLSTM_CHUNK
    n_chunks = seq // L
    dv = v_ref.shape[1] // MLSTM_PAIR
    dqk = q_ref.shape[1] // MLSTM_PAIR

    lane_q = lax.broadcasted_iota(jnp.int32, (1, q_ref.shape[1]), 1) // dqk
    lane_g = lax.broadcasted_iota(jnp.int32, (1, LANES), 1)
    is_forget_lane = (lane_g // MLSTM_PAIR) % 2 == 1
    row = lax.broadcasted_iota(jnp.int32, (L, L), 0)
    col = lax.broadcasted_iota(jnp.int32, (L, L), 1)
    causal = (col <= row, col >= row)
    ones_v = jnp.ones((L, dv), bf16)

    c_s[...] = jnp.zeros_like(c_s)

    def gate_terms(start, d):
        rows = pl.ds(start, L)
        gates = g_ref[rows, :]
        log_f = jnp.minimum(gates, 0.0) - jnp.log(1.0 + jnp.exp(-jnp.abs(gates)))
        cum = _cumsum_rows(log_f, reverse=(d == 1))
        src = gates - pltpu.roll(cum, LANES - MLSTM_PAIR, 1)
        run_max = _cummax_rows(src, reverse=(d == 1))
        return rows, gates, cum, src.T, run_max, q_ref[rows, :], k_ref[rows, :]

    def head_unit(terms, d, j, m_st):
        rows, gates, cum, src_t, run_max, q_all, k_all = terms
        lane_i = 2 * MLSTM_PAIR * d + j
        lane_f = lane_i + MLSTM_PAIR
        b_col, i_col = cum[:, lane_f:lane_f + 1], gates[:, lane_i:lane_i + 1]
        src_row = src_t[lane_i:lane_i + 1, :]
        qz = jnp.where(lane_q == j, q_all, jnp.zeros_like(q_all))
        kz = jnp.where(lane_q == j, k_all, jnp.zeros_like(k_all))
        state = c_s[2 * d + j]
        scores = _nt_dot(qz, kz)
        carried = jnp.dot(qz, state.astype(bf16), preferred_element_type=f32)
        yield
        mu = jnp.maximum(m_st, run_max[:, lane_i:lane_i + 1])
        v_aug = jnp.concatenate([v_ref[rows, j * dv:(j + 1) * dv], ones_v], axis=1)
        qk = scores * (A_DQK ** -0.5) * jnp.exp(jnp.where(causal[d], src_row - mu, MLSTM_NEG))
        edge = L - 1 if d == 0 else 0
        b_last = b_col[edge:edge + 1, :]
        m_new = b_last + mu[edge:edge + 1, :]
        w_s = jnp.exp(b_last - b_col + i_col - m_new)
        local = jnp.dot(qk.astype(bf16), v_aug, preferred_element_type=f32)
        update = _tn_dot((w_s * kz.astype(f32)).astype(bf16), v_aug)
        yield
        nd = local + jnp.exp(m_st - mu) * (A_DQK ** -0.5) * carried
        h_s[d, rows, j * dv:(j + 1) * dv] = nd[:, :dv] / jnp.maximum(jnp.abs(nd[:, dv:]), jnp.exp(-b_col - mu))
        c_s[2 * d + j] = jnp.exp(b_last + m_st - m_new) * state + update
        return m_new

    def body(c, m_all):
        terms = (gate_terms(pl.multiple_of(c * L, L), 0), gate_terms(pl.multiple_of((n_chunks - 1 - c) * L, L), 1))
        m_new = [None] * (2 * MLSTM_PAIR)
        for j in range(MLSTM_PAIR):
            m_new[j], m_new[MLSTM_PAIR + j] = _interleave(
                [head_unit(terms[d], d, j, m_all[MLSTM_PAIR * d + j]) for d in range(2)])
        return tuple(m_new)

    m_init = tuple(jnp.full((1, 1), A_M_INIT, f32) for _ in range(2 * MLSTM_PAIR))
    lax.fori_loop(0, n_chunks, body, m_init)

    def finish(c, carry):
        rows = pl.ds(pl.multiple_of(c * L, L), L)
        for j in range(MLSTM_PAIR):
            cols = slice(j * dv, (j + 1) * dv)
            h = h_s[0, rows, cols] + h_s[1, rows, cols]
            hn = h * lax.rsqrt(jnp.mean(h * h, axis=-1, keepdims=True) + RMS_EPS) * gain_ref[:, cols]
            o_ref[rows, cols] = (hn * jax.nn.sigmoid(og_ref[rows, cols].astype(f32))).astype(o_ref.dtype)
        return carry

    lax.fori_loop(0, n_chunks, finish, 0)


def _mlstm_mixer(x, w_in, b_in, head_gain):
    bsz, s, d = x.shape
    a_qk = A_HEADS * A_DQK
    dv = d // A_HEADS
    n_main = 2 * a_qk + 2 * d
    n_pairs = A_HEADS // MLSTM_PAIR
    pw_qk = MLSTM_PAIR * A_DQK
    pw_v = MLSTM_PAIR * dv
    assert pw_qk == LANES and s % MLSTM_CHUNK == 0
    gate_cols = jnp.asarray([[n_main + t * A_HEADS + MLSTM_PAIR * hp + j for t in range(4) for j in range(MLSTM_PAIR)]
                             for hp in range(n_pairs)])
    n_gate = gate_cols.shape[1]
    w_tail = jnp.pad(w_in[:, gate_cols], ((0, 0), (0, 0), (0, LANES - n_gate))).reshape(d, n_pairs * LANES)
    b_tail = jnp.pad(b_in[gate_cols], ((0, 0), (0, LANES - n_gate))).reshape(n_pairs * LANES)
    w = jnp.concatenate([w_in[:, :n_main], w_tail], axis=1)
    b = jnp.concatenate([b_in[:n_main], b_tail])
    p, gates = _project(x.reshape(bsz * s, d), w, b, jnp.bfloat16, f32_tail=n_pairs * LANES)
    p = p.reshape(bsz, s, n_main)
    gates = gates.reshape(bsz, s, n_pairs * LANES)
    k0 = a_qk // pw_qk
    v0 = 2 * a_qk // pw_v
    o0 = (2 * a_qk + d) // pw_v
    return pl.pallas_call(
        _mlstm_kernel,
        out_shape=jax.ShapeDtypeStruct((bsz, s, d), jnp.bfloat16),
        grid=(bsz, n_pairs),
        in_specs=[pl.BlockSpec((None, s, pw_qk), lambda b, h: (b, 0, h)),
                  pl.BlockSpec((None, s, pw_qk), lambda b, h: (b, 0, k0 + h)),
                  pl.BlockSpec((None, s, pw_v), lambda b, h: (b, 0, v0 + h)),
                  pl.BlockSpec((None, s, pw_v), lambda b, h: (b, 0, o0 + h)),
                  pl.BlockSpec((None, s, LANES), lambda b, h: (b, 0, h)),
                  pl.BlockSpec((1, pw_v), lambda b, h: (0, h))],
        out_specs=pl.BlockSpec((None, s, pw_v), lambda b, h: (b, 0, h)),
        scratch_shapes=[pltpu.VMEM((2, s, pw_v), jnp.float32),
                        pltpu.VMEM((2 * MLSTM_PAIR, pw_qk, 2 * dv), jnp.float32)],
        compiler_params=_compiler_params(("parallel", "parallel")),
        name="mlstm_scan",
    )(p, p, p, p, gates, head_gain.reshape(1, d))


RWKV_CHUNK = 64
RWKV_GROUP = 4
RWKV_LANES = RWKV_GROUP * B_HEAD
RWKV_PREP_ROWS = 256
RWKV_LOCAL_UNROLL = 4


def _f32_dot(a, b_bf16):
    hi = a.astype(jnp.bfloat16)
    lo = (a - hi.astype(jnp.float32)).astype(jnp.bfloat16)
    return (jnp.dot(hi, b_bf16, preferred_element_type=jnp.float32)
            + jnp.dot(lo, b_bf16, preferred_element_type=jnp.float32))


def _nt_dot(a, b):
    return lax.dot_general(a, b, (((1,), (1,)), ((), ())), preferred_element_type=jnp.float32)


def _tn_dot(a, b):
    return lax.dot_general(a, b, (((0,), (0,)), ((), ())), preferred_element_type=jnp.float32)


def _rwkv_scan_kernel(r_ref, k_ref, v_ref, wl_ref, al_ref, gl_ref, mu_r_ref, mu_k_ref, mu_v_ref, mu_wl_ref,
                      mu_al_ref, mu_gl_ref, w0_ref, wup_ref, a0_ref, aup_ref, kk_ref, ka_ref, rk_ref, gup_ref,
                      lng_ref, lnb_ref, o_ref,
                      r_s, v_s, kap_s, g_s, logw_s, kh_s, beta_s, y_s,
                      wmat_s, uloc_s, avq_s, rt_s, aqb_s, kend_s, bend_s, ptot_s):
    f32, bf16 = jnp.float32, jnp.bfloat16
    seq = r_ref.shape[0]
    L, W, RT = RWKV_CHUNK, RWKV_LANES, RWKV_PREP_ROWS
    n_chunks = seq // L
    n_prep = seq // RT

    lane = lax.broadcasted_iota(jnp.int32, (1, W), 1)
    head_masks = [(lane // B_HEAD) == h for h in range(RWKV_GROUP)]
    ones_bd = ((lax.broadcasted_iota(jnp.int32, (W, W), 0) // B_HEAD)
               == (lax.broadcasted_iota(jnp.int32, (W, W), 1) // B_HEAD)).astype(bf16)

    def block_diag(a):
        zero = jnp.zeros_like(a)
        return jnp.concatenate([jnp.where(m, a, zero) for m in head_masks], axis=0)

    def seg_sum(a):
        return _f32_dot(a, ones_bd)

    def shifted(ref, mu_ref, i, rows):
        o = pl.multiple_of(i * RT, RT)
        x = ref[pl.ds(o, RT), :].astype(f32)
        before = ref[pl.ds(pl.multiple_of(jnp.maximum(o - 16, 0), 16), 16), :].astype(f32)[15:16, :]
        after = ref[pl.ds(pl.multiple_of(jnp.minimum(o + RT, seq - 16), 16), 16), :].astype(f32)[0:1, :]
        before = jnp.where(i > 0, before, 0.0)
        after = jnp.where(i < n_prep - 1, after, 0.0)
        prev = jnp.where(rows == 0, before, pltpu.roll(x, 1, 0))
        nxt = jnp.where(rows == RT - 1, after, pltpu.roll(x, RT - 1, 0))
        return x + mu_ref[...] * (0.5 * (prev + nxt) - x)

    def prep(i, carry):
        o = pl.multiple_of(i * RT, RT)
        rows_w = lax.broadcasted_iota(jnp.int32, (RT, W), 0)
        rows_n = lax.broadcasted_iota(jnp.int32, (RT, wl_ref.shape[1]), 0)
        r = shifted(r_ref, mu_r_ref, i, rows_w)
        k = shifted(k_ref, mu_k_ref, i, rows_w)
        v = shifted(v_ref, mu_v_ref, i, rows_w)
        wl = shifted(wl_ref, mu_wl_ref, i, rows_n)
        al = shifted(al_ref, mu_al_ref, i, rows_n)
        gl = shifted(gl_ref, mu_gl_ref, i, rows_n)
        kk0 = k * kk_ref[...]
        kap = kk0 / jnp.maximum(jnp.sqrt(seg_sum(kk0 * kk0)), 1e-12)
        r_s[pl.ds(o, RT), :] = r
        v_s[pl.ds(o, RT), :] = v.astype(bf16)
        kap_s[pl.ds(o, RT), :] = kap
        g_s[pl.ds(o, RT), :] = jnp.dot(jax.nn.sigmoid(gl).astype(bf16), gup_ref[...],
                                       preferred_element_type=f32).astype(bf16)
        for d in range(2):
            wl_d = jnp.tanh(wl[:, d * B_DECAY_LORA:(d + 1) * B_DECAY_LORA]).astype(bf16)
            al_d = al[:, d * B_AAA_LORA:(d + 1) * B_AAA_LORA].astype(bf16)
            w_raw = w0_ref[d:d + 1, :] + jnp.dot(wl_d, wup_ref[d], preferred_element_type=f32)
            a = jax.nn.sigmoid(a0_ref[d:d + 1, :] + jnp.dot(al_d, aup_ref[d], preferred_element_type=f32))
            logw_s[d, pl.ds(o, RT), :] = -jnp.exp(-0.5) * jax.nn.sigmoid(w_raw)
            kh_s[d, pl.ds(o, RT), :] = (k * (1.0 + (a - 1.0) * ka_ref[...])).astype(bf16)
            beta_s[d, pl.ds(o, RT), :] = (kap * a).astype(bf16)
        return carry

    lax.fori_loop(0, n_prep, prep, 0)

    row = lax.broadcasted_iota(jnp.int32, (L, W), 0)
    col = lax.broadcasted_iota(jnp.int32, (L, W), 1) % B_HEAD
    eye_cat = (row == col).astype(f32)
    strict = (row > col, row < col)
    incl = (row >= col, row <= col)

    def cumsum_rows(x, reverse):
        sh = 1
        while sh < L:
            if reverse:
                x = x + jnp.where(row < L - sh, pltpu.roll(x, L - sh, 0), 0.0)
            else:
                x = x + jnp.where(row >= sh, pltpu.roll(x, sh, 0), 0.0)
            sh *= 2
        return x

    interleave = _interleave

    def local_part(c, d):
        rows = pl.ds(pl.multiple_of(c * L, L), L)
        r, kap = r_s[rows, :], kap_s[rows, :]
        logw, kh, beta = logw_s[d, rows, :], kh_s[d, rows, :].astype(f32), beta_s[d, rows, :].astype(f32)
        cum = cumsum_rows(logw, reverse=(d == 1))
        tot = cum[L - 1:L, :] if d == 0 else cum[0:1, :]
        p_in, p_inv, p_end = jnp.exp(cum), jnp.exp(-cum), jnp.exp(tot - cum)
        kap_t = (kap * jnp.exp(cum - logw)).astype(bf16)
        r_t = (r * p_in).astype(bf16)
        v_bd = block_diag(v_s[rows, :])
        g_all = _nt_dot(jnp.concatenate([kap_t, r_t], axis=0),
                        jnp.concatenate([block_diag((beta * p_inv).astype(bf16)),
                                         block_diag((kh * p_inv).astype(bf16))], axis=0))
        yield
        x_pow = -jnp.where(strict[d], g_all[:L, :W], 0.0)
        a_ak = jnp.where(strict[d], g_all[:L, W:], 0.0)
        a_qb = jnp.where(incl[d], g_all[L:, :W], 0.0)
        a_qk = jnp.where(incl[d], g_all[L:, W:], 0.0)
        t_inv = eye_cat + x_pow
        av = jnp.dot(jnp.concatenate([a_ak, a_qk], axis=0).astype(bf16), v_bd, preferred_element_type=f32)
        x_pow = jnp.dot(x_pow.astype(bf16), block_diag(x_pow.astype(bf16)), preferred_element_type=f32)
        yield
        n_steps = L.bit_length() - 2
        for step in range(n_steps):
            if step < n_steps - 1:
                both = jnp.dot(jnp.concatenate([x_pow, t_inv], axis=0).astype(bf16),
                               block_diag(x_pow.astype(bf16)), preferred_element_type=f32)
                yield
                x_pow, t_inv = both[:L], t_inv + both[L:]
            else:
                last = jnp.dot(t_inv.astype(bf16), block_diag(x_pow.astype(bf16)), preferred_element_type=f32)
                yield
                t_inv = t_inv + last
        t_b = t_inv.astype(bf16)
        w_mat = jnp.dot(t_b, block_diag(kap_t), preferred_element_type=f32)
        u_loc = jnp.dot(t_b, block_diag(av[:L].astype(bf16)), preferred_element_type=f32)
        yield
        wmat_s[d, rows, :] = w_mat.astype(bf16)
        uloc_s[d, rows, :] = u_loc.astype(bf16)
        avq_s[d, rows, :] = av[L:].astype(bf16)
        rt_s[d, rows, :] = r_t
        aqb_s[d, rows, :] = a_qb.astype(bf16)
        kend_s[d, rows, :] = (kh * p_end).astype(bf16)
        bend_s[d, rows, :] = (beta * p_end).astype(bf16)
        ptot_s[d, pl.ds(pl.multiple_of(c * 8, 8), 8), :] = jnp.broadcast_to(jnp.exp(tot), (8, W))

    def local_body(i, carry):
        interleave([local_part(i * RWKV_LOCAL_UNROLL + j, d) for j in range(RWKV_LOCAL_UNROLL) for d in range(2)])
        return carry

    lax.fori_loop(0, n_chunks // RWKV_LOCAL_UNROLL, local_body, 0)

    def carried_part(c, state, d):
        rows = pl.ds(pl.multiple_of(c * L, L), L)
        ws = _nt_dot(jnp.concatenate([wmat_s[d, rows, :], rt_s[d, rows, :]], axis=0), block_diag(state.astype(bf16)))
        yield
        u = ws[:L] + uloc_s[d, rows, :].astype(f32)
        u_b = u.astype(bf16)
        full = _tn_dot(jnp.concatenate([v_s[rows, :], -u_b], axis=0),
                       jnp.concatenate([kend_s[d, rows, :], bend_s[d, rows, :]], axis=0))
        y_loc = jnp.dot(aqb_s[d, rows, :], block_diag(u_b), preferred_element_type=f32)
        yield
        new_state = state * ptot_s[d, pl.ds(pl.multiple_of(c * 8, 8), 8), :][0:1, :]
        for h, m in enumerate(head_masks):
            new_state = new_state + jnp.where(m, full[h * B_HEAD:(h + 1) * B_HEAD, :], 0.0)
        y_s[d, rows, :] = ws[L:] + avq_s[d, rows, :].astype(f32) - y_loc
        return new_state

    def carried_body(c, states):
        return tuple(interleave([carried_part(c, states[0], 0), carried_part(n_chunks - 1 - c, states[1], 1)]))

    zero_state = jnp.zeros((B_HEAD, W), f32)
    lax.fori_loop(0, n_chunks, carried_body, (zero_state, zero_state))

    def finish(i, carry):
        rows = pl.ds(pl.multiple_of(i * RT, RT), RT)
        y = y_s[0, rows, :] + y_s[1, rows, :]
        mean = seg_sum(y) * (1.0 / B_HEAD)
        yc = y - mean
        var = seg_sum(yc * yc) * (1.0 / B_HEAD)
        yn = yc * lax.rsqrt(var + B_GN_EPS) * lng_ref[...] + lnb_ref[...]
        kh_both = kh_s[0, rows, :].astype(f32) + kh_s[1, rows, :].astype(f32)
        bonus = seg_sum(r_s[rows, :] * kh_both * rk_ref[...]) * v_s[rows, :].astype(f32)
        o_ref[rows, :] = ((yn + bonus) * g_s[rows, :].astype(f32)).astype(o_ref.dtype)
        return carry

    lax.fori_loop(0, n_prep, finish, 0)


def _rwkv7_mixer(x, w_in, mu, w0, w_up, a0, a_up, k_k, k_a, r_k, g_up, lnx_g, lnx_b):
    bsz, s, d = x.shape
    n_proj = w_in.shape[1]
    W = RWKV_LANES
    assert d % W == 0 and s % RWKV_PREP_ROWS == 0 and B_GATE_LORA == LANES
    assert 2 * B_DECAY_LORA == LANES and 2 * B_AAA_LORA == LANES
    p = _project(x.reshape(bsz * s, d), w_in, jnp.zeros((n_proj,), jnp.float32), jnp.bfloat16)
    p = p.reshape(bsz, s, n_proj)
    n_groups = d // W
    gw = d // W
    lora0 = 3 * d // LANES

    def seq_w(off):
        return pl.BlockSpec((None, s, W), lambda b, g, off=off: (b, 0, off + g))

    def seq_n(idx):
        return pl.BlockSpec((None, s, LANES), lambda b, g, idx=idx: (b, 0, idx))

    def vec_w(off):
        return pl.BlockSpec((1, W), lambda b, g, off=off: (0, off + g))

    def vec_n(idx):
        return pl.BlockSpec((1, LANES), lambda b, g, idx=idx: (0, idx))

    par_w = pl.BlockSpec((1, W), lambda b, g: (0, g))
    two_w = pl.BlockSpec((2, W), lambda b, g: (0, g))
    up_w = pl.BlockSpec((2, B_DECAY_LORA, W), lambda b, g: (0, 0, g))
    mu2 = mu.reshape(1, n_proj)
    row = lambda a: a.reshape(1, d)
    f32_w = pltpu.VMEM((s, W), jnp.float32)
    b16_w = pltpu.VMEM((s, W), jnp.bfloat16)
    f32_2w = pltpu.VMEM((2, s, W), jnp.float32)
    b16_2w = pltpu.VMEM((2, s, W), jnp.bfloat16)
    decay_rows = pltpu.VMEM((2, 8 * (s // RWKV_CHUNK), W), jnp.float32)
    return pl.pallas_call(
        _rwkv_scan_kernel,
        out_shape=jax.ShapeDtypeStruct((bsz, s, d), jnp.bfloat16),
        grid=(bsz, n_groups),
        in_specs=[seq_w(0), seq_w(gw), seq_w(2 * gw), seq_n(lora0), seq_n(lora0 + 1), seq_n(lora0 + 2),
                  vec_w(0), vec_w(gw), vec_w(2 * gw), vec_n(lora0), vec_n(lora0 + 1), vec_n(lora0 + 2),
                  two_w, up_w, two_w, up_w, par_w, par_w, par_w,
                  pl.BlockSpec((B_GATE_LORA, W), lambda b, g: (0, g)), par_w, par_w],
        out_specs=pl.BlockSpec((None, s, W), lambda b, g: (b, 0, g)),
        scratch_shapes=[f32_w, b16_w, f32_w, b16_w, f32_2w, b16_2w, b16_2w, f32_2w,
                        b16_2w, b16_2w, b16_2w, b16_2w, b16_2w, b16_2w, b16_2w, decay_rows],
        compiler_params=_compiler_params(("parallel", "parallel")),
        name="rwkv7_scan",
    )(p, p, p, p, p, p, mu2, mu2, mu2, mu2, mu2, mu2,
      w0, w_up.astype(jnp.bfloat16), a0, a_up.astype(jnp.bfloat16), row(k_k), row(k_a), row(r_k),
      g_up.astype(jnp.bfloat16), row(lnx_g), row(lnx_b))


def _axial_rope_tables(s):
    rows = s // GRID_W
    row = jnp.repeat(jnp.arange(rows, dtype=jnp.float32), GRID_W)
    col = jnp.tile(jnp.arange(GRID_W, dtype=jnp.float32), rows)
    n_freq = C_HEAD_DIM // 4
    inv_freq = ROPE_THETA ** (-jnp.arange(n_freq, dtype=jnp.float32) / n_freq)
    ang = jnp.concatenate([row[:, None] * inv_freq, col[:, None] * inv_freq], axis=-1)
    return jnp.cos(ang), jnp.sin(ang)


ATTN_Q_TILE = 256


def _rms_rope(x, gain, cos_f, sin_f):
    xn = x * lax.rsqrt(jnp.mean(x * x, axis=-1, keepdims=True) + RMS_EPS) * gain
    return xn * cos_f + pltpu.roll(xn, C_HEAD_DIM // 2, 1) * sin_f


def _attn_kernel(q_ref, k_ref, v_ref, cq_ref, sq_ref, ck_ref, sk_ref, qg_ref, kg_ref, o_ref, kr_s, va_s):
    f32, bf16 = jnp.float32, jnp.bfloat16
    dh = C_HEAD_DIM

    @pl.when(pl.program_id(2) == 0)
    def _():
        kr_s[...] = _rms_rope(k_ref[...].astype(f32), kg_ref[...], ck_ref[...], sk_ref[...]).astype(bf16)
        va_s[:, :dh] = v_ref[...]
        va_s[:, dh:] = jnp.ones((va_s.shape[0], dh), bf16)

    group = q_ref.shape[1] // dh
    for g in range(group):
        cols = slice(g * dh, (g + 1) * dh)
        q = _rms_rope(q_ref[:, cols].astype(f32), qg_ref[...], cq_ref[...], sq_ref[...]) * dh ** -0.5
        scores = _nt_dot(q.astype(bf16), kr_s[...])
        p = jnp.exp((scores - jnp.max(scores, axis=-1, keepdims=True)).astype(bf16))
        od = jnp.dot(p, va_s[...], preferred_element_type=f32)
        o_ref[:, cols] = (od[:, :dh] / od[:, dh:]).astype(o_ref.dtype)


def _axial_gqa_mixer(x, w_in, q_gain, k_gain):
    bsz, s, d = x.shape
    dh = C_HEAD_DIM
    qh = d // dh
    group = qh // C_KV_HEADS
    n_proj = w_in.shape[1]
    half = jnp.concatenate([jnp.arange(0, dh, 2), jnp.arange(1, dh, 2)])
    n_rot = (qh + C_KV_HEADS) * dh
    w_rot = w_in[:, :n_rot].reshape(d, n_rot // dh, dh // 2, 2).transpose(0, 1, 3, 2).reshape(d, n_rot)
    w = jnp.concatenate([w_rot, w_in[:, n_rot:]], axis=1)
    p = _project(x.reshape(bsz * s, d), w, jnp.zeros((n_proj,), jnp.float32), jnp.bfloat16)
    p = p.reshape(bsz, s, n_proj)
    cos, sin = _axial_rope_tables(s)
    cos_f = jnp.concatenate([cos, cos], axis=-1)
    sin_f = jnp.concatenate([-sin, sin], axis=-1)
    tq = min(ATTN_Q_TILE, s)
    gw = group * dh
    q_rows = pl.BlockSpec((tq, dh), lambda b, h, i: (i, 0))
    k_rows = pl.BlockSpec((s, dh), lambda b, h, i: (0, 0))
    vec = pl.BlockSpec((1, dh), lambda b, h, i: (0, 0))
    return pl.pallas_call(
        _attn_kernel,
        out_shape=jax.ShapeDtypeStruct((bsz, s, d), jnp.bfloat16),
        grid=(bsz, C_KV_HEADS, s // tq),
        in_specs=[pl.BlockSpec((None, tq, gw), lambda b, h, i: (b, i, h)),
                  pl.BlockSpec((None, s, dh), lambda b, h, i: (b, 0, qh + h)),
                  pl.BlockSpec((None, s, dh), lambda b, h, i: (b, 0, qh + C_KV_HEADS + h)),
                  q_rows, q_rows, k_rows, k_rows, vec, vec],
        out_specs=pl.BlockSpec((None, tq, gw), lambda b, h, i: (b, i, h)),
        scratch_shapes=[pltpu.VMEM((s, dh), jnp.bfloat16), pltpu.VMEM((s, 2 * dh), jnp.bfloat16)],
        compiler_params=_compiler_params(("parallel", "parallel", "arbitrary")),
        name="axial_attention",
    )(p, p, p, cos_f, sin_f, cos_f, sin_f, q_gain[half].reshape(1, dh), k_gain[half].reshape(1, dh))


def kernel(x, a_w_in, a_b_in, a_head_gain, a_w_out, b_w_in, b_mu, b_w0, b_w_up, b_a0, b_a_up, b_k_k, b_k_a, b_r_k, b_g_up, b_lnx_g, b_lnx_b, b_w_out, c_w_in, c_q_gain, c_k_gain, c_w_out, ln1_g, ln1_b, moe_w_router, moe_b_router, moe_w_gu, moe_b_gu, moe_w_dn, moe_b_dn, ln2_g, ln2_b):
    bsz, s, d = x.shape
    depth = ln1_g.shape[0]
    xt = x.reshape(bsz * s, d)
    for i in range(depth):
        kind = i % N_MIXERS
        j = i // N_MIXERS
        x3 = xt.reshape(bsz, s, d)
        if kind == 0:
            h = _mlstm_mixer(x3, a_w_in[j], a_b_in[j], a_head_gain[j])
            w_out = a_w_out[j]
        elif kind == 1:
            h = _rwkv7_mixer(x3, b_w_in[j], b_mu[j], b_w0[j], b_w_up[j], b_a0[j], b_a_up[j],
                             b_k_k[j], b_k_a[j], b_r_k[j], b_g_up[j], b_lnx_g[j], b_lnx_b[j])
            w_out = b_w_out[j]
        else:
            h = _axial_gqa_mixer(x3, c_w_in[j], c_q_gain[j], c_k_gain[j])
            w_out = c_w_out[j]
        xt, xt_bf16 = _outproj_ln(h.reshape(bsz * s, d), w_out, xt, ln1_g[i], ln1_b[i])
        w_gu = _regroup_glu_columns(moe_w_gu, i)
        b_gu = _regroup_glu_bias(moe_b_gu[i])
        w_dn = _layer_to_bf16(moe_w_dn, i)
        b_dn = moe_b_dn[i][:, None, :]
        n_group = bsz * s // MOE_TOKEN_GROUPS
        groups = [_moe_ffn(xt, xt_bf16, grp, n_group, moe_w_router[i], moe_b_router[i], w_gu, b_gu, w_dn, b_dn)
                  for grp in range(MOE_TOKEN_GROUPS)]
        xt = _combine_ln([y for y, _ in groups], [r for _, r in groups], xt, ln2_g[i], ln2_b[i])
    return xt.reshape(bsz, s, d)
```

```python
import functools

import jax
import jax.numpy as jnp
from jax import lax
from jax.experimental import pallas as pl
from jax.experimental.pallas import tpu as pltpu

DEPTH = 4
N_MIXERS = 3
GRID_W = 64
DEEPNORM_ALPHA = (2 * DEPTH) ** 0.25
LN_EPS = 1e-5
RMS_EPS = 1e-6

A_HEADS = 8
A_DQK = 64
A_M_INIT = -1e30

B_HEAD = 64
B_DECAY_LORA = 64
B_AAA_LORA = 64
B_GATE_LORA = 128
B_GN_EPS = 64e-5

C_HEAD_DIM = 128
C_KV_HEADS = 2
ROPE_THETA = 10000.0

TOP_K = 4
SWIGLU_LIMIT = 7.0
SWIGLU_ALPHA = 1.702

LANES = 128
SUBLANES = 8
BF16_TILE_ROWS = 16
VMEM_LIMIT_BYTES = 56 * 1024 * 1024
MAX_COL_CHUNK = 5 * LANES

ROW_TILE = 512
MOE_ROW_TILE = 512
MOE_TOKEN_GROUPS = 2


def _compiler_params(semantics):
    return pltpu.CompilerParams(dimension_semantics=semantics, vmem_limit_bytes=VMEM_LIMIT_BYTES)


def _row_tile(m):
    t = min(ROW_TILE, m)
    assert m % t == 0, (m, t)
    return t


def _col_chunk(n):
    return max(c for c in range(LANES, MAX_COL_CHUNK + 1, LANES) if n % c == 0)


def _proj_kernel(x_ref, w_ref, b_ref, *o_refs):
    xb = x_ref[...].astype(jnp.bfloat16)
    col = 0
    for o_ref in o_refs:
        n = o_ref.shape[1]
        step = _col_chunk(n)
        for j in range(0, n, step):
            acc = jnp.dot(xb, w_ref[:, col + j:col + j + step], preferred_element_type=jnp.float32)
            o_ref[:, j:j + step] = (acc + b_ref[:, col + j:col + j + step]).astype(o_ref.dtype)
        col += n


def _project(x, w, b, out_dtype, f32_tail=0):
    m, k = x.shape
    n = w.shape[1]
    widths = [n - f32_tail, f32_tail] if f32_tail else [n]
    dtypes = [out_dtype, jnp.float32]
    assert all(c % LANES == 0 for c in widths)
    tm = _row_tile(m)
    outs = pl.pallas_call(
        _proj_kernel,
        out_shape=[jax.ShapeDtypeStruct((m, c), dt) for c, dt in zip(widths, dtypes)],
        grid=(m // tm,),
        in_specs=[pl.BlockSpec((tm, k), lambda i: (i, 0)),
                  pl.BlockSpec((k, n), lambda i: (0, 0)),
                  pl.BlockSpec((1, n), lambda i: (0, 0))],
        out_specs=[pl.BlockSpec((tm, c), lambda i: (i, 0)) for c in widths],
        compiler_params=_compiler_params(("parallel",)),
        name="project",
    )(x, w.astype(jnp.bfloat16), b.reshape(1, n).astype(jnp.float32))
    return tuple(outs) if f32_tail else outs[0]


def _layer_norm_rows(z, g, b):
    mu = jnp.mean(z, axis=-1, keepdims=True)
    zc = z - mu
    var = jnp.mean(zc * zc, axis=-1, keepdims=True)
    return zc * lax.rsqrt(var + LN_EPS) * g + b


def _outproj_ln_kernel(h_ref, w_ref, x_ref, g_ref, b_ref, o_ref, ob_ref):
    mix = jnp.dot(h_ref[...].astype(jnp.bfloat16), w_ref[...], preferred_element_type=jnp.float32)
    z = DEEPNORM_ALPHA * x_ref[...] + mix
    y = _layer_norm_rows(z, g_ref[...], b_ref[...])
    o_ref[...] = y
    ob_ref[...] = y.astype(ob_ref.dtype)


def _outproj_ln(h, w_out, x, g, b):
    m, d = x.shape
    tm = _row_tile(m)
    row = pl.BlockSpec((tm, d), lambda i: (i, 0))
    vec = pl.BlockSpec((1, d), lambda i: (0, 0))
    return pl.pallas_call(
        _outproj_ln_kernel,
        out_shape=[jax.ShapeDtypeStruct((m, d), jnp.float32), jax.ShapeDtypeStruct((m, d), jnp.bfloat16)],
        grid=(m // tm,),
        in_specs=[row, pl.BlockSpec((d, d), lambda i: (0, 0)), row, vec, vec],
        out_specs=[row, row],
        compiler_params=_compiler_params(("parallel",)),
        name="outproj_ln",
    )(h, w_out.astype(jnp.bfloat16), x, g.reshape(1, d), b.reshape(1, d))


ROUTE_IDX_LANE = 0
ROUTE_RANK_LANE = TOP_K
ROUTE_GATE_LANE = 2 * TOP_K


def _combine_ln_kernel(x_ref, g_ref, b_ref, *rest, tiles_per_group):
    o_ref = rest[-1]
    n_groups = (len(rest) - 1) // 2
    group = pl.program_id(0) // tiles_per_group
    for grp in range(n_groups):
        y_ref, r_ref = rest[2 * grp], rest[2 * grp + 1]

        @pl.when(group == grp)
        def _():
            y = jnp.zeros(x_ref.shape, jnp.float32)
            for k in range(TOP_K):
                gate = r_ref[:, ROUTE_GATE_LANE + k:ROUTE_GATE_LANE + k + 1]
                y = y + gate * y_ref[k].astype(jnp.float32)
            z = DEEPNORM_ALPHA * x_ref[...] + y
            o_ref[...] = _layer_norm_rows(z, g_ref[...], b_ref[...])


def _combine_ln(y_rows_groups, route_groups, x, g, b):
    n_groups = len(y_rows_groups)
    m = y_rows_groups[0].shape[1]
    d = x.shape[1]
    tm = _row_tile(m)
    tiles = m // tm
    row = pl.BlockSpec((tm, d), lambda i: (i, 0))
    vec = pl.BlockSpec((1, d), lambda i: (0, 0))
    in_specs = [row, vec, vec]
    operands = [x, g.reshape(1, d), b.reshape(1, d)]
    for grp in range(n_groups):
        local = lambda i, grp=grp: jnp.clip(i - grp * tiles, 0, tiles - 1)
        in_specs += [pl.BlockSpec((TOP_K, tm, d), lambda i, local=local: (0, local(i), 0)),
                     pl.BlockSpec((tm, LANES), lambda i, local=local: (local(i), 0))]
        operands += [y_rows_groups[grp], route_groups[grp]]
    return pl.pallas_call(
        functools.partial(_combine_ln_kernel, tiles_per_group=tiles),
        out_shape=jax.ShapeDtypeStruct(x.shape, jnp.float32),
        grid=(n_groups * tiles,),
        in_specs=in_specs,
        out_specs=row,
        compiler_params=_compiler_params(("parallel",)),
        name="combine_ln",
    )(*operands)


ROUTER_PAD_BIAS = -1e30


def _router_kernel(x_ref, whi_ref, wlo_ref, b_ref, o_ref, cnt_ref, base_s):
    f32 = jnp.float32

    @pl.when(pl.program_id(0) == 0)
    def _():
        base_s[...] = jnp.zeros_like(base_s)

    t = x_ref.shape[0]
    x = x_ref[...]
    x_hi = x.astype(jnp.bfloat16)
    x_lo = (x - x_hi.astype(f32)).astype(jnp.bfloat16)
    logits = (jnp.dot(x_hi, whi_ref[...], preferred_element_type=f32)
              + jnp.dot(x_hi, wlo_ref[...], preferred_element_type=f32)
              + jnp.dot(x_lo, whi_ref[...], preferred_element_type=f32)) + b_ref[...]
    lane = lax.broadcasted_iota(jnp.int32, (t, LANES), 1)
    vals = logits
    tops, idxs, sels = [], [], []
    for _ in range(TOP_K):
        top = jnp.max(vals, axis=-1, keepdims=True)
        idx = jnp.min(jnp.where(vals == top, lane, LANES), axis=-1, keepdims=True)
        sel = lane == idx
        vals = jnp.where(sel, -jnp.inf, vals)
        tops.append(top)
        idxs.append(idx)
        sels.append(sel)
    exps = [jnp.exp(top - tops[0]) for top in tops]
    total = exps[0]
    for e in exps[1:]:
        total = total + e
    chosen = jnp.zeros((t, LANES), f32)
    for sel in sels:
        chosen = chosen + sel.astype(f32)
    earlier = (lax.broadcasted_iota(jnp.int32, (t, t), 0) > lax.broadcasted_iota(jnp.int32, (t, t), 1))
    prefix = jnp.dot(earlier.astype(jnp.bfloat16), chosen.astype(jnp.bfloat16),
                     preferred_element_type=f32) + base_s[...]
    packed = jnp.zeros((t, LANES), f32)
    for k in range(TOP_K):
        rank = jnp.sum(jnp.where(sels[k], prefix, 0.0), axis=-1, keepdims=True)
        packed = jnp.where(lane == ROUTE_IDX_LANE + k, idxs[k].astype(f32), packed)
        packed = jnp.where(lane == ROUTE_RANK_LANE + k, rank, packed)
        packed = jnp.where(lane == ROUTE_GATE_LANE + k, exps[k] / total, packed)
    o_ref[...] = packed
    base_s[...] = base_s[...] + jnp.sum(chosen, axis=0, keepdims=True)
    cnt_ref[...] = base_s[...]


def _route(xt, group, m, w_router, b_router):
    d = xt.shape[1]
    tm = _row_tile(m)
    first = group * (m // tm)
    n_e = w_router.shape[1]
    w = jnp.pad(w_router, ((0, 0), (0, LANES - n_e)))
    b = jnp.pad(b_router, (0, LANES - n_e), constant_values=ROUTER_PAD_BIAS)
    w_hi = w.astype(jnp.bfloat16)
    return pl.pallas_call(
        _router_kernel,
        out_shape=[jax.ShapeDtypeStruct((m, LANES), jnp.float32), jax.ShapeDtypeStruct((1, LANES), jnp.float32)],
        grid=(m // tm,),
        in_specs=[pl.BlockSpec((tm, d), lambda i: (first + i, 0)),
                  pl.BlockSpec((d, LANES), lambda i: (0, 0)),
                  pl.BlockSpec((d, LANES), lambda i: (0, 0)),
                  pl.BlockSpec((1, LANES), lambda i: (0, 0))],
        out_specs=[pl.BlockSpec((tm, LANES), lambda i: (i, 0)), pl.BlockSpec((1, LANES), lambda i: (0, 0))],
        scratch_shapes=[pltpu.VMEM((1, LANES), jnp.float32)],
        compiler_params=_compiler_params(("arbitrary",)),
        name="router",
    )(xt, w_hi, (w - w_hi.astype(jnp.float32)).astype(jnp.bfloat16), b.reshape(1, LANES))


GLU_BLOCK = 2 * LANES


def _regroup_kernel(w_ref, p_ref, o_ref):
    w = w_ref[0].astype(jnp.bfloat16)
    for c in range(0, w.shape[1], GLU_BLOCK):
        o_ref[0, :, c:c + GLU_BLOCK] = jnp.dot(w[:, c:c + GLU_BLOCK], p_ref[...],
                                               preferred_element_type=jnp.float32).astype(o_ref.dtype)


def _regroup_glu_columns(w_gu_layers, layer):
    _, n_e, d, f2 = w_gu_layers.shape
    src = jnp.arange(GLU_BLOCK)
    perm = (src[:, None] == (2 * (src % LANES) + src // LANES)[None, :]).astype(jnp.bfloat16)
    tk = _row_tile(d)
    return pl.pallas_call(
        _regroup_kernel,
        out_shape=jax.ShapeDtypeStruct((n_e, d, f2), jnp.bfloat16),
        grid=(n_e, d // tk),
        in_specs=[pl.BlockSpec((None, 1, tk, f2), lambda e, i: (layer, e, i, 0)),
                  pl.BlockSpec((GLU_BLOCK, GLU_BLOCK), lambda e, i: (0, 0))],
        out_specs=pl.BlockSpec((1, tk, f2), lambda e, i: (e, i, 0)),
        compiler_params=_compiler_params(("parallel", "parallel")),
        name="regroup_glu",
    )(w_gu_layers, perm)


def _cast_kernel(w_ref, o_ref):
    o_ref[...] = w_ref[...].astype(o_ref.dtype)


def _layer_to_bf16(w_layers, layer):
    _, n_e, r, c = w_layers.shape
    tr = _row_tile(r)
    return pl.pallas_call(
        _cast_kernel,
        out_shape=jax.ShapeDtypeStruct((n_e, r, c), jnp.bfloat16),
        grid=(n_e, r // tr),
        in_specs=[pl.BlockSpec((None, 1, tr, c), lambda e, i: (layer, e, i, 0))],
        out_specs=pl.BlockSpec((1, tr, c), lambda e, i: (e, i, 0)),
        compiler_params=_compiler_params(("parallel", "parallel")),
        name="layer_to_bf16",
    )(w_layers)


def _slot_kernel(r_ref, start_ref, o_ref):
    lane = lax.broadcasted_iota(jnp.int32, r_ref.shape, 1)
    route = r_ref[...]
    slots = jnp.zeros(r_ref.shape, jnp.float32)
    for k in range(TOP_K):
        expert = route[:, ROUTE_IDX_LANE + k:ROUTE_IDX_LANE + k + 1].astype(jnp.int32)
        first = jnp.sum(jnp.where(lane == expert, start_ref[...], 0.0), axis=-1, keepdims=True)
        slots = jnp.where(lane == k, first + route[:, ROUTE_RANK_LANE + k:ROUTE_RANK_LANE + k + 1], slots)
    o_ref[...] = slots.astype(jnp.int32)


def _slots(route, expert_start):
    m = route.shape[0]
    tm = _row_tile(m)
    return pl.pallas_call(
        _slot_kernel,
        out_shape=jax.ShapeDtypeStruct((m, LANES), jnp.int32),
        grid=(m // tm,),
        in_specs=[pl.BlockSpec((tm, LANES), lambda i: (i, 0)), pl.BlockSpec((1, LANES), lambda i: (0, 0))],
        out_specs=pl.BlockSpec((tm, LANES), lambda i: (i, 0)),
        compiler_params=_compiler_params(("parallel",)),
        name="moe_slots",
    )(route, expert_start)


def _regroup_glu_bias(b_gu):
    n_e, f2 = b_gu.shape
    return b_gu.reshape(n_e, f2 // GLU_BLOCK, LANES, 2).transpose(0, 1, 3, 2).reshape(n_e, 1, f2)


def _expert_kernel(be_ref, nb_ref, x_ref, wgu_ref, bgu_ref, wd_ref, bd_ref, o_ref):
    @pl.when(pl.program_id(0) < nb_ref[0])
    def _():
        xb = x_ref[...]
        acts = []
        for c in range(0, wgu_ref.shape[2], GLU_BLOCK):
            h = (jnp.dot(xb, wgu_ref[0, :, c:c + GLU_BLOCK], preferred_element_type=jnp.float32)
                 + bgu_ref[0, :, c:c + GLU_BLOCK])
            hg = jnp.minimum(h[:, :LANES], SWIGLU_LIMIT)
            hl = jnp.clip(h[:, LANES:], -SWIGLU_LIMIT, SWIGLU_LIMIT)
            acts.append((hg * jax.nn.sigmoid(SWIGLU_ALPHA * hg) * (hl + 1.0)).astype(jnp.bfloat16))
        act = jnp.concatenate(acts, axis=1)
        y = jnp.dot(act, wd_ref[0], preferred_element_type=jnp.float32) + bd_ref[0]
        o_ref[...] = y.astype(o_ref.dtype)


def _expert_ffn(x_sorted, block_expert, n_used, w_gu, b_gu, w_dn, b_dn):
    cap, d = x_sorted.shape
    f2 = w_gu.shape[2]
    tm = MOE_ROW_TILE
    n_blocks = cap // tm

    def blk(i, be, nb):
        return (jnp.minimum(i, nb[0] - 1), 0)

    def wsel(i, be, nb):
        return (be[i], 0, 0)

    grid_spec = pltpu.PrefetchScalarGridSpec(
        num_scalar_prefetch=2,
        grid=(n_blocks,),
        in_specs=[pl.BlockSpec((tm, d), blk),
                  pl.BlockSpec((1, d, f2), wsel),
                  pl.BlockSpec((1, 1, f2), wsel),
                  pl.BlockSpec((1, f2 // 2, d), wsel),
                  pl.BlockSpec((1, 1, d), wsel)],
        out_specs=pl.BlockSpec((tm, d), blk),
    )
    return pl.pallas_call(
        _expert_kernel,
        out_shape=jax.ShapeDtypeStruct((cap, d), jnp.bfloat16),
        grid_spec=grid_spec,
        compiler_params=_compiler_params(("arbitrary",)),
        name="expert_ffn",
    )(block_expert, n_used, x_sorted, w_gu, b_gu, w_dn, b_dn)


def _moe_ffn(x, x_bf16, group, n, w_router, b_router, w_gu, b_gu, w_dn, b_dn):
    d = x.shape[1]
    tm = MOE_ROW_TILE
    n_e = w_router.shape[1]
    route, counts = _route(x, group, n, w_router, b_router)
    counts = counts[0].astype(jnp.int32)
    padded = (counts + tm - 1) // tm * tm
    pad_end = jnp.cumsum(padded)
    pad_start = pad_end - padded
    cap = n * TOP_K + n_e * tm
    n_blocks = cap // tm
    slot = _slots(route, pad_start.astype(jnp.float32).reshape(1, LANES))[:, :TOP_K]
    slot_kmajor = slot.T.reshape(-1)
    block_first_row = jnp.arange(n_blocks, dtype=jnp.int32) * tm
    block_expert = jnp.minimum(jnp.sum(pad_end[None, :n_e] <= block_first_row[:, None], axis=1), n_e - 1)
    n_used = (pad_end[n_e - 1] // tm).reshape(1)
    top_idx = route[:, ROUTE_IDX_LANE:ROUTE_IDX_LANE + TOP_K].astype(jnp.int32).reshape(-1)
    tok_by_expert = jnp.argsort(top_idx, stable=True).astype(jnp.int32) // TOP_K
    unpadded_start = jnp.cumsum(counts) - counts
    row_in_expert = (block_first_row - pad_start[block_expert])[:, None] + jnp.arange(tm, dtype=jnp.int32)[None, :]
    src = jnp.minimum(unpadded_start[block_expert][:, None] + row_in_expert, n * TOP_K - 1).reshape(-1)
    valid = (row_in_expert < counts[block_expert][:, None]).reshape(-1)
    tok_of_slot = group * n + jnp.where(valid, tok_by_expert.at[src].get(mode="promise_in_bounds"), 0)

    x_sorted = x_bf16.at[tok_of_slot].get(mode="promise_in_bounds")
    yb = _expert_ffn(x_sorted, block_expert.astype(jnp.int32), n_used.astype(jnp.int32), w_gu, b_gu, w_dn, b_dn)
    return yb.at[slot_kmajor].get(mode="promise_in_bounds").reshape(TOP_K, n, d), route


MLSTM_CHUNK = 128
MLSTM_PAIR = 2
MLSTM_NEG = -1e30


def _interleave(generators):
    results = [None] * len(generators)
    live = list(enumerate(generators))
    while live:
        still = []
        for idx, gen in live:
            try:
                next(gen)
                still.append((idx, gen))
            except StopIteration as stop:
                results[idx] = stop.value
        live = still
    return results


def _cumsum_rows(x, reverse):
    n = x.shape[0]
    row = lax.broadcasted_iota(jnp.int32, x.shape, 0)
    sh = 1
    while sh < n:
        if reverse:
            x = x + jnp.where(row < n - sh, pltpu.roll(x, n - sh, 0), 0.0)
        else:
            x = x + jnp.where(row >= sh, pltpu.roll(x, sh, 0), 0.0)
        sh *= 2
    return x


def _cummax_rows(x, reverse):
    n = x.shape[0]
    row = lax.broadcasted_iota(jnp.int32, x.shape, 0)
    sh = 1
    while sh < n:
        if reverse:
            x = jnp.maximum(x, jnp.where(row < n - sh, pltpu.roll(x, n - sh, 0), -jnp.inf))
        else:
            x = jnp.maximum(x, jnp.where(row >= sh, pltpu.roll(x, sh, 0), -jnp.inf))
        sh *= 2
    return x


def _mlstm_kernel(q_ref, k_ref, v_ref, og_ref, g_ref, gain_ref, o_ref, h_s, c_s):
    f32, bf16 = jnp.float32, jnp.bfloat16
    seq = q_ref.shape[0]
    L = MLSTM_CHUNK
    n_chunks = seq // L
    dv = v_ref.shape[1] // MLSTM_PAIR
    dqk = q_ref.shape[1] // MLSTM_PAIR

    lane_q = lax.broadcasted_iota(jnp.int32, (1, q_ref.shape[1]), 1) // dqk
    row = lax.broadcasted_iota(jnp.int32, (L, L), 0)
    col = lax.broadcasted_iota(jnp.int32, (L, L), 1)
    causal = (col <= row, col >= row)
    ones_v = jnp.ones((L, dv), bf16)

    c_s[...] = jnp.zeros_like(c_s)

    def gate_terms(start, d):
        rows = pl.ds(start, L)
        gates = g_ref[rows, :]
        log_f = jnp.minimum(gates, 0.0) - jnp.log(1.0 + jnp.exp(-jnp.abs(gates)))
        cum = _cumsum_rows(log_f, reverse=(d == 1))
        src = gates - pltpu.roll(cum, LANES - MLSTM_PAIR, 1)
        run_max = _cummax_rows(src, reverse=(d == 1))
        return rows, gates, cum, src.T, run_max, q_ref[rows, :], k_ref[rows, :]

    def head_unit(terms, d, j, m_st):
        rows, gates, cum, src_t, run_max, q_all, k_all = terms
        lane_i = 2 * MLSTM_PAIR * d + j
        lane_f = lane_i + MLSTM_PAIR
        b_col, i_col = cum[:, lane_f:lane_f + 1], gates[:, lane_i:lane_i + 1]
        src_row = src_t[lane_i:lane_i + 1, :]
        qz = jnp.where(lane_q == j, q_all, jnp.zeros_like(q_all))
        kz = jnp.where(lane_q == j, k_all, jnp.zeros_like(k_all))
        state = c_s[2 * d + j]
        scores = _nt_dot(qz, kz)
        carried = jnp.dot(qz, state.astype(bf16), preferred_element_type=f32)
        yield
        mu = jnp.maximum(m_st, run_max[:, lane_i:lane_i + 1])
        v_aug = jnp.concatenate([v_ref[rows, j * dv:(j + 1) * dv], ones_v], axis=1)
        qk = scores * (A_DQK ** -0.5) * jnp.exp(jnp.where(causal[d], src_row - mu, MLSTM_NEG))
        edge = L - 1 if d == 0 else 0
        b_last = b_col[edge:edge + 1, :]
        m_new = b_last + mu[edge:edge + 1, :]
        w_s = jnp.exp(b_last - b_col + i_col - m_new)
        local = jnp.dot(qk.astype(bf16), v_aug, preferred_element_type=f32)
        update = _tn_dot((w_s * kz.astype(f32)).astype(bf16), v_aug)
        yield
        nd = local + jnp.exp(m_st - mu) * (A_DQK ** -0.5) * carried
        h_s[d, rows, j * dv:(j + 1) * dv] = nd[:, :dv] / jnp.maximum(jnp.abs(nd[:, dv:]), jnp.exp(-b_col - mu))
        c_s[2 * d + j] = jnp.exp(b_last + m_st - m_new) * state + update
        return m_new

    def body(c, m_all):
        terms = (gate_terms(pl.multiple_of(c * L, L), 0), gate_terms(pl.multiple_of((n_chunks - 1 - c) * L, L), 1))
        m_new = [None] * (2 * MLSTM_PAIR)
        for j in range(MLSTM_PAIR):
            m_new[j], m_new[MLSTM_PAIR + j] = _interleave(
                [head_unit(terms[d], d, j, m_all[MLSTM_PAIR * d + j]) for d in range(2)])
        return tuple(m_new)

    m_init = tuple(jnp.full((1, 1), A_M_INIT, f32) for _ in range(2 * MLSTM_PAIR))
    lax.fori_loop(0, n_chunks, body, m_init)

    def finish(c, carry):
        rows = pl.ds(pl.multiple_of(c * L, L), L)
        for j in range(MLSTM_PAIR):
            cols = slice(j * dv, (j + 1) * dv)
            h = h_s[0, rows, cols] + h_s[1, rows, cols]
            hn = h * lax.rsqrt(jnp.mean(h * h, axis=-1, keepdims=True) + RMS_EPS) * gain_ref[:, cols]
            o_ref[rows, cols] = (hn * jax.nn.sigmoid(og_ref[rows, cols].astype(f32))).astype(o_ref.dtype)
        return carry

    lax.fori_loop(0, n_chunks, finish, 0)


def _mlstm_mixer(x, w_in, b_in, head_gain):
    bsz, s, d = x.shape
    a_qk = A_HEADS * A_DQK
    dv = d // A_HEADS
    n_main = 2 * a_qk + 2 * d
    n_pairs = A_HEADS // MLSTM_PAIR
    pw_qk = MLSTM_PAIR * A_DQK
    pw_v = MLSTM_PAIR * dv
    assert pw_qk == LANES and s % MLSTM_CHUNK == 0
    gate_cols = jnp.asarray([[n_main + t * A_HEADS + MLSTM_PAIR * hp + j for t in range(4) for j in range(MLSTM_PAIR)]
                             for hp in range(n_pairs)])
    n_gate = gate_cols.shape[1]
    w_tail = jnp.pad(w_in[:, gate_cols], ((0, 0), (0, 0), (0, LANES - n_gate))).reshape(d, n_pairs * LANES)
    b_tail = jnp.pad(b_in[gate_cols], ((0, 0), (0, LANES - n_gate))).reshape(n_pairs * LANES)
    w = jnp.concatenate([w_in[:, :n_main], w_tail], axis=1)
    b = jnp.concatenate([b_in[:n_main], b_tail])
    p, gates = _project(x.reshape(bsz * s, d), w, b, jnp.bfloat16, f32_tail=n_pairs * LANES)
    p = p.reshape(bsz, s, n_main)
    gates = gates.reshape(bsz, s, n_pairs * LANES)
    k0 = a_qk // pw_qk
    v0 = 2 * a_qk // pw_v
    o0 = (2 * a_qk + d) // pw_v
    return pl.pallas_call(
        _mlstm_kernel,
        out_shape=jax.ShapeDtypeStruct((bsz, s, d), jnp.bfloat16),
        grid=(bsz, n_pairs),
        in_specs=[pl.BlockSpec((None, s, pw_qk), lambda b, h: (b, 0, h)),
                  pl.BlockSpec((None, s, pw_qk), lambda b, h: (b, 0, k0 + h)),
                  pl.BlockSpec((None, s, pw_v), lambda b, h: (b, 0, v0 + h)),
                  pl.BlockSpec((None, s, pw_v), lambda b, h: (b, 0, o0 + h)),
                  pl.BlockSpec((None, s, LANES), lambda b, h: (b, 0, h)),
                  pl.BlockSpec((1, pw_v), lambda b, h: (0, h))],
        out_specs=pl.BlockSpec((None, s, pw_v), lambda b, h: (b, 0, h)),
        scratch_shapes=[pltpu.VMEM((2, s, pw_v), jnp.float32),
                        pltpu.VMEM((2 * MLSTM_PAIR, pw_qk, 2 * dv), jnp.float32)],
        compiler_params=_compiler_params(("parallel", "parallel")),
        name="mlstm_scan",
    )(p, p, p, p, gates, head_gain.reshape(1, d))


RWKV_CHUNK = 64
RWKV_GROUP = 4
RWKV_LANES = RWKV_GROUP * B_HEAD
RWKV_PREP_ROWS = 256
RWKV_LOCAL_UNROLL = 4


def _f32_dot(a, b_bf16):
    hi = a.astype(jnp.bfloat16)
    lo = (a - hi.astype(jnp.float32)).astype(jnp.bfloat16)
    return (jnp.dot(hi, b_bf16, preferred_element_type=jnp.float32)
            + jnp.dot(lo, b_bf16, preferred_element_type=jnp.float32))


def _nt_dot(a, b):
    return lax.dot_general(a, b, (((1,), (1,)), ((), ())), preferred_element_type=jnp.float32)


def _tn_dot(a, b):
    return lax.dot_general(a, b, (((0,), (0,)), ((), ())), preferred_element_type=jnp.float32)


def _rwkv_scan_kernel(r_ref, k_ref, v_ref, wl_ref, al_ref, gl_ref, mu_r_ref, mu_k_ref, mu_v_ref, mu_wl_ref,
                      mu_al_ref, mu_gl_ref, w0_ref, wup_ref, a0_ref, aup_ref, kk_ref, ka_ref, rk_ref, gup_ref,
                      lng_ref, lnb_ref, o_ref,
                      r_s, v_s, kap_s, g_s, logw_s, kh_s, beta_s, y_s,
                      wmat_s, uloc_s, avq_s, rt_s, aqb_s, kend_s, bend_s, ptot_s):
    f32, bf16 = jnp.float32, jnp.bfloat16
    seq = r_ref.shape[0]
    L, W, RT = RWKV_CHUNK, RWKV_LANES, RWKV_PREP_ROWS
    n_chunks = seq // L
    n_prep = seq // RT

    lane = lax.broadcasted_iota(jnp.int32, (1, W), 1)
    head_masks = [(lane // B_HEAD) == h for h in range(RWKV_GROUP)]
    ones_bd = ((lax.broadcasted_iota(jnp.int32, (W, W), 0) // B_HEAD)
               == (lax.broadcasted_iota(jnp.int32, (W, W), 1) // B_HEAD)).astype(bf16)

    def block_diag(a):
        zero = jnp.zeros_like(a)
        return jnp.concatenate([jnp.where(m, a, zero) for m in head_masks], axis=0)

    def seg_sum(a):
        return _f32_dot(a, ones_bd)

    def shifted(ref, mu_ref, i, rows):
        o = pl.multiple_of(i * RT, RT)
        x = ref[pl.ds(o, RT), :].astype(f32)
        nb = BF16_TILE_ROWS
        before = ref[pl.ds(pl.multiple_of(jnp.maximum(o - nb, 0), nb), nb), :].astype(f32)[nb - 1:nb, :]
        after = ref[pl.ds(pl.multiple_of(jnp.minimum(o + RT, seq - nb), nb), nb), :].astype(f32)[0:1, :]
        before = jnp.where(i > 0, before, 0.0)
        after = jnp.where(i < n_prep - 1, after, 0.0)
        prev = jnp.where(rows == 0, before, pltpu.roll(x, 1, 0))
        nxt = jnp.where(rows == RT - 1, after, pltpu.roll(x, RT - 1, 0))
        return x + mu_ref[...] * (0.5 * (prev + nxt) - x)

    def prep(i, carry):
        o = pl.multiple_of(i * RT, RT)
        rows_w = lax.broadcasted_iota(jnp.int32, (RT, W), 0)
        rows_n = lax.broadcasted_iota(jnp.int32, (RT, wl_ref.shape[1]), 0)
        r = shifted(r_ref, mu_r_ref, i, rows_w)
        k = shifted(k_ref, mu_k_ref, i, rows_w)
        v = shifted(v_ref, mu_v_ref, i, rows_w)
        wl = shifted(wl_ref, mu_wl_ref, i, rows_n)
        al = shifted(al_ref, mu_al_ref, i, rows_n)
        gl = shifted(gl_ref, mu_gl_ref, i, rows_n)
        kk0 = k * kk_ref[...]
        kap = kk0 / jnp.maximum(jnp.sqrt(seg_sum(kk0 * kk0)), 1e-12)
        r_s[pl.ds(o, RT), :] = r
        v_s[pl.ds(o, RT), :] = v.astype(bf16)
        kap_s[pl.ds(o, RT), :] = kap
        g_s[pl.ds(o, RT), :] = jnp.dot(jax.nn.sigmoid(gl).astype(bf16), gup_ref[...],
                                       preferred_element_type=f32).astype(bf16)
        for d in range(2):
            wl_d = jnp.tanh(wl[:, d * B_DECAY_LORA:(d + 1) * B_DECAY_LORA]).astype(bf16)
            al_d = al[:, d * B_AAA_LORA:(d + 1) * B_AAA_LORA].astype(bf16)
            w_raw = w0_ref[d:d + 1, :] + jnp.dot(wl_d, wup_ref[d], preferred_element_type=f32)
            a = jax.nn.sigmoid(a0_ref[d:d + 1, :] + jnp.dot(al_d, aup_ref[d], preferred_element_type=f32))
            logw_s[d, pl.ds(o, RT), :] = -jnp.exp(-0.5) * jax.nn.sigmoid(w_raw)
            kh_s[d, pl.ds(o, RT), :] = (k * (1.0 + (a - 1.0) * ka_ref[...])).astype(bf16)
            beta_s[d, pl.ds(o, RT), :] = (kap * a).astype(bf16)
        return carry

    lax.fori_loop(0, n_prep, prep, 0)

    row = lax.broadcasted_iota(jnp.int32, (L, W), 0)
    col = lax.broadcasted_iota(jnp.int32, (L, W), 1) % B_HEAD
    eye_cat = (row == col).astype(f32)
    strict = (row > col, row < col)
    incl = (row >= col, row <= col)

    def cumsum_rows(x, reverse):
        sh = 1
        while sh < L:
            if reverse:
                x = x + jnp.where(row < L - sh, pltpu.roll(x, L - sh, 0), 0.0)
            else:
                x = x + jnp.where(row >= sh, pltpu.roll(x, sh, 0), 0.0)
            sh *= 2
        return x

    interleave = _interleave

    def local_part(c, d):
        rows = pl.ds(pl.multiple_of(c * L, L), L)
        r, kap = r_s[rows, :], kap_s[rows, :]
        logw, kh, beta = logw_s[d, rows, :], kh_s[d, rows, :].astype(f32), beta_s[d, rows, :].astype(f32)
        cum = cumsum_rows(logw, reverse=(d == 1))
        tot = cum[L - 1:L, :] if d == 0 else cum[0:1, :]
        p_in, p_inv, p_end = jnp.exp(cum), jnp.exp(-cum), jnp.exp(tot - cum)
        kap_t = (kap * jnp.exp(cum - logw)).astype(bf16)
        r_t = (r * p_in).astype(bf16)
        v_bd = block_diag(v_s[rows, :])
        g_all = _nt_dot(jnp.concatenate([kap_t, r_t], axis=0),
                        jnp.concatenate([block_diag((beta * p_inv).astype(bf16)),
                                         block_diag((kh * p_inv).astype(bf16))], axis=0))
        yield
        x_pow = -jnp.where(strict[d], g_all[:L, :W], 0.0)
        a_ak = jnp.where(strict[d], g_all[:L, W:], 0.0)
        a_qb = jnp.where(incl[d], g_all[L:, :W], 0.0)
        a_qk = jnp.where(incl[d], g_all[L:, W:], 0.0)
        t_inv = eye_cat + x_pow
        av = jnp.dot(jnp.concatenate([a_ak, a_qk], axis=0).astype(bf16), v_bd, preferred_element_type=f32)
        x_pow = jnp.dot(x_pow.astype(bf16), block_diag(x_pow.astype(bf16)), preferred_element_type=f32)
        yield
        n_steps = L.bit_length() - 2
        for step in range(n_steps):
            if step < n_steps - 1:
                both = jnp.dot(jnp.concatenate([x_pow, t_inv], axis=0).astype(bf16),
                               block_diag(x_pow.astype(bf16)), preferred_element_type=f32)
                yield
                x_pow, t_inv = both[:L], t_inv + both[L:]
            else:
                last = jnp.dot(t_inv.astype(bf16), block_diag(x_pow.astype(bf16)), preferred_element_type=f32)
                yield
                t_inv = t_inv + last
        t_b = t_inv.astype(bf16)
        w_mat = jnp.dot(t_b, block_diag(kap_t), preferred_element_type=f32)
        u_loc = jnp.dot(t_b, block_diag(av[:L].astype(bf16)), preferred_element_type=f32)
        yield
        wmat_s[d, rows, :] = w_mat.astype(bf16)
        uloc_s[d, rows, :] = u_loc.astype(bf16)
        avq_s[d, rows, :] = av[L:].astype(bf16)
        rt_s[d, rows, :] = r_t
        aqb_s[d, rows, :] = a_qb.astype(bf16)
        kend_s[d, rows, :] = (kh * p_end).astype(bf16)
        bend_s[d, rows, :] = (beta * p_end).astype(bf16)
        ptot_s[d, pl.ds(pl.multiple_of(c * SUBLANES, SUBLANES), SUBLANES), :] = jnp.broadcast_to(
            jnp.exp(tot), (SUBLANES, W))

    def local_body(i, carry):
        interleave([local_part(i * RWKV_LOCAL_UNROLL + j, d) for j in range(RWKV_LOCAL_UNROLL) for d in range(2)])
        return carry

    lax.fori_loop(0, n_chunks // RWKV_LOCAL_UNROLL, local_body, 0)

    def carried_part(c, state, d):
        rows = pl.ds(pl.multiple_of(c * L, L), L)
        ws = _nt_dot(jnp.concatenate([wmat_s[d, rows, :], rt_s[d, rows, :]], axis=0), block_diag(state.astype(bf16)))
        yield
        u = ws[:L] + uloc_s[d, rows, :].astype(f32)
        u_b = u.astype(bf16)
        full = _tn_dot(jnp.concatenate([v_s[rows, :], -u_b], axis=0),
                       jnp.concatenate([kend_s[d, rows, :], bend_s[d, rows, :]], axis=0))
        y_loc = jnp.dot(aqb_s[d, rows, :], block_diag(u_b), preferred_element_type=f32)
        yield
        new_state = state * ptot_s[d, pl.ds(pl.multiple_of(c * SUBLANES, SUBLANES), SUBLANES), :][0:1, :]
        for h, m in enumerate(head_masks):
            new_state = new_state + jnp.where(m, full[h * B_HEAD:(h + 1) * B_HEAD, :], 0.0)
        y_s[d, rows, :] = ws[L:] + avq_s[d, rows, :].astype(f32) - y_loc
        return new_state

    def carried_body(c, states):
        return tuple(interleave([carried_part(c, states[0], 0), carried_part(n_chunks - 1 - c, states[1], 1)]))

    zero_state = jnp.zeros((B_HEAD, W), f32)
    lax.fori_loop(0, n_chunks, carried_body, (zero_state, zero_state))

    def finish(i, carry):
        rows = pl.ds(pl.multiple_of(i * RT, RT), RT)
        y = y_s[0, rows, :] + y_s[1, rows, :]
        mean = seg_sum(y) * (1.0 / B_HEAD)
        yc = y - mean
        var = seg_sum(yc * yc) * (1.0 / B_HEAD)
        yn = yc * lax.rsqrt(var + B_GN_EPS) * lng_ref[...] + lnb_ref[...]
        kh_both = kh_s[0, rows, :].astype(f32) + kh_s[1, rows, :].astype(f32)
        bonus = seg_sum(r_s[rows, :] * kh_both * rk_ref[...]) * v_s[rows, :].astype(f32)
        o_ref[rows, :] = ((yn + bonus) * g_s[rows, :].astype(f32)).astype(o_ref.dtype)
        return carry

    lax.fori_loop(0, n_prep, finish, 0)


def _rwkv7_mixer(x, w_in, mu, w0, w_up, a0, a_up, k_k, k_a, r_k, g_up, lnx_g, lnx_b):
    bsz, s, d = x.shape
    n_proj = w_in.shape[1]
    W = RWKV_LANES
    assert d % W == 0 and s % RWKV_PREP_ROWS == 0 and B_GATE_LORA == LANES
    assert 2 * B_DECAY_LORA == LANES and 2 * B_AAA_LORA == LANES
    assert (s // RWKV_CHUNK) % RWKV_LOCAL_UNROLL == 0
    p = _project(x.reshape(bsz * s, d), w_in, jnp.zeros((n_proj,), jnp.float32), jnp.bfloat16)
    p = p.reshape(bsz, s, n_proj)
    n_groups = d // W
    gw = d // W
    lora0 = 3 * d // LANES

    def seq_w(off):
        return pl.BlockSpec((None, s, W), lambda b, g, off=off: (b, 0, off + g))

    def seq_n(idx):
        return pl.BlockSpec((None, s, LANES), lambda b, g, idx=idx: (b, 0, idx))

    def vec_w(off):
        return pl.BlockSpec((1, W), lambda b, g, off=off: (0, off + g))

    def vec_n(idx):
        return pl.BlockSpec((1, LANES), lambda b, g, idx=idx: (0, idx))

    par_w = pl.BlockSpec((1, W), lambda b, g: (0, g))
    two_w = pl.BlockSpec((2, W), lambda b, g: (0, g))
    up_w = pl.BlockSpec((2, B_DECAY_LORA, W), lambda b, g: (0, 0, g))
    mu2 = mu.reshape(1, n_proj)
    row = lambda a: a.reshape(1, d)
    f32_w = pltpu.VMEM((s, W), jnp.float32)
    b16_w = pltpu.VMEM((s, W), jnp.bfloat16)
    f32_2w = pltpu.VMEM((2, s, W), jnp.float32)
    b16_2w = pltpu.VMEM((2, s, W), jnp.bfloat16)
    decay_rows = pltpu.VMEM((2, SUBLANES * (s // RWKV_CHUNK), W), jnp.float32)
    return pl.pallas_call(
        _rwkv_scan_kernel,
        out_shape=jax.ShapeDtypeStruct((bsz, s, d), jnp.bfloat16),
        grid=(bsz, n_groups),
        in_specs=[seq_w(0), seq_w(gw), seq_w(2 * gw), seq_n(lora0), seq_n(lora0 + 1), seq_n(lora0 + 2),
                  vec_w(0), vec_w(gw), vec_w(2 * gw), vec_n(lora0), vec_n(lora0 + 1), vec_n(lora0 + 2),
                  two_w, up_w, two_w, up_w, par_w, par_w, par_w,
                  pl.BlockSpec((B_GATE_LORA, W), lambda b, g: (0, g)), par_w, par_w],
        out_specs=pl.BlockSpec((None, s, W), lambda b, g: (b, 0, g)),
        scratch_shapes=[f32_w, b16_w, f32_w, b16_w, f32_2w, b16_2w, b16_2w, f32_2w,
                        b16_2w, b16_2w, b16_2w, b16_2w, b16_2w, b16_2w, b16_2w, decay_rows],
        compiler_params=_compiler_params(("parallel", "parallel")),
        name="rwkv7_scan",
    )(p, p, p, p, p, p, mu2, mu2, mu2, mu2, mu2, mu2,
      w0, w_up.astype(jnp.bfloat16), a0, a_up.astype(jnp.bfloat16), row(k_k), row(k_a), row(r_k),
      g_up.astype(jnp.bfloat16), row(lnx_g), row(lnx_b))


def _axial_rope_tables(s):
    rows = s // GRID_W
    row = jnp.repeat(jnp.arange(rows, dtype=jnp.float32), GRID_W)
    col = jnp.tile(jnp.arange(GRID_W, dtype=jnp.float32), rows)
    n_freq = C_HEAD_DIM // 4
    inv_freq = ROPE_THETA ** (-jnp.arange(n_freq, dtype=jnp.float32) / n_freq)
    ang = jnp.concatenate([row[:, None] * inv_freq, col[:, None] * inv_freq], axis=-1)
    return jnp.cos(ang), jnp.sin(ang)


ATTN_Q_TILE = 256


def _rms_rope(x, gain, cos_f, sin_f):
    xn = x * lax.rsqrt(jnp.mean(x * x, axis=-1, keepdims=True) + RMS_EPS) * gain
    return xn * cos_f + pltpu.roll(xn, C_HEAD_DIM // 2, 1) * sin_f


def _attn_kernel(q_ref, k_ref, v_ref, cq_ref, sq_ref, ck_ref, sk_ref, qg_ref, kg_ref, o_ref, kr_s, va_s):
    f32, bf16 = jnp.float32, jnp.bfloat16
    dh = C_HEAD_DIM

    @pl.when(pl.program_id(2) == 0)
    def _():
        kr_s[...] = _rms_rope(k_ref[...].astype(f32), kg_ref[...], ck_ref[...], sk_ref[...]).astype(bf16)
        va_s[:, :dh] = v_ref[...]
        va_s[:, dh:] = jnp.ones((va_s.shape[0], dh), bf16)

    group = q_ref.shape[1] // dh
    for g in range(group):
        cols = slice(g * dh, (g + 1) * dh)
        q = _rms_rope(q_ref[:, cols].astype(f32), qg_ref[...], cq_ref[...], sq_ref[...]) * dh ** -0.5
        scores = _nt_dot(q.astype(bf16), kr_s[...])
        p = jnp.exp((scores - jnp.max(scores, axis=-1, keepdims=True)).astype(bf16))
        od = jnp.dot(p, va_s[...], preferred_element_type=f32)
        o_ref[:, cols] = (od[:, :dh] / od[:, dh:]).astype(o_ref.dtype)


def _axial_gqa_mixer(x, w_in, q_gain, k_gain):
    bsz, s, d = x.shape
    dh = C_HEAD_DIM
    qh = d // dh
    group = qh // C_KV_HEADS
    n_proj = w_in.shape[1]
    half = jnp.concatenate([jnp.arange(0, dh, 2), jnp.arange(1, dh, 2)])
    n_rot = (qh + C_KV_HEADS) * dh
    w_rot = w_in[:, :n_rot].reshape(d, n_rot // dh, dh // 2, 2).transpose(0, 1, 3, 2).reshape(d, n_rot)
    w = jnp.concatenate([w_rot, w_in[:, n_rot:]], axis=1)
    p = _project(x.reshape(bsz * s, d), w, jnp.zeros((n_proj,), jnp.float32), jnp.bfloat16)
    p = p.reshape(bsz, s, n_proj)
    cos, sin = _axial_rope_tables(s)
    cos_f = jnp.concatenate([cos, cos], axis=-1)
    sin_f = jnp.concatenate([-sin, sin], axis=-1)
    tq = min(ATTN_Q_TILE, s)
    gw = group * dh
    q_rows = pl.BlockSpec((tq, dh), lambda b, h, i: (i, 0))
    k_rows = pl.BlockSpec((s, dh), lambda b, h, i: (0, 0))
    vec = pl.BlockSpec((1, dh), lambda b, h, i: (0, 0))
    return pl.pallas_call(
        _attn_kernel,
        out_shape=jax.ShapeDtypeStruct((bsz, s, d), jnp.bfloat16),
        grid=(bsz, C_KV_HEADS, s // tq),
        in_specs=[pl.BlockSpec((None, tq, gw), lambda b, h, i: (b, i, h)),
                  pl.BlockSpec((None, s, dh), lambda b, h, i: (b, 0, qh + h)),
                  pl.BlockSpec((None, s, dh), lambda b, h, i: (b, 0, qh + C_KV_HEADS + h)),
                  q_rows, q_rows, k_rows, k_rows, vec, vec],
        out_specs=pl.BlockSpec((None, tq, gw), lambda b, h, i: (b, i, h)),
        scratch_shapes=[pltpu.VMEM((s, dh), jnp.bfloat16), pltpu.VMEM((s, 2 * dh), jnp.bfloat16)],
        compiler_params=_compiler_params(("parallel", "parallel", "arbitrary")),
        name="axial_attention",
    )(p, p, p, cos_f, sin_f, cos_f, sin_f, q_gain[half].reshape(1, dh), k_gain[half].reshape(1, dh))


def kernel(x, a_w_in, a_b_in, a_head_gain, a_w_out, b_w_in, b_mu, b_w0, b_w_up, b_a0, b_a_up, b_k_k, b_k_a, b_r_k, b_g_up, b_lnx_g, b_lnx_b, b_w_out, c_w_in, c_q_gain, c_k_gain, c_w_out, ln1_g, ln1_b, moe_w_router, moe_b_router, moe_w_gu, moe_b_gu, moe_w_dn, moe_b_dn, ln2_g, ln2_b):
    bsz, s, d = x.shape
    depth = ln1_g.shape[0]
    xt = x.reshape(bsz * s, d)
    for i in range(depth):
        kind = i % N_MIXERS
        j = i // N_MIXERS
        x3 = xt.reshape(bsz, s, d)
        if kind == 0:
            h = _mlstm_mixer(x3, a_w_in[j], a_b_in[j], a_head_gain[j])
            w_out = a_w_out[j]
        elif kind == 1:
            h = _rwkv7_mixer(x3, b_w_in[j], b_mu[j], b_w0[j], b_w_up[j], b_a0[j], b_a_up[j],
                             b_k_k[j], b_k_a[j], b_r_k[j], b_g_up[j], b_lnx_g[j], b_lnx_b[j])
            w_out = b_w_out[j]
        else:
            h = _axial_gqa_mixer(x3, c_w_in[j], c_q_gain[j], c_k_gain[j])
            w_out = c_w_out[j]
        xt, xt_bf16 = _outproj_ln(h.reshape(bsz * s, d), w_out, xt, ln1_g[i], ln1_b[i])
        w_gu = _regroup_glu_columns(moe_w_gu, i)
        b_gu = _regroup_glu_bias(moe_b_gu[i])
        w_dn = _layer_to_bf16(moe_w_dn, i)
        b_dn = moe_b_dn[i][:, None, :]
        n_group = bsz * s // MOE_TOKEN_GROUPS
        groups = [_moe_ffn(xt, xt_bf16, grp, n_group, moe_w_router[i], moe_b_router[i], w_gu, b_gu, w_dn, b_dn)
                  for grp in range(MOE_TOKEN_GROUPS)]
        xt = _combine_ln([y for y, _ in groups], [r for _, r in groups], xt, ln2_g[i], ln2_b[i])
    return xt.reshape(bsz, s, d)
```

```python
import functools

import jax
import jax.numpy as jnp
from jax import lax
from jax.experimental import pallas as pl
from jax.experimental.pallas import tpu as pltpu

DEPTH = 4
N_MIXERS = 3
GRID_W = 64
DEEPNORM_ALPHA = (2 * DEPTH) ** 0.25
LN_EPS = 1e-5
RMS_EPS = 1e-6

A_HEADS = 8
A_DQK = 64
A_M_INIT = -1e30

B_HEAD = 64
B_DECAY_LORA = 64
B_AAA_LORA = 64
B_GATE_LORA = 128
B_GN_EPS = 64e-5

C_HEAD_DIM = 128
C_KV_HEADS = 2
ROPE_THETA = 10000.0

TOP_K = 4
SWIGLU_LIMIT = 7.0
SWIGLU_ALPHA = 1.702

LANES = 128
SUBLANES = 8
BF16_TILE_ROWS = 16
VMEM_LIMIT_BYTES = 56 * 1024 * 1024
MAX_COL_CHUNK = 5 * LANES

ROW_TILE = 512
MOE_ROW_TILE = 512
MOE_TOKEN_GROUPS = 2


def _compiler_params(semantics):
    return pltpu.CompilerParams(dimension_semantics=semantics, vmem_limit_bytes=VMEM_LIMIT_BYTES)


def _row_tile(m):
    t = min(ROW_TILE, m)
    assert m % t == 0, (m, t)
    return t


def _col_chunk(n):
    return max(c for c in range(LANES, MAX_COL_CHUNK + 1, LANES) if n % c == 0)


def _proj_kernel(x_ref, w_ref, b_ref, *o_refs):
    xb = x_ref[...].astype(jnp.bfloat16)
    col = 0
    for o_ref in o_refs:
        n = o_ref.shape[1]
        step = _col_chunk(n)
        for j in range(0, n, step):
            acc = jnp.dot(xb, w_ref[:, col + j:col + j + step], preferred_element_type=jnp.float32)
            o_ref[:, j:j + step] = (acc + b_ref[:, col + j:col + j + step]).astype(o_ref.dtype)
        col += n


def _project(x, w, b, out_dtype, f32_tail=0):
    m, k = x.shape
    n = w.shape[1]
    widths = [n - f32_tail, f32_tail] if f32_tail else [n]
    dtypes = [out_dtype, jnp.float32]
    assert all(c % LANES == 0 for c in widths)
    tm = _row_tile(m)
    outs = pl.pallas_call(
        _proj_kernel,
        out_shape=[jax.ShapeDtypeStruct((m, c), dt) for c, dt in zip(widths, dtypes)],
        grid=(m // tm,),
        in_specs=[pl.BlockSpec((tm, k), lambda i: (i, 0)),
                  pl.BlockSpec((k, n), lambda i: (0, 0)),
                  pl.BlockSpec((1, n), lambda i: (0, 0))],
        out_specs=[pl.BlockSpec((tm, c), lambda i: (i, 0)) for c in widths],
        compiler_params=_compiler_params(("parallel",)),
        name="project",
    )(x, w.astype(jnp.bfloat16), b.reshape(1, n).astype(jnp.float32))
    return tuple(outs) if f32_tail else outs[0]


def _layer_norm_rows(z, g, b):
    mu = jnp.mean(z, axis=-1, keepdims=True)
    zc = z - mu
    var = jnp.mean(zc * zc, axis=-1, keepdims=True)
    return zc * lax.rsqrt(var + LN_EPS) * g + b


def _outproj_ln_kernel(h_ref, w_ref, x_ref, g_ref, b_ref, o_ref, ob_ref):
    mix = jnp.dot(h_ref[...].astype(jnp.bfloat16), w_ref[...], preferred_element_type=jnp.float32)
    z = DEEPNORM_ALPHA * x_ref[...] + mix
    y = _layer_norm_rows(z, g_ref[...], b_ref[...])
    o_ref[...] = y
    ob_ref[...] = y.astype(ob_ref.dtype)


def _outproj_ln(h, w_out, x, g, b):
    m, d = x.shape
    tm = _row_tile(m)
    row = pl.BlockSpec((tm, d), lambda i: (i, 0))
    vec = pl.BlockSpec((1, d), lambda i: (0, 0))
    return pl.pallas_call(
        _outproj_ln_kernel,
        out_shape=[jax.ShapeDtypeStruct((m, d), jnp.float32), jax.ShapeDtypeStruct((m, d), jnp.bfloat16)],
        grid=(m // tm,),
        in_specs=[row, pl.BlockSpec((d, d), lambda i: (0, 0)), row, vec, vec],
        out_specs=[row, row],
        compiler_params=_compiler_params(("parallel",)),
        name="outproj_ln",
    )(h, w_out.astype(jnp.bfloat16), x, g.reshape(1, d), b.reshape(1, d))


ROUTE_IDX_LANE = 0
ROUTE_RANK_LANE = TOP_K
ROUTE_GATE_LANE = 2 * TOP_K


def _combine_ln_kernel(x_ref, g_ref, b_ref, *rest, tiles_per_group):
    o_ref = rest[-1]
    n_groups = (len(rest) - 1) // 2
    group = pl.program_id(0) // tiles_per_group
    for grp in range(n_groups):
        y_ref, r_ref = rest[2 * grp], rest[2 * grp + 1]

        @pl.when(group == grp)
        def _():
            y = jnp.zeros(x_ref.shape, jnp.float32)
            for k in range(TOP_K):
                gate = r_ref[:, ROUTE_GATE_LANE + k:ROUTE_GATE_LANE + k + 1]
                y = y + gate * y_ref[k].astype(jnp.float32)
            z = DEEPNORM_ALPHA * x_ref[...] + y
            o_ref[...] = _layer_norm_rows(z, g_ref[...], b_ref[...])


def _combine_ln(y_rows_groups, route_groups, x, g, b):
    n_groups = len(y_rows_groups)
    m = y_rows_groups[0].shape[1]
    d = x.shape[1]
    tm = _row_tile(m)
    tiles = m // tm
    row = pl.BlockSpec((tm, d), lambda i: (i, 0))
    vec = pl.BlockSpec((1, d), lambda i: (0, 0))
    in_specs = [row, vec, vec]
    operands = [x, g.reshape(1, d), b.reshape(1, d)]
    for grp in range(n_groups):
        local = lambda i, grp=grp: jnp.clip(i - grp * tiles, 0, tiles - 1)
        in_specs += [pl.BlockSpec((TOP_K, tm, d), lambda i, local=local: (0, local(i), 0)),
                     pl.BlockSpec((tm, LANES), lambda i, local=local: (local(i), 0))]
        operands += [y_rows_groups[grp], route_groups[grp]]
    return pl.pallas_call(
        functools.partial(_combine_ln_kernel, tiles_per_group=tiles),
        out_shape=jax.ShapeDtypeStruct(x.shape, jnp.float32),
        grid=(n_groups * tiles,),
        in_specs=in_specs,
        out_specs=row,
        compiler_params=_compiler_params(("parallel",)),
        name="combine_ln",
    )(*operands)


ROUTER_PAD_BIAS = -1e30


def _router_kernel(x_ref, w_ref, b_ref, o_ref, cnt_ref, base_s):
    f32 = jnp.float32

    @pl.when(pl.program_id(0) == 0)
    def _():
        base_s[...] = jnp.zeros_like(base_s)

    t = x_ref.shape[0]
    x = x_ref[...]
    x_hi = x.astype(jnp.bfloat16)
    x_lo = (x - x_hi.astype(f32)).astype(jnp.bfloat16)
    both = jnp.dot(x_hi, w_ref[...], preferred_element_type=f32)
    logits = (both[:, :LANES] + both[:, LANES:]
              + jnp.dot(x_lo, w_ref[:, :LANES], preferred_element_type=f32)) + b_ref[...]
    lane = lax.broadcasted_iota(jnp.int32, (t, LANES), 1)
    vals = logits
    tops, idxs, sels = [], [], []
    for _ in range(TOP_K):
        top = jnp.max(vals, axis=-1, keepdims=True)
        idx = jnp.min(jnp.where(vals == top, lane, LANES), axis=-1, keepdims=True)
        sel = lane == idx
        vals = jnp.where(sel, -jnp.inf, vals)
        tops.append(top)
        idxs.append(idx)
        sels.append(sel)
    exps = [jnp.exp(top - tops[0]) for top in tops]
    total = exps[0]
    for e in exps[1:]:
        total = total + e
    chosen = jnp.zeros((t, LANES), f32)
    for sel in sels:
        chosen = chosen + sel.astype(f32)
    earlier = (lax.broadcasted_iota(jnp.int32, (t, t), 0) > lax.broadcasted_iota(jnp.int32, (t, t), 1))
    prefix = jnp.dot(earlier.astype(jnp.bfloat16), chosen.astype(jnp.bfloat16),
                     preferred_element_type=f32) + base_s[...]
    packed = jnp.zeros((t, LANES), f32)
    for k in range(TOP_K):
        rank = jnp.sum(jnp.where(sels[k], prefix, 0.0), axis=-1, keepdims=True)
        packed = jnp.where(lane == ROUTE_IDX_LANE + k, idxs[k].astype(f32), packed)
        packed = jnp.where(lane == ROUTE_RANK_LANE + k, rank, packed)
        packed = jnp.where(lane == ROUTE_GATE_LANE + k, exps[k] / total, packed)
    o_ref[...] = packed
    base_s[...] = base_s[...] + jnp.sum(chosen, axis=0, keepdims=True)
    cnt_ref[...] = base_s[...]


def _route(xt, group, m, w_router, b_router):
    d = xt.shape[1]
    tm = _row_tile(m)
    first = group * (m // tm)
    n_e = w_router.shape[1]
    w = jnp.pad(w_router, ((0, 0), (0, LANES - n_e)))
    b = jnp.pad(b_router, (0, LANES - n_e), constant_values=ROUTER_PAD_BIAS)
    w_hi = w.astype(jnp.bfloat16)
    w_lo = (w - w_hi.astype(jnp.float32)).astype(jnp.bfloat16)
    return pl.pallas_call(
        _router_kernel,
        out_shape=[jax.ShapeDtypeStruct((m, LANES), jnp.float32), jax.ShapeDtypeStruct((1, LANES), jnp.float32)],
        grid=(m // tm,),
        in_specs=[pl.BlockSpec((tm, d), lambda i: (first + i, 0)),
                  pl.BlockSpec((d, 2 * LANES), lambda i: (0, 0)),
                  pl.BlockSpec((1, LANES), lambda i: (0, 0))],
        out_specs=[pl.BlockSpec((tm, LANES), lambda i: (i, 0)), pl.BlockSpec((1, LANES), lambda i: (0, 0))],
        scratch_shapes=[pltpu.VMEM((1, LANES), jnp.float32)],
        compiler_params=_compiler_params(("arbitrary",)),
        name="router",
    )(xt, jnp.concatenate([w_hi, w_lo], axis=1), b.reshape(1, LANES))


GLU_BLOCK = 2 * LANES


def _regroup_kernel(w_ref, p_ref, o_ref):
    w = w_ref[0].astype(jnp.bfloat16)
    for c in range(0, w.shape[1], GLU_BLOCK):
        o_ref[0, :, c:c + GLU_BLOCK] = jnp.dot(w[:, c:c + GLU_BLOCK], p_ref[...],
                                               preferred_element_type=jnp.float32).astype(o_ref.dtype)


def _regroup_glu_columns(w_gu_layers, layer):
    _, n_e, d, f2 = w_gu_layers.shape
    src = jnp.arange(GLU_BLOCK)
    perm = (src[:, None] == (2 * (src % LANES) + src // LANES)[None, :]).astype(jnp.bfloat16)
    tk = _row_tile(d)
    return pl.pallas_call(
        _regroup_kernel,
        out_shape=jax.ShapeDtypeStruct((n_e, d, f2), jnp.bfloat16),
        grid=(n_e, d // tk),
        in_specs=[pl.BlockSpec((None, 1, tk, f2), lambda e, i: (layer, e, i, 0)),
                  pl.BlockSpec((GLU_BLOCK, GLU_BLOCK), lambda e, i: (0, 0))],
        out_specs=pl.BlockSpec((1, tk, f2), lambda e, i: (e, i, 0)),
        compiler_params=_compiler_params(("parallel", "parallel")),
        name="regroup_glu",
    )(w_gu_layers, perm)


def _cast_kernel(w_ref, o_ref):
    o_ref[...] = w_ref[...].astype(o_ref.dtype)


def _layer_to_bf16(w_layers, layer):
    _, n_e, r, c = w_layers.shape
    tr = _row_tile(r)
    return pl.pallas_call(
        _cast_kernel,
        out_shape=jax.ShapeDtypeStruct((n_e, r, c), jnp.bfloat16),
        grid=(n_e, r // tr),
        in_specs=[pl.BlockSpec((None, 1, tr, c), lambda e, i: (layer, e, i, 0))],
        out_specs=pl.BlockSpec((1, tr, c), lambda e, i: (e, i, 0)),
        compiler_params=_compiler_params(("parallel", "parallel")),
        name="layer_to_bf16",
    )(w_layers)


def _slot_kernel(r_ref, start_ref, o_ref):
    lane = lax.broadcasted_iota(jnp.int32, r_ref.shape, 1)
    route = r_ref[...]
    slots = jnp.zeros(r_ref.shape, jnp.float32)
    for k in range(TOP_K):
        expert = route[:, ROUTE_IDX_LANE + k:ROUTE_IDX_LANE + k + 1].astype(jnp.int32)
        first = jnp.sum(jnp.where(lane == expert, start_ref[...], 0.0), axis=-1, keepdims=True)
        slots = jnp.where(lane == k, first + route[:, ROUTE_RANK_LANE + k:ROUTE_RANK_LANE + k + 1], slots)
    o_ref[...] = slots.astype(jnp.int32)


def _slots(route, expert_start):
    m = route.shape[0]
    tm = _row_tile(m)
    return pl.pallas_call(
        _slot_kernel,
        out_shape=jax.ShapeDtypeStruct((m, LANES), jnp.int32),
        grid=(m // tm,),
        in_specs=[pl.BlockSpec((tm, LANES), lambda i: (i, 0)), pl.BlockSpec((1, LANES), lambda i: (0, 0))],
        out_specs=pl.BlockSpec((tm, LANES), lambda i: (i, 0)),
        compiler_params=_compiler_params(("parallel",)),
        name="moe_slots",
    )(route, expert_start)


def _regroup_glu_bias(b_gu):
    n_e, f2 = b_gu.shape
    return b_gu.reshape(n_e, f2 // GLU_BLOCK, LANES, 2).transpose(0, 1, 3, 2).reshape(n_e, 1, f2)


def _expert_kernel(be_ref, nb_ref, x_ref, wgu_ref, bgu_ref, wd_ref, bd_ref, o_ref):
    @pl.when(pl.program_id(0) < nb_ref[0])
    def _():
        xb = x_ref[...]
        acts = []
        for c in range(0, wgu_ref.shape[2], GLU_BLOCK):
            h = (jnp.dot(xb, wgu_ref[0, :, c:c + GLU_BLOCK], preferred_element_type=jnp.float32)
                 + bgu_ref[0, :, c:c + GLU_BLOCK])
            hg = jnp.minimum(h[:, :LANES], SWIGLU_LIMIT)
            hl = jnp.clip(h[:, LANES:], -SWIGLU_LIMIT, SWIGLU_LIMIT)
            acts.append((hg * jax.nn.sigmoid(SWIGLU_ALPHA * hg) * (hl + 1.0)).astype(jnp.bfloat16))
        act = jnp.concatenate(acts, axis=1)
        y = jnp.dot(act, wd_ref[0], preferred_element_type=jnp.float32) + bd_ref[0]
        o_ref[...] = y.astype(o_ref.dtype)


def _expert_ffn(x_sorted, block_expert, n_used, w_gu, b_gu, w_dn, b_dn):
    cap, d = x_sorted.shape
    f2 = w_gu.shape[2]
    tm = MOE_ROW_TILE
    n_blocks = cap // tm

    def blk(i, be, nb):
        return (jnp.minimum(i, nb[0] - 1), 0)

    def wsel(i, be, nb):
        return (be[i], 0, 0)

    grid_spec = pltpu.PrefetchScalarGridSpec(
        num_scalar_prefetch=2,
        grid=(n_blocks,),
        in_specs=[pl.BlockSpec((tm, d), blk),
                  pl.BlockSpec((1, d, f2), wsel),
                  pl.BlockSpec((1, 1, f2), wsel),
                  pl.BlockSpec((1, f2 // 2, d), wsel),
                  pl.BlockSpec((1, 1, d), wsel)],
        out_specs=pl.BlockSpec((tm, d), blk),
    )
    return pl.pallas_call(
        _expert_kernel,
        out_shape=jax.ShapeDtypeStruct((cap, d), jnp.bfloat16),
        grid_spec=grid_spec,
        compiler_params=_compiler_params(("arbitrary",)),
        name="expert_ffn",
    )(block_expert, n_used, x_sorted, w_gu, b_gu, w_dn, b_dn)


def _moe_ffn(x, x_bf16, group, n, w_router, b_router, w_gu, b_gu, w_dn, b_dn):
    d = x.shape[1]
    tm = MOE_ROW_TILE
    n_e = w_router.shape[1]
    route, counts = _route(x, group, n, w_router, b_router)
    counts = counts[0].astype(jnp.int32)
    padded = (counts + tm - 1) // tm * tm
    pad_end = jnp.cumsum(padded)
    pad_start = pad_end - padded
    cap = n * TOP_K + n_e * tm
    n_blocks = cap // tm
    slot = _slots(route, pad_start.astype(jnp.float32).reshape(1, LANES))[:, :TOP_K]
    slot_kmajor = slot.T.reshape(-1)
    block_first_row = jnp.arange(n_blocks, dtype=jnp.int32) * tm
    block_expert = jnp.minimum(jnp.sum(pad_end[None, :n_e] <= block_first_row[:, None], axis=1), n_e - 1)
    n_used = (pad_end[n_e - 1] // tm).reshape(1)
    top_idx = route[:, ROUTE_IDX_LANE:ROUTE_IDX_LANE + TOP_K].astype(jnp.int32).reshape(-1)
    tok_by_expert = jnp.argsort(top_idx, stable=True).astype(jnp.int32) // TOP_K
    unpadded_start = jnp.cumsum(counts) - counts
    row_in_expert = (block_first_row - pad_start[block_expert])[:, None] + jnp.arange(tm, dtype=jnp.int32)[None, :]
    src = jnp.minimum(unpadded_start[block_expert][:, None] + row_in_expert, n * TOP_K - 1).reshape(-1)
    valid = (row_in_expert < counts[block_expert][:, None]).reshape(-1)
    tok_of_slot = group * n + jnp.where(valid, tok_by_expert.at[src].get(mode="promise_in_bounds"), 0)

    x_sorted = x_bf16.at[tok_of_slot].get(mode="promise_in_bounds")
    yb = _expert_ffn(x_sorted, block_expert.astype(jnp.int32), n_used.astype(jnp.int32), w_gu, b_gu, w_dn, b_dn)
    return yb.at[slot_kmajor].get(mode="promise_in_bounds").reshape(TOP_K, n, d), route


MLSTM_CHUNK = 128
MLSTM_PAIR = 2
MLSTM_NEG = -1e30


def _interleave(generators):
    results = [None] * len(generators)
    live = list(enumerate(generators))
    while live:
        still = []
        for idx, gen in live:
            try:
                next(gen)
                still.append((idx, gen))
            except StopIteration as stop:
                results[idx] = stop.value
        live = still
    return results


def _cumsum_rows(x, reverse):
    n = x.shape[0]
    row = lax.broadcasted_iota(jnp.int32, x.shape, 0)
    sh = 1
    while sh < n:
        if reverse:
            x = x + jnp.where(row < n - sh, pltpu.roll(x, n - sh, 0), 0.0)
        else:
            x = x + jnp.where(row >= sh, pltpu.roll(x, sh, 0), 0.0)
        sh *= 2
    return x


def _cummax_rows(x, reverse):
    n = x.shape[0]
    row = lax.broadcasted_iota(jnp.int32, x.shape, 0)
    sh = 1
    while sh < n:
        if reverse:
            x = jnp.maximum(x, jnp.where(row < n - sh, pltpu.roll(x, n - sh, 0), -jnp.inf))
        else:
            x = jnp.maximum(x, jnp.where(row >= sh, pltpu.roll(x, sh, 0), -jnp.inf))
        sh *= 2
    return x


def _mlstm_kernel(q_ref, k_ref, v_ref, og_ref, g_ref, gain_ref, o_ref, h_s, c_s):
    f32, bf16 = jnp.float32, jnp.bfloat16
    seq = q_ref.shape[0]
    L = MLSTM_CHUNK
    n_chunks = seq // L
    dv = v_ref.shape[1] // MLSTM_PAIR
    dqk = q_ref.shape[1] // MLSTM_PAIR

    lane_q = lax.broadcasted_iota(jnp.int32, (1, q_ref.shape[1]), 1) // dqk
    row = lax.broadcasted_iota(jnp.int32, (L, L), 0)
    col = lax.broadcasted_iota(jnp.int32, (L, L), 1)
    causal = (col <= row, col >= row)
    ones_v = jnp.ones((L, dv), bf16)

    c_s[...] = jnp.zeros_like(c_s)

    def gate_terms(start, d):
        rows = pl.ds(start, L)
        gates = g_ref[rows, :]
        log_f = jnp.minimum(gates, 0.0) - jnp.log(1.0 + jnp.exp(-jnp.abs(gates)))
        cum = _cumsum_rows(log_f, reverse=(d == 1))
        src = gates - pltpu.roll(cum, LANES - MLSTM_PAIR, 1)
        run_max = _cummax_rows(src, reverse=(d == 1))
        return rows, gates, cum, src.T, run_max, q_ref[rows, :], k_ref[rows, :]

    def head_unit(terms, d, j, m_st):
        rows, gates, cum, src_t, run_max, q_all, k_all = terms
        lane_i = 2 * MLSTM_PAIR * d + j
        lane_f = lane_i + MLSTM_PAIR
        b_col, i_col = cum[:, lane_f:lane_f + 1], gates[:, lane_i:lane_i + 1]
        src_row = src_t[lane_i:lane_i + 1, :]
        qz = jnp.where(lane_q == j, q_all, jnp.zeros_like(q_all))
        kz = jnp.where(lane_q == j, k_all, jnp.zeros_like(k_all))
        state = c_s[2 * d + j]
        scores = _nt_dot(qz, kz)
        carried = jnp.dot(qz, state.astype(bf16), preferred_element_type=f32)
        yield
        mu = jnp.maximum(m_st, run_max[:, lane_i:lane_i + 1])
        v_aug = jnp.concatenate([v_ref[rows, j * dv:(j + 1) * dv], ones_v], axis=1)
        qk = scores * (A_DQK ** -0.5) * jnp.exp(jnp.where(causal[d], src_row - mu, MLSTM_NEG))
        edge = L - 1 if d == 0 else 0
        b_last = b_col[edge:edge + 1, :]
        m_new = b_last + mu[edge:edge + 1, :]
        w_s = jnp.exp(b_last - b_col + i_col - m_new)
        local = jnp.dot(qk.astype(bf16), v_aug, preferred_element_type=f32)
        update = _tn_dot((w_s * kz.astype(f32)).astype(bf16), v_aug)
        yield
        nd = local + jnp.exp(m_st - mu) * (A_DQK ** -0.5) * carried
        h_s[d, rows, j * dv:(j + 1) * dv] = nd[:, :dv] / jnp.maximum(jnp.abs(nd[:, dv:]), jnp.exp(-b_col - mu))
        c_s[2 * d + j] = jnp.exp(b_last + m_st - m_new) * state + update
        return m_new

    def body(c, m_all):
        terms = (gate_terms(pl.multiple_of(c * L, L), 0), gate_terms(pl.multiple_of((n_chunks - 1 - c) * L, L), 1))
        m_new = [None] * (2 * MLSTM_PAIR)
        for j in range(MLSTM_PAIR):
            m_new[j], m_new[MLSTM_PAIR + j] = _interleave(
                [head_unit(terms[d], d, j, m_all[MLSTM_PAIR * d + j]) for d in range(2)])
        return tuple(m_new)

    m_init = tuple(jnp.full((1, 1), A_M_INIT, f32) for _ in range(2 * MLSTM_PAIR))
    lax.fori_loop(0, n_chunks, body, m_init)

    def finish(c, carry):
        rows = pl.ds(pl.multiple_of(c * L, L), L)
        for j in range(MLSTM_PAIR):
            cols = slice(j * dv, (j + 1) * dv)
            h = h_s[0, rows, cols] + h_s[1, rows, cols]
            hn = h * lax.rsqrt(jnp.mean(h * h, axis=-1, keepdims=True) + RMS_EPS) * gain_ref[:, cols]
            o_ref[rows, cols] = (hn * jax.nn.sigmoid(og_ref[rows, cols].astype(f32))).astype(o_ref.dtype)
        return carry

    lax.fori_loop(0, n_chunks, finish, 0)


def _mlstm_mixer(x, w_in, b_in, head_gain):
    bsz, s, d = x.shape
    a_qk = A_HEADS * A_DQK
    dv = d // A_HEADS
    n_main = 2 * a_qk + 2 * d
    n_pairs = A_HEADS // MLSTM_PAIR
    pw_qk = MLSTM_PAIR * A_DQK
    pw_v = MLSTM_PAIR * dv
    assert pw_qk == LANES and s % MLSTM_CHUNK == 0
    gate_cols = jnp.asarray([[n_main + t * A_HEADS + MLSTM_PAIR * hp + j for t in range(4) for j in range(MLSTM_PAIR)]
                             for hp in range(n_pairs)])
    n_gate = gate_cols.shape[1]
    w_tail = jnp.pad(w_in[:, gate_cols], ((0, 0), (0, 0), (0, LANES - n_gate))).reshape(d, n_pairs * LANES)
    b_tail = jnp.pad(b_in[gate_cols], ((0, 0), (0, LANES - n_gate))).reshape(n_pairs * LANES)
    w = jnp.concatenate([w_in[:, :n_main], w_tail], axis=1)
    b = jnp.concatenate([b_in[:n_main], b_tail])
    p, gates = _project(x.reshape(bsz * s, d), w, b, jnp.bfloat16, f32_tail=n_pairs * LANES)
    p = p.reshape(bsz, s, n_main)
    gates = gates.reshape(bsz, s, n_pairs * LANES)
    k0 = a_qk // pw_qk
    v0 = 2 * a_qk // pw_v
    o0 = (2 * a_qk + d) // pw_v
    return pl.pallas_call(
        _mlstm_kernel,
        out_shape=jax.ShapeDtypeStruct((bsz, s, d), jnp.bfloat16),
        grid=(bsz, n_pairs),
        in_specs=[pl.BlockSpec((None, s, pw_qk), lambda b, h: (b, 0, h)),
                  pl.BlockSpec((None, s, pw_qk), lambda b, h: (b, 0, k0 + h)),
                  pl.BlockSpec((None, s, pw_v), lambda b, h: (b, 0, v0 + h)),
                  pl.BlockSpec((None, s, pw_v), lambda b, h: (b, 0, o0 + h)),
                  pl.BlockSpec((None, s, LANES), lambda b, h: (b, 0, h)),
                  pl.BlockSpec((1, pw_v), lambda b, h: (0, h))],
        out_specs=pl.BlockSpec((None, s, pw_v), lambda b, h: (b, 0, h)),
        scratch_shapes=[pltpu.VMEM((2, s, pw_v), jnp.float32),
                        pltpu.VMEM((2 * MLSTM_PAIR, pw_qk, 2 * dv), jnp.float32)],
        compiler_params=_compiler_params(("parallel", "parallel")),
        name="mlstm_scan",
    )(p, p, p, p, gates, head_gain.reshape(1, d))


RWKV_CHUNK = 64
RWKV_GROUP = 4
RWKV_LANES = RWKV_GROUP * B_HEAD
RWKV_PREP_ROWS = 256
RWKV_LOCAL_UNROLL = 4


def _f32_dot(a, b_bf16):
    hi = a.astype(jnp.bfloat16)
    lo = (a - hi.astype(jnp.float32)).astype(jnp.bfloat16)
    return (jnp.dot(hi, b_bf16, preferred_element_type=jnp.float32)
            + jnp.dot(lo, b_bf16, preferred_element_type=jnp.float32))


def _nt_dot(a, b):
    return lax.dot_general(a, b, (((1,), (1,)), ((), ())), preferred_element_type=jnp.float32)


def _tn_dot(a, b):
    return lax.dot_general(a, b, (((0,), (0,)), ((), ())), preferred_element_type=jnp.float32)


def _rwkv_scan_kernel(r_ref, k_ref, v_ref, wl_ref, al_ref, gl_ref, mu_r_ref, mu_k_ref, mu_v_ref, mu_wl_ref,
                      mu_al_ref, mu_gl_ref, w0_ref, wup_ref, a0_ref, aup_ref, kk_ref, ka_ref, rk_ref, gup_ref,
                      lng_ref, lnb_ref, o_ref,
                      r_s, v_s, kap_s, g_s, logw_s, kh_s, beta_s, y_s,
                      wmat_s, uloc_s, avq_s, rt_s, aqb_s, kend_s, bend_s, ptot_s):
    f32, bf16 = jnp.float32, jnp.bfloat16
    seq = r_ref.shape[0]
    L, W, RT = RWKV_CHUNK, RWKV_LANES, RWKV_PREP_ROWS
    n_chunks = seq // L
    n_prep = seq // RT

    lane = lax.broadcasted_iota(jnp.int32, (1, W), 1)
    head_masks = [(lane // B_HEAD) == h for h in range(RWKV_GROUP)]
    ones_bd = ((lax.broadcasted_iota(jnp.int32, (W, W), 0) // B_HEAD)
               == (lax.broadcasted_iota(jnp.int32, (W, W), 1) // B_HEAD)).astype(bf16)

    def block_diag(a):
        zero = jnp.zeros_like(a)
        return jnp.concatenate([jnp.where(m, a, zero) for m in head_masks], axis=0)

    def seg_sum(a):
        return _f32_dot(a, ones_bd)

    def shifted(ref, mu_ref, i, rows):
        o = pl.multiple_of(i * RT, RT)
        x = ref[pl.ds(o, RT), :].astype(f32)
        nb = BF16_TILE_ROWS
        before = ref[pl.ds(pl.multiple_of(jnp.maximum(o - nb, 0), nb), nb), :].astype(f32)[nb - 1:nb, :]
        after = ref[pl.ds(pl.multiple_of(jnp.minimum(o + RT, seq - nb), nb), nb), :].astype(f32)[0:1, :]
        before = jnp.where(i > 0, before, 0.0)
        after = jnp.where(i < n_prep - 1, after, 0.0)
        prev = jnp.where(rows == 0, before, pltpu.roll(x, 1, 0))
        nxt = jnp.where(rows == RT - 1, after, pltpu.roll(x, RT - 1, 0))
        return x + mu_ref[...] * (0.5 * (prev + nxt) - x)

    def prep(i, carry):
        o = pl.multiple_of(i * RT, RT)
        rows_w = lax.broadcasted_iota(jnp.int32, (RT, W), 0)
        rows_n = lax.broadcasted_iota(jnp.int32, (RT, wl_ref.shape[1]), 0)
        r = shifted(r_ref, mu_r_ref, i, rows_w)
        k = shifted(k_ref, mu_k_ref, i, rows_w)
        v = shifted(v_ref, mu_v_ref, i, rows_w)
        wl = shifted(wl_ref, mu_wl_ref, i, rows_n)
        al = shifted(al_ref, mu_al_ref, i, rows_n)
        gl = shifted(gl_ref, mu_gl_ref, i, rows_n)
        kk0 = k * kk_ref[...]
        kap = kk0 / jnp.maximum(jnp.sqrt(seg_sum(kk0 * kk0)), 1e-12)
        r_s[pl.ds(o, RT), :] = r
        v_s[pl.ds(o, RT), :] = v.astype(bf16)
        kap_s[pl.ds(o, RT), :] = kap
        g_s[pl.ds(o, RT), :] = jnp.dot(jax.nn.sigmoid(gl).astype(bf16), gup_ref[...],
                                       preferred_element_type=f32).astype(bf16)
        for d in range(2):
            wl_d = jnp.tanh(wl[:, d * B_DECAY_LORA:(d + 1) * B_DECAY_LORA]).astype(bf16)
            al_d = al[:, d * B_AAA_LORA:(d + 1) * B_AAA_LORA].astype(bf16)
            w_raw = w0_ref[d:d + 1, :] + jnp.dot(wl_d, wup_ref[d], preferred_element_type=f32)
            a = jax.nn.sigmoid(a0_ref[d:d + 1, :] + jnp.dot(al_d, aup_ref[d], preferred_element_type=f32))
            logw_s[d, pl.ds(o, RT), :] = -jnp.exp(-0.5) * jax.nn.sigmoid(w_raw)
            kh_s[d, pl.ds(o, RT), :] = (k * (1.0 + (a - 1.0) * ka_ref[...])).astype(bf16)
            beta_s[d, pl.ds(o, RT), :] = (kap * a).astype(bf16)
        return carry

    lax.fori_loop(0, n_prep, prep, 0)

    row = lax.broadcasted_iota(jnp.int32, (L, W), 0)
    col = lax.broadcasted_iota(jnp.int32, (L, W), 1) % B_HEAD
    eye_cat = (row == col).astype(f32)
    strict = (row > col, row < col)
    incl = (row >= col, row <= col)

    def cumsum_rows(x, reverse):
        sh = 1
        while sh < L:
            if reverse:
                x = x + jnp.where(row < L - sh, pltpu.roll(x, L - sh, 0), 0.0)
            else:
                x = x + jnp.where(row >= sh, pltpu.roll(x, sh, 0), 0.0)
            sh *= 2
        return x

    interleave = _interleave

    def local_part(c, d):
        rows = pl.ds(pl.multiple_of(c * L, L), L)
        r, kap = r_s[rows, :], kap_s[rows, :]
        logw, kh, beta = logw_s[d, rows, :], kh_s[d, rows, :].astype(f32), beta_s[d, rows, :].astype(f32)
        cum = cumsum_rows(logw, reverse=(d == 1))
        tot = cum[L - 1:L, :] if d == 0 else cum[0:1, :]
        p_in, p_inv, p_end = jnp.exp(cum), jnp.exp(-cum), jnp.exp(tot - cum)
        kap_t = (kap * jnp.exp(cum - logw)).astype(bf16)
        r_t = (r * p_in).astype(bf16)
        v_bd = block_diag(v_s[rows, :])
        g_all = _nt_dot(jnp.concatenate([kap_t, r_t], axis=0),
                        jnp.concatenate([block_diag((beta * p_inv).astype(bf16)),
                                         block_diag((kh * p_inv).astype(bf16))], axis=0))
        yield
        x_pow = -jnp.where(strict[d], g_all[:L, :W], 0.0)
        a_ak = jnp.where(strict[d], g_all[:L, W:], 0.0)
        a_qb = jnp.where(incl[d], g_all[L:, :W], 0.0)
        a_qk = jnp.where(incl[d], g_all[L:, W:], 0.0)
        t_inv = eye_cat + x_pow
        av = jnp.dot(jnp.concatenate([a_ak, a_qk], axis=0).astype(bf16), v_bd, preferred_element_type=f32)
        x_pow = jnp.dot(x_pow.astype(bf16), block_diag(x_pow.astype(bf16)), preferred_element_type=f32)
        yield
        n_steps = L.bit_length() - 2
        for step in range(n_steps):
            if step < n_steps - 1:
                both = jnp.dot(jnp.concatenate([x_pow, t_inv], axis=0).astype(bf16),
                               block_diag(x_pow.astype(bf16)), preferred_element_type=f32)
                yield
                x_pow, t_inv = both[:L], t_inv + both[L:]
            else:
                last = jnp.dot(t_inv.astype(bf16), block_diag(x_pow.astype(bf16)), preferred_element_type=f32)
                yield
                t_inv = t_inv + last
        t_b = t_inv.astype(bf16)
        w_mat = jnp.dot(t_b, block_diag(kap_t), preferred_element_type=f32)
        u_loc = jnp.dot(t_b, block_diag(av[:L].astype(bf16)), preferred_element_type=f32)
        yield
        wmat_s[d, rows, :] = w_mat.astype(bf16)
        uloc_s[d, rows, :] = u_loc.astype(bf16)
        avq_s[d, rows, :] = av[L:].astype(bf16)
        rt_s[d, rows, :] = r_t
        aqb_s[d, rows, :] = a_qb.astype(bf16)
        kend_s[d, rows, :] = (kh * p_end).astype(bf16)
        bend_s[d, rows, :] = (beta * p_end).astype(bf16)
        ptot_s[d, pl.ds(pl.multiple_of(c * SUBLANES, SUBLANES), SUBLANES), :] = jnp.broadcast_to(
            jnp.exp(tot), (SUBLANES, W))

    def local_body(i, carry):
        interleave([local_part(i * RWKV_LOCAL_UNROLL + j, d) for j in range(RWKV_LOCAL_UNROLL) for d in range(2)])
        return carry

    lax.fori_loop(0, n_chunks // RWKV_LOCAL_UNROLL, local_body, 0)

    def carried_part(c, state, d):
        rows = pl.ds(pl.multiple_of(c * L, L), L)
        ws = _nt_dot(jnp.concatenate([wmat_s[d, rows, :], rt_s[d, rows, :]], axis=0), block_diag(state.astype(bf16)))
        yield
        u = ws[:L] + uloc_s[d, rows, :].astype(f32)
        u_b = u.astype(bf16)
        full = _tn_dot(jnp.concatenate([v_s[rows, :], -u_b], axis=0),
                       jnp.concatenate([kend_s[d, rows, :], bend_s[d, rows, :]], axis=0))
        y_loc = jnp.dot(aqb_s[d, rows, :], block_diag(u_b), preferred_element_type=f32)
        yield
        new_state = state * ptot_s[d, pl.ds(pl.multiple_of(c * SUBLANES, SUBLANES), SUBLANES), :][0:1, :]
        for h, m in enumerate(head_masks):
            new_state = new_state + jnp.where(m, full[h * B_HEAD:(h + 1) * B_HEAD, :], 0.0)
        y_s[d, rows, :] = ws[L:] + avq_s[d, rows, :].astype(f32) - y_loc
        return new_state

    def carried_body(c, states):
        return tuple(interleave([carried_part(c, states[0], 0), carried_part(n_chunks - 1 - c, states[1], 1)]))

    zero_state = jnp.zeros((B_HEAD, W), f32)
    lax.fori_loop(0, n_chunks, carried_body, (zero_state, zero_state))

    def finish(i, carry):
        rows = pl.ds(pl.multiple_of(i * RT, RT), RT)
        y = y_s[0, rows, :] + y_s[1, rows, :]
        mean = seg_sum(y) * (1.0 / B_HEAD)
        yc = y - mean
        var = seg_sum(yc * yc) * (1.0 / B_HEAD)
        yn = yc * lax.rsqrt(var + B_GN_EPS) * lng_ref[...] + lnb_ref[...]
        kh_both = kh_s[0, rows, :].astype(f32) + kh_s[1, rows, :].astype(f32)
        bonus = seg_sum(r_s[rows, :] * kh_both * rk_ref[...]) * v_s[rows, :].astype(f32)
        o_ref[rows, :] = ((yn + bonus) * g_s[rows, :].astype(f32)).astype(o_ref.dtype)
        return carry

    lax.fori_loop(0, n_prep, finish, 0)


def _rwkv7_mixer(x, w_in, mu, w0, w_up, a0, a_up, k_k, k_a, r_k, g_up, lnx_g, lnx_b):
    bsz, s, d = x.shape
    n_proj = w_in.shape[1]
    W = RWKV_LANES
    assert d % W == 0 and s % RWKV_PREP_ROWS == 0 and B_GATE_LORA == LANES
    assert 2 * B_DECAY_LORA == LANES and 2 * B_AAA_LORA == LANES
    assert (s // RWKV_CHUNK) % RWKV_LOCAL_UNROLL == 0
    p = _project(x.reshape(bsz * s, d), w_in, jnp.zeros((n_proj,), jnp.float32), jnp.bfloat16)
    p = p.reshape(bsz, s, n_proj)
    n_groups = d // W
    gw = d // W
    lora0 = 3 * d // LANES

    def seq_w(off):
        return pl.BlockSpec((None, s, W), lambda b, g, off=off: (b, 0, off + g))

    def seq_n(idx):
        return pl.BlockSpec((None, s, LANES), lambda b, g, idx=idx: (b, 0, idx))

    def vec_w(off):
        return pl.BlockSpec((1, W), lambda b, g, off=off: (0, off + g))

    def vec_n(idx):
        return pl.BlockSpec((1, LANES), lambda b, g, idx=idx: (0, idx))

    par_w = pl.BlockSpec((1, W), lambda b, g: (0, g))
    two_w = pl.BlockSpec((2, W), lambda b, g: (0, g))
    up_w = pl.BlockSpec((2, B_DECAY_LORA, W), lambda b, g: (0, 0, g))
    mu2 = mu.reshape(1, n_proj)
    row = lambda a: a.reshape(1, d)
    f32_w = pltpu.VMEM((s, W), jnp.float32)
    b16_w = pltpu.VMEM((s, W), jnp.bfloat16)
    f32_2w = pltpu.VMEM((2, s, W), jnp.float32)
    b16_2w = pltpu.VMEM((2, s, W), jnp.bfloat16)
    decay_rows = pltpu.VMEM((2, SUBLANES * (s // RWKV_CHUNK), W), jnp.float32)
    return pl.pallas_call(
        _rwkv_scan_kernel,
        out_shape=jax.ShapeDtypeStruct((bsz, s, d), jnp.bfloat16),
        grid=(bsz, n_groups),
        in_specs=[seq_w(0), seq_w(gw), seq_w(2 * gw), seq_n(lora0), seq_n(lora0 + 1), seq_n(lora0 + 2),
                  vec_w(0), vec_w(gw), vec_w(2 * gw), vec_n(lora0), vec_n(lora0 + 1), vec_n(lora0 + 2),
                  two_w, up_w, two_w, up_w, par_w, par_w, par_w,
                  pl.BlockSpec((B_GATE_LORA, W), lambda b, g: (0, g)), par_w, par_w],
        out_specs=pl.BlockSpec((None, s, W), lambda b, g: (b, 0, g)),
        scratch_shapes=[f32_w, b16_w, f32_w, b16_w, f32_2w, b16_2w, b16_2w, f32_2w,
                        b16_2w, b16_2w, b16_2w, b16_2w, b16_2w, b16_2w, b16_2w, decay_rows],
        compiler_params=_compiler_params(("parallel", "parallel")),
        name="rwkv7_scan",
    )(p, p, p, p, p, p, mu2, mu2, mu2, mu2, mu2, mu2,
      w0, w_up.astype(jnp.bfloat16), a0, a_up.astype(jnp.bfloat16), row(k_k), row(k_a), row(r_k),
      g_up.astype(jnp.bfloat16), row(lnx_g), row(lnx_b))


def _axial_rope_tables(s):
    rows = s // GRID_W
    row = jnp.repeat(jnp.arange(rows, dtype=jnp.float32), GRID_W)
    col = jnp.tile(jnp.arange(GRID_W, dtype=jnp.float32), rows)
    n_freq = C_HEAD_DIM // 4
    inv_freq = ROPE_THETA ** (-jnp.arange(n_freq, dtype=jnp.float32) / n_freq)
    ang = jnp.concatenate([row[:, None] * inv_freq, col[:, None] * inv_freq], axis=-1)
    return jnp.cos(ang), jnp.sin(ang)


ATTN_Q_TILE = 256


def _rms_rope(x, gain, cos_f, sin_f):
    xn = x * lax.rsqrt(jnp.mean(x * x, axis=-1, keepdims=True) + RMS_EPS) * gain
    return xn * cos_f + pltpu.roll(xn, C_HEAD_DIM // 2, 1) * sin_f


def _attn_kernel(q_ref, k_ref, v_ref, cq_ref, sq_ref, ck_ref, sk_ref, qg_ref, kg_ref, o_ref, kr_s, va_s):
    f32, bf16 = jnp.float32, jnp.bfloat16
    dh = C_HEAD_DIM

    @pl.when(pl.program_id(2) == 0)
    def _():
        kr_s[...] = _rms_rope(k_ref[...].astype(f32), kg_ref[...], ck_ref[...], sk_ref[...]).astype(bf16)
        va_s[:, :dh] = v_ref[...]
        va_s[:, dh:] = jnp.ones((va_s.shape[0], dh), bf16)

    group = q_ref.shape[1] // dh
    for g in range(group):
        cols = slice(g * dh, (g + 1) * dh)
        q = _rms_rope(q_ref[:, cols].astype(f32), qg_ref[...], cq_ref[...], sq_ref[...]) * dh ** -0.5
        scores = _nt_dot(q.astype(bf16), kr_s[...])
        p = jnp.exp((scores - jnp.max(scores, axis=-1, keepdims=True)).astype(bf16))
        od = jnp.dot(p, va_s[...], preferred_element_type=f32)
        o_ref[:, cols] = (od[:, :dh] / od[:, dh:]).astype(o_ref.dtype)


def _axial_gqa_mixer(x, w_in, q_gain, k_gain):
    bsz, s, d = x.shape
    dh = C_HEAD_DIM
    qh = d // dh
    group = qh // C_KV_HEADS
    n_proj = w_in.shape[1]
    half = jnp.concatenate([jnp.arange(0, dh, 2), jnp.arange(1, dh, 2)])
    n_rot = (qh + C_KV_HEADS) * dh
    w_rot = w_in[:, :n_rot].reshape(d, n_rot // dh, dh // 2, 2).transpose(0, 1, 3, 2).reshape(d, n_rot)
    w = jnp.concatenate([w_rot, w_in[:, n_rot:]], axis=1)
    p = _project(x.reshape(bsz * s, d), w, jnp.zeros((n_proj,), jnp.float32), jnp.bfloat16)
    p = p.reshape(bsz, s, n_proj)
    cos, sin = _axial_rope_tables(s)
    cos_f = jnp.concatenate([cos, cos], axis=-1)
    sin_f = jnp.concatenate([-sin, sin], axis=-1)
    tq = min(ATTN_Q_TILE, s)
    gw = group * dh
    q_rows = pl.BlockSpec((tq, dh), lambda b, h, i: (i, 0))
    k_rows = pl.BlockSpec((s, dh), lambda b, h, i: (0, 0))
    vec = pl.BlockSpec((1, dh), lambda b, h, i: (0, 0))
    return pl.pallas_call(
        _attn_kernel,
        out_shape=jax.ShapeDtypeStruct((bsz, s, d), jnp.bfloat16),
        grid=(bsz, C_KV_HEADS, s // tq),
        in_specs=[pl.BlockSpec((None, tq, gw), lambda b, h, i: (b, i, h)),
                  pl.BlockSpec((None, s, dh), lambda b, h, i: (b, 0, qh + h)),
                  pl.BlockSpec((None, s, dh), lambda b, h, i: (b, 0, qh + C_KV_HEADS + h)),
                  q_rows, q_rows, k_rows, k_rows, vec, vec],
        out_specs=pl.BlockSpec((None, tq, gw), lambda b, h, i: (b, i, h)),
        scratch_shapes=[pltpu.VMEM((s, dh), jnp.bfloat16), pltpu.VMEM((s, 2 * dh), jnp.bfloat16)],
        compiler_params=_compiler_params(("parallel", "parallel", "arbitrary")),
        name="axial_attention",
    )(p, p, p, cos_f, sin_f, cos_f, sin_f, q_gain[half].reshape(1, dh), k_gain[half].reshape(1, dh))


def kernel(x, a_w_in, a_b_in, a_head_gain, a_w_out, b_w_in, b_mu, b_w0, b_w_up, b_a0, b_a_up, b_k_k, b_k_a, b_r_k, b_g_up, b_lnx_g, b_lnx_b, b_w_out, c_w_in, c_q_gain, c_k_gain, c_w_out, ln1_g, ln1_b, moe_w_router, moe_b_router, moe_w_gu, moe_b_gu, moe_w_dn, moe_b_dn, ln2_g, ln2_b):
    bsz, s, d = x.shape
    depth = ln1_g.shape[0]
    xt = x.reshape(bsz * s, d)
    for i in range(depth):
        kind = i % N_MIXERS
        j = i // N_MIXERS
        x3 = xt.reshape(bsz, s, d)
        if kind == 0:
            h = _mlstm_mixer(x3, a_w_in[j], a_b_in[j], a_head_gain[j])
            w_out = a_w_out[j]
        elif kind == 1:
            h = _rwkv7_mixer(x3, b_w_in[j], b_mu[j], b_w0[j], b_w_up[j], b_a0[j], b_a_up[j],
                             b_k_k[j], b_k_a[j], b_r_k[j], b_g_up[j], b_lnx_g[j], b_lnx_b[j])
            w_out = b_w_out[j]
        else:
            h = _axial_gqa_mixer(x3, c_w_in[j], c_q_gain[j], c_k_gain[j])
            w_out = c_w_out[j]
        xt, xt_bf16 = _outproj_ln(h.reshape(bsz * s, d), w_out, xt, ln1_g[i], ln1_b[i])
        w_gu = _regroup_glu_columns(moe_w_gu, i)
        b_gu = _regroup_glu_bias(moe_b_gu[i])
        w_dn = _layer_to_bf16(moe_w_dn, i)
        b_dn = moe_b_dn[i][:, None, :]
        n_group = bsz * s // MOE_TOKEN_GROUPS
        groups = [_moe_ffn(xt, xt_bf16, grp, n_group, moe_w_router[i], moe_b_router[i], w_gu, b_gu, w_dn, b_dn)
                  for grp in range(MOE_TOKEN_GROUPS)]
        xt = _combine_ln([y for y, _ in groups], [r for _, r in groups], xt, ln2_g[i], ln2_b[i])
    return xt.reshape(bsz, s, d)
```

```python
import functools

import jax
import jax.numpy as jnp
from jax import lax
from jax.experimental import pallas as pl
from jax.experimental.pallas import tpu as pltpu

DEPTH = 4
N_MIXERS = 3
GRID_W = 64
DEEPNORM_ALPHA = (2 * DEPTH) ** 0.25
LN_EPS = 1e-5
RMS_EPS = 1e-6

A_HEADS = 8
A_DQK = 64
A_M_INIT = -1e30

B_HEAD = 64
B_DECAY_LORA = 64
B_AAA_LORA = 64
B_GATE_LORA = 128
B_GN_EPS = 64e-5

C_HEAD_DIM = 128
C_KV_HEADS = 2
ROPE_THETA = 10000.0

TOP_K = 4
SWIGLU_LIMIT = 7.0
SWIGLU_ALPHA = 1.702

LANES = 128
SUBLANES = 8
BF16_TILE_ROWS = 16
VMEM_LIMIT_BYTES = 56 * 1024 * 1024
MAX_COL_CHUNK = 5 * LANES

ROW_TILE = 512
MOE_ROW_TILE = 512
MOE_TOKEN_GROUPS = 2


def _compiler_params(semantics):
    return pltpu.CompilerParams(dimension_semantics=semantics, vmem_limit_bytes=VMEM_LIMIT_BYTES)


def _row_tile(m):
    t = min(ROW_TILE, m)
    assert m % t == 0, (m, t)
    return t


def _col_chunk(n):
    return max(c for c in range(LANES, MAX_COL_CHUNK + 1, LANES) if n % c == 0)


def _proj_kernel(x_ref, w_ref, b_ref, *o_refs):
    xb = x_ref[...].astype(jnp.bfloat16)
    col = 0
    for o_ref in o_refs:
        n = o_ref.shape[1]
        step = _col_chunk(n)
        for j in range(0, n, step):
            acc = jnp.dot(xb, w_ref[:, col + j:col + j + step], preferred_element_type=jnp.float32)
            o_ref[:, j:j + step] = (acc + b_ref[:, col + j:col + j + step]).astype(o_ref.dtype)
        col += n


def _project(x, w, b, out_dtype, f32_tail=0):
    m, k = x.shape
    n = w.shape[1]
    widths = [n - f32_tail, f32_tail] if f32_tail else [n]
    dtypes = [out_dtype, jnp.float32]
    assert all(c % LANES == 0 for c in widths)
    tm = _row_tile(m)
    outs = pl.pallas_call(
        _proj_kernel,
        out_shape=[jax.ShapeDtypeStruct((m, c), dt) for c, dt in zip(widths, dtypes)],
        grid=(m // tm,),
        in_specs=[pl.BlockSpec((tm, k), lambda i: (i, 0)),
                  pl.BlockSpec((k, n), lambda i: (0, 0)),
                  pl.BlockSpec((1, n), lambda i: (0, 0))],
        out_specs=[pl.BlockSpec((tm, c), lambda i: (i, 0)) for c in widths],
        compiler_params=_compiler_params(("parallel",)),
        name="project",
    )(x, w.astype(jnp.bfloat16), b.reshape(1, n).astype(jnp.float32))
    return tuple(outs) if f32_tail else outs[0]


def _layer_norm_rows(z, g, b):
    mu = jnp.mean(z, axis=-1, keepdims=True)
    zc = z - mu
    var = jnp.mean(zc * zc, axis=-1, keepdims=True)
    return zc * lax.rsqrt(var + LN_EPS) * g + b


def _outproj_ln_kernel(h_ref, w_ref, x_ref, g_ref, b_ref, o_ref, ob_ref):
    mix = jnp.dot(h_ref[...].astype(jnp.bfloat16), w_ref[...], preferred_element_type=jnp.float32)
    z = DEEPNORM_ALPHA * x_ref[...] + mix
    y = _layer_norm_rows(z, g_ref[...], b_ref[...])
    o_ref[...] = y
    ob_ref[...] = y.astype(ob_ref.dtype)


def _outproj_ln(h, w_out, x, g, b):
    m, d = x.shape
    tm = _row_tile(m)
    row = pl.BlockSpec((tm, d), lambda i: (i, 0))
    vec = pl.BlockSpec((1, d), lambda i: (0, 0))
    return pl.pallas_call(
        _outproj_ln_kernel,
        out_shape=[jax.ShapeDtypeStruct((m, d), jnp.float32), jax.ShapeDtypeStruct((m, d), jnp.bfloat16)],
        grid=(m // tm,),
        in_specs=[row, pl.BlockSpec((d, d), lambda i: (0, 0)), row, vec, vec],
        out_specs=[row, row],
        compiler_params=_compiler_params(("parallel",)),
        name="outproj_ln",
    )(h, w_out.astype(jnp.bfloat16), x, g.reshape(1, d), b.reshape(1, d))


ROUTE_IDX_LANE = 0
ROUTE_RANK_LANE = TOP_K
ROUTE_GATE_LANE = 2 * TOP_K


def _combine_ln_kernel(x_ref, g_ref, b_ref, *rest, tiles_per_group):
    o_ref = rest[-1]
    n_groups = (len(rest) - 1) // 2
    group = pl.program_id(0) // tiles_per_group
    for grp in range(n_groups):
        y_ref, r_ref = rest[2 * grp], rest[2 * grp + 1]

        @pl.when(group == grp)
        def _():
            y = jnp.zeros(x_ref.shape, jnp.float32)
            for k in range(TOP_K):
                gate = r_ref[:, ROUTE_GATE_LANE + k:ROUTE_GATE_LANE + k + 1]
                y = y + gate * y_ref[k].astype(jnp.float32)
            z = DEEPNORM_ALPHA * x_ref[...] + y
            o_ref[...] = _layer_norm_rows(z, g_ref[...], b_ref[...])


def _combine_ln(y_rows_groups, route_groups, x, g, b):
    n_groups = len(y_rows_groups)
    m = y_rows_groups[0].shape[1]
    d = x.shape[1]
    tm = _row_tile(m)
    tiles = m // tm
    row = pl.BlockSpec((tm, d), lambda i: (i, 0))
    vec = pl.BlockSpec((1, d), lambda i: (0, 0))
    in_specs = [row, vec, vec]
    operands = [x, g.reshape(1, d), b.reshape(1, d)]
    for grp in range(n_groups):
        local = lambda i, grp=grp: jnp.clip(i - grp * tiles, 0, tiles - 1)
        in_specs += [pl.BlockSpec((TOP_K, tm, d), lambda i, local=local: (0, local(i), 0)),
                     pl.BlockSpec((tm, LANES), lambda i, local=local: (local(i), 0))]
        operands += [y_rows_groups[grp], route_groups[grp]]
    return pl.pallas_call(
        functools.partial(_combine_ln_kernel, tiles_per_group=tiles),
        out_shape=jax.ShapeDtypeStruct(x.shape, jnp.float32),
        grid=(n_groups * tiles,),
        in_specs=in_specs,
        out_specs=row,
        compiler_params=_compiler_params(("parallel",)),
        name="combine_ln",
    )(*operands)


ROUTER_PAD_BIAS = -1e30


def _router_kernel(x_ref, w_ref, b_ref, o_ref, cnt_ref, base_s):
    f32 = jnp.float32

    @pl.when(pl.program_id(0) == 0)
    def _():
        base_s[...] = jnp.zeros_like(base_s)

    t = x_ref.shape[0]
    x = x_ref[...]
    x_hi = x.astype(jnp.bfloat16)
    x_lo = (x - x_hi.astype(f32)).astype(jnp.bfloat16)
    both = jnp.dot(x_hi, w_ref[...], preferred_element_type=f32)
    logits = (both[:, :LANES] + both[:, LANES:]
              + jnp.dot(x_lo, w_ref[:, :LANES], preferred_element_type=f32)) + b_ref[...]
    lane = lax.broadcasted_iota(jnp.int32, (t, LANES), 1)
    vals = logits
    tops, idxs, sels = [], [], []
    for _ in range(TOP_K):
        top = jnp.max(vals, axis=-1, keepdims=True)
        idx = jnp.min(jnp.where(vals == top, lane, LANES), axis=-1, keepdims=True)
        sel = lane == idx
        vals = jnp.where(sel, -jnp.inf, vals)
        tops.append(top)
        idxs.append(idx)
        sels.append(sel)
    exps = [jnp.exp(top - tops[0]) for top in tops]
    total = exps[0]
    for e in exps[1:]:
        total = total + e
    chosen = jnp.zeros((t, LANES), f32)
    for sel in sels:
        chosen = chosen + sel.astype(f32)
    earlier = (lax.broadcasted_iota(jnp.int32, (t, t), 0) > lax.broadcasted_iota(jnp.int32, (t, t), 1))
    prefix = jnp.dot(earlier.astype(jnp.bfloat16), chosen.astype(jnp.bfloat16),
                     preferred_element_type=f32) + base_s[...]
    packed = jnp.zeros((t, LANES), f32)
    for k in range(TOP_K):
        rank = jnp.sum(jnp.where(sels[k], prefix, 0.0), axis=-1, keepdims=True)
        packed = jnp.where(lane == ROUTE_IDX_LANE + k, idxs[k].astype(f32), packed)
        packed = jnp.where(lane == ROUTE_RANK_LANE + k, rank, packed)
        packed = jnp.where(lane == ROUTE_GATE_LANE + k, exps[k] / total, packed)
    o_ref[...] = packed
    base_s[...] = base_s[...] + jnp.sum(chosen, axis=0, keepdims=True)
    cnt_ref[...] = base_s[...]


def _route(xt, group, m, w_router, b_router):
    d = xt.shape[1]
    tm = _row_tile(m)
    first = group * (m // tm)
    n_e = w_router.shape[1]
    w = jnp.pad(w_router, ((0, 0), (0, LANES - n_e)))
    b = jnp.pad(b_router, (0, LANES - n_e), constant_values=ROUTER_PAD_BIAS)
    w_hi = w.astype(jnp.bfloat16)
    w_lo = (w - w_hi.astype(jnp.float32)).astype(jnp.bfloat16)
    return pl.pallas_call(
        _router_kernel,
        out_shape=[jax.ShapeDtypeStruct((m, LANES), jnp.float32), jax.ShapeDtypeStruct((1, LANES), jnp.float32)],
        grid=(m // tm,),
        in_specs=[pl.BlockSpec((tm, d), lambda i: (first + i, 0)),
                  pl.BlockSpec((d, 2 * LANES), lambda i: (0, 0)),
                  pl.BlockSpec((1, LANES), lambda i: (0, 0))],
        out_specs=[pl.BlockSpec((tm, LANES), lambda i: (i, 0)), pl.BlockSpec((1, LANES), lambda i: (0, 0))],
        scratch_shapes=[pltpu.VMEM((1, LANES), jnp.float32)],
        compiler_params=_compiler_params(("arbitrary",)),
        name="router",
    )(xt, jnp.concatenate([w_hi, w_lo], axis=1), b.reshape(1, LANES))


GLU_BLOCK = 2 * LANES


def _regroup_kernel(w_ref, p_ref, o_ref):
    w = w_ref[0].astype(jnp.bfloat16)
    for c in range(0, w.shape[1], GLU_BLOCK):
        o_ref[0, :, c:c + GLU_BLOCK] = jnp.dot(w[:, c:c + GLU_BLOCK], p_ref[...],
                                               preferred_element_type=jnp.float32).astype(o_ref.dtype)


def _regroup_glu_columns(w_gu_layers, layer):
    _, n_e, d, f2 = w_gu_layers.shape
    src = jnp.arange(GLU_BLOCK)
    perm = (src[:, None] == (2 * (src % LANES) + src // LANES)[None, :]).astype(jnp.bfloat16)
    tk = _row_tile(d)
    return pl.pallas_call(
        _regroup_kernel,
        out_shape=jax.ShapeDtypeStruct((n_e, d, f2), jnp.bfloat16),
        grid=(n_e, d // tk),
        in_specs=[pl.BlockSpec((None, 1, tk, f2), lambda e, i: (layer, e, i, 0)),
                  pl.BlockSpec((GLU_BLOCK, GLU_BLOCK), lambda e, i: (0, 0))],
        out_specs=pl.BlockSpec((1, tk, f2), lambda e, i: (e, i, 0)),
        compiler_params=_compiler_params(("parallel", "parallel")),
        name="regroup_glu",
    )(w_gu_layers, perm)


def _cast_kernel(w_ref, o_ref):
    o_ref[...] = w_ref[...].astype(o_ref.dtype)


def _layer_to_bf16(w_layers, layer):
    _, n_e, r, c = w_layers.shape
    tr = _row_tile(r)
    return pl.pallas_call(
        _cast_kernel,
        out_shape=jax.ShapeDtypeStruct((n_e, r, c), jnp.bfloat16),
        grid=(n_e, r // tr),
        in_specs=[pl.BlockSpec((None, 1, tr, c), lambda e, i: (layer, e, i, 0))],
        out_specs=pl.BlockSpec((1, tr, c), lambda e, i: (e, i, 0)),
        compiler_params=_compiler_params(("parallel", "parallel")),
        name="layer_to_bf16",
    )(w_layers)


def _slot_kernel(r_ref, start_ref, o_ref):
    lane = lax.broadcasted_iota(jnp.int32, r_ref.shape, 1)
    route = r_ref[...]
    slots = jnp.zeros(r_ref.shape, jnp.float32)
    for k in range(TOP_K):
        expert = route[:, ROUTE_IDX_LANE + k:ROUTE_IDX_LANE + k + 1].astype(jnp.int32)
        first = jnp.sum(jnp.where(lane == expert, start_ref[...], 0.0), axis=-1, keepdims=True)
        slots = jnp.where(lane == k, first + route[:, ROUTE_RANK_LANE + k:ROUTE_RANK_LANE + k + 1], slots)
    o_ref[...] = slots.astype(jnp.int32)


def _slots(route, expert_start):
    m = route.shape[0]
    tm = _row_tile(m)
    return pl.pallas_call(
        _slot_kernel,
        out_shape=jax.ShapeDtypeStruct((m, LANES), jnp.int32),
        grid=(m // tm,),
        in_specs=[pl.BlockSpec((tm, LANES), lambda i: (i, 0)), pl.BlockSpec((1, LANES), lambda i: (0, 0))],
        out_specs=pl.BlockSpec((tm, LANES), lambda i: (i, 0)),
        compiler_params=_compiler_params(("parallel",)),
        name="moe_slots",
    )(route, expert_start)


def _regroup_glu_bias(b_gu):
    n_e, f2 = b_gu.shape
    return b_gu.reshape(n_e, f2 // GLU_BLOCK, LANES, 2).transpose(0, 1, 3, 2).reshape(n_e, 1, f2)


def _expert_kernel(be_ref, nb_ref, x_ref, wgu_ref, bgu_ref, wd_ref, bd_ref, o_ref):
    @pl.when(pl.program_id(0) < nb_ref[0])
    def _():
        xb = x_ref[...]
        acts = []
        for c in range(0, wgu_ref.shape[2], GLU_BLOCK):
            h = (jnp.dot(xb, wgu_ref[0, :, c:c + GLU_BLOCK], preferred_element_type=jnp.float32)
                 + bgu_ref[0, :, c:c + GLU_BLOCK])
            hg = jnp.minimum(h[:, :LANES], SWIGLU_LIMIT)
            hl = jnp.clip(h[:, LANES:], -SWIGLU_LIMIT, SWIGLU_LIMIT)
            acts.append((hg * jax.nn.sigmoid(SWIGLU_ALPHA * hg) * (hl + 1.0)).astype(jnp.bfloat16))
        act = jnp.concatenate(acts, axis=1)
        y = jnp.dot(act, wd_ref[0], preferred_element_type=jnp.float32) + bd_ref[0]
        o_ref[...] = y.astype(o_ref.dtype)

    @pl.when(pl.program_id(0) >= nb_ref[0])
    def _():
        o_ref[...] = jnp.zeros_like(o_ref)


def _expert_ffn(x_sorted, block_expert, n_used, w_gu, b_gu, w_dn, b_dn):
    cap, d = x_sorted.shape
    f2 = w_gu.shape[2]
    tm = MOE_ROW_TILE
    n_blocks = cap // tm

    def blk(i, be, nb):
        return (jnp.minimum(i, nb[0] - 1), 0)

    def wsel(i, be, nb):
        return (be[i], 0, 0)

    grid_spec = pltpu.PrefetchScalarGridSpec(
        num_scalar_prefetch=2,
        grid=(n_blocks,),
        in_specs=[pl.BlockSpec((tm, d), blk),
                  pl.BlockSpec((1, d, f2), wsel),
                  pl.BlockSpec((1, 1, f2), wsel),
                  pl.BlockSpec((1, f2 // 2, d), wsel),
                  pl.BlockSpec((1, 1, d), wsel)],
        out_specs=pl.BlockSpec((tm, d), lambda i, be, nb: (i, 0)),
    )
    return pl.pallas_call(
        _expert_kernel,
        out_shape=jax.ShapeDtypeStruct((cap, d), jnp.bfloat16),
        grid_spec=grid_spec,
        compiler_params=_compiler_params(("arbitrary",)),
        name="expert_ffn",
    )(block_expert, n_used, x_sorted, w_gu, b_gu, w_dn, b_dn)


def _moe_ffn(x, x_bf16, group, n, w_router, b_router, w_gu, b_gu, w_dn, b_dn):
    d = x.shape[1]
    tm = MOE_ROW_TILE
    n_e = w_router.shape[1]
    route, counts = _route(x, group, n, w_router, b_router)
    counts = counts[0].astype(jnp.int32)
    padded = (counts + tm - 1) // tm * tm
    pad_end = jnp.cumsum(padded)
    pad_start = pad_end - padded
    cap = n * TOP_K + n_e * tm
    n_blocks = cap // tm
    slot = _slots(route, pad_start.astype(jnp.float32).reshape(1, LANES))[:, :TOP_K]
    slot_kmajor = slot.T.reshape(-1)
    block_first_row = jnp.arange(n_blocks, dtype=jnp.int32) * tm
    block_expert = jnp.minimum(jnp.sum(pad_end[None, :n_e] <= block_first_row[:, None], axis=1), n_e - 1)
    n_used = (pad_end[n_e - 1] // tm).reshape(1)
    top_idx = route[:, ROUTE_IDX_LANE:ROUTE_IDX_LANE + TOP_K].astype(jnp.int32).reshape(-1)
    tok_by_expert = jnp.argsort(top_idx, stable=True).astype(jnp.int32) // TOP_K
    unpadded_start = jnp.cumsum(counts) - counts
    row_in_expert = (block_first_row - pad_start[block_expert])[:, None] + jnp.arange(tm, dtype=jnp.int32)[None, :]
    src = jnp.minimum(unpadded_start[block_expert][:, None] + row_in_expert, n * TOP_K - 1).reshape(-1)
    valid = (row_in_expert < counts[block_expert][:, None]).reshape(-1)
    tok_of_slot = group * n + jnp.where(valid, tok_by_expert.at[src].get(mode="promise_in_bounds"), 0)

    x_sorted = x_bf16.at[tok_of_slot].get(mode="promise_in_bounds")
    yb = _expert_ffn(x_sorted, block_expert.astype(jnp.int32), n_used.astype(jnp.int32), w_gu, b_gu, w_dn, b_dn)
    return yb.at[slot_kmajor].get(mode="promise_in_bounds").reshape(TOP_K, n, d), route


MLSTM_CHUNK = 128
MLSTM_PAIR = 2
MLSTM_NEG = -1e30


def _interleave(generators):
    results = [None] * len(generators)
    live = list(enumerate(generators))
    while live:
        still = []
        for idx, gen in live:
            try:
                next(gen)
                still.append((idx, gen))
            except StopIteration as stop:
                results[idx] = stop.value
        live = still
    return results


def _cumsum_rows(x, reverse):
    n = x.shape[0]
    row = lax.broadcasted_iota(jnp.int32, x.shape, 0)
    sh = 1
    while sh < n:
        if reverse:
            x = x + jnp.where(row < n - sh, pltpu.roll(x, n - sh, 0), 0.0)
        else:
            x = x + jnp.where(row >= sh, pltpu.roll(x, sh, 0), 0.0)
        sh *= 2
    return x


def _cummax_rows(x, reverse):
    n = x.shape[0]
    row = lax.broadcasted_iota(jnp.int32, x.shape, 0)
    sh = 1
    while sh < n:
        if reverse:
            x = jnp.maximum(x, jnp.where(row < n - sh, pltpu.roll(x, n - sh, 0), -jnp.inf))
        else:
            x = jnp.maximum(x, jnp.where(row >= sh, pltpu.roll(x, sh, 0), -jnp.inf))
        sh *= 2
    return x


def _mlstm_kernel(q_ref, k_ref, v_ref, og_ref, g_ref, gain_ref, o_ref, h_s, c_s):
    f32, bf16 = jnp.float32, jnp.bfloat16
    seq = q_ref.shape[0]
    L = MLSTM_CHUNK
    n_chunks = seq // L
    dv = v_ref.shape[1] // MLSTM_PAIR
    dqk = q_ref.shape[1] // MLSTM_PAIR

    lane_q = lax.broadcasted_iota(jnp.int32, (1, q_ref.shape[1]), 1) // dqk
    row = lax.broadcasted_iota(jnp.int32, (L, L), 0)
    col = lax.broadcasted_iota(jnp.int32, (L, L), 1)
    causal = (col <= row, col >= row)
    ones_v = jnp.ones((L, dv), bf16)

    c_s[...] = jnp.zeros_like(c_s)

    def gate_terms(start, d):
        rows = pl.ds(start, L)
        gates = g_ref[rows, :]
        log_f = jnp.minimum(gates, 0.0) - jnp.log(1.0 + jnp.exp(-jnp.abs(gates)))
        cum = _cumsum_rows(log_f, reverse=(d == 1))
        src = gates - pltpu.roll(cum, LANES - MLSTM_PAIR, 1)
        run_max = _cummax_rows(src, reverse=(d == 1))
        return rows, gates, cum, src.T, run_max, q_ref[rows, :], k_ref[rows, :]

    def head_unit(terms, d, j, m_st):
        rows, gates, cum, src_t, run_max, q_all, k_all = terms
        lane_i = 2 * MLSTM_PAIR * d + j
        lane_f = lane_i + MLSTM_PAIR
        b_col, i_col = cum[:, lane_f:lane_f + 1], gates[:, lane_i:lane_i + 1]
        src_row = src_t[lane_i:lane_i + 1, :]
        qz = jnp.where(lane_q == j, q_all, jnp.zeros_like(q_all))
        kz = jnp.where(lane_q == j, k_all, jnp.zeros_like(k_all))
        state = c_s[2 * d + j]
        scores = _nt_dot(qz, kz)
        carried = jnp.dot(qz, state.astype(bf16), preferred_element_type=f32)
        yield
        mu = jnp.maximum(m_st, run_max[:, lane_i:lane_i + 1])
        v_aug = jnp.concatenate([v_ref[rows, j * dv:(j + 1) * dv], ones_v], axis=1)
        qk = scores * (A_DQK ** -0.5) * jnp.exp(jnp.where(causal[d], src_row - mu, MLSTM_NEG))
        edge = L - 1 if d == 0 else 0
        b_last = b_col[edge:edge + 1, :]
        m_new = b_last + mu[edge:edge + 1, :]
        w_s = jnp.exp(b_last - b_col + i_col - m_new)
        local = jnp.dot(qk.astype(bf16), v_aug, preferred_element_type=f32)
        update = _tn_dot((w_s * kz.astype(f32)).astype(bf16), v_aug)
        yield
        nd = local + jnp.exp(m_st - mu) * (A_DQK ** -0.5) * carried
        h_s[d, rows, j * dv:(j + 1) * dv] = nd[:, :dv] / jnp.maximum(jnp.abs(nd[:, dv:]), jnp.exp(-b_col - mu))
        c_s[2 * d + j] = jnp.exp(b_last + m_st - m_new) * state + update
        return m_new

    def body(c, m_all):
        terms = (gate_terms(pl.multiple_of(c * L, L), 0), gate_terms(pl.multiple_of((n_chunks - 1 - c) * L, L), 1))
        m_new = [None] * (2 * MLSTM_PAIR)
        for j in range(MLSTM_PAIR):
            m_new[j], m_new[MLSTM_PAIR + j] = _interleave(
                [head_unit(terms[d], d, j, m_all[MLSTM_PAIR * d + j]) for d in range(2)])
        return tuple(m_new)

    m_init = tuple(jnp.full((1, 1), A_M_INIT, f32) for _ in range(2 * MLSTM_PAIR))
    lax.fori_loop(0, n_chunks, body, m_init)

    def finish(c, carry):
        rows = pl.ds(pl.multiple_of(c * L, L), L)
        for j in range(MLSTM_PAIR):
            cols = slice(j * dv, (j + 1) * dv)
            h = h_s[0, rows, cols] + h_s[1, rows, cols]
            hn = h * lax.rsqrt(jnp.mean(h * h, axis=-1, keepdims=True) + RMS_EPS) * gain_ref[:, cols]
            o_ref[rows, cols] = (hn * jax.nn.sigmoid(og_ref[rows, cols].astype(f32))).astype(o_ref.dtype)
        return carry

    lax.fori_loop(0, n_chunks, finish, 0)


def _mlstm_mixer(x, w_in, b_in, head_gain):
    bsz, s, d = x.shape
    a_qk = A_HEADS * A_DQK
    dv = d // A_HEADS
    n_main = 2 * a_qk + 2 * d
    n_pairs = A_HEADS // MLSTM_PAIR
    pw_qk = MLSTM_PAIR * A_DQK
    pw_v = MLSTM_PAIR * dv
    assert pw_qk == LANES and s % MLSTM_CHUNK == 0
    gate_cols = jnp.asarray([[n_main + t * A_HEADS + MLSTM_PAIR * hp + j for t in range(4) for j in range(MLSTM_PAIR)]
                             for hp in range(n_pairs)])
    n_gate = gate_cols.shape[1]
    w_tail = jnp.pad(w_in[:, gate_cols], ((0, 0), (0, 0), (0, LANES - n_gate))).reshape(d, n_pairs * LANES)
    b_tail = jnp.pad(b_in[gate_cols], ((0, 0), (0, LANES - n_gate))).reshape(n_pairs * LANES)
    w = jnp.concatenate([w_in[:, :n_main], w_tail], axis=1)
    b = jnp.concatenate([b_in[:n_main], b_tail])
    p, gates = _project(x.reshape(bsz * s, d), w, b, jnp.bfloat16, f32_tail=n_pairs * LANES)
    p = p.reshape(bsz, s, n_main)
    gates = gates.reshape(bsz, s, n_pairs * LANES)
    k0 = a_qk // pw_qk
    v0 = 2 * a_qk // pw_v
    o0 = (2 * a_qk + d) // pw_v
    return pl.pallas_call(
        _mlstm_kernel,
        out_shape=jax.ShapeDtypeStruct((bsz, s, d), jnp.bfloat16),
        grid=(bsz, n_pairs),
        in_specs=[pl.BlockSpec((None, s, pw_qk), lambda b, h: (b, 0, h)),
                  pl.BlockSpec((None, s, pw_qk), lambda b, h: (b, 0, k0 + h)),
                  pl.BlockSpec((None, s, pw_v), lambda b, h: (b, 0, v0 + h)),
                  pl.BlockSpec((None, s, pw_v), lambda b, h: (b, 0, o0 + h)),
                  pl.BlockSpec((None, s, LANES), lambda b, h: (b, 0, h)),
                  pl.BlockSpec((1, pw_v), lambda b, h: (0, h))],
        out_specs=pl.BlockSpec((None, s, pw_v), lambda b, h: (b, 0, h)),
        scratch_shapes=[pltpu.VMEM((2, s, pw_v), jnp.float32),
                        pltpu.VMEM((2 * MLSTM_PAIR, pw_qk, 2 * dv), jnp.float32)],
        compiler_params=_compiler_params(("parallel", "parallel")),
        name="mlstm_scan",
    )(p, p, p, p, gates, head_gain.reshape(1, d))


RWKV_CHUNK = 64
RWKV_GROUP = 4
RWKV_LANES = RWKV_GROUP * B_HEAD
RWKV_PREP_ROWS = 256
RWKV_LOCAL_UNROLL = 4


def _f32_dot(a, b_bf16):
    hi = a.astype(jnp.bfloat16)
    lo = (a - hi.astype(jnp.float32)).astype(jnp.bfloat16)
    return (jnp.dot(hi, b_bf16, preferred_element_type=jnp.float32)
            + jnp.dot(lo, b_bf16, preferred_element_type=jnp.float32))


def _nt_dot(a, b):
    return lax.dot_general(a, b, (((1,), (1,)), ((), ())), preferred_element_type=jnp.float32)


def _tn_dot(a, b):
    return lax.dot_general(a, b, (((0,), (0,)), ((), ())), preferred_element_type=jnp.float32)


def _rwkv_scan_kernel(r_ref, k_ref, v_ref, wl_ref, al_ref, gl_ref, mu_r_ref, mu_k_ref, mu_v_ref, mu_wl_ref,
                      mu_al_ref, mu_gl_ref, w0_ref, wup_ref, a0_ref, aup_ref, kk_ref, ka_ref, rk_ref, gup_ref,
                      lng_ref, lnb_ref, o_ref,
                      r_s, v_s, kap_s, g_s, logw_s, kh_s, beta_s, y_s,
                      wmat_s, uloc_s, avq_s, rt_s, aqb_s, kend_s, bend_s, ptot_s):
    f32, bf16 = jnp.float32, jnp.bfloat16
    seq = r_ref.shape[0]
    L, W, RT = RWKV_CHUNK, RWKV_LANES, RWKV_PREP_ROWS
    n_chunks = seq // L
    n_prep = seq // RT

    lane = lax.broadcasted_iota(jnp.int32, (1, W), 1)
    head_masks = [(lane // B_HEAD) == h for h in range(RWKV_GROUP)]
    ones_bd = ((lax.broadcasted_iota(jnp.int32, (W, W), 0) // B_HEAD)
               == (lax.broadcasted_iota(jnp.int32, (W, W), 1) // B_HEAD)).astype(bf16)

    def block_diag(a):
        zero = jnp.zeros_like(a)
        return jnp.concatenate([jnp.where(m, a, zero) for m in head_masks], axis=0)

    def seg_sum(a):
        return _f32_dot(a, ones_bd)

    def shifted(ref, mu_ref, i, rows):
        o = pl.multiple_of(i * RT, RT)
        x = ref[pl.ds(o, RT), :].astype(f32)
        nb = BF16_TILE_ROWS
        before = ref[pl.ds(pl.multiple_of(jnp.maximum(o - nb, 0), nb), nb), :].astype(f32)[nb - 1:nb, :]
        after = ref[pl.ds(pl.multiple_of(jnp.minimum(o + RT, seq - nb), nb), nb), :].astype(f32)[0:1, :]
        before = jnp.where(i > 0, before, 0.0)
        after = jnp.where(i < n_prep - 1, after, 0.0)
        prev = jnp.where(rows == 0, before, pltpu.roll(x, 1, 0))
        nxt = jnp.where(rows == RT - 1, after, pltpu.roll(x, RT - 1, 0))
        return x + mu_ref[...] * (0.5 * (prev + nxt) - x)

    def prep(i, carry):
        o = pl.multiple_of(i * RT, RT)
        rows_w = lax.broadcasted_iota(jnp.int32, (RT, W), 0)
        rows_n = lax.broadcasted_iota(jnp.int32, (RT, wl_ref.shape[1]), 0)
        r = shifted(r_ref, mu_r_ref, i, rows_w)
        k = shifted(k_ref, mu_k_ref, i, rows_w)
        v = shifted(v_ref, mu_v_ref, i, rows_w)
        wl = shifted(wl_ref, mu_wl_ref, i, rows_n)
        al = shifted(al_ref, mu_al_ref, i, rows_n)
        gl = shifted(gl_ref, mu_gl_ref, i, rows_n)
        kk0 = k * kk_ref[...]
        kap = kk0 / jnp.maximum(jnp.sqrt(seg_sum(kk0 * kk0)), 1e-12)
        r_s[pl.ds(o, RT), :] = r
        v_s[pl.ds(o, RT), :] = v.astype(bf16)
        kap_s[pl.ds(o, RT), :] = kap
        g_s[pl.ds(o, RT), :] = jnp.dot(jax.nn.sigmoid(gl).astype(bf16), gup_ref[...],
                                       preferred_element_type=f32).astype(bf16)
        for d in range(2):
            wl_d = jnp.tanh(wl[:, d * B_DECAY_LORA:(d + 1) * B_DECAY_LORA]).astype(bf16)
            al_d = al[:, d * B_AAA_LORA:(d + 1) * B_AAA_LORA].astype(bf16)
            w_raw = w0_ref[d:d + 1, :] + jnp.dot(wl_d, wup_ref[d], preferred_element_type=f32)
            a = jax.nn.sigmoid(a0_ref[d:d + 1, :] + jnp.dot(al_d, aup_ref[d], preferred_element_type=f32))
            logw_s[d, pl.ds(o, RT), :] = -jnp.exp(-0.5) * jax.nn.sigmoid(w_raw)
            kh_s[d, pl.ds(o, RT), :] = (k * (1.0 + (a - 1.0) * ka_ref[...])).astype(bf16)
            beta_s[d, pl.ds(o, RT), :] = (kap * a).astype(bf16)
        return carry

    lax.fori_loop(0, n_prep, prep, 0)

    row = lax.broadcasted_iota(jnp.int32, (L, W), 0)
    col = lax.broadcasted_iota(jnp.int32, (L, W), 1) % B_HEAD
    eye_cat = (row == col).astype(f32)
    strict = (row > col, row < col)
    incl = (row >= col, row <= col)

    def cumsum_rows(x, reverse):
        sh = 1
        while sh < L:
            if reverse:
                x = x + jnp.where(row < L - sh, pltpu.roll(x, L - sh, 0), 0.0)
            else:
                x = x + jnp.where(row >= sh, pltpu.roll(x, sh, 0), 0.0)
            sh *= 2
        return x

    interleave = _interleave

    def local_part(c, d):
        rows = pl.ds(pl.multiple_of(c * L, L), L)
        r, kap = r_s[rows, :], kap_s[rows, :]
        logw, kh, beta = logw_s[d, rows, :], kh_s[d, rows, :].astype(f32), beta_s[d, rows, :].astype(f32)
        cum = cumsum_rows(logw, reverse=(d == 1))
        tot = cum[L - 1:L, :] if d == 0 else cum[0:1, :]
        p_in, p_inv, p_end = jnp.exp(cum), jnp.exp(-cum), jnp.exp(tot - cum)
        kap_t = (kap * jnp.exp(cum - logw)).astype(bf16)
        r_t = (r * p_in).astype(bf16)
        v_bd = block_diag(v_s[rows, :])
        g_all = _nt_dot(jnp.concatenate([kap_t, r_t], axis=0),
                        jnp.concatenate([block_diag((beta * p_inv).astype(bf16)),
                                         block_diag((kh * p_inv).astype(bf16))], axis=0))
        yield
        x_pow = -jnp.where(strict[d], g_all[:L, :W], 0.0)
        a_ak = jnp.where(strict[d], g_all[:L, W:], 0.0)
        a_qb = jnp.where(incl[d], g_all[L:, :W], 0.0)
        a_qk = jnp.where(incl[d], g_all[L:, W:], 0.0)
        t_inv = eye_cat + x_pow
        av = jnp.dot(jnp.concatenate([a_ak, a_qk], axis=0).astype(bf16), v_bd, preferred_element_type=f32)
        x_pow = jnp.dot(x_pow.astype(bf16), block_diag(x_pow.astype(bf16)), preferred_element_type=f32)
        yield
        n_steps = L.bit_length() - 2
        for step in range(n_steps):
            if step < n_steps - 1:
                both = jnp.dot(jnp.concatenate([x_pow, t_inv], axis=0).astype(bf16),
                               block_diag(x_pow.astype(bf16)), preferred_element_type=f32)
                yield
                x_pow, t_inv = both[:L], t_inv + both[L:]
            else:
                last = jnp.dot(t_inv.astype(bf16), block_diag(x_pow.astype(bf16)), preferred_element_type=f32)
                yield
                t_inv = t_inv + last
        t_b = t_inv.astype(bf16)
        w_mat = jnp.dot(t_b, block_diag(kap_t), preferred_element_type=f32)
        u_loc = jnp.dot(t_b, block_diag(av[:L].astype(bf16)), preferred_element_type=f32)
        yield
        wmat_s[d, rows, :] = w_mat.astype(bf16)
        uloc_s[d, rows, :] = u_loc.astype(bf16)
        avq_s[d, rows, :] = av[L:].astype(bf16)
        rt_s[d, rows, :] = r_t
        aqb_s[d, rows, :] = a_qb.astype(bf16)
        kend_s[d, rows, :] = (kh * p_end).astype(bf16)
        bend_s[d, rows, :] = (beta * p_end).astype(bf16)
        ptot_s[d, pl.ds(pl.multiple_of(c * SUBLANES, SUBLANES), SUBLANES), :] = jnp.broadcast_to(
            jnp.exp(tot), (SUBLANES, W))

    def local_body(i, carry):
        interleave([local_part(i * RWKV_LOCAL_UNROLL + j, d) for j in range(RWKV_LOCAL_UNROLL) for d in range(2)])
        return carry

    lax.fori_loop(0, n_chunks // RWKV_LOCAL_UNROLL, local_body, 0)

    def carried_part(c, state, d):
        rows = pl.ds(pl.multiple_of(c * L, L), L)
        ws = _nt_dot(jnp.concatenate([wmat_s[d, rows, :], rt_s[d, rows, :]], axis=0), block_diag(state.astype(bf16)))
        yield
        u = ws[:L] + uloc_s[d, rows, :].astype(f32)
        u_b = u.astype(bf16)
        full = _tn_dot(jnp.concatenate([v_s[rows, :], -u_b], axis=0),
                       jnp.concatenate([kend_s[d, rows, :], bend_s[d, rows, :]], axis=0))
        y_loc = jnp.dot(aqb_s[d, rows, :], block_diag(u_b), preferred_element_type=f32)
        yield
        new_state = state * ptot_s[d, pl.ds(pl.multiple_of(c * SUBLANES, SUBLANES), SUBLANES), :][0:1, :]
        for h, m in enumerate(head_masks):
            new_state = new_state + jnp.where(m, full[h * B_HEAD:(h + 1) * B_HEAD, :], 0.0)
        y_s[d, rows, :] = ws[L:] + avq_s[d, rows, :].astype(f32) - y_loc
        return new_state

    def carried_body(c, states):
        return tuple(interleave([carried_part(c, states[0], 0), carried_part(n_chunks - 1 - c, states[1], 1)]))

    zero_state = jnp.zeros((B_HEAD, W), f32)
    lax.fori_loop(0, n_chunks, carried_body, (zero_state, zero_state))

    def finish(i, carry):
        rows = pl.ds(pl.multiple_of(i * RT, RT), RT)
        y = y_s[0, rows, :] + y_s[1, rows, :]
        mean = seg_sum(y) * (1.0 / B_HEAD)
        yc = y - mean
        var = seg_sum(yc * yc) * (1.0 / B_HEAD)
        yn = yc * lax.rsqrt(var + B_GN_EPS) * lng_ref[...] + lnb_ref[...]
        kh_both = kh_s[0, rows, :].astype(f32) + kh_s[1, rows, :].astype(f32)
        bonus = seg_sum(r_s[rows, :] * kh_both * rk_ref[...]) * v_s[rows, :].astype(f32)
        o_ref[rows, :] = ((yn + bonus) * g_s[rows, :].astype(f32)).astype(o_ref.dtype)
        return carry

    lax.fori_loop(0, n_prep, finish, 0)


def _rwkv7_mixer(x, w_in, mu, w0, w_up, a0, a_up, k_k, k_a, r_k, g_up, lnx_g, lnx_b):
    bsz, s, d = x.shape
    n_proj = w_in.shape[1]
    W = RWKV_LANES
    assert d % W == 0 and s % RWKV_PREP_ROWS == 0 and B_GATE_LORA == LANES
    assert 2 * B_DECAY_LORA == LANES and 2 * B_AAA_LORA == LANES
    assert (s // RWKV_CHUNK) % RWKV_LOCAL_UNROLL == 0
    p = _project(x.reshape(bsz * s, d), w_in, jnp.zeros((n_proj,), jnp.float32), jnp.bfloat16)
    p = p.reshape(bsz, s, n_proj)
    n_groups = d // W
    gw = d // W
    lora0 = 3 * d // LANES

    def seq_w(off):
        return pl.BlockSpec((None, s, W), lambda b, g, off=off: (b, 0, off + g))

    def seq_n(idx):
        return pl.BlockSpec((None, s, LANES), lambda b, g, idx=idx: (b, 0, idx))

    def vec_w(off):
        return pl.BlockSpec((1, W), lambda b, g, off=off: (0, off + g))

    def vec_n(idx):
        return pl.BlockSpec((1, LANES), lambda b, g, idx=idx: (0, idx))

    par_w = pl.BlockSpec((1, W), lambda b, g: (0, g))
    two_w = pl.BlockSpec((2, W), lambda b, g: (0, g))
    up_w = pl.BlockSpec((2, B_DECAY_LORA, W), lambda b, g: (0, 0, g))
    mu2 = mu.reshape(1, n_proj)
    row = lambda a: a.reshape(1, d)
    f32_w = pltpu.VMEM((s, W), jnp.float32)
    b16_w = pltpu.VMEM((s, W), jnp.bfloat16)
    f32_2w = pltpu.VMEM((2, s, W), jnp.float32)
    b16_2w = pltpu.VMEM((2, s, W), jnp.bfloat16)
    decay_rows = pltpu.VMEM((2, SUBLANES * (s // RWKV_CHUNK), W), jnp.float32)
    return pl.pallas_call(
        _rwkv_scan_kernel,
        out_shape=jax.ShapeDtypeStruct((bsz, s, d), jnp.bfloat16),
        grid=(bsz, n_groups),
        in_specs=[seq_w(0), seq_w(gw), seq_w(2 * gw), seq_n(lora0), seq_n(lora0 + 1), seq_n(lora0 + 2),
                  vec_w(0), vec_w(gw), vec_w(2 * gw), vec_n(lora0), vec_n(lora0 + 1), vec_n(lora0 + 2),
                  two_w, up_w, two_w, up_w, par_w, par_w, par_w,
                  pl.BlockSpec((B_GATE_LORA, W), lambda b, g: (0, g)), par_w, par_w],
        out_specs=pl.BlockSpec((None, s, W), lambda b, g: (b, 0, g)),
        scratch_shapes=[f32_w, b16_w, f32_w, b16_w, f32_2w, b16_2w, b16_2w, f32_2w,
                        b16_2w, b16_2w, b16_2w, b16_2w, b16_2w, b16_2w, b16_2w, decay_rows],
        compiler_params=_compiler_params(("parallel", "parallel")),
        name="rwkv7_scan",
    )(p, p, p, p, p, p, mu2, mu2, mu2, mu2, mu2, mu2,
      w0, w_up.astype(jnp.bfloat16), a0, a_up.astype(jnp.bfloat16), row(k_k), row(k_a), row(r_k),
      g_up.astype(jnp.bfloat16), row(lnx_g), row(lnx_b))


def _axial_rope_tables(s):
    rows = s // GRID_W
    row = jnp.repeat(jnp.arange(rows, dtype=jnp.float32), GRID_W)
    col = jnp.tile(jnp.arange(GRID_W, dtype=jnp.float32), rows)
    n_freq = C_HEAD_DIM // 4
    inv_freq = ROPE_THETA ** (-jnp.arange(n_freq, dtype=jnp.float32) / n_freq)
    ang = jnp.concatenate([row[:, None] * inv_freq, col[:, None] * inv_freq], axis=-1)
    return jnp.cos(ang), jnp.sin(ang)


ATTN_Q_TILE = 256


def _rms_rope(x, gain, cos_f, sin_f):
    xn = x * lax.rsqrt(jnp.mean(x * x, axis=-1, keepdims=True) + RMS_EPS) * gain
    return xn * cos_f + pltpu.roll(xn, C_HEAD_DIM // 2, 1) * sin_f


def _attn_kernel(q_ref, k_ref, v_ref, cq_ref, sq_ref, ck_ref, sk_ref, qg_ref, kg_ref, o_ref, kr_s, va_s):
    f32, bf16 = jnp.float32, jnp.bfloat16
    dh = C_HEAD_DIM

    @pl.when(pl.program_id(2) == 0)
    def _():
        kr_s[...] = _rms_rope(k_ref[...].astype(f32), kg_ref[...], ck_ref[...], sk_ref[...]).astype(bf16)
        va_s[:, :dh] = v_ref[...]
        va_s[:, dh:] = jnp.ones((va_s.shape[0], dh), bf16)

    group = q_ref.shape[1] // dh
    for g in range(group):
        cols = slice(g * dh, (g + 1) * dh)
        q = _rms_rope(q_ref[:, cols].astype(f32), qg_ref[...], cq_ref[...], sq_ref[...]) * dh ** -0.5
        scores = _nt_dot(q.astype(bf16), kr_s[...])
        p = jnp.exp((scores - jnp.max(scores, axis=-1, keepdims=True)).astype(bf16))
        od = jnp.dot(p, va_s[...], preferred_element_type=f32)
        o_ref[:, cols] = (od[:, :dh] / od[:, dh:]).astype(o_ref.dtype)


def _axial_gqa_mixer(x, w_in, q_gain, k_gain):
    bsz, s, d = x.shape
    dh = C_HEAD_DIM
    qh = d // dh
    group = qh // C_KV_HEADS
    n_proj = w_in.shape[1]
    half = jnp.concatenate([jnp.arange(0, dh, 2), jnp.arange(1, dh, 2)])
    n_rot = (qh + C_KV_HEADS) * dh
    w_rot = w_in[:, :n_rot].reshape(d, n_rot // dh, dh // 2, 2).transpose(0, 1, 3, 2).reshape(d, n_rot)
    w = jnp.concatenate([w_rot, w_in[:, n_rot:]], axis=1)
    p = _project(x.reshape(bsz * s, d), w, jnp.zeros((n_proj,), jnp.float32), jnp.bfloat16)
    p = p.reshape(bsz, s, n_proj)
    cos, sin = _axial_rope_tables(s)
    cos_f = jnp.concatenate([cos, cos], axis=-1)
    sin_f = jnp.concatenate([-sin, sin], axis=-1)
    tq = min(ATTN_Q_TILE, s)
    gw = group * dh
    q_rows = pl.BlockSpec((tq, dh), lambda b, h, i: (i, 0))
    k_rows = pl.BlockSpec((s, dh), lambda b, h, i: (0, 0))
    vec = pl.BlockSpec((1, dh), lambda b, h, i: (0, 0))
    return pl.pallas_call(
        _attn_kernel,
        out_shape=jax.ShapeDtypeStruct((bsz, s, d), jnp.bfloat16),
        grid=(bsz, C_KV_HEADS, s // tq),
        in_specs=[pl.BlockSpec((None, tq, gw), lambda b, h, i: (b, i, h)),
                  pl.BlockSpec((None, s, dh), lambda b, h, i: (b, 0, qh + h)),
                  pl.BlockSpec((None, s, dh), lambda b, h, i: (b, 0, qh + C_KV_HEADS + h)),
                  q_rows, q_rows, k_rows, k_rows, vec, vec],
        out_specs=pl.BlockSpec((None, tq, gw), lambda b, h, i: (b, i, h)),
        scratch_shapes=[pltpu.VMEM((s, dh), jnp.bfloat16), pltpu.VMEM((s, 2 * dh), jnp.bfloat16)],
        compiler_params=_compiler_params(("parallel", "parallel", "arbitrary")),
        name="axial_attention",
    )(p, p, p, cos_f, sin_f, cos_f, sin_f, q_gain[half].reshape(1, dh), k_gain[half].reshape(1, dh))


def kernel(x, a_w_in, a_b_in, a_head_gain, a_w_out, b_w_in, b_mu, b_w0, b_w_up, b_a0, b_a_up, b_k_k, b_k_a, b_r_k, b_g_up, b_lnx_g, b_lnx_b, b_w_out, c_w_in, c_q_gain, c_k_gain, c_w_out, ln1_g, ln1_b, moe_w_router, moe_b_router, moe_w_gu, moe_b_gu, moe_w_dn, moe_b_dn, ln2_g, ln2_b):
    bsz, s, d = x.shape
    depth = ln1_g.shape[0]
    xt = x.reshape(bsz * s, d)
    for i in range(depth):
        kind = i % N_MIXERS
        j = i // N_MIXERS
        x3 = xt.reshape(bsz, s, d)
        if kind == 0:
            h = _mlstm_mixer(x3, a_w_in[j], a_b_in[j], a_head_gain[j])
            w_out = a_w_out[j]
        elif kind == 1:
            h = _rwkv7_mixer(x3, b_w_in[j], b_mu[j], b_w0[j], b_w_up[j], b_a0[j], b_a_up[j],
                             b_k_k[j], b_k_a[j], b_r_k[j], b_g_up[j], b_lnx_g[j], b_lnx_b[j])
            w_out = b_w_out[j]
        else:
            h = _axial_gqa_mixer(x3, c_w_in[j], c_q_gain[j], c_k_gain[j])
            w_out = c_w_out[j]
        xt, xt_bf16 = _outproj_ln(h.reshape(bsz * s, d), w_out, xt, ln1_g[i], ln1_b[i])
        w_gu = _regroup_glu_columns(moe_w_gu, i)
        b_gu = _regroup_glu_bias(moe_b_gu[i])
        w_dn = _layer_to_bf16(moe_w_dn, i)
        b_dn = moe_b_dn[i][:, None, :]
        n_group = bsz * s // MOE_TOKEN_GROUPS
        groups = [_moe_ffn(xt, xt_bf16, grp, n_group, moe_w_router[i], moe_b_router[i], w_gu, b_gu, w_dn, b_dn)
                  for grp in range(MOE_TOKEN_GROUPS)]
        xt = _combine_ln([y for y, _ in groups], [r for _, r in groups], xt, ln2_g[i], ln2_b[i])
    return xt.reshape(bsz, s, d)
```

```python
import functools

import jax
import jax.numpy as jnp
from jax import lax
from jax.experimental import pallas as pl
from jax.experimental.pallas import tpu as pltpu

DEPTH = 4
N_MIXERS = 3
GRID_W = 64
DEEPNORM_ALPHA = (2 * DEPTH) ** 0.25
LN_EPS = 1e-5
RMS_EPS = 1e-6

A_HEADS = 8
A_DQK = 64
A_M_INIT = -1e30

B_HEAD = 64
B_DECAY_LORA = 64
B_AAA_LORA = 64
B_GATE_LORA = 128
B_GN_EPS = 64e-5

C_HEAD_DIM = 128
C_KV_HEADS = 2
ROPE_THETA = 10000.0

TOP_K = 4
SWIGLU_LIMIT = 7.0
SWIGLU_ALPHA = 1.702

LANES = 128
SUBLANES = 8
BF16_TILE_ROWS = 16
VMEM_LIMIT_BYTES = 56 * 1024 * 1024
MAX_COL_CHUNK = 5 * LANES

ROW_TILE = 512
MOE_ROW_TILE = 512
MOE_TOKEN_GROUPS = 2


def _compiler_params(semantics):
    return pltpu.CompilerParams(dimension_semantics=semantics, vmem_limit_bytes=VMEM_LIMIT_BYTES)


def _row_tile(m):
    t = min(ROW_TILE, m)
    assert m % t == 0, (m, t)
    return t


def _col_chunk(n):
    return max(c for c in range(LANES, MAX_COL_CHUNK + 1, LANES) if n % c == 0)


def _proj_kernel(x_ref, w_ref, b_ref, *o_refs):
    xb = x_ref[...].astype(jnp.bfloat16)
    col = 0
    for o_ref in o_refs:
        n = o_ref.shape[1]
        step = _col_chunk(n)
        for j in range(0, n, step):
            acc = jnp.dot(xb, w_ref[:, col + j:col + j + step], preferred_element_type=jnp.float32)
            o_ref[:, j:j + step] = (acc + b_ref[:, col + j:col + j + step]).astype(o_ref.dtype)
        col += n


def _project(x, w, b, out_dtype, f32_tail=0):
    m, k = x.shape
    n = w.shape[1]
    widths = [n - f32_tail, f32_tail] if f32_tail else [n]
    dtypes = [out_dtype, jnp.float32]
    assert all(c % LANES == 0 for c in widths)
    tm = _row_tile(m)
    outs = pl.pallas_call(
        _proj_kernel,
        out_shape=[jax.ShapeDtypeStruct((m, c), dt) for c, dt in zip(widths, dtypes)],
        grid=(m // tm,),
        in_specs=[pl.BlockSpec((tm, k), lambda i: (i, 0)),
                  pl.BlockSpec((k, n), lambda i: (0, 0)),
                  pl.BlockSpec((1, n), lambda i: (0, 0))],
        out_specs=[pl.BlockSpec((tm, c), lambda i: (i, 0)) for c in widths],
        compiler_params=_compiler_params(("parallel",)),
        name="project",
    )(x, w.astype(jnp.bfloat16), b.reshape(1, n).astype(jnp.float32))
    return tuple(outs) if f32_tail else outs[0]


def _layer_norm_rows(z, g, b):
    mu = jnp.mean(z, axis=-1, keepdims=True)
    zc = z - mu
    var = jnp.mean(zc * zc, axis=-1, keepdims=True)
    return zc * lax.rsqrt(var + LN_EPS) * g + b


def _outproj_ln_kernel(h_ref, w_ref, x_ref, g_ref, b_ref, o_ref, ob_ref):
    mix = jnp.dot(h_ref[...].astype(jnp.bfloat16), w_ref[...], preferred_element_type=jnp.float32)
    z = DEEPNORM_ALPHA * x_ref[...] + mix
    y = _layer_norm_rows(z, g_ref[...], b_ref[...])
    o_ref[...] = y
    ob_ref[...] = y.astype(ob_ref.dtype)


def _outproj_ln(h, w_out, x, g, b):
    m, d = x.shape
    tm = _row_tile(m)
    row = pl.BlockSpec((tm, d), lambda i: (i, 0))
    vec = pl.BlockSpec((1, d), lambda i: (0, 0))
    return pl.pallas_call(
        _outproj_ln_kernel,
        out_shape=[jax.ShapeDtypeStruct((m, d), jnp.float32), jax.ShapeDtypeStruct((m, d), jnp.bfloat16)],
        grid=(m // tm,),
        in_specs=[row, pl.BlockSpec((d, d), lambda i: (0, 0)), row, vec, vec],
        out_specs=[row, row],
        compiler_params=_compiler_params(("parallel",)),
        name="outproj_ln",
    )(h, w_out.astype(jnp.bfloat16), x, g.reshape(1, d), b.reshape(1, d))


ROUTE_IDX_LANE = 0
ROUTE_RANK_LANE = TOP_K
ROUTE_GATE_LANE = 2 * TOP_K


def _combine_ln_kernel(x_ref, g_ref, b_ref, *rest, tiles_per_group):
    o_ref = rest[-1]
    n_groups = (len(rest) - 1) // 2
    group = pl.program_id(0) // tiles_per_group
    for grp in range(n_groups):
        y_ref, r_ref = rest[2 * grp], rest[2 * grp + 1]

        @pl.when(group == grp)
        def _():
            y = jnp.zeros(x_ref.shape, jnp.float32)
            for k in range(TOP_K):
                gate = r_ref[:, ROUTE_GATE_LANE + k:ROUTE_GATE_LANE + k + 1]
                y = y + gate * y_ref[k].astype(jnp.float32)
            z = DEEPNORM_ALPHA * x_ref[...] + y
            o_ref[...] = _layer_norm_rows(z, g_ref[...], b_ref[...])


def _combine_ln(y_rows_groups, route_groups, x, g, b):
    n_groups = len(y_rows_groups)
    m = y_rows_groups[0].shape[1]
    d = x.shape[1]
    tm = _row_tile(m)
    tiles = m // tm
    row = pl.BlockSpec((tm, d), lambda i: (i, 0))
    vec = pl.BlockSpec((1, d), lambda i: (0, 0))
    in_specs = [row, vec, vec]
    operands = [x, g.reshape(1, d), b.reshape(1, d)]
    for grp in range(n_groups):
        local = lambda i, grp=grp: jnp.clip(i - grp * tiles, 0, tiles - 1)
        in_specs += [pl.BlockSpec((TOP_K, tm, d), lambda i, local=local: (0, local(i), 0)),
                     pl.BlockSpec((tm, LANES), lambda i, local=local: (local(i), 0))]
        operands += [y_rows_groups[grp], route_groups[grp]]
    return pl.pallas_call(
        functools.partial(_combine_ln_kernel, tiles_per_group=tiles),
        out_shape=jax.ShapeDtypeStruct(x.shape, jnp.float32),
        grid=(n_groups * tiles,),
        in_specs=in_specs,
        out_specs=row,
        compiler_params=_compiler_params(("parallel",)),
        name="combine_ln",
    )(*operands)


ROUTER_PAD_BIAS = -1e30


def _router_kernel(x_ref, w_ref, b_ref, o_ref, cnt_ref, base_s):
    f32 = jnp.float32

    @pl.when(pl.program_id(0) == 0)
    def _():
        base_s[...] = jnp.zeros_like(base_s)

    t = x_ref.shape[0]
    x = x_ref[...]
    x_hi = x.astype(jnp.bfloat16)
    x_lo = (x - x_hi.astype(f32)).astype(jnp.bfloat16)
    both = jnp.dot(x_hi, w_ref[...], preferred_element_type=f32)
    logits = (both[:, :LANES] + both[:, LANES:]
              + jnp.dot(x_lo, w_ref[:, :LANES], preferred_element_type=f32)) + b_ref[...]
    lane = lax.broadcasted_iota(jnp.int32, (t, LANES), 1)
    vals = logits
    tops, idxs, sels = [], [], []
    for _ in range(TOP_K):
        top = jnp.max(vals, axis=-1, keepdims=True)
        idx = jnp.min(jnp.where(vals == top, lane, LANES), axis=-1, keepdims=True)
        sel = lane == idx
        vals = jnp.where(sel, -jnp.inf, vals)
        tops.append(top)
        idxs.append(idx)
        sels.append(sel)
    exps = [jnp.exp(top - tops[0]) for top in tops]
    total = exps[0]
    for e in exps[1:]:
        total = total + e
    chosen = jnp.zeros((t, LANES), f32)
    for sel in sels:
        chosen = chosen + sel.astype(f32)
    earlier = (lax.broadcasted_iota(jnp.int32, (t, t), 0) > lax.broadcasted_iota(jnp.int32, (t, t), 1))
    prefix = jnp.dot(earlier.astype(jnp.bfloat16), chosen.astype(jnp.bfloat16),
                     preferred_element_type=f32) + base_s[...]
    packed = jnp.zeros((t, LANES), f32)
    for k in range(TOP_K):
        rank = jnp.sum(jnp.where(sels[k], prefix, 0.0), axis=-1, keepdims=True)
        packed = jnp.where(lane == ROUTE_IDX_LANE + k, idxs[k].astype(f32), packed)
        packed = jnp.where(lane == ROUTE_RANK_LANE + k, rank, packed)
        packed = jnp.where(lane == ROUTE_GATE_LANE + k, exps[k] / total, packed)
    o_ref[...] = packed
    base_s[...] = base_s[...] + jnp.sum(chosen, axis=0, keepdims=True)
    cnt_ref[...] = base_s[...]


def _route(xt, group, m, w_router, b_router):
    d = xt.shape[1]
    tm = _row_tile(m)
    first = group * (m // tm)
    n_e = w_router.shape[1]
    w = jnp.pad(w_router, ((0, 0), (0, LANES - n_e)))
    b = jnp.pad(b_router, (0, LANES - n_e), constant_values=ROUTER_PAD_BIAS)
    w_hi = w.astype(jnp.bfloat16)
    w_lo = (w - w_hi.astype(jnp.float32)).astype(jnp.bfloat16)
    return pl.pallas_call(
        _router_kernel,
        out_shape=[jax.ShapeDtypeStruct((m, LANES), jnp.float32), jax.ShapeDtypeStruct((1, LANES), jnp.float32)],
        grid=(m // tm,),
        in_specs=[pl.BlockSpec((tm, d), lambda i: (first + i, 0)),
                  pl.BlockSpec((d, 2 * LANES), lambda i: (0, 0)),
                  pl.BlockSpec((1, LANES), lambda i: (0, 0))],
        out_specs=[pl.BlockSpec((tm, LANES), lambda i: (i, 0)), pl.BlockSpec((1, LANES), lambda i: (0, 0))],
        scratch_shapes=[pltpu.VMEM((1, LANES), jnp.float32)],
        compiler_params=_compiler_params(("arbitrary",)),
        name="router",
    )(xt, jnp.concatenate([w_hi, w_lo], axis=1), b.reshape(1, LANES))


GLU_BLOCK = 2 * LANES


def _regroup_kernel(w_ref, p_ref, o_ref):
    w = w_ref[0].astype(jnp.bfloat16)
    for c in range(0, w.shape[1], GLU_BLOCK):
        o_ref[0, :, c:c + GLU_BLOCK] = jnp.dot(w[:, c:c + GLU_BLOCK], p_ref[...],
                                               preferred_element_type=jnp.float32).astype(o_ref.dtype)


def _regroup_glu_columns(w_gu_layers, layer):
    _, n_e, d, f2 = w_gu_layers.shape
    src = jnp.arange(GLU_BLOCK)
    perm = (src[:, None] == (2 * (src % LANES) + src // LANES)[None, :]).astype(jnp.bfloat16)
    tk = _row_tile(d)
    return pl.pallas_call(
        _regroup_kernel,
        out_shape=jax.ShapeDtypeStruct((n_e, d, f2), jnp.bfloat16),
        grid=(n_e, d // tk),
        in_specs=[pl.BlockSpec((None, 1, tk, f2), lambda e, i: (layer, e, i, 0)),
                  pl.BlockSpec((GLU_BLOCK, GLU_BLOCK), lambda e, i: (0, 0))],
        out_specs=pl.BlockSpec((1, tk, f2), lambda e, i: (e, i, 0)),
        compiler_params=_compiler_params(("parallel", "parallel")),
        name="regroup_glu",
    )(w_gu_layers, perm)


def _cast_kernel(w_ref, o_ref):
    o_ref[...] = w_ref[...].astype(o_ref.dtype)


def _layer_to_bf16(w_layers, layer):
    _, n_e, r, c = w_layers.shape
    tr = _row_tile(r)
    return pl.pallas_call(
        _cast_kernel,
        out_shape=jax.ShapeDtypeStruct((n_e, r, c), jnp.bfloat16),
        grid=(n_e, r // tr),
        in_specs=[pl.BlockSpec((None, 1, tr, c), lambda e, i: (layer, e, i, 0))],
        out_specs=pl.BlockSpec((1, tr, c), lambda e, i: (e, i, 0)),
        compiler_params=_compiler_params(("parallel", "parallel")),
        name="layer_to_bf16",
    )(w_layers)


def _slot_kernel(r_ref, start_ref, o_ref):
    lane = lax.broadcasted_iota(jnp.int32, r_ref.shape, 1)
    route = r_ref[...]
    slots = jnp.zeros(r_ref.shape, jnp.float32)
    for k in range(TOP_K):
        expert = route[:, ROUTE_IDX_LANE + k:ROUTE_IDX_LANE + k + 1].astype(jnp.int32)
        first = jnp.sum(jnp.where(lane == expert, start_ref[...], 0.0), axis=-1, keepdims=True)
        slots = jnp.where(lane == k, first + route[:, ROUTE_RANK_LANE + k:ROUTE_RANK_LANE + k + 1], slots)
    o_ref[...] = slots.astype(jnp.int32)


def _slots(route, expert_start):
    m = route.shape[0]
    tm = _row_tile(m)
    return pl.pallas_call(
        _slot_kernel,
        out_shape=jax.ShapeDtypeStruct((m, LANES), jnp.int32),
        grid=(m // tm,),
        in_specs=[pl.BlockSpec((tm, LANES), lambda i: (i, 0)), pl.BlockSpec((1, LANES), lambda i: (0, 0))],
        out_specs=pl.BlockSpec((tm, LANES), lambda i: (i, 0)),
        compiler_params=_compiler_params(("parallel",)),
        name="moe_slots",
    )(route, expert_start)


def _regroup_glu_bias(b_gu):
    n_e, f2 = b_gu.shape
    return b_gu.reshape(n_e, f2 // GLU_BLOCK, LANES, 2).transpose(0, 1, 3, 2).reshape(n_e, 1, f2)


def _expert_kernel(be_ref, nb_ref, x_ref, wgu_ref, bgu_ref, wd_ref, bd_ref, o_ref):
    @pl.when(pl.program_id(0) < nb_ref[0])
    def _():
        xb = x_ref[...]
        acts = []
        for c in range(0, wgu_ref.shape[2], GLU_BLOCK):
            h = (jnp.dot(xb, wgu_ref[0, :, c:c + GLU_BLOCK], preferred_element_type=jnp.float32)
                 + bgu_ref[0, :, c:c + GLU_BLOCK])
            hg = jnp.minimum(h[:, :LANES], SWIGLU_LIMIT)
            hl = jnp.clip(h[:, LANES:], -SWIGLU_LIMIT, SWIGLU_LIMIT)
            acts.append((hg * jax.nn.sigmoid(SWIGLU_ALPHA * hg) * (hl + 1.0)).astype(jnp.bfloat16))
        act = jnp.concatenate(acts, axis=1)
        y = jnp.dot(act, wd_ref[0], preferred_element_type=jnp.float32) + bd_ref[0]
        o_ref[...] = y.astype(o_ref.dtype)

    @pl.when(pl.program_id(0) >= nb_ref[0])
    def _():
        o_ref[...] = jnp.zeros_like(o_ref)


def _expert_ffn(x_sorted, block_expert, n_used, w_gu, b_gu, w_dn, b_dn):
    cap, d = x_sorted.shape
    f2 = w_gu.shape[2]
    tm = MOE_ROW_TILE
    n_blocks = cap // tm

    def blk(i, be, nb):
        return (jnp.minimum(i, nb[0] - 1), 0)

    def wsel(i, be, nb):
        return (be[i], 0, 0)

    grid_spec = pltpu.PrefetchScalarGridSpec(
        num_scalar_prefetch=2,
        grid=(n_blocks,),
        in_specs=[pl.BlockSpec((tm, d), blk),
                  pl.BlockSpec((1, d, f2), wsel),
                  pl.BlockSpec((1, 1, f2), wsel),
                  pl.BlockSpec((1, f2 // 2, d), wsel),
                  pl.BlockSpec((1, 1, d), wsel)],
        out_specs=pl.BlockSpec((tm, d), lambda i, be, nb: (i, 0)),
    )
    return pl.pallas_call(
        _expert_kernel,
        out_shape=jax.ShapeDtypeStruct((cap, d), jnp.bfloat16),
        grid_spec=grid_spec,
        compiler_params=_compiler_params(("arbitrary",)),
        name="expert_ffn",
    )(block_expert, n_used, x_sorted, w_gu, b_gu, w_dn, b_dn)


def _moe_ffn(x, x_bf16, group, n, w_router, b_router, w_gu, b_gu, w_dn, b_dn):
    d = x.shape[1]
    tm = MOE_ROW_TILE
    n_e = w_router.shape[1]
    route, counts = _route(x, group, n, w_router, b_router)
    counts = counts[0].astype(jnp.int32)
    padded = (counts + tm - 1) // tm * tm
    pad_end = jnp.cumsum(padded)
    pad_start = pad_end - padded
    cap = n * TOP_K + n_e * tm
    n_blocks = cap // tm
    slot = _slots(route, pad_start.astype(jnp.float32).reshape(1, LANES))[:, :TOP_K]
    slot_kmajor = slot.T.reshape(-1)
    block_first_row = jnp.arange(n_blocks, dtype=jnp.int32) * tm
    block_expert = jnp.minimum(jnp.sum(pad_end[None, :n_e] <= block_first_row[:, None], axis=1), n_e - 1)
    n_used = (pad_end[n_e - 1] // tm).reshape(1)
    top_idx = route[:, ROUTE_IDX_LANE:ROUTE_IDX_LANE + TOP_K].astype(jnp.int32).reshape(-1)
    tok_by_expert = jnp.argsort(top_idx, stable=True).astype(jnp.int32) // TOP_K
    unpadded_start = jnp.cumsum(counts) - counts
    row_in_expert = (block_first_row - pad_start[block_expert])[:, None] + jnp.arange(tm, dtype=jnp.int32)[None, :]
    src = jnp.minimum(unpadded_start[block_expert][:, None] + row_in_expert, n * TOP_K - 1).reshape(-1)
    valid = (row_in_expert < counts[block_expert][:, None]).reshape(-1)
    tok_of_slot = group * n + jnp.where(valid, tok_by_expert.at[src].get(mode="promise_in_bounds"), 0)

    x_sorted = x_bf16.at[tok_of_slot].get(mode="promise_in_bounds")
    yb = _expert_ffn(x_sorted, block_expert.astype(jnp.int32), n_used.astype(jnp.int32), w_gu, b_gu, w_dn, b_dn)
    return yb.at[slot_kmajor].get(mode="promise_in_bounds").reshape(TOP_K, n, d), route


MLSTM_CHUNK = 128
MLSTM_PAIR = 2
MLSTM_NEG = -1e30


def _interleave(generators):
    results = [None] * len(generators)
    live = list(enumerate(generators))
    while live:
        still = []
        for idx, gen in live:
            try:
                next(gen)
                still.append((idx, gen))
            except StopIteration as stop:
                results[idx] = stop.value
        live = still
    return results


def _cumsum_rows(x, reverse):
    n = x.shape[0]
    row = lax.broadcasted_iota(jnp.int32, x.shape, 0)
    sh = 1
    while sh < n:
        if reverse:
            x = x + jnp.where(row < n - sh, pltpu.roll(x, n - sh, 0), 0.0)
        else:
            x = x + jnp.where(row >= sh, pltpu.roll(x, sh, 0), 0.0)
        sh *= 2
    return x


def _cummax_rows(x, reverse):
    n = x.shape[0]
    row = lax.broadcasted_iota(jnp.int32, x.shape, 0)
    sh = 1
    while sh < n:
        if reverse:
            x = jnp.maximum(x, jnp.where(row < n - sh, pltpu.roll(x, n - sh, 0), -jnp.inf))
        else:
            x = jnp.maximum(x, jnp.where(row >= sh, pltpu.roll(x, sh, 0), -jnp.inf))
        sh *= 2
    return x


def _mlstm_kernel(q_ref, k_ref, v_ref, og_ref, g_ref, gain_ref, o_ref, h_s, c_s):
    f32, bf16 = jnp.float32, jnp.bfloat16
    seq = q_ref.shape[0]
    L = MLSTM_CHUNK
    n_chunks = seq // L
    dv = v_ref.shape[1] // MLSTM_PAIR
    dqk = q_ref.shape[1] // MLSTM_PAIR

    lane_q = lax.broadcasted_iota(jnp.int32, (1, q_ref.shape[1]), 1) // dqk
    row = lax.broadcasted_iota(jnp.int32, (L, L), 0)
    col = lax.broadcasted_iota(jnp.int32, (L, L), 1)
    causal = (col <= row, col >= row)
    ones_v = jnp.ones((L, dv), bf16)

    c_s[...] = jnp.zeros_like(c_s)

    def gate_terms(start, d):
        rows = pl.ds(start, L)
        gates = g_ref[rows, :]
        log_f = jnp.minimum(gates, 0.0) - jnp.log(1.0 + jnp.exp(-jnp.abs(gates)))
        cum = _cumsum_rows(log_f, reverse=(d == 1))
        src = gates - pltpu.roll(cum, LANES - MLSTM_PAIR, 1)
        run_max = _cummax_rows(src, reverse=(d == 1))
        return rows, gates, cum, src.T, run_max, q_ref[rows, :], k_ref[rows, :]

    def head_unit(terms, d, j, m_st):
        rows, gates, cum, src_t, run_max, q_all, k_all = terms
        lane_i = 2 * MLSTM_PAIR * d + j
        lane_f = lane_i + MLSTM_PAIR
        b_col, i_col = cum[:, lane_f:lane_f + 1], gates[:, lane_i:lane_i + 1]
        src_row = src_t[lane_i:lane_i + 1, :]
        qz = jnp.where(lane_q == j, q_all, jnp.zeros_like(q_all))
        kz = jnp.where(lane_q == j, k_all, jnp.zeros_like(k_all))
        state = c_s[2 * d + j]
        scores = _nt_dot(qz, kz)
        carried = jnp.dot(qz, state.astype(bf16), preferred_element_type=f32)
        yield
        mu = jnp.maximum(m_st, run_max[:, lane_i:lane_i + 1])
        v_aug = jnp.concatenate([v_ref[rows, j * dv:(j + 1) * dv], ones_v], axis=1)
        qk = scores * (A_DQK ** -0.5) * jnp.exp(jnp.where(causal[d], src_row - mu, MLSTM_NEG))
        edge = L - 1 if d == 0 else 0
        b_last = b_col[edge:edge + 1, :]
        m_new = b_last + mu[edge:edge + 1, :]
        w_s = jnp.exp(b_last - b_col + i_col - m_new)
        local = jnp.dot(qk.astype(bf16), v_aug, preferred_element_type=f32)
        update = _tn_dot((w_s * kz.astype(f32)).astype(bf16), v_aug)
        yield
        nd = local + jnp.exp(m_st - mu) * (A_DQK ** -0.5) * carried
        h_s[d, rows, j * dv:(j + 1) * dv] = nd[:, :dv] / jnp.maximum(jnp.abs(nd[:, dv:]), jnp.exp(-b_col - mu))
        c_s[2 * d + j] = jnp.exp(b_last + m_st - m_new) * state + update
        return m_new

    def body(c, m_all):
        terms = (gate_terms(pl.multiple_of(c * L, L), 0), gate_terms(pl.multiple_of((n_chunks - 1 - c) * L, L), 1))
        m_new = [None] * (2 * MLSTM_PAIR)
        for j in range(MLSTM_PAIR):
            m_new[j], m_new[MLSTM_PAIR + j] = _interleave(
                [head_unit(terms[d], d, j, m_all[MLSTM_PAIR * d + j]) for d in range(2)])
        return tuple(m_new)

    m_init = tuple(jnp.full((1, 1), A_M_INIT, f32) for _ in range(2 * MLSTM_PAIR))
    lax.fori_loop(0, n_chunks, body, m_init)

    def finish(c, carry):
        rows = pl.ds(pl.multiple_of(c * L, L), L)
        for j in range(MLSTM_PAIR):
            cols = slice(j * dv, (j + 1) * dv)
            h = h_s[0, rows, cols] + h_s[1, rows, cols]
            hn = h * lax.rsqrt(jnp.mean(h * h, axis=-1, keepdims=True) + RMS_EPS) * gain_ref[:, cols]
            o_ref[rows, cols] = (hn * jax.nn.sigmoid(og_ref[rows, cols].astype(f32))).astype(o_ref.dtype)
        return carry

    lax.fori_loop(0, n_chunks, finish, 0)


def _mlstm_mixer(x, w_in, b_in, head_gain):
    bsz, s, d = x.shape
    a_qk = A_HEADS * A_DQK
    dv = d // A_HEADS
    n_main = 2 * a_qk + 2 * d
    n_pairs = A_HEADS // MLSTM_PAIR
    pw_qk = MLSTM_PAIR * A_DQK
    pw_v = MLSTM_PAIR * dv
    assert pw_qk == LANES and s % MLSTM_CHUNK == 0
    gate_cols = jnp.asarray([[n_main + t * A_HEADS + MLSTM_PAIR * hp + j for t in range(4) for j in range(MLSTM_PAIR)]
                             for hp in range(n_pairs)])
    n_gate = gate_cols.shape[1]
    w_tail = jnp.pad(w_in[:, gate_cols], ((0, 0), (0, 0), (0, LANES - n_gate))).reshape(d, n_pairs * LANES)
    b_tail = jnp.pad(b_in[gate_cols], ((0, 0), (0, LANES - n_gate))).reshape(n_pairs * LANES)
    w = jnp.concatenate([w_in[:, :n_main], w_tail], axis=1)
    b = jnp.concatenate([b_in[:n_main], b_tail])
    p, gates = _project(x.reshape(bsz * s, d), w, b, jnp.bfloat16, f32_tail=n_pairs * LANES)
    p = p.reshape(bsz, s, n_main)
    gates = gates.reshape(bsz, s, n_pairs * LANES)
    k0 = a_qk // pw_qk
    v0 = 2 * a_qk // pw_v
    o0 = (2 * a_qk + d) // pw_v
    return pl.pallas_call(
        _mlstm_kernel,
        out_shape=jax.ShapeDtypeStruct((bsz, s, d), jnp.bfloat16),
        grid=(bsz, n_pairs),
        in_specs=[pl.BlockSpec((None, s, pw_qk), lambda b, h: (b, 0, h)),
                  pl.BlockSpec((None, s, pw_qk), lambda b, h: (b, 0, k0 + h)),
                  pl.BlockSpec((None, s, pw_v), lambda b, h: (b, 0, v0 + h)),
                  pl.BlockSpec((None, s, pw_v), lambda b, h: (b, 0, o0 + h)),
                  pl.BlockSpec((None, s, LANES), lambda b, h: (b, 0, h)),
                  pl.BlockSpec((1, pw_v), lambda b, h: (0, h))],
        out_specs=pl.BlockSpec((None, s, pw_v), lambda b, h: (b, 0, h)),
        scratch_shapes=[pltpu.VMEM((2, s, pw_v), jnp.float32),
                        pltpu.VMEM((2 * MLSTM_PAIR, pw_qk, 2 * dv), jnp.float32)],
        compiler_params=_compiler_params(("parallel", "parallel")),
        name="mlstm_scan",
    )(p, p, p, p, gates, head_gain.reshape(1, d))


RWKV_CHUNK = 64
RWKV_GROUP = 4
RWKV_LANES = RWKV_GROUP * B_HEAD
RWKV_PREP_ROWS = 256
RWKV_LOCAL_UNROLL = 4


def _f32_dot(a, b_bf16):
    hi = a.astype(jnp.bfloat16)
    lo = (a - hi.astype(jnp.float32)).astype(jnp.bfloat16)
    return (jnp.dot(hi, b_bf16, preferred_element_type=jnp.float32)
            + jnp.dot(lo, b_bf16, preferred_element_type=jnp.float32))


def _nt_dot(a, b):
    return lax.dot_general(a, b, (((1,), (1,)), ((), ())), preferred_element_type=jnp.float32)


def _tn_dot(a, b):
    return lax.dot_general(a, b, (((0,), (0,)), ((), ())), preferred_element_type=jnp.float32)


def _rwkv_scan_kernel(r_ref, k_ref, v_ref, wl_ref, al_ref, gl_ref, mu_r_ref, mu_k_ref, mu_v_ref, mu_wl_ref,
                      mu_al_ref, mu_gl_ref, w0_ref, wup_ref, a0_ref, aup_ref, kk_ref, ka_ref, rk_ref, gup_ref,
                      lng_ref, lnb_ref, o_ref,
                      r_s, v_s, kap_s, g_s, logw_s, kh_s, beta_s, y_s,
                      wmat_s, uloc_s, avq_s, rt_s, aqb_s, kend_s, bend_s, ptot_s):
    f32, bf16 = jnp.float32, jnp.bfloat16
    seq = r_ref.shape[0]
    L, W, RT = RWKV_CHUNK, RWKV_LANES, RWKV_PREP_ROWS
    n_chunks = seq // L
    n_prep = seq // RT

    lane = lax.broadcasted_iota(jnp.int32, (1, W), 1)
    head_masks = [(lane // B_HEAD) == h for h in range(RWKV_GROUP)]
    ones_bd = ((lax.broadcasted_iota(jnp.int32, (W, W), 0) // B_HEAD)
               == (lax.broadcasted_iota(jnp.int32, (W, W), 1) // B_HEAD)).astype(bf16)

    def block_diag(a):
        zero = jnp.zeros_like(a)
        return jnp.concatenate([jnp.where(m, a, zero) for m in head_masks], axis=0)

    def seg_sum(a):
        return _f32_dot(a, ones_bd)

    def shifted(ref, mu_ref, i, rows):
        o = pl.multiple_of(i * RT, RT)
        x = ref[pl.ds(o, RT), :].astype(f32)
        nb = BF16_TILE_ROWS
        before = ref[pl.ds(pl.multiple_of(jnp.maximum(o - nb, 0), nb), nb), :].astype(f32)[nb - 1:nb, :]
        after = ref[pl.ds(pl.multiple_of(jnp.minimum(o + RT, seq - nb), nb), nb), :].astype(f32)[0:1, :]
        before = jnp.where(i > 0, before, 0.0)
        after = jnp.where(i < n_prep - 1, after, 0.0)
        prev = jnp.where(rows == 0, before, pltpu.roll(x, 1, 0))
        nxt = jnp.where(rows == RT - 1, after, pltpu.roll(x, RT - 1, 0))
        return x + mu_ref[...] * (0.5 * (prev + nxt) - x)

    def prep(i, carry):
        o = pl.multiple_of(i * RT, RT)
        rows_w = lax.broadcasted_iota(jnp.int32, (RT, W), 0)
        rows_n = lax.broadcasted_iota(jnp.int32, (RT, wl_ref.shape[1]), 0)
        r = shifted(r_ref, mu_r_ref, i, rows_w)
        k = shifted(k_ref, mu_k_ref, i, rows_w)
        v = shifted(v_ref, mu_v_ref, i, rows_w)
        wl = shifted(wl_ref, mu_wl_ref, i, rows_n)
        al = shifted(al_ref, mu_al_ref, i, rows_n)
        gl = shifted(gl_ref, mu_gl_ref, i, rows_n)
        kk0 = k * kk_ref[...]
        kap = kk0 / jnp.maximum(jnp.sqrt(seg_sum(kk0 * kk0)), 1e-12)
        r_s[pl.ds(o, RT), :] = r
        v_s[pl.ds(o, RT), :] = v.astype(bf16)
        kap_s[pl.ds(o, RT), :] = kap
        g_s[pl.ds(o, RT), :] = jnp.dot(jax.nn.sigmoid(gl).astype(bf16), gup_ref[...],
                                       preferred_element_type=f32).astype(bf16)
        for d in range(2):
            wl_d = jnp.tanh(wl[:, d * B_DECAY_LORA:(d + 1) * B_DECAY_LORA]).astype(bf16)
            al_d = al[:, d * B_AAA_LORA:(d + 1) * B_AAA_LORA].astype(bf16)
            w_raw = w0_ref[d:d + 1, :] + jnp.dot(wl_d, wup_ref[d], preferred_element_type=f32)
            a = jax.nn.sigmoid(a0_ref[d:d + 1, :] + jnp.dot(al_d, aup_ref[d], preferred_element_type=f32))
            logw_s[d, pl.ds(o, RT), :] = -jnp.exp(-0.5) * jax.nn.sigmoid(w_raw)
            kh_s[d, pl.ds(o, RT), :] = (k * (1.0 + (a - 1.0) * ka_ref[...])).astype(bf16)
            beta_s[d, pl.ds(o, RT), :] = (kap * a).astype(bf16)
        return carry

    lax.fori_loop(0, n_prep, prep, 0)

    row = lax.broadcasted_iota(jnp.int32, (L, W), 0)
    col = lax.broadcasted_iota(jnp.int32, (L, W), 1) % B_HEAD
    eye_cat = (row == col).astype(f32)
    strict = (row > col, row < col)
    incl = (row >= col, row <= col)

    def cumsum_rows(x, reverse):
        sh = 1
        while sh < L:
            if reverse:
                x = x + jnp.where(row < L - sh, pltpu.roll(x, L - sh, 0), 0.0)
            else:
                x = x + jnp.where(row >= sh, pltpu.roll(x, sh, 0), 0.0)
            sh *= 2
        return x

    interleave = _interleave

    def local_part(c, d):
        rows = pl.ds(pl.multiple_of(c * L, L), L)
        r, kap = r_s[rows, :], kap_s[rows, :]
        logw, kh, beta = logw_s[d, rows, :], kh_s[d, rows, :].astype(f32), beta_s[d, rows, :].astype(f32)
        cum = cumsum_rows(logw, reverse=(d == 1))
        tot = cum[L - 1:L, :] if d == 0 else cum[0:1, :]
        p_in, p_inv, p_end = jnp.exp(cum), jnp.exp(-cum), jnp.exp(tot - cum)
        kap_t = (kap * jnp.exp(cum - logw)).astype(bf16)
        r_t = (r * p_in).astype(bf16)
        v_bd = block_diag(v_s[rows, :])
        g_all = _nt_dot(jnp.concatenate([kap_t, r_t], axis=0),
                        jnp.concatenate([block_diag((beta * p_inv).astype(bf16)),
                                         block_diag((kh * p_inv).astype(bf16))], axis=0))
        yield
        x_pow = -jnp.where(strict[d], g_all[:L, :W], 0.0)
        a_ak = jnp.where(strict[d], g_all[:L, W:], 0.0)
        a_qb = jnp.where(incl[d], g_all[L:, :W], 0.0)
        a_qk = jnp.where(incl[d], g_all[L:, W:], 0.0)
        t_inv = eye_cat + x_pow
        av = jnp.dot(jnp.concatenate([a_ak, a_qk], axis=0).astype(bf16), v_bd, preferred_element_type=f32)
        x_pow = jnp.dot(x_pow.astype(bf16), block_diag(x_pow.astype(bf16)), preferred_element_type=f32)
        yield
        n_steps = L.bit_length() - 2
        for step in range(n_steps):
            if step < n_steps - 1:
                both = jnp.dot(jnp.concatenate([x_pow, t_inv], axis=0).astype(bf16),
                               block_diag(x_pow.astype(bf16)), preferred_element_type=f32)
                yield
                x_pow, t_inv = both[:L], t_inv + both[L:]
            else:
                last = jnp.dot(t_inv.astype(bf16), block_diag(x_pow.astype(bf16)), preferred_element_type=f32)
                yield
                t_inv = t_inv + last
        t_b = t_inv.astype(bf16)
        w_mat = jnp.dot(t_b, block_diag(kap_t), preferred_element_type=f32)
        u_loc = jnp.dot(t_b, block_diag(av[:L].astype(bf16)), preferred_element_type=f32)
        yield
        wmat_s[d, rows, :] = w_mat.astype(bf16)
        uloc_s[d, rows, :] = u_loc.astype(bf16)
        avq_s[d, rows, :] = av[L:].astype(bf16)
        rt_s[d, rows, :] = r_t
        aqb_s[d, rows, :] = a_qb.astype(bf16)
        kend_s[d, rows, :] = (kh * p_end).astype(bf16)
        bend_s[d, rows, :] = (beta * p_end).astype(bf16)
        ptot_s[d, pl.ds(pl.multiple_of(c * SUBLANES, SUBLANES), SUBLANES), :] = jnp.broadcast_to(
            jnp.exp(tot), (SUBLANES, W))

    def local_body(i, carry):
        interleave([local_part(i * RWKV_LOCAL_UNROLL + j, d) for j in range(RWKV_LOCAL_UNROLL) for d in range(2)])
        return carry

    lax.fori_loop(0, n_chunks // RWKV_LOCAL_UNROLL, local_body, 0)

    def carried_part(c, state, d):
        rows = pl.ds(pl.multiple_of(c * L, L), L)
        ws = _nt_dot(jnp.concatenate([wmat_s[d, rows, :], rt_s[d, rows, :]], axis=0), block_diag(state.astype(bf16)))
        yield
        u = ws[:L] + uloc_s[d, rows, :].astype(f32)
        u_b = u.astype(bf16)
        full = _tn_dot(jnp.concatenate([v_s[rows, :], -u_b], axis=0),
                       jnp.concatenate([kend_s[d, rows, :], bend_s[d, rows, :]], axis=0))
        y_loc = jnp.dot(aqb_s[d, rows, :], block_diag(u_b), preferred_element_type=f32)
        yield
        new_state = state * ptot_s[d, pl.ds(pl.multiple_of(c * SUBLANES, SUBLANES), SUBLANES), :][0:1, :]
        for h, m in enumerate(head_masks):
            new_state = new_state + jnp.where(m, full[h * B_HEAD:(h + 1) * B_HEAD, :], 0.0)
        y_s[d, rows, :] = ws[L:] + avq_s[d, rows, :].astype(f32) - y_loc
        return new_state

    def carried_body(c, states):
        return tuple(interleave([carried_part(c, states[0], 0), carried_part(n_chunks - 1 - c, states[1], 1)]))

    zero_state = jnp.zeros((B_HEAD, W), f32)
    lax.fori_loop(0, n_chunks, carried_body, (zero_state, zero_state))

    def finish(i, carry):
        rows = pl.ds(pl.multiple_of(i * RT, RT), RT)
        y = y_s[0, rows, :] + y_s[1, rows, :]
        mean = seg_sum(y) * (1.0 / B_HEAD)
        yc = y - mean
        var = seg_sum(yc * yc) * (1.0 / B_HEAD)
        yn = yc * lax.rsqrt(var + B_GN_EPS) * lng_ref[...] + lnb_ref[...]
        kh_both = kh_s[0, rows, :].astype(f32) + kh_s[1, rows, :].astype(f32)
        bonus = seg_sum(r_s[rows, :] * kh_both * rk_ref[...]) * v_s[rows, :].astype(f32)
        o_ref[rows, :] = ((yn + bonus) * g_s[rows, :].astype(f32)).astype(o_ref.dtype)
        return carry

    lax.fori_loop(0, n_prep, finish, 0)


def _rwkv7_mixer(x, w_in, mu, w0, w_up, a0, a_up, k_k, k_a, r_k, g_up, lnx_g, lnx_b):
    bsz, s, d = x.shape
    n_proj = w_in.shape[1]
    W = RWKV_LANES
    assert d % W == 0 and s % RWKV_PREP_ROWS == 0 and B_GATE_LORA == LANES
    assert 2 * B_DECAY_LORA == LANES and 2 * B_AAA_LORA == LANES
    assert (s // RWKV_CHUNK) % RWKV_LOCAL_UNROLL == 0
    p = _project(x.reshape(bsz * s, d), w_in, jnp.zeros((n_proj,), jnp.float32), jnp.bfloat16)
    p = p.reshape(bsz, s, n_proj)
    n_groups = d // W
    gw = d // W
    lora0 = 3 * d // LANES

    def seq_w(off):
        return pl.BlockSpec((None, s, W), lambda b, g, off=off: (b, 0, off + g))

    def seq_n(idx):
        return pl.BlockSpec((None, s, LANES), lambda b, g, idx=idx: (b, 0, idx))

    def vec_w(off):
        return pl.BlockSpec((1, W), lambda b, g, off=off: (0, off + g))

    def vec_n(idx):
        return pl.BlockSpec((1, LANES), lambda b, g, idx=idx: (0, idx))

    par_w = pl.BlockSpec((1, W), lambda b, g: (0, g))
    two_w = pl.BlockSpec((2, W), lambda b, g: (0, g))
    up_w = pl.BlockSpec((2, B_DECAY_LORA, W), lambda b, g: (0, 0, g))
    mu2 = mu.reshape(1, n_proj)
    row = lambda a: a.reshape(1, d)
    f32_w = pltpu.VMEM((s, W), jnp.float32)
    b16_w = pltpu.VMEM((s, W), jnp.bfloat16)
    f32_2w = pltpu.VMEM((2, s, W), jnp.float32)
    b16_2w = pltpu.VMEM((2, s, W), jnp.bfloat16)
    decay_rows = pltpu.VMEM((2, SUBLANES * (s // RWKV_CHUNK), W), jnp.float32)
    return pl.pallas_call(
        _rwkv_scan_kernel,
        out_shape=jax.ShapeDtypeStruct((bsz, s, d), jnp.bfloat16),
        grid=(bsz, n_groups),
        in_specs=[seq_w(0), seq_w(gw), seq_w(2 * gw), seq_n(lora0), seq_n(lora0 + 1), seq_n(lora0 + 2),
                  vec_w(0), vec_w(gw), vec_w(2 * gw), vec_n(lora0), vec_n(lora0 + 1), vec_n(lora0 + 2),
                  two_w, up_w, two_w, up_w, par_w, par_w, par_w,
                  pl.BlockSpec((B_GATE_LORA, W), lambda b, g: (0, g)), par_w, par_w],
        out_specs=pl.BlockSpec((None, s, W), lambda b, g: (b, 0, g)),
        scratch_shapes=[f32_w, b16_w, f32_w, b16_w, f32_2w, b16_2w, b16_2w, f32_2w,
                        b16_2w, b16_2w, b16_2w, b16_2w, b16_2w, b16_2w, b16_2w, decay_rows],
        compiler_params=_compiler_params(("parallel", "parallel")),
        name="rwkv7_scan",
    )(p, p, p, p, p, p, mu2, mu2, mu2, mu2, mu2, mu2,
      w0, w_up.astype(jnp.bfloat16), a0, a_up.astype(jnp.bfloat16), row(k_k), row(k_a), row(r_k),
      g_up.astype(jnp.bfloat16), row(lnx_g), row(lnx_b))


def _axial_rope_tables(s):
    rows = s // GRID_W
    row = jnp.repeat(jnp.arange(rows, dtype=jnp.float32), GRID_W)
    col = jnp.tile(jnp.arange(GRID_W, dtype=jnp.float32), rows)
    n_freq = C_HEAD_DIM // 4
    inv_freq = ROPE_THETA ** (-jnp.arange(n_freq, dtype=jnp.float32) / n_freq)
    ang = jnp.concatenate([row[:, None] * inv_freq, col[:, None] * inv_freq], axis=-1)
    return jnp.cos(ang), jnp.sin(ang)


ATTN_Q_TILE = 256


def _rms_rope(x, gain, cos_f, sin_f):
    xn = x * lax.rsqrt(jnp.mean(x * x, axis=-1, keepdims=True) + RMS_EPS) * gain
    return xn * cos_f + pltpu.roll(xn, C_HEAD_DIM // 2, 1) * sin_f


def _attn_kernel(q_ref, k_ref, v_ref, cq_ref, sq_ref, ck_ref, sk_ref, qg_ref, kg_ref, o_ref, kr_s, va_s):
    f32, bf16 = jnp.float32, jnp.bfloat16
    dh = C_HEAD_DIM

    @pl.when(pl.program_id(2) == 0)
    def _():
        kr_s[...] = _rms_rope(k_ref[...].astype(f32), kg_ref[...], ck_ref[...], sk_ref[...]).astype(bf16)
        va_s[:, :dh] = v_ref[...]
        va_s[:, dh:] = jnp.ones((va_s.shape[0], dh), bf16)

    group = q_ref.shape[1] // dh
    for g in range(group):
        cols = slice(g * dh, (g + 1) * dh)
        q = _rms_rope(q_ref[:, cols].astype(f32), qg_ref[...], cq_ref[...], sq_ref[...]) * dh ** -0.5
        scores = _nt_dot(q.astype(bf16), kr_s[...])
        p = jnp.exp((scores - jnp.max(scores, axis=-1, keepdims=True)).astype(bf16))
        od = jnp.dot(p, va_s[...], preferred_element_type=f32)
        o_ref[:, cols] = (od[:, :dh] / od[:, dh:]).astype(o_ref.dtype)


def _axial_gqa_mixer(x, w_in, q_gain, k_gain):
    bsz, s, d = x.shape
    dh = C_HEAD_DIM
    qh = d // dh
    group = qh // C_KV_HEADS
    n_proj = w_in.shape[1]
    half = jnp.concatenate([jnp.arange(0, dh, 2), jnp.arange(1, dh, 2)])
    n_rot = (qh + C_KV_HEADS) * dh
    w_rot = w_in[:, :n_rot].reshape(d, n_rot // dh, dh // 2, 2).transpose(0, 1, 3, 2).reshape(d, n_rot)
    w = jnp.concatenate([w_rot, w_in[:, n_rot:]], axis=1)
    p = _project(x.reshape(bsz * s, d), w, jnp.zeros((n_proj,), jnp.float32), jnp.bfloat16)
    p = p.reshape(bsz, s, n_proj)
    cos, sin = _axial_rope_tables(s)
    cos_f = jnp.concatenate([cos, cos], axis=-1)
    sin_f = jnp.concatenate([-sin, sin], axis=-1)
    tq = min(ATTN_Q_TILE, s)
    gw = group * dh
    q_rows = pl.BlockSpec((tq, dh), lambda b, h, i: (i, 0))
    k_rows = pl.BlockSpec((s, dh), lambda b, h, i: (0, 0))
    vec = pl.BlockSpec((1, dh), lambda b, h, i: (0, 0))
    return pl.pallas_call(
        _attn_kernel,
        out_shape=jax.ShapeDtypeStruct((bsz, s, d), jnp.bfloat16),
        grid=(bsz, C_KV_HEADS, s // tq),
        in_specs=[pl.BlockSpec((None, tq, gw), lambda b, h, i: (b, i, h)),
                  pl.BlockSpec((None, s, dh), lambda b, h, i: (b, 0, qh + h)),
                  pl.BlockSpec((None, s, dh), lambda b, h, i: (b, 0, qh + C_KV_HEADS + h)),
                  q_rows, q_rows, k_rows, k_rows, vec, vec],
        out_specs=pl.BlockSpec((None, tq, gw), lambda b, h, i: (b, i, h)),
        scratch_shapes=[pltpu.VMEM((s, dh), jnp.bfloat16), pltpu.VMEM((s, 2 * dh), jnp.bfloat16)],
        compiler_params=_compiler_params(("parallel", "parallel", "arbitrary")),
        name="axial_attention",
    )(p, p, p, cos_f, sin_f, cos_f, sin_f, q_gain[half].reshape(1, dh), k_gain[half].reshape(1, dh))


def kernel(x, a_w_in, a_b_in, a_head_gain, a_w_out, b_w_in, b_mu, b_w0, b_w_up, b_a0, b_a_up, b_k_k, b_k_a, b_r_k, b_g_up, b_lnx_g, b_lnx_b, b_w_out, c_w_in, c_q_gain, c_k_gain, c_w_out, ln1_g, ln1_b, moe_w_router, moe_b_router, moe_w_gu, moe_b_gu, moe_w_dn, moe_b_dn, ln2_g, ln2_b):
    bsz, s, d = x.shape
    depth = ln1_g.shape[0]
    xt = x.reshape(bsz * s, d)

    def prepare_experts(layer):
        return (_regroup_glu_columns(moe_w_gu, layer), _regroup_glu_bias(moe_b_gu[layer]),
                _layer_to_bf16(moe_w_dn, layer), moe_b_dn[layer][:, None, :])

    expert_weights = prepare_experts(0)
    for i in range(depth):
        kind = i % N_MIXERS
        j = i // N_MIXERS
        x3 = xt.reshape(bsz, s, d)
        if kind == 0:
            h = _mlstm_mixer(x3, a_w_in[j], a_b_in[j], a_head_gain[j])
            w_out = a_w_out[j]
        elif kind == 1:
            h = _rwkv7_mixer(x3, b_w_in[j], b_mu[j], b_w0[j], b_w_up[j], b_a0[j], b_a_up[j],
                             b_k_k[j], b_k_a[j], b_r_k[j], b_g_up[j], b_lnx_g[j], b_lnx_b[j])
            w_out = b_w_out[j]
        else:
            h = _axial_gqa_mixer(x3, c_w_in[j], c_q_gain[j], c_k_gain[j])
            w_out = c_w_out[j]
        xt, xt_bf16 = _outproj_ln(h.reshape(bsz * s, d), w_out, xt, ln1_g[i], ln1_b[i])
        w_gu, b_gu, w_dn, b_dn = expert_weights
        n_group = bsz * s // MOE_TOKEN_GROUPS
        groups = [_moe_ffn(xt, xt_bf16, grp, n_group, moe_w_router[i], moe_b_router[i], w_gu, b_gu, w_dn, b_dn)
                  for grp in range(MOE_TOKEN_GROUPS)]
        if i + 1 < depth:
            expert_weights = prepare_experts(i + 1)
        xt = _combine_ln([y for y, _ in groups], [r for _, r in groups], xt, ln2_g[i], ln2_b[i])
    return xt.reshape(bsz, s, d)
```

```python
import functools

import jax
import jax.numpy as jnp
from jax import lax
from jax.experimental import pallas as pl
from jax.experimental.pallas import tpu as pltpu

DEPTH = 4
N_MIXERS = 3
GRID_W = 64
DEEPNORM_ALPHA = (2 * DEPTH) ** 0.25
LN_EPS = 1e-5
RMS_EPS = 1e-6

A_HEADS = 8
A_DQK = 64
A_M_INIT = -1e30

B_HEAD = 64
B_DECAY_LORA = 64
B_AAA_LORA = 64
B_GATE_LORA = 128
B_GN_EPS = 64e-5

C_HEAD_DIM = 128
C_KV_HEADS = 2
ROPE_THETA = 10000.0

TOP_K = 4
SWIGLU_LIMIT = 7.0
SWIGLU_ALPHA = 1.702

LANES = 128
SUBLANES = 8
BF16_TILE_ROWS = 16
VMEM_LIMIT_BYTES = 56 * 1024 * 1024
MAX_COL_CHUNK = 5 * LANES

ROW_TILE = 512
MOE_ROW_TILE = 512
MOE_TOKEN_GROUPS = 2


def _compiler_params(semantics):
    return pltpu.CompilerParams(dimension_semantics=semantics, vmem_limit_bytes=VMEM_LIMIT_BYTES)


def _row_tile(m):
    t = min(ROW_TILE, m)
    assert m % t == 0, (m, t)
    return t


def _col_chunk(n):
    return max(c for c in range(LANES, MAX_COL_CHUNK + 1, LANES) if n % c == 0)


def _proj_kernel(x_ref, w_ref, b_ref, *o_refs):
    xb = x_ref[...].astype(jnp.bfloat16)
    col = 0
    for o_ref in o_refs:
        n = o_ref.shape[1]
        step = _col_chunk(n)
        for j in range(0, n, step):
            acc = jnp.dot(xb, w_ref[:, col + j:col + j + step], preferred_element_type=jnp.float32)
            o_ref[:, j:j + step] = (acc + b_ref[:, col + j:col + j + step]).astype(o_ref.dtype)
        col += n


def _project(x, w, b, out_dtype, f32_tail=0):
    m, k = x.shape
    n = w.shape[1]
    widths = [n - f32_tail, f32_tail] if f32_tail else [n]
    dtypes = [out_dtype, jnp.float32]
    assert all(c % LANES == 0 for c in widths)
    tm = _row_tile(m)
    outs = pl.pallas_call(
        _proj_kernel,
        out_shape=[jax.ShapeDtypeStruct((m, c), dt) for c, dt in zip(widths, dtypes)],
        grid=(m // tm,),
        in_specs=[pl.BlockSpec((tm, k), lambda i: (i, 0)),
                  pl.BlockSpec((k, n), lambda i: (0, 0)),
                  pl.BlockSpec((1, n), lambda i: (0, 0))],
        out_specs=[pl.BlockSpec((tm, c), lambda i: (i, 0)) for c in widths],
        compiler_params=_compiler_params(("parallel",)),
        name="project",
    )(x, w.astype(jnp.bfloat16), b.reshape(1, n).astype(jnp.float32))
    return tuple(outs) if f32_tail else outs[0]


def _layer_norm_rows(z, g, b):
    mu = jnp.mean(z, axis=-1, keepdims=True)
    zc = z - mu
    var = jnp.mean(zc * zc, axis=-1, keepdims=True)
    return zc * lax.rsqrt(var + LN_EPS) * g + b


def _outproj_ln_kernel(h_ref, w_ref, x_ref, g_ref, b_ref, o_ref, ob_ref):
    mix = jnp.dot(h_ref[...].astype(jnp.bfloat16), w_ref[...], preferred_element_type=jnp.float32)
    z = DEEPNORM_ALPHA * x_ref[...] + mix
    y = _layer_norm_rows(z, g_ref[...], b_ref[...])
    o_ref[...] = y
    ob_ref[...] = y.astype(ob_ref.dtype)


def _outproj_ln(h, w_out, x, g, b):
    m, d = x.shape
    tm = _row_tile(m)
    row = pl.BlockSpec((tm, d), lambda i: (i, 0))
    vec = pl.BlockSpec((1, d), lambda i: (0, 0))
    return pl.pallas_call(
        _outproj_ln_kernel,
        out_shape=[jax.ShapeDtypeStruct((m, d), jnp.float32), jax.ShapeDtypeStruct((m, d), jnp.bfloat16)],
        grid=(m // tm,),
        in_specs=[row, pl.BlockSpec((d, d), lambda i: (0, 0)), row, vec, vec],
        out_specs=[row, row],
        compiler_params=_compiler_params(("parallel",)),
        name="outproj_ln",
    )(h, w_out.astype(jnp.bfloat16), x, g.reshape(1, d), b.reshape(1, d))


ROUTE_IDX_LANE = 0
ROUTE_RANK_LANE = TOP_K
ROUTE_GATE_LANE = 2 * TOP_K


def _combine_ln_kernel(x_ref, g_ref, b_ref, *rest, tiles_per_group):
    o_ref = rest[-1]
    n_groups = (len(rest) - 1) // 2
    group = pl.program_id(0) // tiles_per_group
    for grp in range(n_groups):
        y_ref, r_ref = rest[2 * grp], rest[2 * grp + 1]

        @pl.when(group == grp)
        def _():
            y = jnp.zeros(x_ref.shape, jnp.float32)
            for k in range(TOP_K):
                gate = r_ref[:, ROUTE_GATE_LANE + k:ROUTE_GATE_LANE + k + 1]
                y = y + gate * y_ref[k].astype(jnp.float32)
            z = DEEPNORM_ALPHA * x_ref[...] + y
            o_ref[...] = _layer_norm_rows(z, g_ref[...], b_ref[...])


def _combine_ln(y_rows_groups, route_groups, x, g, b):
    n_groups = len(y_rows_groups)
    m = y_rows_groups[0].shape[1]
    d = x.shape[1]
    tm = _row_tile(m)
    tiles = m // tm
    row = pl.BlockSpec((tm, d), lambda i: (i, 0))
    vec = pl.BlockSpec((1, d), lambda i: (0, 0))
    in_specs = [row, vec, vec]
    operands = [x, g.reshape(1, d), b.reshape(1, d)]
    for grp in range(n_groups):
        local = lambda i, grp=grp: jnp.clip(i - grp * tiles, 0, tiles - 1)
        in_specs += [pl.BlockSpec((TOP_K, tm, d), lambda i, local=local: (0, local(i), 0)),
                     pl.BlockSpec((tm, LANES), lambda i, local=local: (local(i), 0))]
        operands += [y_rows_groups[grp], route_groups[grp]]
    return pl.pallas_call(
        functools.partial(_combine_ln_kernel, tiles_per_group=tiles),
        out_shape=jax.ShapeDtypeStruct(x.shape, jnp.float32),
        grid=(n_groups * tiles,),
        in_specs=in_specs,
        out_specs=row,
        compiler_params=_compiler_params(("parallel",)),
        name="combine_ln",
    )(*operands)


ROUTER_PAD_BIAS = -1e30


def _router_kernel(x_ref, w_ref, b_ref, o_ref, cnt_ref, base_s):
    f32 = jnp.float32

    @pl.when(pl.program_id(0) == 0)
    def _():
        base_s[...] = jnp.zeros_like(base_s)

    t = x_ref.shape[0]
    x = x_ref[...]
    x_hi = x.astype(jnp.bfloat16)
    x_lo = (x - x_hi.astype(f32)).astype(jnp.bfloat16)
    both = jnp.dot(x_hi, w_ref[...], preferred_element_type=f32)
    logits = (both[:, :LANES] + both[:, LANES:]
              + jnp.dot(x_lo, w_ref[:, :LANES], preferred_element_type=f32)) + b_ref[...]
    lane = lax.broadcasted_iota(jnp.int32, (t, LANES), 1)
    vals = logits
    tops, idxs, sels = [], [], []
    for _ in range(TOP_K):
        top = jnp.max(vals, axis=-1, keepdims=True)
        idx = jnp.min(jnp.where(vals == top, lane, LANES), axis=-1, keepdims=True)
        sel = lane == idx
        vals = jnp.where(sel, -jnp.inf, vals)
        tops.append(top)
        idxs.append(idx)
        sels.append(sel)
    exps = [jnp.exp(top - tops[0]) for top in tops]
    total = exps[0]
    for e in exps[1:]:
        total = total + e
    chosen = jnp.zeros((t, LANES), f32)
    for sel in sels:
        chosen = chosen + sel.astype(f32)
    earlier = (lax.broadcasted_iota(jnp.int32, (t, t), 0) > lax.broadcasted_iota(jnp.int32, (t, t), 1))
    prefix = jnp.dot(earlier.astype(jnp.bfloat16), chosen.astype(jnp.bfloat16),
                     preferred_element_type=f32) + base_s[...]
    packed = jnp.zeros((t, LANES), f32)
    for k in range(TOP_K):
        rank = jnp.sum(jnp.where(sels[k], prefix, 0.0), axis=-1, keepdims=True)
        packed = jnp.where(lane == ROUTE_IDX_LANE + k, idxs[k].astype(f32), packed)
        packed = jnp.where(lane == ROUTE_RANK_LANE + k, rank, packed)
        packed = jnp.where(lane == ROUTE_GATE_LANE + k, exps[k] / total, packed)
    o_ref[...] = packed
    base_s[...] = base_s[...] + jnp.sum(chosen, axis=0, keepdims=True)
    cnt_ref[...] = base_s[...]


def _route(xt, group, m, w_router, b_router):
    d = xt.shape[1]
    tm = _row_tile(m)
    first = group * (m // tm)
    n_e = w_router.shape[1]
    w = jnp.pad(w_router, ((0, 0), (0, LANES - n_e)))
    b = jnp.pad(b_router, (0, LANES - n_e), constant_values=ROUTER_PAD_BIAS)
    w_hi = w.astype(jnp.bfloat16)
    w_lo = (w - w_hi.astype(jnp.float32)).astype(jnp.bfloat16)
    return pl.pallas_call(
        _router_kernel,
        out_shape=[jax.ShapeDtypeStruct((m, LANES), jnp.float32), jax.ShapeDtypeStruct((1, LANES), jnp.float32)],
        grid=(m // tm,),
        in_specs=[pl.BlockSpec((tm, d), lambda i: (first + i, 0)),
                  pl.BlockSpec((d, 2 * LANES), lambda i: (0, 0)),
                  pl.BlockSpec((1, LANES), lambda i: (0, 0))],
        out_specs=[pl.BlockSpec((tm, LANES), lambda i: (i, 0)), pl.BlockSpec((1, LANES), lambda i: (0, 0))],
        scratch_shapes=[pltpu.VMEM((1, LANES), jnp.float32)],
        compiler_params=_compiler_params(("arbitrary",)),
        name="router",
    )(xt, jnp.concatenate([w_hi, w_lo], axis=1), b.reshape(1, LANES))


GLU_BLOCK = 2 * LANES


def _regroup_kernel(w_ref, p_ref, o_ref):
    w = w_ref[0].astype(jnp.bfloat16)
    for c in range(0, w.shape[1], GLU_BLOCK):
        o_ref[0, :, c:c + GLU_BLOCK] = jnp.dot(w[:, c:c + GLU_BLOCK], p_ref[...],
                                               preferred_element_type=jnp.float32).astype(o_ref.dtype)


def _regroup_glu_columns(w_gu_layers, layer):
    _, n_e, d, f2 = w_gu_layers.shape
    src = jnp.arange(GLU_BLOCK)
    perm = (src[:, None] == (2 * (src % LANES) + src // LANES)[None, :]).astype(jnp.bfloat16)
    tk = _row_tile(d)
    return pl.pallas_call(
        _regroup_kernel,
        out_shape=jax.ShapeDtypeStruct((n_e, d, f2), jnp.bfloat16),
        grid=(n_e, d // tk),
        in_specs=[pl.BlockSpec((None, 1, tk, f2), lambda e, i: (layer, e, i, 0)),
                  pl.BlockSpec((GLU_BLOCK, GLU_BLOCK), lambda e, i: (0, 0))],
        out_specs=pl.BlockSpec((1, tk, f2), lambda e, i: (e, i, 0)),
        compiler_params=_compiler_params(("parallel", "parallel")),
        name="regroup_glu",
    )(w_gu_layers, perm)


def _cast_kernel(w_ref, o_ref):
    o_ref[...] = w_ref[...].astype(o_ref.dtype)


def _layer_to_bf16(w_layers, layer):
    _, n_e, r, c = w_layers.shape
    tr = _row_tile(r)
    return pl.pallas_call(
        _cast_kernel,
        out_shape=jax.ShapeDtypeStruct((n_e, r, c), jnp.bfloat16),
        grid=(n_e, r // tr),
        in_specs=[pl.BlockSpec((None, 1, tr, c), lambda e, i: (layer, e, i, 0))],
        out_specs=pl.BlockSpec((1, tr, c), lambda e, i: (e, i, 0)),
        compiler_params=_compiler_params(("parallel", "parallel")),
        name="layer_to_bf16",
    )(w_layers)


def _slot_kernel(r_ref, start_ref, o_ref):
    lane = lax.broadcasted_iota(jnp.int32, r_ref.shape, 1)
    route = r_ref[...]
    slots = jnp.zeros(r_ref.shape, jnp.float32)
    for k in range(TOP_K):
        expert = route[:, ROUTE_IDX_LANE + k:ROUTE_IDX_LANE + k + 1].astype(jnp.int32)
        first = jnp.sum(jnp.where(lane == expert, start_ref[...], 0.0), axis=-1, keepdims=True)
        slots = jnp.where(lane == k, first + route[:, ROUTE_RANK_LANE + k:ROUTE_RANK_LANE + k + 1], slots)
    o_ref[...] = slots.astype(jnp.int32)


def _slots(route, expert_start):
    m = route.shape[0]
    tm = _row_tile(m)
    return pl.pallas_call(
        _slot_kernel,
        out_shape=jax.ShapeDtypeStruct((m, LANES), jnp.int32),
        grid=(m // tm,),
        in_specs=[pl.BlockSpec((tm, LANES), lambda i: (i, 0)), pl.BlockSpec((1, LANES), lambda i: (0, 0))],
        out_specs=pl.BlockSpec((tm, LANES), lambda i: (i, 0)),
        compiler_params=_compiler_params(("parallel",)),
        name="moe_slots",
    )(route, expert_start)


def _regroup_glu_bias(b_gu):
    n_e, f2 = b_gu.shape
    return b_gu.reshape(n_e, f2 // GLU_BLOCK, LANES, 2).transpose(0, 1, 3, 2).reshape(n_e, 1, f2)


def _expert_kernel(be_ref, nb_ref, x_ref, wgu_ref, bgu_ref, wd_ref, bd_ref, o_ref):
    @pl.when(pl.program_id(0) < nb_ref[0])
    def _():
        xb = x_ref[...]
        acts = []
        for c in range(0, wgu_ref.shape[2], GLU_BLOCK):
            h = (jnp.dot(xb, wgu_ref[0, :, c:c + GLU_BLOCK], preferred_element_type=jnp.float32)
                 + bgu_ref[0, :, c:c + GLU_BLOCK])
            hg = jnp.minimum(h[:, :LANES], SWIGLU_LIMIT)
            hl = jnp.clip(h[:, LANES:], -SWIGLU_LIMIT, SWIGLU_LIMIT)
            acts.append((hg * jax.nn.sigmoid(SWIGLU_ALPHA * hg) * (hl + 1.0)).astype(jnp.bfloat16))
        act = jnp.concatenate(acts, axis=1)
        y = jnp.dot(act, wd_ref[0], preferred_element_type=jnp.float32) + bd_ref[0]
        o_ref[...] = y.astype(o_ref.dtype)

    @pl.when(pl.program_id(0) >= nb_ref[0])
    def _():
        o_ref[...] = jnp.zeros_like(o_ref)


def _expert_ffn(x_sorted, block_expert, n_used, w_gu, b_gu, w_dn, b_dn):
    cap, d = x_sorted.shape
    f2 = w_gu.shape[2]
    tm = MOE_ROW_TILE
    n_blocks = cap // tm

    def blk(i, be, nb):
        return (jnp.minimum(i, nb[0] - 1), 0)

    def wsel(i, be, nb):
        return (be[i], 0, 0)

    grid_spec = pltpu.PrefetchScalarGridSpec(
        num_scalar_prefetch=2,
        grid=(n_blocks,),
        in_specs=[pl.BlockSpec((tm, d), blk),
                  pl.BlockSpec((1, d, f2), wsel),
                  pl.BlockSpec((1, 1, f2), wsel),
                  pl.BlockSpec((1, f2 // 2, d), wsel),
                  pl.BlockSpec((1, 1, d), wsel)],
        out_specs=pl.BlockSpec((tm, d), lambda i, be, nb: (i, 0)),
    )
    return pl.pallas_call(
        _expert_kernel,
        out_shape=jax.ShapeDtypeStruct((cap, d), jnp.bfloat16),
        grid_spec=grid_spec,
        compiler_params=_compiler_params(("arbitrary",)),
        name="expert_ffn",
    )(block_expert, n_used, x_sorted, w_gu, b_gu, w_dn, b_dn)


def _moe_ffn(x, x_bf16, group, n, w_router, b_router, w_gu, b_gu, w_dn, b_dn):
    d = x.shape[1]
    tm = MOE_ROW_TILE
    n_e = w_router.shape[1]
    route, counts = _route(x, group, n, w_router, b_router)
    counts = counts[0].astype(jnp.int32)
    padded = (counts + tm - 1) // tm * tm
    pad_end = jnp.cumsum(padded)
    pad_start = pad_end - padded
    cap = n * TOP_K + n_e * tm
    n_blocks = cap // tm
    slot = _slots(route, pad_start.astype(jnp.float32).reshape(1, LANES))[:, :TOP_K]
    slot_kmajor = slot.T.reshape(-1)
    block_first_row = jnp.arange(n_blocks, dtype=jnp.int32) * tm
    block_expert = jnp.minimum(jnp.sum(pad_end[None, :n_e] <= block_first_row[:, None], axis=1), n_e - 1)
    n_used = (pad_end[n_e - 1] // tm).reshape(1)
    top_idx = route[:, ROUTE_IDX_LANE:ROUTE_IDX_LANE + TOP_K].astype(jnp.int32).reshape(-1)
    tok_by_expert = jnp.argsort(top_idx, stable=True).astype(jnp.int32) // TOP_K
    unpadded_start = jnp.cumsum(counts) - counts
    row_in_expert = (block_first_row - pad_start[block_expert])[:, None] + jnp.arange(tm, dtype=jnp.int32)[None, :]
    src = jnp.minimum(unpadded_start[block_expert][:, None] + row_in_expert, n * TOP_K - 1).reshape(-1)
    valid = (row_in_expert < counts[block_expert][:, None]).reshape(-1)
    tok_of_slot = group * n + jnp.where(valid, tok_by_expert.at[src].get(mode="promise_in_bounds"), 0)

    x_sorted = x_bf16.at[tok_of_slot].get(mode="promise_in_bounds")
    yb = _expert_ffn(x_sorted, block_expert.astype(jnp.int32), n_used.astype(jnp.int32), w_gu, b_gu, w_dn, b_dn)
    return yb.at[slot_kmajor].get(mode="promise_in_bounds").reshape(TOP_K, n, d), route


MLSTM_CHUNK = 128
MLSTM_PAIR = 2
MLSTM_NEG = -1e30


def _interleave(generators):
    results = [None] * len(generators)
    live = list(enumerate(generators))
    while live:
        still = []
        for idx, gen in live:
            try:
                next(gen)
                still.append((idx, gen))
            except StopIteration as stop:
                results[idx] = stop.value
        live = still
    return results


def _cumsum_rows(x, reverse):
    n = x.shape[0]
    row = lax.broadcasted_iota(jnp.int32, x.shape, 0)
    sh = 1
    while sh < n:
        if reverse:
            x = x + jnp.where(row < n - sh, pltpu.roll(x, n - sh, 0), 0.0)
        else:
            x = x + jnp.where(row >= sh, pltpu.roll(x, sh, 0), 0.0)
        sh *= 2
    return x


def _cummax_rows(x, reverse):
    n = x.shape[0]
    row = lax.broadcasted_iota(jnp.int32, x.shape, 0)
    sh = 1
    while sh < n:
        if reverse:
            x = jnp.maximum(x, jnp.where(row < n - sh, pltpu.roll(x, n - sh, 0), -jnp.inf))
        else:
            x = jnp.maximum(x, jnp.where(row >= sh, pltpu.roll(x, sh, 0), -jnp.inf))
        sh *= 2
    return x


def _mlstm_kernel(q_ref, k_ref, v_ref, og_ref, g_ref, gain_ref, o_ref, h_s, c_s):
    f32, bf16 = jnp.float32, jnp.bfloat16
    seq = q_ref.shape[0]
    L = MLSTM_CHUNK
    n_chunks = seq // L
    dv = v_ref.shape[1] // MLSTM_PAIR
    dqk = q_ref.shape[1] // MLSTM_PAIR

    lane_q = lax.broadcasted_iota(jnp.int32, (1, q_ref.shape[1]), 1) // dqk
    row = lax.broadcasted_iota(jnp.int32, (L, L), 0)
    col = lax.broadcasted_iota(jnp.int32, (L, L), 1)
    causal = (col <= row, col >= row)
    ones_v = jnp.ones((L, dv), bf16)

    c_s[...] = jnp.zeros_like(c_s)

    def gate_terms(start, d):
        rows = pl.ds(start, L)
        gates = g_ref[rows, :]
        log_f = jnp.minimum(gates, 0.0) - jnp.log(1.0 + jnp.exp(-jnp.abs(gates)))
        cum = _cumsum_rows(log_f, reverse=(d == 1))
        src = gates - pltpu.roll(cum, LANES - MLSTM_PAIR, 1)
        run_max = _cummax_rows(src, reverse=(d == 1))
        return rows, gates, cum, src.T, run_max, q_ref[rows, :], k_ref[rows, :]

    def head_unit(terms, d, j, m_st):
        rows, gates, cum, src_t, run_max, q_all, k_all = terms
        lane_i = 2 * MLSTM_PAIR * d + j
        lane_f = lane_i + MLSTM_PAIR
        b_col, i_col = cum[:, lane_f:lane_f + 1], gates[:, lane_i:lane_i + 1]
        src_row = src_t[lane_i:lane_i + 1, :]
        qz = jnp.where(lane_q == j, q_all, jnp.zeros_like(q_all))
        kz = jnp.where(lane_q == j, k_all, jnp.zeros_like(k_all))
        state = c_s[2 * d + j]
        scores = _nt_dot(qz, kz)
        carried = jnp.dot(qz, state.astype(bf16), preferred_element_type=f32)
        yield
        mu = jnp.maximum(m_st, run_max[:, lane_i:lane_i + 1])
        v_aug = jnp.concatenate([v_ref[rows, j * dv:(j + 1) * dv], ones_v], axis=1)
        qk = scores * (A_DQK ** -0.5) * jnp.exp(jnp.where(causal[d], src_row - mu, MLSTM_NEG))
        edge = L - 1 if d == 0 else 0
        b_last = b_col[edge:edge + 1, :]
        m_new = b_last + mu[edge:edge + 1, :]
        w_s = jnp.exp(b_last - b_col + i_col - m_new)
        local = jnp.dot(qk.astype(bf16), v_aug, preferred_element_type=f32)
        update = _tn_dot((w_s * kz.astype(f32)).astype(bf16), v_aug)
        yield
        nd = local + jnp.exp(m_st - mu) * (A_DQK ** -0.5) * carried
        h_s[d, rows, j * dv:(j + 1) * dv] = nd[:, :dv] / jnp.maximum(jnp.abs(nd[:, dv:]), jnp.exp(-b_col - mu))
        c_s[2 * d + j] = jnp.exp(b_last + m_st - m_new) * state + update
        return m_new

    def body(c, m_all):
        terms = (gate_terms(pl.multiple_of(c * L, L), 0), gate_terms(pl.multiple_of((n_chunks - 1 - c) * L, L), 1))
        return tuple(_interleave([head_unit(terms[d], d, j, m_all[MLSTM_PAIR * d + j])
                                  for d in range(2) for j in range(MLSTM_PAIR)]))

    m_init = tuple(jnp.full((1, 1), A_M_INIT, f32) for _ in range(2 * MLSTM_PAIR))
    lax.fori_loop(0, n_chunks, body, m_init)

    def finish(c, carry):
        rows = pl.ds(pl.multiple_of(c * L, L), L)
        for j in range(MLSTM_PAIR):
            cols = slice(j * dv, (j + 1) * dv)
            h = h_s[0, rows, cols] + h_s[1, rows, cols]
            hn = h * lax.rsqrt(jnp.mean(h * h, axis=-1, keepdims=True) + RMS_EPS) * gain_ref[:, cols]
            o_ref[rows, cols] = (hn * jax.nn.sigmoid(og_ref[rows, cols].astype(f32))).astype(o_ref.dtype)
        return carry

    lax.fori_loop(0, n_chunks, finish, 0)


def _mlstm_mixer(x, w_in, b_in, head_gain):
    bsz, s, d = x.shape
    a_qk = A_HEADS * A_DQK
    dv = d // A_HEADS
    n_main = 2 * a_qk + 2 * d
    n_pairs = A_HEADS // MLSTM_PAIR
    pw_qk = MLSTM_PAIR * A_DQK
    pw_v = MLSTM_PAIR * dv
    assert pw_qk == LANES and s % MLSTM_CHUNK == 0
    gate_cols = jnp.asarray([[n_main + t * A_HEADS + MLSTM_PAIR * hp + j for t in range(4) for j in range(MLSTM_PAIR)]
                             for hp in range(n_pairs)])
    n_gate = gate_cols.shape[1]
    w_tail = jnp.pad(w_in[:, gate_cols], ((0, 0), (0, 0), (0, LANES - n_gate))).reshape(d, n_pairs * LANES)
    b_tail = jnp.pad(b_in[gate_cols], ((0, 0), (0, LANES - n_gate))).reshape(n_pairs * LANES)
    w = jnp.concatenate([w_in[:, :n_main], w_tail], axis=1)
    b = jnp.concatenate([b_in[:n_main], b_tail])
    p, gates = _project(x.reshape(bsz * s, d), w, b, jnp.bfloat16, f32_tail=n_pairs * LANES)
    p = p.reshape(bsz, s, n_main)
    gates = gates.reshape(bsz, s, n_pairs * LANES)
    k0 = a_qk // pw_qk
    v0 = 2 * a_qk // pw_v
    o0 = (2 * a_qk + d) // pw_v
    return pl.pallas_call(
        _mlstm_kernel,
        out_shape=jax.ShapeDtypeStruct((bsz, s, d), jnp.bfloat16),
        grid=(bsz, n_pairs),
        in_specs=[pl.BlockSpec((None, s, pw_qk), lambda b, h: (b, 0, h)),
                  pl.BlockSpec((None, s, pw_qk), lambda b, h: (b, 0, k0 + h)),
                  pl.BlockSpec((None, s, pw_v), lambda b, h: (b, 0, v0 + h)),
                  pl.BlockSpec((None, s, pw_v), lambda b, h: (b, 0, o0 + h)),
                  pl.BlockSpec((None, s, LANES), lambda b, h: (b, 0, h)),
                  pl.BlockSpec((1, pw_v), lambda b, h: (0, h))],
        out_specs=pl.BlockSpec((None, s, pw_v), lambda b, h: (b, 0, h)),
        scratch_shapes=[pltpu.VMEM((2, s, pw_v), jnp.float32),
                        pltpu.VMEM((2 * MLSTM_PAIR, pw_qk, 2 * dv), jnp.float32)],
        compiler_params=_compiler_params(("parallel", "parallel")),
        name="mlstm_scan",
    )(p, p, p, p, gates, head_gain.reshape(1, d))


RWKV_CHUNK = 64
RWKV_GROUP = 4
RWKV_LANES = RWKV_GROUP * B_HEAD
RWKV_PREP_ROWS = 256
RWKV_LOCAL_UNROLL = 4


def _f32_dot(a, b_bf16):
    hi = a.astype(jnp.bfloat16)
    lo = (a - hi.astype(jnp.float32)).astype(jnp.bfloat16)
    return (jnp.dot(hi, b_bf16, preferred_element_type=jnp.float32)
            + jnp.dot(lo, b_bf16, preferred_element_type=jnp.float32))


def _nt_dot(a, b):
    return lax.dot_general(a, b, (((1,), (1,)), ((), ())), preferred_element_type=jnp.float32)


def _tn_dot(a, b):
    return lax.dot_general(a, b, (((0,), (0,)), ((), ())), preferred_element_type=jnp.float32)


def _rwkv_scan_kernel(r_ref, k_ref, v_ref, wl_ref, al_ref, gl_ref, mu_r_ref, mu_k_ref, mu_v_ref, mu_wl_ref,
                      mu_al_ref, mu_gl_ref, w0_ref, wup_ref, a0_ref, aup_ref, kk_ref, ka_ref, rk_ref, gup_ref,
                      lng_ref, lnb_ref, o_ref,
                      r_s, v_s, kap_s, g_s, logw_s, kh_s, beta_s, y_s,
                      wmat_s, uloc_s, avq_s, rt_s, aqb_s, kend_s, bend_s, ptot_s):
    f32, bf16 = jnp.float32, jnp.bfloat16
    seq = r_ref.shape[0]
    L, W, RT = RWKV_CHUNK, RWKV_LANES, RWKV_PREP_ROWS
    n_chunks = seq // L
    n_prep = seq // RT

    lane = lax.broadcasted_iota(jnp.int32, (1, W), 1)
    head_masks = [(lane // B_HEAD) == h for h in range(RWKV_GROUP)]
    ones_bd = ((lax.broadcasted_iota(jnp.int32, (W, W), 0) // B_HEAD)
               == (lax.broadcasted_iota(jnp.int32, (W, W), 1) // B_HEAD)).astype(bf16)

    def block_diag(a):
        zero = jnp.zeros_like(a)
        return jnp.concatenate([jnp.where(m, a, zero) for m in head_masks], axis=0)

    def seg_sum(a):
        return _f32_dot(a, ones_bd)

    def shifted(ref, mu_ref, i, rows):
        o = pl.multiple_of(i * RT, RT)
        x = ref[pl.ds(o, RT), :].astype(f32)
        nb = BF16_TILE_ROWS
        before = ref[pl.ds(pl.multiple_of(jnp.maximum(o - nb, 0), nb), nb), :].astype(f32)[nb - 1:nb, :]
        after = ref[pl.ds(pl.multiple_of(jnp.minimum(o + RT, seq - nb), nb), nb), :].astype(f32)[0:1, :]
        before = jnp.where(i > 0, before, 0.0)
        after = jnp.where(i < n_prep - 1, after, 0.0)
        prev = jnp.where(rows == 0, before, pltpu.roll(x, 1, 0))
        nxt = jnp.where(rows == RT - 1, after, pltpu.roll(x, RT - 1, 0))
        return x + mu_ref[...] * (0.5 * (prev + nxt) - x)

    def prep(i, carry):
        o = pl.multiple_of(i * RT, RT)
        rows_w = lax.broadcasted_iota(jnp.int32, (RT, W), 0)
        rows_n = lax.broadcasted_iota(jnp.int32, (RT, wl_ref.shape[1]), 0)
        r = shifted(r_ref, mu_r_ref, i, rows_w)
        k = shifted(k_ref, mu_k_ref, i, rows_w)
        v = shifted(v_ref, mu_v_ref, i, rows_w)
        wl = shifted(wl_ref, mu_wl_ref, i, rows_n)
        al = shifted(al_ref, mu_al_ref, i, rows_n)
        gl = shifted(gl_ref, mu_gl_ref, i, rows_n)
        kk0 = k * kk_ref[...]
        kap = kk0 / jnp.maximum(jnp.sqrt(seg_sum(kk0 * kk0)), 1e-12)
        r_s[pl.ds(o, RT), :] = r
        v_s[pl.ds(o, RT), :] = v.astype(bf16)
        kap_s[pl.ds(o, RT), :] = kap
        g_s[pl.ds(o, RT), :] = jnp.dot(jax.nn.sigmoid(gl).astype(bf16), gup_ref[...],
                                       preferred_element_type=f32).astype(bf16)
        for d in range(2):
            wl_d = jnp.tanh(wl[:, d * B_DECAY_LORA:(d + 1) * B_DECAY_LORA]).astype(bf16)
            al_d = al[:, d * B_AAA_LORA:(d + 1) * B_AAA_LORA].astype(bf16)
            w_raw = w0_ref[d:d + 1, :] + jnp.dot(wl_d, wup_ref[d], preferred_element_type=f32)
            a = jax.nn.sigmoid(a0_ref[d:d + 1, :] + jnp.dot(al_d, aup_ref[d], preferred_element_type=f32))
            logw_s[d, pl.ds(o, RT), :] = -jnp.exp(-0.5) * jax.nn.sigmoid(w_raw)
            kh_s[d, pl.ds(o, RT), :] = (k * (1.0 + (a - 1.0) * ka_ref[...])).astype(bf16)
            beta_s[d, pl.ds(o, RT), :] = (kap * a).astype(bf16)
        return carry

    lax.fori_loop(0, n_prep, prep, 0)

    row = lax.broadcasted_iota(jnp.int32, (L, W), 0)
    col = lax.broadcasted_iota(jnp.int32, (L, W), 1) % B_HEAD
    eye_cat = (row == col).astype(f32)
    strict = (row > col, row < col)
    incl = (row >= col, row <= col)

    def cumsum_rows(x, reverse):
        sh = 1
        while sh < L:
            if reverse:
                x = x + jnp.where(row < L - sh, pltpu.roll(x, L - sh, 0), 0.0)
            else:
                x = x + jnp.where(row >= sh, pltpu.roll(x, sh, 0), 0.0)
            sh *= 2
        return x

    interleave = _interleave

    def local_part(c, d):
        rows = pl.ds(pl.multiple_of(c * L, L), L)
        r, kap = r_s[rows, :], kap_s[rows, :]
        logw, kh, beta = logw_s[d, rows, :], kh_s[d, rows, :].astype(f32), beta_s[d, rows, :].astype(f32)
        cum = cumsum_rows(logw, reverse=(d == 1))
        tot = cum[L - 1:L, :] if d == 0 else cum[0:1, :]
        p_in, p_inv, p_end = jnp.exp(cum), jnp.exp(-cum), jnp.exp(tot - cum)
        kap_t = (kap * jnp.exp(cum - logw)).astype(bf16)
        r_t = (r * p_in).astype(bf16)
        v_bd = block_diag(v_s[rows, :])
        g_all = _nt_dot(jnp.concatenate([kap_t, r_t], axis=0),
                        jnp.concatenate([block_diag((beta * p_inv).astype(bf16)),
                                         block_diag((kh * p_inv).astype(bf16))], axis=0))
        yield
        x_pow = -jnp.where(strict[d], g_all[:L, :W], 0.0)
        a_ak = jnp.where(strict[d], g_all[:L, W:], 0.0)
        a_qb = jnp.where(incl[d], g_all[L:, :W], 0.0)
        a_qk = jnp.where(incl[d], g_all[L:, W:], 0.0)
        t_inv = eye_cat + x_pow
        av = jnp.dot(jnp.concatenate([a_ak, a_qk], axis=0).astype(bf16), v_bd, preferred_element_type=f32)
        x_pow = jnp.dot(x_pow.astype(bf16), block_diag(x_pow.astype(bf16)), preferred_element_type=f32)
        yield
        n_steps = L.bit_length() - 2
        for step in range(n_steps):
            if step < n_steps - 1:
                both = jnp.dot(jnp.concatenate([x_pow, t_inv], axis=0).astype(bf16),
                               block_diag(x_pow.astype(bf16)), preferred_element_type=f32)
                yield
                x_pow, t_inv = both[:L], t_inv + both[L:]
            else:
                last = jnp.dot(t_inv.astype(bf16), block_diag(x_pow.astype(bf16)), preferred_element_type=f32)
                yield
                t_inv = t_inv + last
        t_b = t_inv.astype(bf16)
        w_mat = jnp.dot(t_b, block_diag(kap_t), preferred_element_type=f32)
        u_loc = jnp.dot(t_b, block_diag(av[:L].astype(bf16)), preferred_element_type=f32)
        yield
        wmat_s[d, rows, :] = w_mat.astype(bf16)
        uloc_s[d, rows, :] = u_loc.astype(bf16)
        avq_s[d, rows, :] = av[L:].astype(bf16)
        rt_s[d, rows, :] = r_t
        aqb_s[d, rows, :] = a_qb.astype(bf16)
        kend_s[d, rows, :] = (kh * p_end).astype(bf16)
        bend_s[d, rows, :] = (beta * p_end).astype(bf16)
        ptot_s[d, pl.ds(pl.multiple_of(c * SUBLANES, SUBLANES), SUBLANES), :] = jnp.broadcast_to(
            jnp.exp(tot), (SUBLANES, W))

    def local_body(i, carry):
        interleave([local_part(i * RWKV_LOCAL_UNROLL + j, d) for j in range(RWKV_LOCAL_UNROLL) for d in range(2)])
        return carry

    lax.fori_loop(0, n_chunks // RWKV_LOCAL_UNROLL, local_body, 0)

    def carried_part(c, state, d):
        rows = pl.ds(pl.multiple_of(c * L, L), L)
        ws = _nt_dot(jnp.concatenate([wmat_s[d, rows, :], rt_s[d, rows, :]], axis=0), block_diag(state.astype(bf16)))
        yield
        u = ws[:L] + uloc_s[d, rows, :].astype(f32)
        u_b = u.astype(bf16)
        full = _tn_dot(jnp.concatenate([v_s[rows, :], -u_b], axis=0),
                       jnp.concatenate([kend_s[d, rows, :], bend_s[d, rows, :]], axis=0))
        y_loc = jnp.dot(aqb_s[d, rows, :], block_diag(u_b), preferred_element_type=f32)
        yield
        new_state = state * ptot_s[d, pl.ds(pl.multiple_of(c * SUBLANES, SUBLANES), SUBLANES), :][0:1, :]
        for h, m in enumerate(head_masks):
            new_state = new_state + jnp.where(m, full[h * B_HEAD:(h + 1) * B_HEAD, :], 0.0)
        y_s[d, rows, :] = ws[L:] + avq_s[d, rows, :].astype(f32) - y_loc
        return new_state

    def carried_body(c, states):
        return tuple(interleave([carried_part(c, states[0], 0), carried_part(n_chunks - 1 - c, states[1], 1)]))

    zero_state = jnp.zeros((B_HEAD, W), f32)
    lax.fori_loop(0, n_chunks, carried_body, (zero_state, zero_state))

    def finish(i, carry):
        rows = pl.ds(pl.multiple_of(i * RT, RT), RT)
        y = y_s[0, rows, :] + y_s[1, rows, :]
        mean = seg_sum(y) * (1.0 / B_HEAD)
        yc = y - mean
        var = seg_sum(yc * yc) * (1.0 / B_HEAD)
        yn = yc * lax.rsqrt(var + B_GN_EPS) * lng_ref[...] + lnb_ref[...]
        kh_both = kh_s[0, rows, :].astype(f32) + kh_s[1, rows, :].astype(f32)
        bonus = seg_sum(r_s[rows, :] * kh_both * rk_ref[...]) * v_s[rows, :].astype(f32)
        o_ref[rows, :] = ((yn + bonus) * g_s[rows, :].astype(f32)).astype(o_ref.dtype)
        return carry

    lax.fori_loop(0, n_prep, finish, 0)


def _rwkv7_mixer(x, w_in, mu, w0, w_up, a0, a_up, k_k, k_a, r_k, g_up, lnx_g, lnx_b):
    bsz, s, d = x.shape
    n_proj = w_in.shape[1]
    W = RWKV_LANES
    assert d % W == 0 and s % RWKV_PREP_ROWS == 0 and B_GATE_LORA == LANES
    assert 2 * B_DECAY_LORA == LANES and 2 * B_AAA_LORA == LANES
    assert (s // RWKV_CHUNK) % RWKV_LOCAL_UNROLL == 0
    p = _project(x.reshape(bsz * s, d), w_in, jnp.zeros((n_proj,), jnp.float32), jnp.bfloat16)
    p = p.reshape(bsz, s, n_proj)
    n_groups = d // W
    gw = d // W
    lora0 = 3 * d // LANES

    def seq_w(off):
        return pl.BlockSpec((None, s, W), lambda b, g, off=off: (b, 0, off + g))

    def seq_n(idx):
        return pl.BlockSpec((None, s, LANES), lambda b, g, idx=idx: (b, 0, idx))

    def vec_w(off):
        return pl.BlockSpec((1, W), lambda b, g, off=off: (0, off + g))

    def vec_n(idx):
        return pl.BlockSpec((1, LANES), lambda b, g, idx=idx: (0, idx))

    par_w = pl.BlockSpec((1, W), lambda b, g: (0, g))
    two_w = pl.BlockSpec((2, W), lambda b, g: (0, g))
    up_w = pl.BlockSpec((2, B_DECAY_LORA, W), lambda b, g: (0, 0, g))
    mu2 = mu.reshape(1, n_proj)
    row = lambda a: a.reshape(1, d)
    f32_w = pltpu.VMEM((s, W), jnp.float32)
    b16_w = pltpu.VMEM((s, W), jnp.bfloat16)
    f32_2w = pltpu.VMEM((2, s, W), jnp.float32)
    b16_2w = pltpu.VMEM((2, s, W), jnp.bfloat16)
    decay_rows = pltpu.VMEM((2, SUBLANES * (s // RWKV_CHUNK), W), jnp.float32)
    return pl.pallas_call(
        _rwkv_scan_kernel,
        out_shape=jax.ShapeDtypeStruct((bsz, s, d), jnp.bfloat16),
        grid=(bsz, n_groups),
        in_specs=[seq_w(0), seq_w(gw), seq_w(2 * gw), seq_n(lora0), seq_n(lora0 + 1), seq_n(lora0 + 2),
                  vec_w(0), vec_w(gw), vec_w(2 * gw), vec_n(lora0), vec_n(lora0 + 1), vec_n(lora0 + 2),
                  two_w, up_w, two_w, up_w, par_w, par_w, par_w,
                  pl.BlockSpec((B_GATE_LORA, W), lambda b, g: (0, g)), par_w, par_w],
        out_specs=pl.BlockSpec((None, s, W), lambda b, g: (b, 0, g)),
        scratch_shapes=[f32_w, b16_w, f32_w, b16_w, f32_2w, b16_2w, b16_2w, f32_2w,
                        b16_2w, b16_2w, b16_2w, b16_2w, b16_2w, b16_2w, b16_2w, decay_rows],
        compiler_params=_compiler_params(("parallel", "parallel")),
        name="rwkv7_scan",
    )(p, p, p, p, p, p, mu2, mu2, mu2, mu2, mu2, mu2,
      w0, w_up.astype(jnp.bfloat16), a0, a_up.astype(jnp.bfloat16), row(k_k), row(k_a), row(r_k),
      g_up.astype(jnp.bfloat16), row(lnx_g), row(lnx_b))


def _axial_rope_tables(s):
    rows = s // GRID_W
    row = jnp.repeat(jnp.arange(rows, dtype=jnp.float32), GRID_W)
    col = jnp.tile(jnp.arange(GRID_W, dtype=jnp.float32), rows)
    n_freq = C_HEAD_DIM // 4
    inv_freq = ROPE_THETA ** (-jnp.arange(n_freq, dtype=jnp.float32) / n_freq)
    ang = jnp.concatenate([row[:, None] * inv_freq, col[:, None] * inv_freq], axis=-1)
    return jnp.cos(ang), jnp.sin(ang)


ATTN_Q_TILE = 256


def _rms_rope(x, gain, cos_f, sin_f):
    xn = x * lax.rsqrt(jnp.mean(x * x, axis=-1, keepdims=True) + RMS_EPS) * gain
    return xn * cos_f + pltpu.roll(xn, C_HEAD_DIM // 2, 1) * sin_f


def _attn_kernel(q_ref, k_ref, v_ref, cq_ref, sq_ref, ck_ref, sk_ref, qg_ref, kg_ref, o_ref, kr_s, va_s):
    f32, bf16 = jnp.float32, jnp.bfloat16
    dh = C_HEAD_DIM

    @pl.when(pl.program_id(2) == 0)
    def _():
        kr_s[...] = _rms_rope(k_ref[...].astype(f32), kg_ref[...], ck_ref[...], sk_ref[...]).astype(bf16)
        va_s[:, :dh] = v_ref[...]
        va_s[:, dh:] = jnp.ones((va_s.shape[0], dh), bf16)

    group = q_ref.shape[1] // dh
    for g in range(group):
        cols = slice(g * dh, (g + 1) * dh)
        q = _rms_rope(q_ref[:, cols].astype(f32), qg_ref[...], cq_ref[...], sq_ref[...]) * dh ** -0.5
        scores = _nt_dot(q.astype(bf16), kr_s[...])
        p = jnp.exp((scores - jnp.max(scores, axis=-1, keepdims=True)).astype(bf16))
        od = jnp.dot(p, va_s[...], preferred_element_type=f32)
        o_ref[:, cols] = (od[:, :dh] / od[:, dh:]).astype(o_ref.dtype)


def _axial_gqa_mixer(x, w_in, q_gain, k_gain):
    bsz, s, d = x.shape
    dh = C_HEAD_DIM
    qh = d // dh
    group = qh // C_KV_HEADS
    n_proj = w_in.shape[1]
    half = jnp.concatenate([jnp.arange(0, dh, 2), jnp.arange(1, dh, 2)])
    n_rot = (qh + C_KV_HEADS) * dh
    w_rot = w_in[:, :n_rot].reshape(d, n_rot // dh, dh // 2, 2).transpose(0, 1, 3, 2).reshape(d, n_rot)
    w = jnp.concatenate([w_rot, w_in[:, n_rot:]], axis=1)
    p = _project(x.reshape(bsz * s, d), w, jnp.zeros((n_proj,), jnp.float32), jnp.bfloat16)
    p = p.reshape(bsz, s, n_proj)
    cos, sin = _axial_rope_tables(s)
    cos_f = jnp.concatenate([cos, cos], axis=-1)
    sin_f = jnp.concatenate([-sin, sin], axis=-1)
    tq = min(ATTN_Q_TILE, s)
    gw = group * dh
    q_rows = pl.BlockSpec((tq, dh), lambda b, h, i: (i, 0))
    k_rows = pl.BlockSpec((s, dh), lambda b, h, i: (0, 0))
    vec = pl.BlockSpec((1, dh), lambda b, h, i: (0, 0))
    return pl.pallas_call(
        _attn_kernel,
        out_shape=jax.ShapeDtypeStruct((bsz, s, d), jnp.bfloat16),
        grid=(bsz, C_KV_HEADS, s // tq),
        in_specs=[pl.BlockSpec((None, tq, gw), lambda b, h, i: (b, i, h)),
                  pl.BlockSpec((None, s, dh), lambda b, h, i: (b, 0, qh + h)),
                  pl.BlockSpec((None, s, dh), lambda b, h, i: (b, 0, qh + C_KV_HEADS + h)),
                  q_rows, q_rows, k_rows, k_rows, vec, vec],
        out_specs=pl.BlockSpec((None, tq, gw), lambda b, h, i: (b, i, h)),
        scratch_shapes=[pltpu.VMEM((s, dh), jnp.bfloat16), pltpu.VMEM((s, 2 * dh), jnp.bfloat16)],
        compiler_params=_compiler_params(("parallel", "parallel", "arbitrary")),
        name="axial_attention",
    )(p, p, p, cos_f, sin_f, cos_f, sin_f, q_gain[half].reshape(1, dh), k_gain[half].reshape(1, dh))


def kernel(x, a_w_in, a_b_in, a_head_gain, a_w_out, b_w_in, b_mu, b_w0, b_w_up, b_a0, b_a_up, b_k_k, b_k_a, b_r_k, b_g_up, b_lnx_g, b_lnx_b, b_w_out, c_w_in, c_q_gain, c_k_gain, c_w_out, ln1_g, ln1_b, moe_w_router, moe_b_router, moe_w_gu, moe_b_gu, moe_w_dn, moe_b_dn, ln2_g, ln2_b):
    bsz, s, d = x.shape
    depth = ln1_g.shape[0]
    xt = x.reshape(bsz * s, d)
    for i in range(depth):
        kind = i % N_MIXERS
        j = i // N_MIXERS
        x3 = xt.reshape(bsz, s, d)
        if kind == 0:
            h = _mlstm_mixer(x3, a_w_in[j], a_b_in[j], a_head_gain[j])
            w_out = a_w_out[j]
        elif kind == 1:
            h = _rwkv7_mixer(x3, b_w_in[j], b_mu[j], b_w0[j], b_w_up[j], b_a0[j], b_a_up[j],
                             b_k_k[j], b_k_a[j], b_r_k[j], b_g_up[j], b_lnx_g[j], b_lnx_b[j])
            w_out = b_w_out[j]
        else:
            h = _axial_gqa_mixer(x3, c_w_in[j], c_q_gain[j], c_k_gain[j])
            w_out = c_w_out[j]
        xt, xt_bf16 = _outproj_ln(h.reshape(bsz * s, d), w_out, xt, ln1_g[i], ln1_b[i])
        w_gu = _regroup_glu_columns(moe_w_gu, i)
        b_gu = _regroup_glu_bias(moe_b_gu[i])
        w_dn = _layer_to_bf16(moe_w_dn, i)
        b_dn = moe_b_dn[i][:, None, :]
        n_group = bsz * s // MOE_TOKEN_GROUPS
        groups = [_moe_ffn(xt, xt_bf16, grp, n_group, moe_w_router[i], moe_b_router[i], w_gu, b_gu, w_dn, b_dn)
                  for grp in range(MOE_TOKEN_GROUPS)]
        xt = _combine_ln([y for y, _ in groups], [r for _, r in groups], xt, ln2_g[i], ln2_b[i])
    return xt.reshape(bsz, s, d)
```
